```python
import math
import jax, jax.numpy as jnp
from jax import lax
import numpy as np

D_MODEL = 1024
BATCH = 16
SEQ = 256
DEPTH = 1
DEC_BATCH = 4
DEC_SEQ = 2048
PAST_LEN = 256

GRID_W = 64
N_FOURIER_GROUPS = 4
FOURIER_GROUP = D_MODEL // 8
D_FOURIER = N_FOURIER_GROUPS * FOURIER_GROUP
N_HEADS = 8
QK_NOPE = D_MODEL // 16
QK_ROPE = D_MODEL // 32
V_DIM = D_MODEL // 16
Q_RANK = D_MODEL // 4
KV_RANK = D_MODEL // 8
D_ATTN = N_HEADS * V_DIM
ROPE_THETA = 10000.0
EPS = 1e-6
Q_BLOCK = 128
SPLITS = (D_FOURIER, D_FOURIER, Q_RANK, KV_RANK, QK_ROPE, D_ATTN, D_MODEL, D_MODEL)
D_IN = 2 * D_FOURIER + Q_RANK + KV_RANK + QK_ROPE + D_ATTN + 2 * D_MODEL

kernel_name = 'fnet_mla_gated_hybrid_dit_step'


def rms_norm(x, g=None):
    xf = x.astype(jnp.float32)
    y = xf * lax.rsqrt(jnp.mean(xf * xf, axis=-1, keepdims=True) + EPS)
    if g is not None:
        y = y * g.astype(jnp.float32)
    return y.astype(x.dtype)


def axial_rope(x):
    n = x.shape[1]
    rows = n // GRID_W
    t = jnp.arange(rows * GRID_W)
    row = (t // GRID_W).astype(jnp.float32)
    col = (t % GRID_W).astype(jnp.float32)
    half = QK_ROPE // 2
    inv = ROPE_THETA ** (-jnp.arange(0, half, 2, dtype=jnp.float32) / half)
    ar = row[:, None] * inv
    ac = col[:, None] * inv
    ang = jnp.concatenate([ar, ar, ac, ac], axis=-1)
    ang = ang.reshape((n,) + (1,) * (x.ndim - 3) + (QK_ROPE,))
    cos, sin = jnp.cos(ang), jnp.sin(ang)
    xf = x.astype(jnp.float32)
    q4 = half // 2
    xr, xc = xf[..., :half], xf[..., half:]
    rot = jnp.concatenate([-xr[..., q4:], xr[..., :q4], -xc[..., q4:], xc[..., :q4]], axis=-1)
    return (xf * cos + rot * sin).astype(x.dtype)


def fourier_mix(u):
    b, n, _ = u.shape
    ug = u.astype(jnp.float32).reshape(b, n, N_FOURIER_GROUPS, FOURIER_GROUP)
    f = jnp.fft.fftn(ug, axes=(1, 3), norm='ortho').real
    return f.reshape(b, n, D_FOURIER).astype(u.dtype)


def mla_queries(cq, q_norm_g, w_uq):
    q = rms_norm(cq, q_norm_g) @ w_uq
    q = q.reshape(cq.shape[0], cq.shape[1], N_HEADS, QK_NOPE + QK_ROPE)
    return q[..., :QK_NOPE], q[..., QK_NOPE:]


def mla_keys_values(ckv_n, w_ukv):
    kv = (ckv_n @ w_ukv).reshape(ckv_n.shape[0], ckv_n.shape[1], N_HEADS, QK_NOPE + V_DIM)
    return kv[..., :QK_NOPE], kv[..., QK_NOPE:]


def mla_attention(q_nope, q_rope, k_nope, k_rope, v):
    b, n, h, _ = q_nope.shape
    nb = n // Q_BLOCK
    scale = (QK_NOPE + QK_ROPE) ** -0.5

    def to_blocks(t):
        return t.reshape((b, nb, Q_BLOCK) + t.shape[2:]).swapaxes(0, 1)

    def block(qs):
        qn, qr = qs
        s = jnp.einsum('bqhd,bkhd->bhqk', qn, k_nope) + jnp.einsum('bqhr,bkr->bhqk', qr, k_rope)
        p = jax.nn.softmax(s.astype(jnp.float32) * scale, axis=-1).astype(v.dtype)
        return jnp.einsum('bhqk,bkhd->bqhd', p, v)

    o = lax.map(block, (to_blocks(q_nope), to_blocks(q_rope)))
    return o.swapaxes(0, 1).reshape(b, n, h * V_DIM)


def layer_in(x, cond, w_ada_l, b_ada_l, w_in_l):
    shift, scale, gate = jnp.split(jax.nn.silu(cond) @ w_ada_l + b_ada_l, 3, axis=-1)
    h = rms_norm(x) * (1 + scale) + shift
    parts = jnp.split(h @ w_in_l, np.cumsum(SPLITS)[:-1], axis=-1)
    return gate, parts


def merge_out(x, gate, u_f, z_f, attn, z_a, g_f, g_a, w_f_out_l, w_a_out_l, w_out_l):
    y_f = (fourier_mix(u_f) * jax.nn.silu(z_f)) @ w_f_out_l
    y_a = (attn * jax.nn.silu(z_a)) @ w_a_out_l
    merged = jax.nn.sigmoid(g_f) * y_f + jax.nn.sigmoid(g_a) * y_a
    return x + gate * (merged @ w_out_l)


def context_layer(x, c_ctx, w_ada_l, b_ada_l, w_in_l, q_norm_g_l, w_uq_l, kv_norm_g_l,
                  w_ukv_l, w_f_out_l, w_a_out_l, w_out_l):
    gate, (u_f, z_f, cq, ckv, k_rope, z_a, g_f, g_a) = layer_in(
        x, c_ctx[None, None, :], w_ada_l, b_ada_l, w_in_l)
    q_nope, q_rope = mla_queries(cq, q_norm_g_l, w_uq_l)
    ckv_n = rms_norm(ckv, kv_norm_g_l)
    k_nope, v = mla_keys_values(ckv_n, w_ukv_l)
    attn = mla_attention(q_nope, q_rope, k_nope, k_rope, v)
    x = merge_out(x, gate, u_f, z_f, attn, z_a, g_f, g_a, w_f_out_l, w_a_out_l, w_out_l)
    return x, ckv_n, k_rope


def latent_layer(x, c, ckv_ctx, krope_ctx, w_ada_l, b_ada_l, w_in_l, q_norm_g_l, w_uq_l,
                 kv_norm_g_l, w_ukv_l, w_f_out_l, w_a_out_l, w_out_l):
    gate, (u_f, z_f, cq, ckv, k_rope, z_a, g_f, g_a) = layer_in(
        x, c[:, None, :], w_ada_l, b_ada_l, w_in_l)
    q_nope, q_rope = mla_queries(cq, q_norm_g_l, w_uq_l)
    q_rope = axial_rope(q_rope)
    k_nope_lat, v_lat = mla_keys_values(rms_norm(ckv, kv_norm_g_l), w_ukv_l)
    k_nope_ctx, v_ctx = mla_keys_values(ckv_ctx, w_ukv_l)
    k_nope = jnp.concatenate([k_nope_ctx, k_nope_lat], axis=1)
    k_rope_all = jnp.concatenate([krope_ctx, axial_rope(k_rope)], axis=1)
    v = jnp.concatenate([v_ctx, v_lat], axis=1)
    attn = mla_attention(q_nope, q_rope, k_nope, k_rope_all, v)
    return merge_out(x, gate, u_f, z_f, attn, z_a, g_f, g_a, w_f_out_l, w_a_out_l, w_out_l)


def setup_inputs(seed: int = 0) -> dict:
    key = jax.random.key(seed)
    ks = jax.random.split(key, 20)
    f32 = jnp.float32

    def nrm(k, shape, scale=1.0):
        return jax.random.normal(k, shape, dtype=f32) * scale

    return {
        'x_prompt': nrm(ks[0], (BATCH, SEQ, D_MODEL)),
        'x_sample': nrm(ks[1], (DEC_BATCH, DEC_SEQ, D_MODEL)),
        'cache_ckv': nrm(ks[2], (DEC_BATCH, DEPTH, PAST_LEN, KV_RANK)),
        'cache_krope': nrm(ks[3], (DEC_BATCH, DEPTH, PAST_LEN, QK_ROPE)),
        'c': nrm(ks[4], (DEC_BATCH, D_MODEL)),
        'c_ctx': nrm(ks[5], (D_MODEL,)),
        'w_ada': nrm(ks[6], (DEPTH, D_MODEL, 3 * D_MODEL), 0.5 * D_MODEL ** -0.5),
        'b_ada': nrm(ks[7], (DEPTH, 3 * D_MODEL), 0.02),
        'w_in': nrm(ks[8], (DEPTH, D_MODEL, D_IN), D_MODEL ** -0.5),
        'q_norm_g': 1.0 + nrm(ks[9], (DEPTH, Q_RANK), 0.02),
        'w_uq': nrm(ks[10], (DEPTH, Q_RANK, N_HEADS * (QK_NOPE + QK_ROPE)), Q_RANK ** -0.5),
        'kv_norm_g': 1.0 + nrm(ks[11], (DEPTH, KV_RANK), 0.02),
        'w_ukv': nrm(ks[12], (DEPTH, KV_RANK, N_HEADS * (QK_NOPE + V_DIM)), KV_RANK ** -0.5),
        'w_f_out': nrm(ks[13], (DEPTH, D_FOURIER, D_MODEL), D_FOURIER ** -0.5),
        'w_a_out': nrm(ks[14], (DEPTH, D_ATTN, D_MODEL), D_ATTN ** -0.5),
        'w_out': nrm(ks[15], (DEPTH, D_MODEL, D_MODEL), D_MODEL ** -0.5),
        'final_norm_g': 1.0 + nrm(ks[16], (D_MODEL,), 0.02),
    }


def reference(x_prompt, x_sample, cache_ckv, cache_krope, c, c_ctx, w_ada, b_ada, w_in,
              q_norm_g, w_uq, kv_norm_g, w_ukv, w_f_out, w_a_out, w_out, final_norm_g):
    xp = x_prompt
    ckv_states = []
    krope_states = []
    for l in range(DEPTH):
        xp, ckv_n, k_rope = context_layer(
            xp, c_ctx, w_ada[l], b_ada[l], w_in[l], q_norm_g[l], w_uq[l], kv_norm_g[l],
            w_ukv[l], w_f_out[l], w_a_out[l], w_out[l])
        ckv_states.append(ckv_n)
        krope_states.append(k_rope)
    y_prompt = rms_norm(xp, final_norm_g)
    state_ckv = jnp.stack(ckv_states, axis=1)
    state_krope = jnp.stack(krope_states, axis=1)

    xs = x_sample
    for l in range(DEPTH):
        xs = latent_layer(
            xs, c, cache_ckv[:, l], cache_krope[:, l], w_ada[l], b_ada[l], w_in[l],
            q_norm_g[l], w_uq[l], kv_norm_g[l], w_ukv[l], w_f_out[l], w_a_out[l], w_out[l])
    y_sample = rms_norm(xs, final_norm_g)
    return (y_prompt, y_sample, state_ckv, state_krope)
```

```python
import functools
import math

import jax
import jax.numpy as jnp
import numpy as np
from jax import lax
from jax.experimental import pallas as pl
from jax.experimental.pallas import tpu as pltpu

F32 = jnp.float32
BF16 = jnp.bfloat16

D_MODEL = 1024
GRID_W = 64
N_GROUPS = 4
GROUP = 128
D_FOURIER = N_GROUPS * GROUP
N_HEADS = 8
QK_NOPE = 64
QK_ROPE = 32
V_DIM = 64
Q_RANK = 256
KV_RANK = 128
D_ATTN = N_HEADS * V_DIM
ROPE_THETA = 10000.0
EPS = 1e-6
HEAD_PAD = 128
ROPE_LANE = QK_NOPE
SM_SCALE = (QK_NOPE + QK_ROPE) ** -0.5
C_UF, C_ZF, C_MID, C_ZA, C_GF, C_GA, C_END = 0, 512, 1024, 1536, 2048, 3072, 4096
VMEM_LIMIT_BYTES = 56 * 1024 * 1024


def _const_spec(shape):
    nd = len(shape)
    return pl.BlockSpec(shape, lambda *_: (0,) * nd, pipeline_mode=pl.Buffered(1))


def _params(sem):
    return pltpu.CompilerParams(dimension_semantics=sem, vmem_limit_bytes=VMEM_LIMIT_BYTES)


def _rms(x):
    return x * lax.rsqrt(jnp.mean(x * x, axis=-1, keepdims=True) + EPS)


def _sigmoid(x):
    return 1.0 / (1.0 + jnp.exp(-x))


def _dot(a, b):
    return jnp.dot(a, b, preferred_element_type=F32)


def _adaln_kernel(c_ref, w_ref, b_ref, o_ref):
    c = c_ref[...]
    s = (c * _sigmoid(c)).astype(BF16)
    o_ref[...] = _dot(s, w_ref[...].astype(BF16)) + b_ref[...]


def _adaln(cond8, w_ada, b_ada):
    n_blk = 4
    bw = 3 * D_MODEL // n_blk
    return pl.pallas_call(
        _adaln_kernel,
        out_shape=jax.ShapeDtypeStruct((8, 3 * D_MODEL), F32),
        grid=(n_blk,),
        in_specs=[pl.BlockSpec((8, D_MODEL), lambda j: (0, 0)),
                  pl.BlockSpec((D_MODEL, bw), lambda j: (0, j)),
                  pl.BlockSpec((1, bw), lambda j: (0, j))],
        out_specs=pl.BlockSpec((8, bw), lambda j: (0, j)),
        compiler_params=_params(("arbitrary",)),
        name="adaln",
    )(cond8, w_ada, b_ada)


def _rope(x, cos, sa, sb):
    return (x * cos + pltpu.roll(x, HEAD_PAD - 8, 1) * sa + pltpu.roll(x, 8, 1) * sb)


def _layer_in_kernel(*refs, rope, state):
    (x_ref, mod_ref, w_in_ref, qg_ref, kvg_ref, w_uq_ref, w_k_ref, w_uvt_ref, cs_ref), refs = refs[:9], refs[9:]
    if rope:
        (cos_ref, sa_ref, sb_ref), refs = refs[:3], refs[3:]
    (ucs_ref, szf_ref, sza_ref, sgf_ref, sga_ref, q_ref, k_ref, vt_ref), refs = refs[:8], refs[8:]
    if state:
        ckvn_ref, krope_ref = refs

    x = x_ref[0]
    mod = mod_ref[0]
    shift, scale = mod[:, 0:D_MODEL], mod[:, D_MODEL:2 * D_MODEL]
    h = (_rms(x) * (1.0 + scale) + shift).astype(BF16)

    def proj(lo, hi):
        return _dot(h, w_in_ref[:, lo:hi])

    u = proj(C_UF, C_ZF).astype(BF16)
    for g in range(N_GROUPS):
        r = _dot(u[:, g * GROUP:(g + 1) * GROUP], cs_ref[...])
        ucs_ref[0, :, g * GROUP:(g + 1) * GROUP] = r[:, :GROUP].astype(BF16)
        ucs_ref[0, :, D_FOURIER + g * GROUP:D_FOURIER + (g + 1) * GROUP] = r[:, GROUP:].astype(BF16)
    zf = proj(C_ZF, C_MID)
    szf_ref[0] = (zf * _sigmoid(zf)).astype(BF16)

    mid = proj(C_MID, C_ZA)
    cq, ckv, kr = mid[:, :Q_RANK], mid[:, Q_RANK:Q_RANK + KV_RANK], mid[:, Q_RANK + KV_RANK:]
    cqn = (_rms(cq) * (qg_ref[...] * SM_SCALE)).astype(BF16)
    qp = _dot(cqn, w_uq_ref[...])
    ckvn = _rms(ckv) * kvg_ref[...]
    ckvn_b = ckvn.astype(BF16)
    if state:
        ckvn_ref[0, 0] = ckvn
        krope_ref[0, 0] = kr[:, ROPE_LANE:ROPE_LANE + QK_ROPE]
    if rope:
        cos, sa, sb = cos_ref[...], sa_ref[...], sb_ref[...]
        kr = _rope(kr, cos, sa, sb)
    kp = _dot(jnp.concatenate([ckvn_b, kr.astype(BF16)], axis=1), w_k_ref[...])
    for hh in range(N_HEADS):
        qh = qp[:, hh * HEAD_PAD:(hh + 1) * HEAD_PAD]
        if rope:
            qh = _rope(qh, cos, sa, sb)
        q_ref[0, hh] = qh.astype(BF16)
        k_ref[0, hh] = kp[:, hh * HEAD_PAD:(hh + 1) * HEAD_PAD].astype(BF16)
    vt = lax.dot_general(w_uvt_ref[...], ckvn_b, (((1,), (1,)), ((), ())), preferred_element_type=F32)
    vt_ref[0] = vt.astype(BF16)

    za = proj(C_ZA, C_GF)
    sza_ref[0] = (za * _sigmoid(za)).astype(BF16)
    sgf_ref[0] = _sigmoid(proj(C_GF, C_GA)).astype(BF16)
    sga_ref[0] = _sigmoid(proj(C_GA, C_END)).astype(BF16)


def _layer_in(x, mod3, mod_row, wts, rope_tabs, *, tm, state):
    b, n, _ = x.shape
    rope = rope_tabs is not None
    tok = lambda w: pl.BlockSpec((1, tm, w), lambda bi, i: (bi, i, 0))
    in_specs = [tok(D_MODEL),
                pl.BlockSpec((1, 1, 3 * D_MODEL), lambda bi, i: (mod_row(bi), 0, 0))]
    in_specs += [_const_spec(w.shape) for w in wts]
    args = [x, mod3, *wts]
    if rope:
        in_specs += [pl.BlockSpec((tm, HEAD_PAD), lambda bi, i: (i, 0))] * 3
        args += list(rope_tabs)
    head = pl.BlockSpec((1, N_HEADS, tm, HEAD_PAD), lambda bi, i: (bi, 0, i, 0))
    out_specs = [tok(2 * D_FOURIER), tok(D_FOURIER), tok(D_ATTN), tok(D_MODEL), tok(D_MODEL), head, head,
                 pl.BlockSpec((1, D_ATTN, tm), lambda bi, i: (bi, 0, i))]
    out_shape = [jax.ShapeDtypeStruct((b, n, 2 * D_FOURIER), BF16),
                 jax.ShapeDtypeStruct((b, n, D_FOURIER), BF16),
                 jax.ShapeDtypeStruct((b, n, D_ATTN), BF16),
                 jax.ShapeDtypeStruct((b, n, D_MODEL), BF16),
                 jax.ShapeDtypeStruct((b, n, D_MODEL), BF16),
                 jax.ShapeDtypeStruct((b, N_HEADS, n, HEAD_PAD), BF16),
                 jax.ShapeDtypeStruct((b, N_HEADS, n, HEAD_PAD), BF16),
                 jax.ShapeDtypeStruct((b, D_ATTN, n), BF16)]
    if state:
        out_specs += [pl.BlockSpec((1, 1, tm, KV_RANK), lambda bi, i: (bi, 0, i, 0)),
                      pl.BlockSpec((1, 1, tm, QK_ROPE), lambda bi, i: (bi, 0, i, 0))]
        out_shape += [jax.ShapeDtypeStruct((b, 1, n, KV_RANK), F32),
                      jax.ShapeDtypeStruct((b, 1, n, QK_ROPE), F32)]
    return pl.pallas_call(
        functools.partial(_layer_in_kernel, rope=rope, state=state),
        out_shape=out_shape,
        grid=(b, n // tm),
        in_specs=in_specs,
        out_specs=out_specs,
        compiler_params=_params(("arbitrary", "arbitrary")),
        name="layer_in_latent" if rope else "layer_in_context",
    )(*args)


def _cache_kv_kernel(ckv_ref, kr_ref, w_k_ref, w_uvt_ref, k_ref, vt_ref):
    ckv_b = ckv_ref[0, 0].astype(BF16)
    kin = jnp.concatenate([ckv_b, kr_ref[0, 0].astype(BF16)], axis=1)
    kp = _dot(kin, w_k_ref[...])
    for hh in range(N_HEADS):
        k_ref[0, hh] = kp[:, hh * HEAD_PAD:(hh + 1) * HEAD_PAD].astype(BF16)
    vt = lax.dot_general(w_uvt_ref[...], ckv_b, (((1,), (1,)), ((), ())), preferred_element_type=F32)
    vt_ref[0] = vt.astype(BF16)


def _cache_kv(cache_ckv, krope_pad, w_k, w_uvt):
    b, _, m, _ = cache_ckv.shape
    return pl.pallas_call(
        _cache_kv_kernel,
        out_shape=[jax.ShapeDtypeStruct((b, N_HEADS, m, HEAD_PAD), BF16),
                   jax.ShapeDtypeStruct((b, D_ATTN, m), BF16)],
        grid=(b,),
        in_specs=[pl.BlockSpec((1, 1, m, KV_RANK), lambda bi: (bi, 0, 0, 0)),
                  pl.BlockSpec((1, 1, m, HEAD_PAD), lambda bi: (bi, 0, 0, 0)),
                  _const_spec(w_k.shape), _const_spec(w_uvt.shape)],
        out_specs=[pl.BlockSpec((1, N_HEADS, m, HEAD_PAD), lambda bi: (bi, 0, 0, 0)),
                   pl.BlockSpec((1, D_ATTN, m), lambda bi: (bi, 0, 0))],
        compiler_params=_params(("arbitrary",)),
        name="cache_kv",
    )(cache_ckv, krope_pad, w_k, w_uvt)


def _pos_dft_kernel(cn_ref, sn_ref, ucs_ref, f_ref):
    f = _dot(cn_ref[...], ucs_ref[0, :, :D_FOURIER]) + _dot(sn_ref[...], ucs_ref[0, :, D_FOURIER:])
    f_ref[0] = f.astype(BF16)


def _pos_dft(cn, snn, ucs, *, tm):
    b, n, _ = ucs.shape
    return pl.pallas_call(
        _pos_dft_kernel,
        out_shape=jax.ShapeDtypeStruct((b, n, D_FOURIER), BF16),
        grid=(n // tm, b),
        in_specs=[pl.BlockSpec((tm, n), lambda i, bi: (i, 0)),
                  pl.BlockSpec((tm, n), lambda i, bi: (i, 0)),
                  pl.BlockSpec((1, n, 2 * D_FOURIER), lambda i, bi: (bi, 0, 0))],
        out_specs=pl.BlockSpec((1, tm, D_FOURIER), lambda i, bi: (bi, i, 0)),
        compiler_params=_params(("arbitrary", "arbitrary")),
        name=f"pos_dft_{n}",
    )(cn, snn, ucs)


def _attention_kernel(*refs, n_kv):
    q_ref, refs = refs[0], refs[1:]
    k_refs, vt_refs, (o_ref, ot_ref) = refs[:n_kv], refs[n_kv:2 * n_kv], refs[2 * n_kv:]

    def head(hh, carry):
        qh = q_ref[0, hh]
        s = [lax.dot_general(k_ref[0, hh], qh, (((1,), (1,)), ((), ())), preferred_element_type=F32)
             for k_ref in k_refs]
        m = functools.reduce(jnp.maximum, [jnp.max(si, axis=0, keepdims=True) for si in s])
        p = [jnp.exp(si - m) for si in s]
        l = functools.reduce(jnp.add, [jnp.sum(pi, axis=0, keepdims=True) for pi in p])
        r0 = pl.multiple_of(hh * V_DIM, V_DIM)
        ot = functools.reduce(jnp.add, [
            _dot(vt_ref[0, pl.ds(r0, V_DIM), :], pi.astype(BF16)) for vt_ref, pi in zip(vt_refs, p)])
        ot_ref[pl.ds(r0, V_DIM), :] = ot / l
        return carry

    lax.fori_loop(0, N_HEADS, head, 0)
    o_ref[0] = ot_ref[...].T.astype(BF16)


def _attention(q, ks, vts, *, tq):
    b, _, n, _ = q.shape
    in_specs = [pl.BlockSpec((1, N_HEADS, tq, HEAD_PAD), lambda bi, i: (bi, 0, i, 0))]
    in_specs += [pl.BlockSpec((1, N_HEADS, k.shape[2], HEAD_PAD), lambda bi, i: (bi, 0, 0, 0)) for k in ks]
    in_specs += [pl.BlockSpec((1, D_ATTN, vt.shape[2]), lambda bi, i: (bi, 0, 0)) for vt in vts]
    return pl.pallas_call(
        functools.partial(_attention_kernel, n_kv=len(ks)),
        out_shape=jax.ShapeDtypeStruct((b, n, D_ATTN), BF16),
        grid=(b, n // tq),
        in_specs=in_specs,
        out_specs=pl.BlockSpec((1, tq, D_ATTN), lambda bi, i: (bi, i, 0)),
        scratch_shapes=[pltpu.VMEM((D_ATTN, tq), F32)],
        compiler_params=_params(("arbitrary", "arbitrary")),
        name=f"attention_{n}",
    )(q, *ks, *vts)


def _merge_kernel(f_ref, szf_ref, a_ref, sza_ref, sgf_ref, sga_ref, x_ref, mod_ref,
                  w_f_ref, w_a_ref, w_o_ref, g_ref, y_ref):
    y_f = _dot(f_ref[0] * szf_ref[0], w_f_ref[...])
    y_a = _dot(a_ref[0] * sza_ref[0], w_a_ref[...])
    merged = sgf_ref[0].astype(F32) * y_f + sga_ref[0].astype(F32) * y_a
    gate = mod_ref[0][:, 2 * D_MODEL:]
    out = x_ref[0] + gate * _dot(merged.astype(BF16), w_o_ref[...])
    y_ref[0] = _rms(out) * g_ref[...]


def _merge(f, szf, attn, sza, sgf, sga, x, mod3, mod_row, w_f, w_a, w_o, g, *, tm):
    b, n, _ = x.shape
    tok = lambda w: pl.BlockSpec((1, tm, w), lambda bi, i: (bi, i, 0))
    return pl.pallas_call(
        _merge_kernel,
        out_shape=jax.ShapeDtypeStruct((b, n, D_MODEL), F32),
        grid=(b, n // tm),
        in_specs=[tok(D_FOURIER), tok(D_FOURIER), tok(D_ATTN), tok(D_ATTN), tok(D_MODEL), tok(D_MODEL),
                  tok(D_MODEL),
                  pl.BlockSpec((1, 1, 3 * D_MODEL), lambda bi, i: (mod_row(bi), 0, 0)),
                  _const_spec(w_f.shape), _const_spec(w_a.shape), _const_spec(w_o.shape),
                  _const_spec(g.shape)],
        out_specs=tok(D_MODEL),
        compiler_params=_params(("arbitrary", "arbitrary")),
        name=f"merge_{n}",
    )(f, szf, attn, sza, sgf, sga, x, mod3, w_f, w_a, w_o, g)


def _dft_tables(n):
    norm = 1.0 / math.sqrt(n)
    if n <= 256:
        k = jnp.arange(n, dtype=jnp.int32)
        ang = ((k[:, None] * k[None, :]) % n).astype(F32) * (2.0 * math.pi / n)
        return (jnp.cos(ang) * norm).astype(BF16), (jnp.sin(ang) * -norm).astype(BF16)
    lo = 64
    hi = n // lo
    pos = jnp.arange(n, dtype=jnp.int32)
    a1 = ((jnp.arange(hi, dtype=jnp.int32)[:, None] * pos[None, :]) % hi).astype(F32) * (2.0 * math.pi / hi)
    a0 = ((jnp.arange(lo, dtype=jnp.int32)[:, None] * pos[None, :]) % n).astype(F32) * (2.0 * math.pi / n)
    c1, s1 = jnp.cos(a1)[:, None, :], jnp.sin(a1)[:, None, :]
    c0, s0 = jnp.cos(a0)[None, :, :] * norm, jnp.sin(a0)[None, :, :] * norm
    cn = (c1 * c0 - s1 * s0).reshape(n, n)
    sn = (s1 * c0 + c1 * s0).reshape(n, n)
    return cn.astype(BF16), (-sn).astype(BF16)


def _channel_dft_table():
    c = jnp.arange(GROUP, dtype=jnp.int32)
    ang = ((c[:, None] * c[None, :]) % GROUP).astype(F32) * (2.0 * math.pi / GROUP)
    norm = 1.0 / math.sqrt(GROUP)
    return (jnp.concatenate([jnp.cos(ang), jnp.sin(ang)], axis=1) * norm).astype(BF16)


def _rope_tables(n):
    t = jnp.arange(n)
    row = (t // GRID_W).astype(F32)
    col = (t % GRID_W).astype(F32)
    half = QK_ROPE // 2
    inv = ROPE_THETA ** (-jnp.arange(0, half, 2, dtype=F32) / half)
    ar, ac = row[:, None] * inv, col[:, None] * inv
    ang = jnp.concatenate([ar, ar, ac, ac], axis=-1)
    cos, sin = jnp.cos(ang), jnp.sin(ang)
    first = (np.arange(QK_ROPE) % half) < (half // 2)
    sa = jnp.where(first[None, :], -sin, 0.0)
    sb = jnp.where(first[None, :], 0.0, sin)
    pad = lambda a, fill: jnp.concatenate(
        [jnp.full((n, ROPE_LANE), fill, F32), a, jnp.full((n, HEAD_PAD - ROPE_LANE - QK_ROPE), fill, F32)], axis=1)
    return pad(cos, 1.0), pad(sa, 0.0), pad(sb, 0.0)


def kernel(x_prompt, x_sample, cache_ckv, cache_krope, c, c_ctx, w_ada, b_ada, w_in, q_norm_g, w_uq,
           kv_norm_g, w_ukv, w_f_out, w_a_out, w_out, final_norm_g):
    depth = w_in.shape[0]
    assert depth == 1
    n_ctx, n_lat = x_prompt.shape[1], x_sample.shape[1]
    dec_b = x_sample.shape[0]

    wl = w_in[0]
    o = np.cumsum([0, 512, 512, Q_RANK, KV_RANK, QK_ROPE, 512, 1024, 1024])
    zeros = lambda w: jnp.zeros((D_MODEL, w), F32)
    w_in_p = jnp.concatenate([
        wl[:, o[0]:o[2]],
        wl[:, o[2]:o[4]],
        zeros(ROPE_LANE), wl[:, o[4]:o[5]], zeros(HEAD_PAD - ROPE_LANE - QK_ROPE),
        wl[:, o[5]:o[8]]], axis=1).astype(BF16)
    w_uq_p = jnp.pad(w_uq[0].reshape(Q_RANK, N_HEADS, QK_NOPE + QK_ROPE),
                     ((0, 0), (0, 0), (0, HEAD_PAD - QK_NOPE - QK_ROPE))).reshape(Q_RANK, N_HEADS * HEAD_PAD)
    w_uq_p = w_uq_p.astype(BF16)
    w_kv3 = w_ukv[0].reshape(KV_RANK, N_HEADS, QK_NOPE + V_DIM)
    w_knope = jnp.pad(w_kv3[:, :, :QK_NOPE], ((0, 0), (0, 0), (0, HEAD_PAD - QK_NOPE)))
    place = np.zeros((HEAD_PAD, N_HEADS, HEAD_PAD), np.float32)
    for j in range(QK_ROPE):
        place[ROPE_LANE + j, :, ROPE_LANE + j] = 1.0
    w_k = jnp.concatenate([w_knope.reshape(KV_RANK, -1), jnp.asarray(place).reshape(HEAD_PAD, -1)],
                          axis=0).astype(BF16)
    w_uvt = w_kv3[:, :, QK_NOPE:].reshape(KV_RANK, D_ATTN).T.astype(BF16)
    w_f = w_f_out[0].astype(BF16)
    w_a = w_a_out[0].astype(BF16)
    w_o = w_out[0].astype(BF16)
    qg = q_norm_g[0].reshape(1, Q_RANK)
    kvg = kv_norm_g[0].reshape(1, KV_RANK)
    fg = final_norm_g.reshape(1, D_MODEL)
    cs = _channel_dft_table()
    wts = (w_in_p, qg, kvg, w_uq_p, w_k, w_uvt, cs)

    cond8 = jnp.concatenate([c_ctx[None, :], c, jnp.zeros((8 - 1 - dec_b, D_MODEL), F32)], axis=0)
    mod3 = _adaln(cond8, w_ada[0], b_ada[0].reshape(1, -1)).reshape(8, 1, 3 * D_MODEL)
    ctx_row = lambda bi: 0
    lat_row = lambda bi: bi + 1

    ucs, szf, sza, sgf, sga, q, k, vt, state_ckv, state_krope = _layer_in(
        x_prompt, mod3, ctx_row, wts, None, tm=n_ctx, state=True)
    f = _pos_dft(*_dft_tables(n_ctx), ucs, tm=n_ctx)
    attn = _attention(q, [k], [vt], tq=n_ctx)
    y_prompt = _merge(f, szf, attn, sza, sgf, sga, x_prompt, mod3, ctx_row, w_f, w_a, w_o, fg, tm=n_ctx)

    ucs, szf, sza, sgf, sga, q, k, vt = _layer_in(
        x_sample, mod3, lat_row, wts, _rope_tables(n_lat), tm=512, state=False)
    f = _pos_dft(*_dft_tables(n_lat), ucs, tm=1024)
    krope_pad = jnp.pad(cache_krope, ((0, 0), (0, 0), (0, 0), (ROPE_LANE, HEAD_PAD - ROPE_LANE - QK_ROPE)))
    k_c, vt_c = _cache_kv(cache_ckv, krope_pad, w_k, w_uvt)
    attn = _attention(q, [k, k_c], [vt, vt_c], tq=256)
    y_sample = _merge(f, szf, attn, sza, sgf, sga, x_sample, mod3, lat_row, w_f, w_a, w_o, fg, tm=512)

    return (y_prompt, y_sample, state_ckv, state_krope)
```

```python
import functools
import math

import jax
import jax.numpy as jnp
import numpy as np
from jax import lax
from jax.experimental import pallas as pl
from jax.experimental.pallas import tpu as pltpu

F32 = jnp.float32
BF16 = jnp.bfloat16

D_MODEL = 1024
GRID_W = 64
N_GROUPS = 4
GROUP = 128
D_FOURIER = N_GROUPS * GROUP
N_HEADS = 8
QK_NOPE = 64
QK_ROPE = 32
V_DIM = 64
Q_RANK = 256
KV_RANK = 128
D_ATTN = N_HEADS * V_DIM
ROPE_THETA = 10000.0
EPS = 1e-6
HEAD_PAD = 128
ROPE_LANE = QK_NOPE
Q_SCALE = (QK_NOPE + QK_ROPE) ** -0.5 * math.log2(math.e)
KEY_CHUNK = 256
C_UF, C_ZF, C_MID, C_ZA, C_GF, C_GA, C_END = 0, 512, 1024, 1536, 2048, 3072, 4096
VMEM_LIMIT_BYTES = 56 * 1024 * 1024


def _const_spec(shape):
    nd = len(shape)
    return pl.BlockSpec(shape, lambda *_: (0,) * nd, pipeline_mode=pl.Buffered(1))


def _params(sem):
    return pltpu.CompilerParams(dimension_semantics=sem, vmem_limit_bytes=VMEM_LIMIT_BYTES)


def _rms(x):
    return x * lax.rsqrt(jnp.mean(x * x, axis=-1, keepdims=True) + EPS)


def _sigmoid(x):
    return 1.0 / (1.0 + jnp.exp(-x))


def _dot(a, b):
    return jnp.dot(a, b, preferred_element_type=F32)


def _adaln_kernel(c_ref, w_ref, b_ref, o_ref):
    c = c_ref[...]
    s = (c * _sigmoid(c)).astype(BF16)
    o_ref[...] = _dot(s, w_ref[...].astype(BF16)) + b_ref[...]


def _adaln(cond8, w_ada, b_ada):
    n_blk = 4
    bw = 3 * D_MODEL // n_blk
    return pl.pallas_call(
        _adaln_kernel,
        out_shape=jax.ShapeDtypeStruct((8, 3 * D_MODEL), F32),
        grid=(n_blk,),
        in_specs=[pl.BlockSpec((8, D_MODEL), lambda j: (0, 0)),
                  pl.BlockSpec((D_MODEL, bw), lambda j: (0, j)),
                  pl.BlockSpec((1, bw), lambda j: (0, j))],
        out_specs=pl.BlockSpec((8, bw), lambda j: (0, j)),
        compiler_params=_params(("arbitrary",)),
        name="adaln",
    )(cond8, w_ada, b_ada)


def _rope(x, cos, sa, sb):
    return (x * cos + pltpu.roll(x, HEAD_PAD - 8, 1) * sa + pltpu.roll(x, 8, 1) * sb)


def _layer_in_kernel(*refs, rope, state):
    (x_ref, mod_ref, w_in_ref, qg_ref, kvg_ref, w_uq_ref, w_k_ref, w_uvt_ref, cs_ref), refs = refs[:9], refs[9:]
    if rope:
        (cos_ref, sa_ref, sb_ref), refs = refs[:3], refs[3:]
    (ucs_ref, szf_ref, sza_ref, sgf_ref, sga_ref, q_ref, k_ref, vt_ref), refs = refs[:8], refs[8:]
    if state:
        ckvn_ref, krope_ref = refs

    x = x_ref[0]
    mod = mod_ref[0]
    shift, scale = mod[:, 0:D_MODEL], mod[:, D_MODEL:2 * D_MODEL]
    h = (_rms(x) * (1.0 + scale) + shift).astype(BF16)

    def proj(lo, hi):
        return _dot(h, w_in_ref[:, lo:hi])

    u = proj(C_UF, C_ZF).astype(BF16)
    for g in range(N_GROUPS):
        r = _dot(u[:, g * GROUP:(g + 1) * GROUP], cs_ref[...])
        ucs_ref[0, :, g * GROUP:(g + 1) * GROUP] = r[:, :GROUP].astype(BF16)
        ucs_ref[0, :, D_FOURIER + g * GROUP:D_FOURIER + (g + 1) * GROUP] = r[:, GROUP:].astype(BF16)
    zf = proj(C_ZF, C_MID)
    szf_ref[0] = (zf * _sigmoid(zf)).astype(BF16)

    mid = proj(C_MID, C_ZA)
    cq, ckv, kr = mid[:, :Q_RANK], mid[:, Q_RANK:Q_RANK + KV_RANK], mid[:, Q_RANK + KV_RANK:]
    cqn = (_rms(cq) * (qg_ref[...] * Q_SCALE)).astype(BF16)
    qp = _dot(cqn, w_uq_ref[...])
    ckvn = _rms(ckv) * kvg_ref[...]
    ckvn_b = ckvn.astype(BF16)
    if state:
        ckvn_ref[0, 0] = ckvn
        krope_ref[0, 0] = kr[:, ROPE_LANE:ROPE_LANE + QK_ROPE]
    if rope:
        cos, sa, sb = cos_ref[...], sa_ref[...], sb_ref[...]
        kr = _rope(kr, cos, sa, sb)
    kp = _dot(jnp.concatenate([ckvn_b, kr.astype(BF16)], axis=1), w_k_ref[...])
    for hh in range(N_HEADS):
        qh = qp[:, hh * HEAD_PAD:(hh + 1) * HEAD_PAD]
        if rope:
            qh = _rope(qh, cos, sa, sb)
        q_ref[0, hh] = qh.astype(BF16)
        k_ref[0, hh] = kp[:, hh * HEAD_PAD:(hh + 1) * HEAD_PAD].astype(BF16)
    vt = lax.dot_general(w_uvt_ref[...], ckvn_b, (((1,), (1,)), ((), ())), preferred_element_type=F32)
    vt_ref[0] = vt.astype(BF16)

    za = proj(C_ZA, C_GF)
    sza_ref[0] = (za * _sigmoid(za)).astype(BF16)
    sgf_ref[0] = _sigmoid(proj(C_GF, C_GA)).astype(BF16)
    sga_ref[0] = _sigmoid(proj(C_GA, C_END)).astype(BF16)


def _layer_in(x, mod3, mod_row, wts, rope_tabs, *, tm, state):
    b, n, _ = x.shape
    rope = rope_tabs is not None
    tok = lambda w: pl.BlockSpec((1, tm, w), lambda bi, i: (bi, i, 0))
    in_specs = [tok(D_MODEL),
                pl.BlockSpec((1, 1, 3 * D_MODEL), lambda bi, i: (mod_row(bi), 0, 0))]
    in_specs += [_const_spec(w.shape) for w in wts]
    args = [x, mod3, *wts]
    if rope:
        in_specs += [pl.BlockSpec((tm, HEAD_PAD), lambda bi, i: (i, 0))] * 3
        args += list(rope_tabs)
    head = pl.BlockSpec((1, N_HEADS, tm, HEAD_PAD), lambda bi, i: (bi, 0, i, 0))
    out_specs = [tok(2 * D_FOURIER), tok(D_FOURIER), tok(D_ATTN), tok(D_MODEL), tok(D_MODEL), head, head,
                 pl.BlockSpec((1, D_ATTN, tm), lambda bi, i: (bi, 0, i))]
    out_shape = [jax.ShapeDtypeStruct((b, n, 2 * D_FOURIER), BF16),
                 jax.ShapeDtypeStruct((b, n, D_FOURIER), BF16),
                 jax.ShapeDtypeStruct((b, n, D_ATTN), BF16),
                 jax.ShapeDtypeStruct((b, n, D_MODEL), BF16),
                 jax.ShapeDtypeStruct((b, n, D_MODEL), BF16),
                 jax.ShapeDtypeStruct((b, N_HEADS, n, HEAD_PAD), BF16),
                 jax.ShapeDtypeStruct((b, N_HEADS, n, HEAD_PAD), BF16),
                 jax.ShapeDtypeStruct((b, D_ATTN, n), BF16)]
    if state:
        out_specs += [pl.BlockSpec((1, 1, tm, KV_RANK), lambda bi, i: (bi, 0, i, 0)),
                      pl.BlockSpec((1, 1, tm, QK_ROPE), lambda bi, i: (bi, 0, i, 0))]
        out_shape += [jax.ShapeDtypeStruct((b, 1, n, KV_RANK), F32),
                      jax.ShapeDtypeStruct((b, 1, n, QK_ROPE), F32)]
    return pl.pallas_call(
        functools.partial(_layer_in_kernel, rope=rope, state=state),
        out_shape=out_shape,
        grid=(b, n // tm),
        in_specs=in_specs,
        out_specs=out_specs,
        compiler_params=_params(("arbitrary", "arbitrary")),
        name="layer_in_latent" if rope else "layer_in_context",
    )(*args)


def _cache_kv_kernel(ckv_ref, kr_ref, w_k_ref, w_uvt_ref, k_ref, vt_ref):
    ckv_b = ckv_ref[0, 0].astype(BF16)
    kin = jnp.concatenate([ckv_b, kr_ref[0, 0].astype(BF16)], axis=1)
    kp = _dot(kin, w_k_ref[...])
    for hh in range(N_HEADS):
        k_ref[0, hh] = kp[:, hh * HEAD_PAD:(hh + 1) * HEAD_PAD].astype(BF16)
    vt = lax.dot_general(w_uvt_ref[...], ckv_b, (((1,), (1,)), ((), ())), preferred_element_type=F32)
    vt_ref[0] = vt.astype(BF16)


def _cache_kv(cache_ckv, krope_pad, w_k, w_uvt):
    b, _, m, _ = cache_ckv.shape
    return pl.pallas_call(
        _cache_kv_kernel,
        out_shape=[jax.ShapeDtypeStruct((b, N_HEADS, m, HEAD_PAD), BF16),
                   jax.ShapeDtypeStruct((b, D_ATTN, m), BF16)],
        grid=(b,),
        in_specs=[pl.BlockSpec((1, 1, m, KV_RANK), lambda bi: (bi, 0, 0, 0)),
                  pl.BlockSpec((1, 1, m, HEAD_PAD), lambda bi: (bi, 0, 0, 0)),
                  _const_spec(w_k.shape), _const_spec(w_uvt.shape)],
        out_specs=[pl.BlockSpec((1, N_HEADS, m, HEAD_PAD), lambda bi: (bi, 0, 0, 0)),
                   pl.BlockSpec((1, D_ATTN, m), lambda bi: (bi, 0, 0))],
        compiler_params=_params(("arbitrary",)),
        name="cache_kv",
    )(cache_ckv, krope_pad, w_k, w_uvt)


def _pos_dft_kernel(cn_ref, sn_ref, ucs_ref, f_ref):
    f = _dot(cn_ref[...], ucs_ref[0, :, :D_FOURIER]) + _dot(sn_ref[...], ucs_ref[0, :, D_FOURIER:])
    f_ref[0] = f.astype(BF16)


def _pos_dft(cn, snn, ucs, *, tm):
    b, n, _ = ucs.shape
    return pl.pallas_call(
        _pos_dft_kernel,
        out_shape=jax.ShapeDtypeStruct((b, n, D_FOURIER), BF16),
        grid=(n // tm, b),
        in_specs=[pl.BlockSpec((tm, n), lambda i, bi: (i, 0)),
                  pl.BlockSpec((tm, n), lambda i, bi: (i, 0)),
                  pl.BlockSpec((1, n, 2 * D_FOURIER), lambda i, bi: (bi, 0, 0))],
        out_specs=pl.BlockSpec((1, tm, D_FOURIER), lambda i, bi: (bi, i, 0)),
        compiler_params=_params(("arbitrary", "arbitrary")),
        name=f"pos_dft_{n}",
    )(cn, snn, ucs)


def _attention_kernel(*refs, n_kv):
    q_ref, refs = refs[0], refs[1:]
    k_refs, vt_refs, (o_ref, s_ref, p_ref, m_ref, ot_ref) = refs[:n_kv], refs[n_kv:2 * n_kv], refs[2 * n_kv:]
    tq = q_ref.shape[2]
    n_slots = s_ref.shape[0]
    chunks, base = [], 0
    for j, k_ref in enumerate(k_refs):
        chunks += [(j, c, base + c) for c in range(0, k_ref.shape[2], KEY_CHUNK)]
        base += k_ref.shape[2]

    def fold(a, op):
        return op(a.reshape(KEY_CHUNK // 8, 8, tq), axis=0)

    def scores(hh, slot):
        qh = q_ref[0, hh]
        m8 = None
        for j, c, r in chunks:
            s = lax.dot_general(k_refs[j][0, hh, c:c + KEY_CHUNK, :], qh, (((1,), (1,)), ((), ())),
                                preferred_element_type=F32)
            s_ref[slot, r:r + KEY_CHUNK, :] = s
            mc = fold(s, jnp.max)
            m8 = mc if m8 is None else jnp.maximum(m8, mc)
        m_ref[slot] = m8

    def softmax_pv(hh, slot):
        m = jnp.max(m_ref[slot], axis=0, keepdims=True)
        l8 = None
        for _, _, r in chunks:
            p = jnp.exp2(s_ref[slot, r:r + KEY_CHUNK, :] - m)
            p_ref[slot, r:r + KEY_CHUNK, :] = p.astype(BF16)
            lc = fold(p, jnp.sum)
            l8 = lc if l8 is None else l8 + lc
        l = jnp.sum(l8, axis=0, keepdims=True)
        rows = pl.ds(pl.multiple_of(hh * V_DIM, V_DIM), V_DIM)
        ot, base = None, 0
        for vt_ref in vt_refs:
            mk = vt_ref.shape[2]
            part = _dot(vt_ref[0, rows, :], p_ref[slot, base:base + mk, :])
            ot = part if ot is None else ot + part
            base += mk
        ot_ref[rows, :] = ot / l

    if n_slots == N_HEADS:
        for hh in range(N_HEADS):
            scores(hh, hh)
        for hh in range(N_HEADS):
            softmax_pv(hh, hh)
    else:
        scores(0, 0)

        def body(i, carry):
            scores(2 * i + 1, 1)
            softmax_pv(2 * i, 0)
            scores(2 * i + 2, 0)
            softmax_pv(2 * i + 1, 1)
            return carry

        lax.fori_loop(0, N_HEADS // 2 - 1, body, 0)
        scores(N_HEADS - 1, 1)
        softmax_pv(N_HEADS - 2, 0)
        softmax_pv(N_HEADS - 1, 1)
    o_ref[0] = ot_ref[...].T.astype(BF16)


def _attention(q, ks, vts, *, tq):
    b, _, n, _ = q.shape
    m_tot = sum(k.shape[2] for k in ks)
    n_slots = N_HEADS if m_tot <= 2 * KEY_CHUNK else 2
    in_specs =[pl.BlockSpec((1, N_HEADS, tq, HEAD_PAD), lambda bi, i: (bi, 0, i, 0))]
    in_specs += [pl.BlockSpec((1, N_HEADS, k.shape[2], HEAD_PAD), lambda bi, i: (bi, 0, 0, 0)) for k in ks]
    in_specs += [pl.BlockSpec((1, D_ATTN, vt.shape[2]), lambda bi, i: (bi, 0, 0)) for vt in vts]
    return pl.pallas_call(
        functools.partial(_attention_kernel, n_kv=len(ks)),
        out_shape=jax.ShapeDtypeStruct((b, n, D_ATTN), BF16),
        grid=(b, n // tq),
        in_specs=in_specs,
        out_specs=pl.BlockSpec((1, tq, D_ATTN), lambda bi, i: (bi, i, 0)),
        scratch_shapes=[pltpu.VMEM((n_slots, m_tot, tq), F32), pltpu.VMEM((n_slots, m_tot, tq), BF16),
                        pltpu.VMEM((n_slots, 8, tq), F32), pltpu.VMEM((D_ATTN, tq), F32)],
        compiler_params=_params(("arbitrary", "arbitrary")),
        name=f"attention_{n}",
    )(q, *ks, *vts)


def _merge_kernel(f_ref, szf_ref, a_ref, sza_ref, sgf_ref, sga_ref, x_ref, mod_ref,
                  w_f_ref, w_a_ref, w_o_ref, g_ref, y_ref):
    y_f = _dot(f_ref[0] * szf_ref[0], w_f_ref[...])
    y_a = _dot(a_ref[0] * sza_ref[0], w_a_ref[...])
    merged = sgf_ref[0].astype(F32) * y_f + sga_ref[0].astype(F32) * y_a
    gate = mod_ref[0][:, 2 * D_MODEL:]
    out = x_ref[0] + gate * _dot(merged.astype(BF16), w_o_ref[...])
    y_ref[0] = _rms(out) * g_ref[...]


def _merge(f, szf, attn, sza, sgf, sga, x, mod3, mod_row, w_f, w_a, w_o, g, *, tm):
    b, n, _ = x.shape
    tok = lambda w: pl.BlockSpec((1, tm, w), lambda bi, i: (bi, i, 0))
    return pl.pallas_call(
        _merge_kernel,
        out_shape=jax.ShapeDtypeStruct((b, n, D_MODEL), F32),
        grid=(b, n // tm),
        in_specs=[tok(D_FOURIER), tok(D_FOURIER), tok(D_ATTN), tok(D_ATTN), tok(D_MODEL), tok(D_MODEL),
                  tok(D_MODEL),
                  pl.BlockSpec((1, 1, 3 * D_MODEL), lambda bi, i: (mod_row(bi), 0, 0)),
                  _const_spec(w_f.shape), _const_spec(w_a.shape), _const_spec(w_o.shape),
                  _const_spec(g.shape)],
        out_specs=tok(D_MODEL),
        compiler_params=_params(("arbitrary", "arbitrary")),
        name=f"merge_{n}",
    )(f, szf, attn, sza, sgf, sga, x, mod3, w_f, w_a, w_o, g)


def _dft_tables(n):
    norm = 1.0 / math.sqrt(n)
    if n <= 256:
        k = jnp.arange(n, dtype=jnp.int32)
        ang = ((k[:, None] * k[None, :]) % n).astype(F32) * (2.0 * math.pi / n)
        return (jnp.cos(ang) * norm).astype(BF16), (jnp.sin(ang) * -norm).astype(BF16)
    lo = 64
    hi = n // lo
    pos = jnp.arange(n, dtype=jnp.int32)
    a1 = ((jnp.arange(hi, dtype=jnp.int32)[:, None] * pos[None, :]) % hi).astype(F32) * (2.0 * math.pi / hi)
    a0 = ((jnp.arange(lo, dtype=jnp.int32)[:, None] * pos[None, :]) % n).astype(F32) * (2.0 * math.pi / n)
    c1, s1 = jnp.cos(a1)[:, None, :], jnp.sin(a1)[:, None, :]
    c0, s0 = jnp.cos(a0)[None, :, :] * norm, jnp.sin(a0)[None, :, :] * norm
    cn = (c1 * c0 - s1 * s0).reshape(n, n)
    sn = (s1 * c0 + c1 * s0).reshape(n, n)
    return cn.astype(BF16), (-sn).astype(BF16)


def _channel_dft_table():
    c = jnp.arange(GROUP, dtype=jnp.int32)
    ang = ((c[:, None] * c[None, :]) % GROUP).astype(F32) * (2.0 * math.pi / GROUP)
    norm = 1.0 / math.sqrt(GROUP)
    return (jnp.concatenate([jnp.cos(ang), jnp.sin(ang)], axis=1) * norm).astype(BF16)


def _rope_tables(n):
    t = jnp.arange(n)
    row = (t // GRID_W).astype(F32)
    col = (t % GRID_W).astype(F32)
    half = QK_ROPE // 2
    inv = ROPE_THETA ** (-jnp.arange(0, half, 2, dtype=F32) / half)
    ar, ac = row[:, None] * inv, col[:, None] * inv
    ang = jnp.concatenate([ar, ar, ac, ac], axis=-1)
    cos, sin = jnp.cos(ang), jnp.sin(ang)
    first = (np.arange(QK_ROPE) % half) < (half // 2)
    sa = jnp.where(first[None, :], -sin, 0.0)
    sb = jnp.where(first[None, :], 0.0, sin)
    pad = lambda a, fill: jnp.concatenate(
        [jnp.full((n, ROPE_LANE), fill, F32), a, jnp.full((n, HEAD_PAD - ROPE_LANE - QK_ROPE), fill, F32)], axis=1)
    return pad(cos, 1.0), pad(sa, 0.0), pad(sb, 0.0)


def kernel(x_prompt, x_sample, cache_ckv, cache_krope, c, c_ctx, w_ada, b_ada, w_in, q_norm_g, w_uq,
           kv_norm_g, w_ukv, w_f_out, w_a_out, w_out, final_norm_g):
    depth = w_in.shape[0]
    assert depth == 1
    n_ctx, n_lat = x_prompt.shape[1], x_sample.shape[1]
    dec_b = x_sample.shape[0]

    wl = w_in[0]
    o = np.cumsum([0, 512, 512, Q_RANK, KV_RANK, QK_ROPE, 512, 1024, 1024])
    zeros = lambda w: jnp.zeros((D_MODEL, w), F32)
    w_in_p = jnp.concatenate([
        wl[:, o[0]:o[2]],
        wl[:, o[2]:o[4]],
        zeros(ROPE_LANE), wl[:, o[4]:o[5]], zeros(HEAD_PAD - ROPE_LANE - QK_ROPE),
        wl[:, o[5]:o[8]]], axis=1).astype(BF16)
    w_uq_p = jnp.pad(w_uq[0].reshape(Q_RANK, N_HEADS, QK_NOPE + QK_ROPE),
                     ((0, 0), (0, 0), (0, HEAD_PAD - QK_NOPE - QK_ROPE))).reshape(Q_RANK, N_HEADS * HEAD_PAD)
    w_uq_p = w_uq_p.astype(BF16)
    w_kv3 = w_ukv[0].reshape(KV_RANK, N_HEADS, QK_NOPE + V_DIM)
    w_knope = jnp.pad(w_kv3[:, :, :QK_NOPE], ((0, 0), (0, 0), (0, HEAD_PAD - QK_NOPE)))
    place = np.zeros((HEAD_PAD, N_HEADS, HEAD_PAD), np.float32)
    for j in range(QK_ROPE):
        place[ROPE_LANE + j, :, ROPE_LANE + j] = 1.0
    w_k = jnp.concatenate([w_knope.reshape(KV_RANK, -1), jnp.asarray(place).reshape(HEAD_PAD, -1)],
                          axis=0).astype(BF16)
    w_uvt = w_kv3[:, :, QK_NOPE:].reshape(KV_RANK, D_ATTN).T.astype(BF16)
    w_f = w_f_out[0].astype(BF16)
    w_a = w_a_out[0].astype(BF16)
    w_o = w_out[0].astype(BF16)
    qg = q_norm_g[0].reshape(1, Q_RANK)
    kvg = kv_norm_g[0].reshape(1, KV_RANK)
    fg = final_norm_g.reshape(1, D_MODEL)
    cs = _channel_dft_table()
    wts = (w_in_p, qg, kvg, w_uq_p, w_k, w_uvt, cs)

    cond8 = jnp.concatenate([c_ctx[None, :], c, jnp.zeros((8 - 1 - dec_b, D_MODEL), F32)], axis=0)
    mod3 = _adaln(cond8, w_ada[0], b_ada[0].reshape(1, -1)).reshape(8, 1, 3 * D_MODEL)
    ctx_row = lambda bi: 0
    lat_row = lambda bi: bi + 1

    ucs, szf, sza, sgf, sga, q, k, vt, state_ckv, state_krope = _layer_in(
        x_prompt, mod3, ctx_row, wts, None, tm=n_ctx, state=True)
    f = _pos_dft(*_dft_tables(n_ctx), ucs, tm=n_ctx)
    attn = _attention(q, [k], [vt], tq=n_ctx)
    y_prompt = _merge(f, szf, attn, sza, sgf, sga, x_prompt, mod3, ctx_row, w_f, w_a, w_o, fg, tm=n_ctx)

    ucs, szf, sza, sgf, sga, q, k, vt = _layer_in(
        x_sample, mod3, lat_row, wts, _rope_tables(n_lat), tm=512, state=False)
    f = _pos_dft(*_dft_tables(n_lat), ucs, tm=1024)
    krope_pad = jnp.pad(cache_krope, ((0, 0), (0, 0), (0, 0), (ROPE_LANE, HEAD_PAD - ROPE_LANE - QK_ROPE)))
    k_c, vt_c = _cache_kv(cache_ckv, krope_pad, w_k, w_uvt)
    attn = _attention(q, [k, k_c], [vt, vt_c], tq=256)
    y_sample = _merge(f, szf, attn, sza, sgf, sga, x_sample, mod3, lat_row, w_f, w_a, w_o, fg, tm=512)

    return (y_prompt, y_sample, state_ckv, state_krope)
```

```python
import functools
import math

import jax
import jax.numpy as jnp
import numpy as np
from jax import lax
from jax.experimental import pallas as pl
from jax.experimental.pallas import tpu as pltpu

F32 = jnp.float32
BF16 = jnp.bfloat16

D_MODEL = 1024
GRID_W = 64
N_GROUPS = 4
GROUP = 128
D_FOURIER = N_GROUPS * GROUP
N_HEADS = 8
QK_NOPE = 64
QK_ROPE = 32
V_DIM = 64
Q_RANK = 256
KV_RANK = 128
D_ATTN = N_HEADS * V_DIM
D_IN = 2 * D_FOURIER + Q_RANK + KV_RANK + QK_ROPE + D_ATTN + 2 * D_MODEL
ROPE_THETA = 10000.0
EPS = 1e-6
LANES = 128
HEAD_PAD = LANES
ROPE_LANE = QK_NOPE
V_ROWS = V_DIM + 16
VT_ROWS = N_HEADS * V_ROWS
Q_SCALE = (QK_NOPE + QK_ROPE) ** -0.5 * math.log2(math.e)
KEY_CHUNK = 256
W_BLK = 512
B_UF, B_ZF, B_MID, B_ZA, B_G = 0, 1, 2, 3, 2
D_IN_PAD = 8 * W_BLK
TOKEN_TILE = 512
VMEM_LIMIT_BYTES = 56 * 1024 * 1024


def _const_spec(shape):
    nd = len(shape)
    return pl.BlockSpec(shape, lambda *_: (0,) * nd, pipeline_mode=pl.Buffered(1))


def _col_spec(rows, width, blk):
    return pl.BlockSpec((rows, width), lambda *_: (0, blk), pipeline_mode=pl.Buffered(1))


def _params(n_axes):
    return pltpu.CompilerParams(dimension_semantics=("arbitrary",) * n_axes,
                                vmem_limit_bytes=VMEM_LIMIT_BYTES)


def _rms(x):
    return x * lax.rsqrt(jnp.mean(x * x, axis=-1, keepdims=True) + EPS)


def _sigmoid(x):
    return 1.0 / (1.0 + jnp.exp(-x))


def _dot(a, b):
    return jnp.dot(a, b, preferred_element_type=F32)


def _dot_nt(a, b):
    return lax.dot_general(a, b, (((1,), (1,)), ((), ())), preferred_element_type=F32)


def _token_tiling(t, n, *, per_sequence):
    tm = TOKEN_TILE
    assert t % tm == 0 and (not per_sequence or n % tm == 0)
    per_seq = n // tm if per_sequence else None
    mod_row = (lambda i: 1 + i // per_seq) if per_sequence else (lambda i: 0)
    return tm, per_seq, mod_row


def _modulated(x_ref, mod_ref):
    mod = mod_ref[0]
    shift, scale = mod[:, 0:D_MODEL], mod[:, D_MODEL:2 * D_MODEL]
    return (_rms(x_ref[...]) * (1.0 + scale) + shift).astype(BF16)


def _adaln_kernel(c_ref, w_ref, b_ref, o_ref):
    c = c_ref[...]
    s = (c * _sigmoid(c)).astype(BF16)
    o_ref[...] = _dot(s, w_ref[...].astype(BF16)) + b_ref[...]


def _adaln(cond8, w_ada, b_ada):
    n_blk = 4
    bw = 3 * D_MODEL // n_blk
    return pl.pallas_call(
        _adaln_kernel,
        out_shape=jax.ShapeDtypeStruct((8, 3 * D_MODEL), F32),
        grid=(n_blk,),
        in_specs=[pl.BlockSpec((8, D_MODEL), lambda j: (0, 0)),
                  pl.BlockSpec((D_MODEL, bw), lambda j: (0, j)),
                  pl.BlockSpec((1, bw), lambda j: (0, j))],
        out_specs=pl.BlockSpec((8, bw), lambda j: (0, j)),
        compiler_params=_params(1),
        name="adaln",
    )(cond8, w_ada, b_ada)


def _pack_w_in_kernel(w_ref, o_ref):
    rows = w_ref.shape[0]
    lo = 2 * D_FOURIER + Q_RANK + KV_RANK
    o_ref[:, :lo] = w_ref[:, :lo].astype(BF16)
    blk = pltpu.roll(w_ref[:, lo:lo + LANES], ROPE_LANE, 1)
    lane = lax.broadcasted_iota(jnp.int32, (rows, LANES), 1)
    keep = (lane >= ROPE_LANE) & (lane < ROPE_LANE + QK_ROPE)
    o_ref[:, lo:lo + LANES] = jnp.where(keep, blk, 0.0).astype(BF16)
    o_ref[:, lo + LANES:] = w_ref[:, lo + QK_ROPE:].astype(BF16)


def _pack_w_in(w_in):
    rows = 128
    return pl.pallas_call(
        _pack_w_in_kernel,
        out_shape=jax.ShapeDtypeStruct((D_MODEL, D_IN_PAD), BF16),
        grid=(D_MODEL // rows,),
        in_specs=[pl.BlockSpec((rows, D_IN), lambda i: (i, 0))],
        out_specs=pl.BlockSpec((rows, D_IN_PAD), lambda i: (i, 0)),
        compiler_params=_params(1),
        name="pack_w_in",
    )(w_in)


def _rope(x, cos, sa, sb):
    return (x * cos + pltpu.roll(x, HEAD_PAD - 8, 1) * sa + pltpu.roll(x, 8, 1) * sb)


def _kv_outputs(ckvn_b, kr_b, w_k_ref, w_uvt_ref, ones_ref, k_ref, vt_ref):
    tm = ckvn_b.shape[0]
    kp = _dot(jnp.concatenate([ckvn_b, kr_b], axis=1), w_k_ref[...])
    for hh in range(N_HEADS):
        k_ref[hh] = kp[:, hh * HEAD_PAD:(hh + 1) * HEAD_PAD].astype(BF16)
    ones = jnp.concatenate([ones_ref[...]] * (tm // LANES), axis=1)
    vt_ref[...] = (_dot_nt(w_uvt_ref[...], ckvn_b) + ones).astype(BF16)


def _layer_in_kernel(*refs, rope, state):
    (x_ref, mod_ref, w_uf_ref, w_mid_ref, qg_ref, kvg_ref, w_uq_ref, w_k_ref, w_uvt_ref, ones_ref,
     cs_ref), refs = refs[:11], refs[11:]
    if rope:
        (cos_ref, sa_ref, sb_ref), refs = refs[:3], refs[3:]
    (ucs_ref, q_ref, k_ref, vt_ref), refs = refs[:4], refs[4:]
    if state:
        ckvn_ref, krope_ref = refs

    h = _modulated(x_ref, mod_ref)

    u = _dot(h, w_uf_ref[...]).astype(BF16)
    for g in range(N_GROUPS):
        r = _dot(u[:, g * GROUP:(g + 1) * GROUP], cs_ref[...])
        ucs_ref[:, g * GROUP:(g + 1) * GROUP] = r[:, :GROUP].astype(BF16)
        ucs_ref[:, D_FOURIER + g * GROUP:D_FOURIER + (g + 1) * GROUP] = r[:, GROUP:].astype(BF16)

    mid = _dot(h, w_mid_ref[...])
    cq, ckv, kr = mid[:, :Q_RANK], mid[:, Q_RANK:Q_RANK + KV_RANK], mid[:, Q_RANK + KV_RANK:]
    cqn = (_rms(cq) * (qg_ref[...] * Q_SCALE)).astype(BF16)
    qp = _dot(cqn, w_uq_ref[...])
    ckvn = _rms(ckv) * kvg_ref[...]
    if state:
        ckvn_ref[...] = ckvn
        krope_ref[...] = kr[:, ROPE_LANE:ROPE_LANE + QK_ROPE]
    if rope:
        cos, sa, sb = cos_ref[...], sa_ref[...], sb_ref[...]
        kr = _rope(kr, cos, sa, sb)
    for hh in range(N_HEADS):
        qh = qp[:, hh * HEAD_PAD:(hh + 1) * HEAD_PAD]
        if rope:
            qh = _rope(qh, cos, sa, sb)
        q_ref[hh] = qh.astype(BF16)
    _kv_outputs(ckvn.astype(BF16), kr.astype(BF16), w_k_ref, w_uvt_ref, ones_ref, k_ref, vt_ref)


def _layer_in(x, mod3, w_in_p, wts, rope_tabs, *, n, state):
    t = x.shape[0]
    rope = rope_tabs is not None
    tm, per_seq, mod_row = _token_tiling(t, n, per_sequence=rope)
    tok = lambda w: pl.BlockSpec((tm, w), lambda i: (i, 0))
    in_specs = [tok(D_MODEL),
                pl.BlockSpec((1, 1, 3 * D_MODEL), lambda i: (mod_row(i), 0, 0)),
                _col_spec(D_MODEL, W_BLK, B_UF), _col_spec(D_MODEL, W_BLK, B_MID)]
    in_specs += [_const_spec(w.shape) for w in wts]
    args = [x, mod3, w_in_p, w_in_p, *wts]
    if rope:
        in_specs += [pl.BlockSpec((tm, HEAD_PAD), lambda i: (i % per_seq, 0))] * 3
        args += list(rope_tabs)
    head = pl.BlockSpec((N_HEADS, tm, HEAD_PAD), lambda i: (0, i, 0))
    out_specs = [tok(2 * D_FOURIER), head, head, pl.BlockSpec((VT_ROWS, tm), lambda i: (0, i))]
    out_shape = [jax.ShapeDtypeStruct((t, 2 * D_FOURIER), BF16),
                 jax.ShapeDtypeStruct((N_HEADS, t, HEAD_PAD), BF16),
                 jax.ShapeDtypeStruct((N_HEADS, t, HEAD_PAD), BF16),
                 jax.ShapeDtypeStruct((VT_ROWS, t), BF16)]
    if state:
        out_specs += [tok(KV_RANK), tok(QK_ROPE)]
        out_shape += [jax.ShapeDtypeStruct((t, KV_RANK), F32), jax.ShapeDtypeStruct((t, QK_ROPE), F32)]
    return pl.pallas_call(
        functools.partial(_layer_in_kernel, rope=rope, state=state),
        out_shape=out_shape,
        grid=(t // tm,),
        in_specs=in_specs,
        out_specs=out_specs,
        compiler_params=_params(1),
        name="layer_in_latent" if rope else "layer_in_context",
    )(*args)


def _cache_kv_kernel(ckv_ref, kr_ref, w_k_ref, w_uvt_ref, ones_ref, k_ref, vt_ref):
    _kv_outputs(ckv_ref[...].astype(BF16), kr_ref[...].astype(BF16), w_k_ref, w_uvt_ref, ones_ref,
                k_ref, vt_ref)


def _cache_kv(ckv, krope_pad, w_k, w_uvt, ones_col, *, m):
    t = ckv.shape[0]
    return pl.pallas_call(
        _cache_kv_kernel,
        out_shape=[jax.ShapeDtypeStruct((N_HEADS, t, HEAD_PAD), BF16),
                   jax.ShapeDtypeStruct((VT_ROWS, t), BF16)],
        grid=(t // m,),
        in_specs=[pl.BlockSpec((m, KV_RANK), lambda i: (i, 0)),
                  pl.BlockSpec((m, HEAD_PAD), lambda i: (i, 0)),
                  _const_spec(w_k.shape), _const_spec(w_uvt.shape), _const_spec(ones_col.shape)],
        out_specs=[pl.BlockSpec((N_HEADS, m, HEAD_PAD), lambda i: (0, i, 0)),
                   pl.BlockSpec((VT_ROWS, m), lambda i: (0, i))],
        compiler_params=_params(1),
        name="cache_kv",
    )(ckv, krope_pad, w_k, w_uvt, ones_col)


def _pos_dft_kernel(cn_ref, sn_ref, ucs_ref, f_ref):
    f = _dot(cn_ref[...], ucs_ref[:, :D_FOURIER]) + _dot(sn_ref[...], ucs_ref[:, D_FOURIER:])
    f_ref[...] = f.astype(BF16)


def _pos_dft(cn, snn, ucs, *, n, tm):
    t = ucs.shape[0]
    per_seq = n // tm
    return pl.pallas_call(
        _pos_dft_kernel,
        out_shape=jax.ShapeDtypeStruct((t, D_FOURIER), BF16),
        grid=(per_seq, t // n),
        in_specs=[pl.BlockSpec((tm, n), lambda i, bi: (i, 0)),
                  pl.BlockSpec((tm, n), lambda i, bi: (i, 0)),
                  pl.BlockSpec((n, 2 * D_FOURIER), lambda i, bi: (bi, 0))],
        out_specs=pl.BlockSpec((tm, D_FOURIER), lambda i, bi: (bi * per_seq + i, 0)),
        compiler_params=_params(2),
        name=f"pos_dft_{n}",
    )(cn, snn, ucs)


def _attention_kernel(*refs, n_kv):
    q_ref, refs = refs[0], refs[1:]
    k_refs, vt_refs, (o_ref, s_ref, p_ref, m_ref, ot_ref) = refs[:n_kv], refs[n_kv:2 * n_kv], refs[2 * n_kv:]
    tq = q_ref.shape[1]
    n_slots = s_ref.shape[0]
    chunks, base = [], 0
    for j, k_ref in enumerate(k_refs):
        chunks += [(j, c, base + c) for c in range(0, k_ref.shape[1], KEY_CHUNK)]
        base += k_ref.shape[1]

    def scores(hh, slot):
        qh = q_ref[hh]
        m8 = None
        for j, c, r in chunks:
            s = _dot_nt(k_refs[j][hh, c:c + KEY_CHUNK, :], qh)
            s_ref[slot, r:r + KEY_CHUNK, :] = s
            mc = jnp.max(s.reshape(KEY_CHUNK // 8, 8, tq), axis=0)
            m8 = mc if m8 is None else jnp.maximum(m8, mc)
        m_ref[slot] = m8

    def softmax_pv(hh, slot):
        m = jnp.max(m_ref[slot], axis=0, keepdims=True)
        for _, _, r in chunks:
            p_ref[slot, r:r + KEY_CHUNK, :] = jnp.exp2(s_ref[slot, r:r + KEY_CHUNK, :] - m).astype(BF16)
        ot, base = None, 0
        for vt_ref in vt_refs:
            mk = vt_ref.shape[1]
            part = _dot(vt_ref[pl.ds(pl.multiple_of(hh * V_ROWS, V_ROWS), V_ROWS), :],
                        p_ref[slot, base:base + mk, :])
            ot = part if ot is None else ot + part
            base += mk
        ot_ref[pl.ds(pl.multiple_of(hh * V_DIM, V_DIM), V_DIM), :] = ot[:V_DIM] / ot[V_DIM:V_DIM + 1]

    if n_slots == N_HEADS:
        for hh in range(N_HEADS):
            scores(hh, hh)
        for hh in range(N_HEADS):
            softmax_pv(hh, hh)
    else:
        scores(0, 0)
        for hh in range(N_HEADS):
            if hh + 1 < N_HEADS:
                scores(hh + 1, (hh + 1) % 2)
            softmax_pv(hh, hh % 2)
    o_ref[...] = ot_ref[...].T.astype(BF16)


def _attention(q, ks, vts, key_lens, *, n, tq):
    t = q.shape[1]
    per_seq = n // tq
    m_tot = sum(key_lens)
    n_slots = N_HEADS if m_tot <= 2 * KEY_CHUNK else 2
    in_specs = [pl.BlockSpec((N_HEADS, tq, HEAD_PAD), lambda bi, i: (0, bi * per_seq + i, 0))]
    in_specs += [pl.BlockSpec((N_HEADS, m, HEAD_PAD), lambda bi, i: (0, bi, 0)) for m in key_lens]
    in_specs += [pl.BlockSpec((VT_ROWS, m), lambda bi, i: (0, bi)) for m in key_lens]
    return pl.pallas_call(
        functools.partial(_attention_kernel, n_kv=len(ks)),
        out_shape=jax.ShapeDtypeStruct((t, D_ATTN), BF16),
        grid=(t // n, per_seq),
        in_specs=in_specs,
        out_specs=pl.BlockSpec((tq, D_ATTN), lambda bi, i: (bi * per_seq + i, 0)),
        scratch_shapes=[pltpu.VMEM((n_slots, m_tot, tq), F32), pltpu.VMEM((n_slots, m_tot, tq), BF16),
                        pltpu.VMEM((n_slots, 8, tq), F32), pltpu.VMEM((D_ATTN, tq), F32)],
        compiler_params=_params(2),
        name=f"attention_{n}",
    )(q, *ks, *vts)


def _merge_kernel(x_ref, mod_ref, f_ref, a_ref, w_zf_ref, w_za_ref, w_g_ref, w_f_ref, w_a_ref, w_o_ref,
                  g_ref, y_ref):
    h = _modulated(x_ref, mod_ref)
    zf = _dot(h, w_zf_ref[...])
    y_f = _dot((f_ref[...] * (zf * _sigmoid(zf))).astype(BF16), w_f_ref[...])
    za = _dot(h, w_za_ref[...])
    y_a = _dot((a_ref[...] * (za * _sigmoid(za))).astype(BF16), w_a_ref[...])
    merged = (_sigmoid(_dot(h, w_g_ref[:, :D_MODEL])) * y_f
              + _sigmoid(_dot(h, w_g_ref[:, D_MODEL:])) * y_a)
    gate = mod_ref[0][:, 2 * D_MODEL:]
    out = x_ref[...] + gate * _dot(merged.astype(BF16), w_o_ref[...])
    y_ref[...] = _rms(out) * g_ref[...]


def _merge(x, mod3, f, attn, w_in_p, w_f, w_a, w_o, g, *, n, latent):
    t = x.shape[0]
    tm, _, mod_row = _token_tiling(t, n, per_sequence=latent)
    tok = lambda w: pl.BlockSpec((tm, w), lambda i: (i, 0))
    return pl.pallas_call(
        _merge_kernel,
        out_shape=jax.ShapeDtypeStruct((t, D_MODEL), F32),
        grid=(t // tm,),
        in_specs=[tok(D_MODEL),
                  pl.BlockSpec((1, 1, 3 * D_MODEL), lambda i: (mod_row(i), 0, 0)),
                  tok(D_FOURIER), tok(D_ATTN),
                  _col_spec(D_MODEL, W_BLK, B_ZF), _col_spec(D_MODEL, W_BLK, B_ZA),
                  _col_spec(D_MODEL, 2 * D_MODEL, 1),
                  _const_spec(w_f.shape), _const_spec(w_a.shape), _const_spec(w_o.shape),
                  _const_spec(g.shape)],
        out_specs=tok(D_MODEL),
        compiler_params=_params(1),
        name=f"merge_{n}",
    )(x, mod3, f, attn, w_in_p, w_in_p, w_in_p, w_f, w_a, w_o, g)


def _dft_tables(n):
    norm = 1.0 / math.sqrt(n)
    if n <= 256:
        k = jnp.arange(n, dtype=jnp.int32)
        ang = ((k[:, None] * k[None, :]) % n).astype(F32) * (2.0 * math.pi / n)
        return (jnp.cos(ang) * norm).astype(BF16), (jnp.sin(ang) * -norm).astype(BF16)
    lo = 64
    hi = n // lo
    pos = jnp.arange(n, dtype=jnp.int32)
    a1 = ((jnp.arange(hi, dtype=jnp.int32)[:, None] * pos[None, :]) % hi).astype(F32) * (2.0 * math.pi / hi)
    a0 = ((jnp.arange(lo, dtype=jnp.int32)[:, None] * pos[None, :]) % n).astype(F32) * (2.0 * math.pi / n)
    c1, s1 = jnp.cos(a1)[:, None, :], jnp.sin(a1)[:, None, :]
    c0, s0 = jnp.cos(a0)[None, :, :] * norm, jnp.sin(a0)[None, :, :] * norm
    cn = (c1 * c0 - s1 * s0).reshape(n, n)
    sn = (s1 * c0 + c1 * s0).reshape(n, n)
    return cn.astype(BF16), (-sn).astype(BF16)


def _channel_dft_table():
    c = jnp.arange(GROUP, dtype=jnp.int32)
    ang = ((c[:, None] * c[None, :]) % GROUP).astype(F32) * (2.0 * math.pi / GROUP)
    norm = 1.0 / math.sqrt(GROUP)
    return (jnp.concatenate([jnp.cos(ang), jnp.sin(ang)], axis=1) * norm).astype(BF16)


def _rope_tables(n):
    t = jnp.arange(n)
    row = (t // GRID_W).astype(F32)
    col = (t % GRID_W).astype(F32)
    half = QK_ROPE // 2
    inv = ROPE_THETA ** (-jnp.arange(0, half, 2, dtype=F32) / half)
    ar, ac = row[:, None] * inv, col[:, None] * inv
    ang = jnp.concatenate([ar, ar, ac, ac], axis=-1)
    cos, sin = jnp.cos(ang), jnp.sin(ang)
    first = (np.arange(QK_ROPE) % half) < (half // 2)
    sa = jnp.where(first[None, :], -sin, 0.0)
    sb = jnp.where(first[None, :], 0.0, sin)
    pad = lambda a, fill: jnp.concatenate(
        [jnp.full((n, ROPE_LANE), fill, F32), a, jnp.full((n, HEAD_PAD - ROPE_LANE - QK_ROPE), fill, F32)], axis=1)
    return pad(cos, 1.0), pad(sa, 0.0), pad(sb, 0.0)


def kernel(x_prompt, x_sample, cache_ckv, cache_krope, c, c_ctx, w_ada, b_ada, w_in, q_norm_g, w_uq,
           kv_norm_g, w_ukv, w_f_out, w_a_out, w_out, final_norm_g):
    assert w_in.shape[0] == 1
    b_ctx, n_ctx, _ = x_prompt.shape
    dec_b, n_lat, _ = x_sample.shape
    past = cache_ckv.shape[2]

    w_uq_p = jnp.pad(w_uq[0].reshape(Q_RANK, N_HEADS, QK_NOPE + QK_ROPE),
                     ((0, 0), (0, 0), (0, HEAD_PAD - QK_NOPE - QK_ROPE))).reshape(Q_RANK, N_HEADS * HEAD_PAD)
    w_uq_p = w_uq_p.astype(BF16)
    w_kv3 = w_ukv[0].reshape(KV_RANK, N_HEADS, QK_NOPE + V_DIM)
    w_knope = jnp.pad(w_kv3[:, :, :QK_NOPE], ((0, 0), (0, 0), (0, HEAD_PAD - QK_NOPE)))
    place = np.zeros((HEAD_PAD, N_HEADS, HEAD_PAD), np.float32)
    for j in range(QK_ROPE):
        place[ROPE_LANE + j, :, ROPE_LANE + j] = 1.0
    w_k = jnp.concatenate([w_knope.reshape(KV_RANK, -1), jnp.asarray(place).reshape(HEAD_PAD, -1)],
                          axis=0).astype(BF16)
    w_uvt = jnp.pad(jnp.transpose(w_kv3[:, :, QK_NOPE:], (1, 2, 0)),
                    ((0, 0), (0, V_ROWS - V_DIM), (0, 0))).reshape(VT_ROWS, KV_RANK).astype(BF16)
    ones_col = np.zeros((N_HEADS, V_ROWS, LANES), np.float32)
    ones_col[:, V_DIM, :] = 1.0
    ones_col = jnp.asarray(ones_col.reshape(VT_ROWS, LANES))
    w_f = w_f_out[0].astype(BF16)
    w_a = w_a_out[0].astype(BF16)
    w_o = w_out[0].astype(BF16)
    qg = q_norm_g[0].reshape(1, Q_RANK)
    kvg = kv_norm_g[0].reshape(1, KV_RANK)
    fg = final_norm_g.reshape(1, D_MODEL)
    wts = (qg, kvg, w_uq_p, w_k, w_uvt, ones_col, _channel_dft_table())
    w_in_p = _pack_w_in(w_in[0])

    cond8 = jnp.concatenate([c_ctx[None, :], c, jnp.zeros((8 - 1 - dec_b, D_MODEL), F32)], axis=0)
    mod3 = _adaln(cond8, w_ada[0], b_ada[0].reshape(1, -1)).reshape(8, 1, 3 * D_MODEL)

    xp = x_prompt.reshape(b_ctx * n_ctx, D_MODEL)
    ucs, q, k, vt, state_ckv, state_krope = _layer_in(xp, mod3, w_in_p, wts, None, n=n_ctx, state=True)
    f = _pos_dft(*_dft_tables(n_ctx), ucs, n=n_ctx, tm=n_ctx)
    attn = _attention(q, [k], [vt], [n_ctx], n=n_ctx, tq=n_ctx)
    y_prompt = _merge(xp, mod3, f, attn, w_in_p, w_f, w_a, w_o, fg, n=n_ctx, latent=False)

    xs = x_sample.reshape(dec_b * n_lat, D_MODEL)
    ucs, q, k, vt = _layer_in(xs, mod3, w_in_p, wts, _rope_tables(n_lat), n=n_lat, state=False)
    f = _pos_dft(*_dft_tables(n_lat), ucs, n=n_lat, tm=1024)
    krope_pad = jnp.pad(cache_krope[:, 0], ((0, 0), (0, 0), (ROPE_LANE, HEAD_PAD - ROPE_LANE - QK_ROPE)))
    k_c, vt_c = _cache_kv(cache_ckv[:, 0].reshape(dec_b * past, KV_RANK),
                          krope_pad.reshape(dec_b * past, HEAD_PAD), w_k, w_uvt, ones_col, m=past)
    attn = _attention(q, [k, k_c], [vt, vt_c], [n_lat, past], n=n_lat, tq=256)
    y_sample = _merge(xs, mod3, f, attn, w_in_p, w_f, w_a, w_o, fg, n=n_lat, latent=True)

    return (y_prompt.reshape(b_ctx, n_ctx, D_MODEL), y_sample.reshape(dec_b, n_lat, D_MODEL),
            state_ckv.reshape(b_ctx, 1, n_ctx, KV_RANK), state_krope.reshape(b_ctx, 1, n_ctx, QK_ROPE))
```

```python
import functools
import math

import jax
import jax.numpy as jnp
import numpy as np
from jax import lax
from jax.experimental import pallas as pl
from jax.experimental.pallas import tpu as pltpu

F32 = jnp.float32
BF16 = jnp.bfloat16

D_MODEL = 1024
GRID_W = 64
N_GROUPS = 4
GROUP = 128
D_FOURIER = N_GROUPS * GROUP
N_HEADS = 8
QK_NOPE = 64
QK_ROPE = 32
V_DIM = 64
Q_RANK = 256
KV_RANK = 128
D_ATTN = N_HEADS * V_DIM
D_IN = 2 * D_FOURIER + Q_RANK + KV_RANK + QK_ROPE + D_ATTN + 2 * D_MODEL
ROPE_THETA = 10000.0
EPS = 1e-6
LANES = 128
HEAD_PAD = LANES
ROPE_LANE = QK_NOPE
V_ROWS = V_DIM + 16
VT_ROWS = N_HEADS * V_ROWS
Q_SCALE = (QK_NOPE + QK_ROPE) ** -0.5 * math.log2(math.e)
KEY_CHUNK = 256
W_BLK = 512
B_UF, B_ZF, B_MID, B_ZA = 0, 1, 2, 3
D_IN_PAD = 8 * W_BLK
TOKEN_TILE = 512
VMEM_LIMIT_BYTES = 56 * 1024 * 1024


def _const_spec(shape):
    nd = len(shape)
    return pl.BlockSpec(shape, lambda *_: (0,) * nd, pipeline_mode=pl.Buffered(1))


def _row_spec(rows, blk):
    return pl.BlockSpec((rows, D_MODEL), lambda *_: (blk, 0), pipeline_mode=pl.Buffered(1))


def _params(n_axes):
    return pltpu.CompilerParams(dimension_semantics=("arbitrary",) * n_axes,
                                vmem_limit_bytes=VMEM_LIMIT_BYTES)


def _rms(x):
    return x * lax.rsqrt(jnp.mean(x * x, axis=-1, keepdims=True) + EPS)


def _sigmoid(x):
    return 1.0 / (1.0 + jnp.exp(-x))


def _dot(a, b):
    return jnp.dot(a, b, preferred_element_type=F32)


def _dot_nt(a, b):
    return lax.dot_general(a, b, (((1,), (1,)), ((), ())), preferred_element_type=F32)


def _token_tiling(t, n, *, per_sequence):
    tm = TOKEN_TILE
    assert t % tm == 0 and (not per_sequence or n % tm == 0)
    per_seq = n // tm if per_sequence else None
    mod_row = (lambda i: 1 + i // per_seq) if per_sequence else (lambda i: 0)
    return tm, per_seq, mod_row


def _modulated(x_ref, mod_ref):
    mod = mod_ref[0]
    shift, scale = mod[:, 0:D_MODEL], mod[:, D_MODEL:2 * D_MODEL]
    return (_rms(x_ref[...]) * (1.0 + scale) + shift).astype(BF16)


def _adaln_kernel(c_ref, w_ref, b_ref, o_ref):
    c = c_ref[...]
    s = (c * _sigmoid(c)).astype(BF16)
    o_ref[...] = _dot(s, w_ref[...].astype(BF16)) + b_ref[...]


def _adaln(cond8, w_ada, b_ada):
    n_blk = 4
    bw = 3 * D_MODEL // n_blk
    return pl.pallas_call(
        _adaln_kernel,
        out_shape=jax.ShapeDtypeStruct((8, 3 * D_MODEL), F32),
        grid=(n_blk,),
        in_specs=[pl.BlockSpec((8, D_MODEL), lambda j: (0, 0)),
                  pl.BlockSpec((D_MODEL, bw), lambda j: (0, j)),
                  pl.BlockSpec((1, bw), lambda j: (0, j))],
        out_specs=pl.BlockSpec((8, bw), lambda j: (0, j)),
        compiler_params=_params(1),
        name="adaln",
    )(cond8, w_ada, b_ada)


def _rot_rows(w):
    q = QK_ROPE // 4
    return jnp.concatenate([-w[q:2 * q], w[0:q], -w[3 * q:4 * q], w[2 * q:3 * q]], axis=0)


def _pack_w_in_kernel(wt_ref, o_ref):
    lo = 2 * D_FOURIER + Q_RANK + KV_RANK
    o_ref[:lo] = wt_ref[:lo].astype(BF16)
    kr = wt_ref[lo:lo + QK_ROPE]
    zeros = jnp.zeros((QK_ROPE, D_MODEL), F32)
    o_ref[lo:lo + LANES] = jnp.concatenate([_rot_rows(kr), zeros, kr, zeros], axis=0).astype(BF16)
    o_ref[lo + LANES:] = wt_ref[lo + QK_ROPE:].astype(BF16)


def _pack_w_in(w_in_t):
    return pl.pallas_call(
        _pack_w_in_kernel,
        out_shape=jax.ShapeDtypeStruct((D_IN_PAD, D_MODEL), BF16),
        grid=(1,),
        in_specs=[_const_spec((D_IN, D_MODEL))],
        out_specs=_const_spec((D_IN_PAD, D_MODEL)),
        compiler_params=_params(1),
        name="pack_w_in",
    )(w_in_t)


def _kv_outputs(ckvn_b, kr_b, w_k_ref, w_uvt_ref, ones_ref, k_ref, vt_ref):
    tm = ckvn_b.shape[0]
    kp = _dot(jnp.concatenate([ckvn_b, kr_b], axis=1), w_k_ref[...])
    for hh in range(N_HEADS):
        k_ref[hh] = kp[:, hh * HEAD_PAD:(hh + 1) * HEAD_PAD].astype(BF16)
    ones = jnp.concatenate([ones_ref[...]] * (tm // LANES), axis=1)
    vt_ref[...] = (_dot_nt(w_uvt_ref[...], ckvn_b) + ones).astype(BF16)


def _layer_in_kernel(*refs, rope, state):
    (x_ref, mod_ref, w_uf_ref, w_mid_ref, qg_ref, kvg_ref, w_uq_ref, w_k_ref, w_uvt_ref, ones_ref,
     cs_ref), refs = refs[:11], refs[11:]
    if rope:
        (cos_ref, sin_ref), refs = refs[:2], refs[2:]
    (ucs_ref, q_ref, k_ref, vt_ref), refs = refs[:4], refs[4:]
    if state:
        ckvn_ref, krope_ref = refs

    h = _modulated(x_ref, mod_ref)

    u = _dot_nt(h, w_uf_ref[...]).astype(BF16)
    for g in range(N_GROUPS):
        r = _dot(u[:, g * GROUP:(g + 1) * GROUP], cs_ref[...])
        ucs_ref[:, g * GROUP:(g + 1) * GROUP] = r[:, :GROUP].astype(BF16)
        ucs_ref[:, D_FOURIER + g * GROUP:D_FOURIER + (g + 1) * GROUP] = r[:, GROUP:].astype(BF16)

    mid = _dot_nt(h, w_mid_ref[...])
    cq, ckv, kr = mid[:, :Q_RANK], mid[:, Q_RANK:Q_RANK + KV_RANK], mid[:, Q_RANK + KV_RANK:]
    cqn = (_rms(cq) * (qg_ref[...] * Q_SCALE)).astype(BF16)
    ckvn = _rms(ckv) * kvg_ref[...]
    if state:
        ckvn_ref[...] = ckvn
        krope_ref[...] = kr[:, ROPE_LANE:ROPE_LANE + QK_ROPE]
    n_q = N_HEADS * HEAD_PAD
    if rope:
        cos, sin = cos_ref[...], sin_ref[...]
        kr = kr * cos + pltpu.roll(kr, ROPE_LANE, 1) * sin
        qp = _dot(cqn, w_uq_ref[...])
    else:
        qp = _dot(cqn, w_uq_ref[:, :n_q])
    for hh in range(N_HEADS):
        qh = qp[:, hh * HEAD_PAD:(hh + 1) * HEAD_PAD]
        if rope:
            qh = qh * cos + qp[:, n_q + hh * HEAD_PAD:n_q + (hh + 1) * HEAD_PAD] * sin
        q_ref[hh] = qh.astype(BF16)
    _kv_outputs(ckvn.astype(BF16), kr.astype(BF16), w_k_ref, w_uvt_ref, ones_ref, k_ref, vt_ref)


def _layer_in(x, mod3, w_in_p, wts, rope_tabs, *, n, state):
    t = x.shape[0]
    rope = rope_tabs is not None
    tm, per_seq, mod_row = _token_tiling(t, n, per_sequence=rope)
    tok = lambda w: pl.BlockSpec((tm, w), lambda i: (i, 0))
    in_specs = [tok(D_MODEL),
                pl.BlockSpec((1, 1, 3 * D_MODEL), lambda i: (mod_row(i), 0, 0)),
                _row_spec(W_BLK, B_UF), _row_spec(W_BLK, B_MID)]
    in_specs += [_const_spec(w.shape) for w in wts]
    args = [x, mod3, w_in_p, w_in_p, *wts]
    if rope:
        in_specs += [pl.BlockSpec((tm, HEAD_PAD), lambda i: (i % per_seq, 0))] * 2
        args += list(rope_tabs)
    head = pl.BlockSpec((N_HEADS, tm, HEAD_PAD), lambda i: (0, i, 0))
    out_specs = [tok(2 * D_FOURIER), head, head, pl.BlockSpec((VT_ROWS, tm), lambda i: (0, i))]
    out_shape = [jax.ShapeDtypeStruct((t, 2 * D_FOURIER), BF16),
                 jax.ShapeDtypeStruct((N_HEADS, t, HEAD_PAD), BF16),
                 jax.ShapeDtypeStruct((N_HEADS, t, HEAD_PAD), BF16),
                 jax.ShapeDtypeStruct((VT_ROWS, t), BF16)]
    if state:
        out_specs += [tok(KV_RANK), tok(QK_ROPE)]
        out_shape += [jax.ShapeDtypeStruct((t, KV_RANK), F32), jax.ShapeDtypeStruct((t, QK_ROPE), F32)]
    return pl.pallas_call(
        functools.partial(_layer_in_kernel, rope=rope, state=state),
        out_shape=out_shape,
        grid=(t // tm,),
        in_specs=in_specs,
        out_specs=out_specs,
        compiler_params=_params(1),
        name="layer_in_latent" if rope else "layer_in_context",
    )(*args)


def _cache_kv_kernel(ckv_ref, kr_ref, w_k_ref, w_uvt_ref, ones_ref, k_ref, vt_ref):
    _kv_outputs(ckv_ref[...].astype(BF16), kr_ref[...].astype(BF16), w_k_ref, w_uvt_ref, ones_ref,
                k_ref, vt_ref)


def _cache_kv(ckv, krope_pad, w_k, w_uvt, ones_col, *, m):
    t = ckv.shape[0]
    return pl.pallas_call(
        _cache_kv_kernel,
        out_shape=[jax.ShapeDtypeStruct((N_HEADS, t, HEAD_PAD), BF16),
                   jax.ShapeDtypeStruct((VT_ROWS, t), BF16)],
        grid=(t // m,),
        in_specs=[pl.BlockSpec((m, KV_RANK), lambda i: (i, 0)),
                  pl.BlockSpec((m, HEAD_PAD), lambda i: (i, 0)),
                  _const_spec(w_k.shape), _const_spec(w_uvt.shape), _const_spec(ones_col.shape)],
        out_specs=[pl.BlockSpec((N_HEADS, m, HEAD_PAD), lambda i: (0, i, 0)),
                   pl.BlockSpec((VT_ROWS, m), lambda i: (0, i))],
        compiler_params=_params(1),
        name="cache_kv",
    )(ckv, krope_pad, w_k, w_uvt, ones_col)


def _pos_dft_kernel(cn_ref, sn_ref, ucs_ref, f_ref):
    f = _dot(cn_ref[...], ucs_ref[:, :D_FOURIER]) + _dot(sn_ref[...], ucs_ref[:, D_FOURIER:])
    f_ref[...] = f.astype(BF16)


def _pos_dft(cn, snn, ucs, *, n, tm):
    t = ucs.shape[0]
    per_seq = n // tm
    return pl.pallas_call(
        _pos_dft_kernel,
        out_shape=jax.ShapeDtypeStruct((t, D_FOURIER), BF16),
        grid=(per_seq, t // n),
        in_specs=[pl.BlockSpec((tm, n), lambda i, bi: (i, 0)),
                  pl.BlockSpec((tm, n), lambda i, bi: (i, 0)),
                  pl.BlockSpec((n, 2 * D_FOURIER), lambda i, bi: (bi, 0))],
        out_specs=pl.BlockSpec((tm, D_FOURIER), lambda i, bi: (bi * per_seq + i, 0)),
        compiler_params=_params(2),
        name=f"pos_dft_{n}",
    )(cn, snn, ucs)


def _attention_kernel(*refs, n_kv):
    q_ref, refs = refs[0], refs[1:]
    k_refs, vt_refs, (o_ref, s_ref, p_ref, m_ref, ot_ref) = refs[:n_kv], refs[n_kv:2 * n_kv], refs[2 * n_kv:]
    tq = q_ref.shape[1]
    n_slots = s_ref.shape[0]
    chunks, base = [], 0
    for j, k_ref in enumerate(k_refs):
        chunks += [(j, c, base + c) for c in range(0, k_ref.shape[1], KEY_CHUNK)]
        base += k_ref.shape[1]

    def scores(hh, slot):
        qh = q_ref[hh]
        m8 = None
        for j, c, r in chunks:
            s = _dot_nt(k_refs[j][hh, c:c + KEY_CHUNK, :], qh)
            s_ref[slot, r:r + KEY_CHUNK, :] = s
            mc = jnp.max(s.reshape(KEY_CHUNK // 8, 8, tq), axis=0)
            m8 = mc if m8 is None else jnp.maximum(m8, mc)
        m_ref[slot] = m8

    def softmax_pv(hh, slot):
        m = jnp.max(m_ref[slot], axis=0, keepdims=True)
        for _, _, r in chunks:
            p_ref[slot, r:r + KEY_CHUNK, :] = jnp.exp2(s_ref[slot, r:r + KEY_CHUNK, :] - m).astype(BF16)
        ot, base = None, 0
        for vt_ref in vt_refs:
            mk = vt_ref.shape[1]
            part = _dot(vt_ref[pl.ds(pl.multiple_of(hh * V_ROWS, V_ROWS), V_ROWS), :],
                        p_ref[slot, base:base + mk, :])
            ot = part if ot is None else ot + part
            base += mk
        ot_ref[pl.ds(pl.multiple_of(hh * V_DIM, V_DIM), V_DIM), :] = ot[:V_DIM] / ot[V_DIM:V_DIM + 1]

    if n_slots == N_HEADS:
        for hh in range(N_HEADS):
            scores(hh, hh)
        for hh in range(N_HEADS):
            softmax_pv(hh, hh)
    else:
        scores(0, 0)
        for hh in range(N_HEADS):
            if hh + 1 < N_HEADS:
                scores(hh + 1, (hh + 1) % 2)
            softmax_pv(hh, hh % 2)
    o_ref[...] = ot_ref[...].T.astype(BF16)


def _attention(q, ks, vts, key_lens, *, n, tq):
    t = q.shape[1]
    per_seq = n // tq
    m_tot = sum(key_lens)
    n_slots = N_HEADS if m_tot <= 2 * KEY_CHUNK else 2
    in_specs = [pl.BlockSpec((N_HEADS, tq, HEAD_PAD), lambda bi, i: (0, bi * per_seq + i, 0))]
    in_specs += [pl.BlockSpec((N_HEADS, m, HEAD_PAD), lambda bi, i: (0, bi, 0)) for m in key_lens]
    in_specs += [pl.BlockSpec((VT_ROWS, m), lambda bi, i: (0, bi)) for m in key_lens]
    return pl.pallas_call(
        functools.partial(_attention_kernel, n_kv=len(ks)),
        out_shape=jax.ShapeDtypeStruct((t, D_ATTN), BF16),
        grid=(t // n, per_seq),
        in_specs=in_specs,
        out_specs=pl.BlockSpec((tq, D_ATTN), lambda bi, i: (bi * per_seq + i, 0)),
        scratch_shapes=[pltpu.VMEM((n_slots, m_tot, tq), F32), pltpu.VMEM((n_slots, m_tot, tq), BF16),
                        pltpu.VMEM((n_slots, 8, tq), F32), pltpu.VMEM((D_ATTN, tq), F32)],
        compiler_params=_params(2),
        name=f"attention_{n}",
    )(q, *ks, *vts)


def _merge_kernel(x_ref, mod_ref, f_ref, a_ref, w_zf_ref, w_za_ref, w_g_ref, w_f_ref, w_a_ref, w_o_ref,
                  g_ref, y_ref):
    h = _modulated(x_ref, mod_ref)
    zf = _dot_nt(h, w_zf_ref[...])
    y_f = _dot((f_ref[...] * (zf * _sigmoid(zf))).astype(BF16), w_f_ref[...])
    za = _dot_nt(h, w_za_ref[...])
    y_a = _dot((a_ref[...] * (za * _sigmoid(za))).astype(BF16), w_a_ref[...])
    merged = (_sigmoid(_dot_nt(h, w_g_ref[:D_MODEL])) * y_f
              + _sigmoid(_dot_nt(h, w_g_ref[D_MODEL:])) * y_a)
    gate = mod_ref[0][:, 2 * D_MODEL:]
    out = x_ref[...] + gate * _dot(merged.astype(BF16), w_o_ref[...])
    y_ref[...] = _rms(out) * g_ref[...]


def _merge(x, mod3, f, attn, w_in_p, w_f, w_a, w_o, g, *, n, latent):
    t = x.shape[0]
    tm, _, mod_row = _token_tiling(t, n, per_sequence=latent)
    tok = lambda w: pl.BlockSpec((tm, w), lambda i: (i, 0))
    return pl.pallas_call(
        _merge_kernel,
        out_shape=jax.ShapeDtypeStruct((t, D_MODEL), F32),
        grid=(t // tm,),
        in_specs=[tok(D_MODEL),
                  pl.BlockSpec((1, 1, 3 * D_MODEL), lambda i: (mod_row(i), 0, 0)),
                  tok(D_FOURIER), tok(D_ATTN),
                  _row_spec(W_BLK, B_ZF), _row_spec(W_BLK, B_ZA), _row_spec(2 * D_MODEL, 1),
                  _const_spec(w_f.shape), _const_spec(w_a.shape), _const_spec(w_o.shape),
                  _const_spec(g.shape)],
        out_specs=tok(D_MODEL),
        compiler_params=_params(1),
        name=f"merge_{n}",
    )(x, mod3, f, attn, w_in_p, w_in_p, w_in_p, w_f, w_a, w_o, g)


def _dft_tables(n):
    norm = 1.0 / math.sqrt(n)
    if n <= 256:
        k = jnp.arange(n, dtype=jnp.int32)
        ang = ((k[:, None] * k[None, :]) % n).astype(F32) * (2.0 * math.pi / n)
        return (jnp.cos(ang) * norm).astype(BF16), (jnp.sin(ang) * -norm).astype(BF16)
    lo = 64
    hi = n // lo
    pos = jnp.arange(n, dtype=jnp.int32)
    a1 = ((jnp.arange(hi, dtype=jnp.int32)[:, None] * pos[None, :]) % hi).astype(F32) * (2.0 * math.pi / hi)
    a0 = ((jnp.arange(lo, dtype=jnp.int32)[:, None] * pos[None, :]) % n).astype(F32) * (2.0 * math.pi / n)
    c1, s1 = jnp.cos(a1)[:, None, :], jnp.sin(a1)[:, None, :]
    c0, s0 = jnp.cos(a0)[None, :, :] * norm, jnp.sin(a0)[None, :, :] * norm
    cn = (c1 * c0 - s1 * s0).reshape(n, n)
    sn = (s1 * c0 + c1 * s0).reshape(n, n)
    return cn.astype(BF16), (-sn).astype(BF16)


def _channel_dft_table():
    c = jnp.arange(GROUP, dtype=jnp.int32)
    ang = ((c[:, None] * c[None, :]) % GROUP).astype(F32) * (2.0 * math.pi / GROUP)
    norm = 1.0 / math.sqrt(GROUP)
    return (jnp.concatenate([jnp.cos(ang), jnp.sin(ang)], axis=1) * norm).astype(BF16)


def _rope_tables(n):
    t = jnp.arange(n)
    row = (t // GRID_W).astype(F32)
    col = (t % GRID_W).astype(F32)
    half = QK_ROPE // 2
    inv = ROPE_THETA ** (-jnp.arange(0, half, 2, dtype=F32) / half)
    ar, ac = row[:, None] * inv, col[:, None] * inv
    ang = jnp.concatenate([ar, ar, ac, ac], axis=-1)
    pad = lambda a, fill: jnp.concatenate(
        [jnp.full((n, ROPE_LANE), fill, F32), a, jnp.full((n, HEAD_PAD - ROPE_LANE - QK_ROPE), fill, F32)], axis=1)
    return pad(jnp.cos(ang), 1.0), pad(jnp.sin(ang), 0.0)


def kernel(x_prompt, x_sample, cache_ckv, cache_krope, c, c_ctx, w_ada, b_ada, w_in, q_norm_g, w_uq,
           kv_norm_g, w_ukv, w_f_out, w_a_out, w_out, final_norm_g):
    assert w_in.shape[0] == 1
    b_ctx, n_ctx, _ = x_prompt.shape
    dec_b, n_lat, _ = x_sample.shape
    past = cache_ckv.shape[2]

    w_q3 = w_uq[0].reshape(Q_RANK, N_HEADS, QK_NOPE + QK_ROPE)
    head_pad = lambda a, left: jnp.pad(
        a, ((0, 0), (0, 0), (left, HEAD_PAD - left - a.shape[2]))).reshape(Q_RANK, N_HEADS * HEAD_PAD)
    rp = w_q3[:, :, QK_NOPE:].reshape(Q_RANK, N_HEADS, 2, 2, QK_ROPE // 4)
    w_q_rot = jnp.stack([-rp[:, :, :, 1], rp[:, :, :, 0]], axis=3).reshape(Q_RANK, N_HEADS, QK_ROPE)
    w_uq_p = jnp.concatenate([head_pad(w_q3, 0), head_pad(w_q_rot, ROPE_LANE)], axis=1).astype(BF16)
    w_kv3 = w_ukv[0].reshape(KV_RANK, N_HEADS, QK_NOPE + V_DIM)
    w_knope = jnp.pad(w_kv3[:, :, :QK_NOPE], ((0, 0), (0, 0), (0, HEAD_PAD - QK_NOPE)))
    place = np.zeros((HEAD_PAD, N_HEADS, HEAD_PAD), np.float32)
    for j in range(QK_ROPE):
        place[ROPE_LANE + j, :, ROPE_LANE + j] = 1.0
    w_k = jnp.concatenate([w_knope.reshape(KV_RANK, -1), jnp.asarray(place).reshape(HEAD_PAD, -1)],
                          axis=0).astype(BF16)
    w_uvt = jnp.pad(jnp.transpose(w_kv3[:, :, QK_NOPE:], (1, 2, 0)),
                    ((0, 0), (0, V_ROWS - V_DIM), (0, 0))).reshape(VT_ROWS, KV_RANK).astype(BF16)
    ones_col = np.zeros((N_HEADS, V_ROWS, LANES), np.float32)
    ones_col[:, V_DIM, :] = 1.0
    ones_col = jnp.asarray(ones_col.reshape(VT_ROWS, LANES))
    w_f = w_f_out[0].astype(BF16)
    w_a = w_a_out[0].astype(BF16)
    w_o = w_out[0].astype(BF16)
    qg = q_norm_g[0].reshape(1, Q_RANK)
    kvg = kv_norm_g[0].reshape(1, KV_RANK)
    fg = final_norm_g.reshape(1, D_MODEL)
    wts = (qg, kvg, w_uq_p, w_k, w_uvt, ones_col, _channel_dft_table())
    w_in_p = _pack_w_in(jnp.swapaxes(w_in[0], 0, 1))

    cond8 = jnp.concatenate([c_ctx[None, :], c, jnp.zeros((8 - 1 - dec_b, D_MODEL), F32)], axis=0)
    mod3 = _adaln(cond8, w_ada[0], b_ada[0].reshape(1, -1)).reshape(8, 1, 3 * D_MODEL)

    xp = x_prompt.reshape(b_ctx * n_ctx, D_MODEL)
    ucs, q, k, vt, state_ckv, state_krope = _layer_in(xp, mod3, w_in_p, wts, None, n=n_ctx, state=True)
    f = _pos_dft(*_dft_tables(n_ctx), ucs, n=n_ctx, tm=n_ctx)
    attn = _attention(q, [k], [vt], [n_ctx], n=n_ctx, tq=n_ctx)
    y_prompt = _merge(xp, mod3, f, attn, w_in_p, w_f, w_a, w_o, fg, n=n_ctx, latent=False)

    xs = x_sample.reshape(dec_b * n_lat, D_MODEL)
    ucs, q, k, vt = _layer_in(xs, mod3, w_in_p, wts, _rope_tables(n_lat), n=n_lat, state=False)
    f = _pos_dft(*_dft_tables(n_lat), ucs, n=n_lat, tm=1024)
    krope_pad = jnp.pad(cache_krope[:, 0], ((0, 0), (0, 0), (ROPE_LANE, HEAD_PAD - ROPE_LANE - QK_ROPE)))
    k_c, vt_c = _cache_kv(cache_ckv[:, 0].reshape(dec_b * past, KV_RANK),
                          krope_pad.reshape(dec_b * past, HEAD_PAD), w_k, w_uvt, ones_col, m=past)
    attn = _attention(q, [k, k_c], [vt, vt_c], [n_lat, past], n=n_lat, tq=256)
    y_sample = _merge(xs, mod3, f, attn, w_in_p, w_f, w_a, w_o, fg, n=n_lat, latent=True)

    return (y_prompt.reshape(b_ctx, n_ctx, D_MODEL), y_sample.reshape(dec_b, n_lat, D_MODEL),
            state_ckv.reshape(b_ctx, 1, n_ctx, KV_RANK), state_krope.reshape(b_ctx, 1, n_ctx, QK_ROPE))
```

```python
import functools
import math

import jax
import jax.numpy as jnp
import numpy as np
from jax import lax
from jax.experimental import pallas as pl
from jax.experimental.pallas import tpu as pltpu

F32 = jnp.float32
BF16 = jnp.bfloat16

D_MODEL = 1024
GRID_W = 64
N_GROUPS = 4
GROUP = 128
D_FOURIER = N_GROUPS * GROUP
N_HEADS = 8
QK_NOPE = 64
QK_ROPE = 32
V_DIM = 64
Q_RANK = 256
KV_RANK = 128
D_ATTN = N_HEADS * V_DIM
D_IN = 2 * D_FOURIER + Q_RANK + KV_RANK + QK_ROPE + D_ATTN + 2 * D_MODEL
ROPE_THETA = 10000.0
EPS = 1e-6
LANES = 128
HEAD_PAD = LANES
ROPE_LANE = QK_NOPE
V_ROWS = V_DIM + 16
VT_ROWS = N_HEADS * V_ROWS
Q_SCALE = (QK_NOPE + QK_ROPE) ** -0.5 * math.log2(math.e)
KEY_CHUNK = 256
Q_SHIFT_LANE = QK_NOPE + QK_ROPE
K_NORM_LANE = Q_SHIFT_LANE + 1
BOUND_SLACK = 1.0 + 2.0 ** -6
MIN_DENOMINATOR = 2.0 ** -60
W_BLK = 512
B_UF, B_ZF, B_MID, B_ZA = 0, 1, 2, 3
D_IN_PAD = 8 * W_BLK
TOKEN_TILE = 512
VMEM_LIMIT_BYTES = 56 * 1024 * 1024


def _const_spec(shape):
    nd = len(shape)
    return pl.BlockSpec(shape, lambda *_: (0,) * nd, pipeline_mode=pl.Buffered(1))


def _row_spec(rows, blk):
    return pl.BlockSpec((rows, D_MODEL), lambda *_: (blk, 0), pipeline_mode=pl.Buffered(1))


def _params(n_axes):
    return pltpu.CompilerParams(dimension_semantics=("arbitrary",) * n_axes,
                                vmem_limit_bytes=VMEM_LIMIT_BYTES)


def _rms(x):
    return x * lax.rsqrt(jnp.mean(x * x, axis=-1, keepdims=True) + EPS)


def _sigmoid(x):
    return 1.0 / (1.0 + jnp.exp(-x))


def _dot(a, b):
    return jnp.dot(a, b, preferred_element_type=F32)


def _dot_nt(a, b):
    return lax.dot_general(a, b, (((1,), (1,)), ((), ())), preferred_element_type=F32)


def _token_tiling(t, n, *, per_sequence):
    tm = TOKEN_TILE
    assert t % tm == 0 and (not per_sequence or n % tm == 0)
    per_seq = n // tm if per_sequence else None
    mod_row = (lambda i: 1 + i // per_seq) if per_sequence else (lambda i: 0)
    return tm, per_seq, mod_row


def _modulated(x_ref, mod_ref):
    mod = mod_ref[0]
    shift, scale = mod[:, 0:D_MODEL], mod[:, D_MODEL:2 * D_MODEL]
    return (_rms(x_ref[...]) * (1.0 + scale) + shift).astype(BF16)


def _adaln_kernel(c_ref, w_ref, b_ref, o_ref):
    c = c_ref[...]
    s = (c * _sigmoid(c)).astype(BF16)
    o_ref[...] = _dot(s, w_ref[...].astype(BF16)) + b_ref[...]


def _adaln(cond8, w_ada, b_ada):
    n_blk = 4
    bw = 3 * D_MODEL // n_blk
    return pl.pallas_call(
        _adaln_kernel,
        out_shape=jax.ShapeDtypeStruct((8, 3 * D_MODEL), F32),
        grid=(n_blk,),
        in_specs=[pl.BlockSpec((8, D_MODEL), lambda j: (0, 0)),
                  pl.BlockSpec((D_MODEL, bw), lambda j: (0, j)),
                  pl.BlockSpec((1, bw), lambda j: (0, j))],
        out_specs=pl.BlockSpec((8, bw), lambda j: (0, j)),
        compiler_params=_params(1),
        name="adaln",
    )(cond8, w_ada, b_ada)


def _rot_rows(w):
    q = QK_ROPE // 4
    return jnp.concatenate([-w[q:2 * q], w[0:q], -w[3 * q:4 * q], w[2 * q:3 * q]], axis=0)


def _pack_w_in_kernel(wt_ref, o_ref):
    lo = 2 * D_FOURIER + Q_RANK + KV_RANK
    o_ref[:lo] = wt_ref[:lo].astype(BF16)
    kr = wt_ref[lo:lo + QK_ROPE]
    zeros = jnp.zeros((QK_ROPE, D_MODEL), F32)
    o_ref[lo:lo + LANES] = jnp.concatenate([_rot_rows(kr), zeros, kr, zeros], axis=0).astype(BF16)
    o_ref[lo + LANES:] = wt_ref[lo + QK_ROPE:].astype(BF16)


def _pack_w_in(w_in_t):
    return pl.pallas_call(
        _pack_w_in_kernel,
        out_shape=jax.ShapeDtypeStruct((D_IN_PAD, D_MODEL), BF16),
        grid=(1,),
        in_specs=[_const_spec((D_IN, D_MODEL))],
        out_specs=_const_spec((D_IN_PAD, D_MODEL)),
        compiler_params=_params(1),
        name="pack_w_in",
    )(w_in_t)


def _kv_outputs(ckvn_b, kr_b, w_k_ref, w_uvt_ref, ones_ref, k_ref, vt_ref):
    tm = ckvn_b.shape[0]
    kp = _dot(jnp.concatenate([ckvn_b, kr_b], axis=1), w_k_ref[...])
    lane = lax.broadcasted_iota(jnp.int32, (1, HEAD_PAD), 1)
    for hh in range(N_HEADS):
        kh = kp[:, hh * HEAD_PAD:(hh + 1) * HEAD_PAD]
        kn2 = jnp.sum(kh * kh, axis=-1, keepdims=True)
        kh = jnp.where(lane == Q_SHIFT_LANE, 1.0, jnp.where(lane == K_NORM_LANE, kn2, kh))
        k_ref[hh] = kh.astype(BF16)
    ones = jnp.concatenate([ones_ref[...]] * (tm // LANES), axis=1)
    vt_ref[...] = (_dot_nt(w_uvt_ref[...], ckvn_b) + ones).astype(BF16)


def _layer_in_kernel(*refs, rope, state):
    (x_ref, mod_ref, w_uf_ref, w_mid_ref, qg_ref, kvg_ref, w_uq_ref, w_k_ref, w_uvt_ref, ones_ref,
     cs_ref), refs = refs[:11], refs[11:]
    if rope:
        (cos_ref, sin_ref), refs = refs[:2], refs[2:]
    (ucs_ref, q_ref, k_ref, vt_ref), refs = refs[:4], refs[4:]
    if state:
        ckvn_ref, krope_ref = refs

    h = _modulated(x_ref, mod_ref)

    u = _dot_nt(h, w_uf_ref[...]).astype(BF16)
    for g in range(N_GROUPS):
        r = _dot(u[:, g * GROUP:(g + 1) * GROUP], cs_ref[...])
        ucs_ref[:, g * GROUP:(g + 1) * GROUP] = r[:, :GROUP].astype(BF16)
        ucs_ref[:, D_FOURIER + g * GROUP:D_FOURIER + (g + 1) * GROUP] = r[:, GROUP:].astype(BF16)

    mid = _dot_nt(h, w_mid_ref[...])
    cq, ckv, kr = mid[:, :Q_RANK], mid[:, Q_RANK:Q_RANK + KV_RANK], mid[:, Q_RANK + KV_RANK:]
    cqn = (_rms(cq) * (qg_ref[...] * Q_SCALE)).astype(BF16)
    ckvn = _rms(ckv) * kvg_ref[...]
    if state:
        ckvn_ref[...] = ckvn
        krope_ref[...] = kr[:, ROPE_LANE:ROPE_LANE + QK_ROPE]
    n_q = N_HEADS * HEAD_PAD
    if rope:
        cos, sin = cos_ref[...], sin_ref[...]
        kr = kr * cos + pltpu.roll(kr, ROPE_LANE, 1) * sin
        qp = _dot(cqn, w_uq_ref[...])
    else:
        qp = _dot(cqn, w_uq_ref[:, :n_q])
    lane = lax.broadcasted_iota(jnp.int32, (1, HEAD_PAD), 1)
    for hh in range(N_HEADS):
        qh = qp[:, hh * HEAD_PAD:(hh + 1) * HEAD_PAD]
        if rope:
            qh = qh * cos + qp[:, n_q + hh * HEAD_PAD:n_q + (hh + 1) * HEAD_PAD] * sin
        qn = jnp.sqrt(jnp.sum(qh * qh, axis=-1, keepdims=True))
        q_ref[hh] = jnp.where(lane == Q_SHIFT_LANE, -qn, qh).astype(BF16)
    _kv_outputs(ckvn.astype(BF16), kr.astype(BF16), w_k_ref, w_uvt_ref, ones_ref, k_ref, vt_ref)


def _layer_in(x, mod3, w_in_p, wts, rope_tabs, *, n, state):
    t = x.shape[0]
    rope = rope_tabs is not None
    tm, per_seq, mod_row = _token_tiling(t, n, per_sequence=rope)
    tok = lambda w: pl.BlockSpec((tm, w), lambda i: (i, 0))
    in_specs = [tok(D_MODEL),
                pl.BlockSpec((1, 1, 3 * D_MODEL), lambda i: (mod_row(i), 0, 0)),
                _row_spec(W_BLK, B_UF), _row_spec(W_BLK, B_MID)]
    in_specs += [_const_spec(w.shape) for w in wts]
    args = [x, mod3, w_in_p, w_in_p, *wts]
    if rope:
        in_specs += [pl.BlockSpec((tm, HEAD_PAD), lambda i: (i % per_seq, 0))] * 2
        args += list(rope_tabs)
    head = pl.BlockSpec((N_HEADS, tm, HEAD_PAD), lambda i: (0, i, 0))
    out_specs = [tok(2 * D_FOURIER), head, head, pl.BlockSpec((VT_ROWS, tm), lambda i: (0, i))]
    out_shape = [jax.ShapeDtypeStruct((t, 2 * D_FOURIER), BF16),
                 jax.ShapeDtypeStruct((N_HEADS, t, HEAD_PAD), BF16),
                 jax.ShapeDtypeStruct((N_HEADS, t, HEAD_PAD), BF16),
                 jax.ShapeDtypeStruct((VT_ROWS, t), BF16)]
    if state:
        out_specs += [tok(KV_RANK), tok(QK_ROPE)]
        out_shape += [jax.ShapeDtypeStruct((t, KV_RANK), F32), jax.ShapeDtypeStruct((t, QK_ROPE), F32)]
    return pl.pallas_call(
        functools.partial(_layer_in_kernel, rope=rope, state=state),
        out_shape=out_shape,
        grid=(t // tm,),
        in_specs=in_specs,
        out_specs=out_specs,
        compiler_params=_params(1),
        name="layer_in_latent" if rope else "layer_in_context",
    )(*args)


def _cache_kv_kernel(ckv_ref, kr_ref, w_k_ref, w_uvt_ref, ones_ref, k_ref, vt_ref):
    _kv_outputs(ckv_ref[...].astype(BF16), kr_ref[...].astype(BF16), w_k_ref, w_uvt_ref, ones_ref,
                k_ref, vt_ref)


def _cache_kv(ckv, krope_pad, w_k, w_uvt, ones_col, *, m):
    t = ckv.shape[0]
    return pl.pallas_call(
        _cache_kv_kernel,
        out_shape=[jax.ShapeDtypeStruct((N_HEADS, t, HEAD_PAD), BF16),
                   jax.ShapeDtypeStruct((VT_ROWS, t), BF16)],
        grid=(t // m,),
        in_specs=[pl.BlockSpec((m, KV_RANK), lambda i: (i, 0)),
                  pl.BlockSpec((m, HEAD_PAD), lambda i: (i, 0)),
                  _const_spec(w_k.shape), _const_spec(w_uvt.shape), _const_spec(ones_col.shape)],
        out_specs=[pl.BlockSpec((N_HEADS, m, HEAD_PAD), lambda i: (0, i, 0)),
                   pl.BlockSpec((VT_ROWS, m), lambda i: (0, i))],
        compiler_params=_params(1),
        name="cache_kv",
    )(ckv, krope_pad, w_k, w_uvt, ones_col)


def _pos_dft_kernel(cn_ref, sn_ref, ucs_ref, f_ref):
    f = _dot(cn_ref[...], ucs_ref[:, :D_FOURIER]) + _dot(sn_ref[...], ucs_ref[:, D_FOURIER:])
    f_ref[...] = f.astype(BF16)


def _pos_dft(cn, snn, ucs, *, n, tm):
    t = ucs.shape[0]
    per_seq = n // tm
    return pl.pallas_call(
        _pos_dft_kernel,
        out_shape=jax.ShapeDtypeStruct((t, D_FOURIER), BF16),
        grid=(per_seq, t // n),
        in_specs=[pl.BlockSpec((tm, n), lambda i, bi: (i, 0)),
                  pl.BlockSpec((tm, n), lambda i, bi: (i, 0)),
                  pl.BlockSpec((n, 2 * D_FOURIER), lambda i, bi: (bi, 0))],
        out_specs=pl.BlockSpec((tm, D_FOURIER), lambda i, bi: (bi * per_seq + i, 0)),
        compiler_params=_params(2),
        name=f"pos_dft_{n}",
    )(cn, snn, ucs)


def _attention_kernel(*refs, n_kv):
    q_ref, refs = refs[0], refs[1:]
    k_refs, vt_refs, (o_ref, s_ref, pp_ref, qs_ref, ot_ref) = (
        refs[:n_kv], refs[n_kv:2 * n_kv], refs[2 * n_kv:])
    tq = q_ref.shape[1]
    n_slots = pp_ref.shape[0]
    chunks, base = [], 0
    for j, k_ref in enumerate(k_refs):
        chunks += [(j, c, base + c) for c in range(0, k_ref.shape[1], KEY_CHUNK)]
        base += k_ref.shape[1]
    lane = lax.broadcasted_iota(jnp.int32, (1, HEAD_PAD), 1)

    def finish(hh, ot):
        l = ot[V_DIM:V_DIM + 1]
        ot_ref[pl.ds(pl.multiple_of(hh * V_DIM, V_DIM), V_DIM), :] = ot[:V_DIM] / l
        return l

    def shifted_softmax_pv(score_head, pv_head):
        if score_head is not None:
            qh = q_ref[score_head] * qs_ref[score_head, 0:1, :].astype(BF16)
        ot = None
        for j, c, r in chunks:
            if score_head is not None:
                s = _dot_nt(k_refs[j][score_head, c:c + KEY_CHUNK, :], qh)
                pp_ref[score_head % n_slots, r:r + KEY_CHUNK, :] = jnp.exp2(s).astype(BF16)
            if pv_head is not None:
                part = _dot(vt_refs[j][pv_head * V_ROWS:(pv_head + 1) * V_ROWS, c:c + KEY_CHUNK],
                            pp_ref[pv_head % n_slots, r:r + KEY_CHUNK, :])
                ot = part if ot is None else ot + part
        return finish(pv_head, ot) if pv_head is not None else None

    def exact_scores(hh, slot):
        qh = q_ref[hh]
        m8 = None
        for j, c, r in chunks:
            s = _dot_nt(k_refs[j][hh, c:c + KEY_CHUNK, :], qh)
            s_ref[slot, r:r + KEY_CHUNK, :] = s
            mc = jnp.max(s.reshape(KEY_CHUNK // 8, 8, tq), axis=0)
            m8 = mc if m8 is None else jnp.maximum(m8, mc)
        return jnp.max(m8, axis=0, keepdims=True)

    def exact_softmax_pv(hh, slot, m):
        for _, _, r in chunks:
            pp_ref[slot, r:r + KEY_CHUNK, :] = jnp.exp2(s_ref[slot, r:r + KEY_CHUNK, :] - m).astype(BF16)
        rows = pl.ds(pl.multiple_of(hh * V_ROWS, V_ROWS), V_ROWS)
        ot, base = None, 0
        for vt_ref in vt_refs:
            mk = vt_ref.shape[1]
            part = _dot(vt_ref[rows, :], pp_ref[slot, base:base + mk, :])
            ot = part if ot is None else ot + part
            base += mk
        finish(hh, ot)

    if n_slots == N_HEADS:
        maxima = [exact_scores(hh, hh) for hh in range(N_HEADS)]
        for hh in range(N_HEADS):
            exact_softmax_pv(hh, hh, maxima[hh])
    else:
        @pl.when(pl.program_id(1) == 0)
        def _():
            for hh in range(N_HEADS):
                kn2 = None
                for k_ref in k_refs:
                    kb = k_ref[hh]
                    c = jnp.max(kb.reshape(kb.shape[0] // 16, 16, HEAD_PAD), axis=0).astype(F32)
                    kn2 = c if kn2 is None else jnp.maximum(kn2, c)
                kmax = jnp.sqrt(jnp.max(kn2, axis=0, keepdims=True)) * BOUND_SLACK
                kmax = pltpu.roll(kmax, HEAD_PAD + Q_SHIFT_LANE - K_NORM_LANE, 1)
                qs_ref[hh] = jnp.broadcast_to(jnp.where(lane == Q_SHIFT_LANE, kmax, 1.0), (8, HEAD_PAD))

        l_min = None
        for hh in range(N_HEADS + 1):
            l = shifted_softmax_pv(hh if hh < N_HEADS else None, hh - 1 if hh > 0 else None)
            if l is not None:
                l_min = l if l_min is None else jnp.minimum(l_min, l)

        @pl.when(jnp.logical_not(jnp.min(l_min) >= MIN_DENOMINATOR))
        def _():
            def body(hh, carry):
                exact_softmax_pv(hh, 0, exact_scores(hh, 0))
                return carry

            lax.fori_loop(0, N_HEADS, body, 0)

    o_ref[...] = ot_ref[...].T.astype(BF16)


def _attention(q, ks, vts, key_lens, *, n, tq):
    t = q.shape[1]
    per_seq = n // tq
    m_tot = sum(key_lens)
    few_keys = m_tot <= 2 * KEY_CHUNK
    n_slots = N_HEADS if few_keys else 2
    in_specs = [pl.BlockSpec((N_HEADS, tq, HEAD_PAD), lambda bi, i: (0, bi * per_seq + i, 0))]
    in_specs += [pl.BlockSpec((N_HEADS, m, HEAD_PAD), lambda bi, i: (0, bi, 0)) for m in key_lens]
    in_specs += [pl.BlockSpec((VT_ROWS, m), lambda bi, i: (0, bi)) for m in key_lens]
    return pl.pallas_call(
        functools.partial(_attention_kernel, n_kv=len(ks)),
        out_shape=jax.ShapeDtypeStruct((t, D_ATTN), BF16),
        grid=(t // n, per_seq),
        in_specs=in_specs,
        out_specs=pl.BlockSpec((tq, D_ATTN), lambda bi, i: (bi * per_seq + i, 0)),
        scratch_shapes=[pltpu.VMEM((n_slots if few_keys else 1, m_tot, tq), F32),
                        pltpu.VMEM((n_slots, m_tot, tq), BF16), pltpu.VMEM((N_HEADS, 8, HEAD_PAD), F32),
                        pltpu.VMEM((D_ATTN, tq), F32)],
        compiler_params=_params(2),
        name=f"attention_{n}",
    )(q, *ks, *vts)


def _merge_kernel(x_ref, mod_ref, f_ref, a_ref, w_zf_ref, w_za_ref, w_g_ref, w_f_ref, w_a_ref, w_o_ref,
                  g_ref, y_ref):
    h = _modulated(x_ref, mod_ref)
    zf = _dot_nt(h, w_zf_ref[...])
    y_f = _dot((f_ref[...] * (zf * _sigmoid(zf))).astype(BF16), w_f_ref[...])
    za = _dot_nt(h, w_za_ref[...])
    y_a = _dot((a_ref[...] * (za * _sigmoid(za))).astype(BF16), w_a_ref[...])
    merged = (_sigmoid(_dot_nt(h, w_g_ref[:D_MODEL])) * y_f
              + _sigmoid(_dot_nt(h, w_g_ref[D_MODEL:])) * y_a)
    gate = mod_ref[0][:, 2 * D_MODEL:]
    out = x_ref[...] + gate * _dot(merged.astype(BF16), w_o_ref[...])
    y_ref[...] = _rms(out) * g_ref[...]


def _merge(x, mod3, f, attn, w_in_p, w_f, w_a, w_o, g, *, n, latent):
    t = x.shape[0]
    tm, _, mod_row = _token_tiling(t, n, per_sequence=latent)
    tok = lambda w: pl.BlockSpec((tm, w), lambda i: (i, 0))
    return pl.pallas_call(
        _merge_kernel,
        out_shape=jax.ShapeDtypeStruct((t, D_MODEL), F32),
        grid=(t // tm,),
        in_specs=[tok(D_MODEL),
                  pl.BlockSpec((1, 1, 3 * D_MODEL), lambda i: (mod_row(i), 0, 0)),
                  tok(D_FOURIER), tok(D_ATTN),
                  _row_spec(W_BLK, B_ZF), _row_spec(W_BLK, B_ZA), _row_spec(2 * D_MODEL, 1),
                  _const_spec(w_f.shape), _const_spec(w_a.shape), _const_spec(w_o.shape),
                  _const_spec(g.shape)],
        out_specs=tok(D_MODEL),
        compiler_params=_params(1),
        name=f"merge_{n}",
    )(x, mod3, f, attn, w_in_p, w_in_p, w_in_p, w_f, w_a, w_o, g)


def _dft_tables(n):
    norm = 1.0 / math.sqrt(n)
    if n <= 256:
        k = jnp.arange(n, dtype=jnp.int32)
        ang = ((k[:, None] * k[None, :]) % n).astype(F32) * (2.0 * math.pi / n)
        return (jnp.cos(ang) * norm).astype(BF16), (jnp.sin(ang) * -norm).astype(BF16)
    lo = 64
    hi = n // lo
    pos = jnp.arange(n, dtype=jnp.int32)
    a1 = ((jnp.arange(hi, dtype=jnp.int32)[:, None] * pos[None, :]) % hi).astype(F32) * (2.0 * math.pi / hi)
    a0 = ((jnp.arange(lo, dtype=jnp.int32)[:, None] * pos[None, :]) % n).astype(F32) * (2.0 * math.pi / n)
    c1, s1 = jnp.cos(a1)[:, None, :], jnp.sin(a1)[:, None, :]
    c0, s0 = jnp.cos(a0)[None, :, :] * norm, jnp.sin(a0)[None, :, :] * norm
    cn = (c1 * c0 - s1 * s0).reshape(n, n)
    sn = (s1 * c0 + c1 * s0).reshape(n, n)
    return cn.astype(BF16), (-sn).astype(BF16)


def _channel_dft_table():
    c = jnp.arange(GROUP, dtype=jnp.int32)
    ang = ((c[:, None] * c[None, :]) % GROUP).astype(F32) * (2.0 * math.pi / GROUP)
    norm = 1.0 / math.sqrt(GROUP)
    return (jnp.concatenate([jnp.cos(ang), jnp.sin(ang)], axis=1) * norm).astype(BF16)


def _rope_tables(n):
    t = jnp.arange(n)
    row = (t // GRID_W).astype(F32)
    col = (t % GRID_W).astype(F32)
    half = QK_ROPE // 2
    inv = ROPE_THETA ** (-jnp.arange(0, half, 2, dtype=F32) / half)
    ar, ac = row[:, None] * inv, col[:, None] * inv
    ang = jnp.concatenate([ar, ar, ac, ac], axis=-1)
    pad = lambda a, fill: jnp.concatenate(
        [jnp.full((n, ROPE_LANE), fill, F32), a, jnp.full((n, HEAD_PAD - ROPE_LANE - QK_ROPE), fill, F32)], axis=1)
    return pad(jnp.cos(ang), 1.0), pad(jnp.sin(ang), 0.0)


def kernel(x_prompt, x_sample, cache_ckv, cache_krope, c, c_ctx, w_ada, b_ada, w_in, q_norm_g, w_uq,
           kv_norm_g, w_ukv, w_f_out, w_a_out, w_out, final_norm_g):
    assert w_in.shape[0] == 1
    b_ctx, n_ctx, _ = x_prompt.shape
    dec_b, n_lat, _ = x_sample.shape
    past = cache_ckv.shape[2]

    w_q3 = w_uq[0].reshape(Q_RANK, N_HEADS, QK_NOPE + QK_ROPE)
    head_pad = lambda a, left: jnp.pad(
        a, ((0, 0), (0, 0), (left, HEAD_PAD - left - a.shape[2]))).reshape(Q_RANK, N_HEADS * HEAD_PAD)
    rp = w_q3[:, :, QK_NOPE:].reshape(Q_RANK, N_HEADS, 2, 2, QK_ROPE // 4)
    w_q_rot = jnp.stack([-rp[:, :, :, 1], rp[:, :, :, 0]], axis=3).reshape(Q_RANK, N_HEADS, QK_ROPE)
    w_uq_p = jnp.concatenate([head_pad(w_q3, 0), head_pad(w_q_rot, ROPE_LANE)], axis=1).astype(BF16)
    w_kv3 = w_ukv[0].reshape(KV_RANK, N_HEADS, QK_NOPE + V_DIM)
    w_knope = jnp.pad(w_kv3[:, :, :QK_NOPE], ((0, 0), (0, 0), (0, HEAD_PAD - QK_NOPE)))
    place = np.zeros((HEAD_PAD, N_HEADS, HEAD_PAD), np.float32)
    for j in range(QK_ROPE):
        place[ROPE_LANE + j, :, ROPE_LANE + j] = 1.0
    w_k = jnp.concatenate([w_knope.reshape(KV_RANK, -1), jnp.asarray(place).reshape(HEAD_PAD, -1)],
                          axis=0).astype(BF16)
    w_uvt = jnp.pad(jnp.transpose(w_kv3[:, :, QK_NOPE:], (1, 2, 0)),
                    ((0, 0), (0, V_ROWS - V_DIM), (0, 0))).reshape(VT_ROWS, KV_RANK).astype(BF16)
    ones_col = np.zeros((N_HEADS, V_ROWS, LANES), np.float32)
    ones_col[:, V_DIM, :] = 1.0
    ones_col = jnp.asarray(ones_col.reshape(VT_ROWS, LANES))
    w_f = w_f_out[0].astype(BF16)
    w_a = w_a_out[0].astype(BF16)
    w_o = w_out[0].astype(BF16)
    qg = q_norm_g[0].reshape(1, Q_RANK)
    kvg = kv_norm_g[0].reshape(1, KV_RANK)
    fg = final_norm_g.reshape(1, D_MODEL)
    wts = (qg, kvg, w_uq_p, w_k, w_uvt, ones_col, _channel_dft_table())
    w_in_p = _pack_w_in(jnp.swapaxes(w_in[0], 0, 1))

    cond8 = jnp.concatenate([c_ctx[None, :], c, jnp.zeros((8 - 1 - dec_b, D_MODEL), F32)], axis=0)
    mod3 = _adaln(cond8, w_ada[0], b_ada[0].reshape(1, -1)).reshape(8, 1, 3 * D_MODEL)

    xp = x_prompt.reshape(b_ctx * n_ctx, D_MODEL)
    ucs, q, k, vt, state_ckv, state_krope = _layer_in(xp, mod3, w_in_p, wts, None, n=n_ctx, state=True)
    f = _pos_dft(*_dft_tables(n_ctx), ucs, n=n_ctx, tm=n_ctx)
    attn = _attention(q, [k], [vt], [n_ctx], n=n_ctx, tq=n_ctx)
    y_prompt = _merge(xp, mod3, f, attn, w_in_p, w_f, w_a, w_o, fg, n=n_ctx, latent=False)

    xs = x_sample.reshape(dec_b * n_lat, D_MODEL)
    ucs, q, k, vt = _layer_in(xs, mod3, w_in_p, wts, _rope_tables(n_lat), n=n_lat, state=False)
    f = _pos_dft(*_dft_tables(n_lat), ucs, n=n_lat, tm=1024)
    krope_pad = jnp.pad(cache_krope[:, 0], ((0, 0), (0, 0), (ROPE_LANE, HEAD_PAD - ROPE_LANE - QK_ROPE)))
    k_c, vt_c = _cache_kv(cache_ckv[:, 0].reshape(dec_b * past, KV_RANK),
                          krope_pad.reshape(dec_b * past, HEAD_PAD), w_k, w_uvt, ones_col, m=past)
    attn = _attention(q, [k, k_c], [vt, vt_c], [n_lat, past], n=n_lat, tq=256)
    y_sample = _merge(xs, mod3, f, attn, w_in_p, w_f, w_a, w_o, fg, n=n_lat, latent=True)

    return (y_prompt.reshape(b_ctx, n_ctx, D_MODEL), y_sample.reshape(dec_b, n_lat, D_MODEL),
            state_ckv.reshape(b_ctx, 1, n_ctx, KV_RANK), state_krope.reshape(b_ctx, 1, n_ctx, QK_ROPE))
```

```python
import functools
import math

import jax
import jax.numpy as jnp
import numpy as np
from jax import lax
from jax.experimental import pallas as pl
from jax.experimental.pallas import tpu as pltpu

F32 = jnp.float32
BF16 = jnp.bfloat16

D_MODEL = 1024
GRID_W = 64
N_GROUPS = 4
GROUP = 128
D_FOURIER = N_GROUPS * GROUP
N_HEADS = 8
QK_NOPE = 64
QK_ROPE = 32
V_DIM = 64
Q_RANK = 256
KV_RANK = 128
D_ATTN = N_HEADS * V_DIM
D_IN = 2 * D_FOURIER + Q_RANK + KV_RANK + QK_ROPE + D_ATTN + 2 * D_MODEL
ROPE_THETA = 10000.0
EPS = 1e-6
LANES = 128
HEAD_PAD = LANES
ROPE_LANE = QK_NOPE
V_ROWS = V_DIM + 16
VT_ROWS = N_HEADS * V_ROWS
Q_SCALE = (QK_NOPE + QK_ROPE) ** -0.5 * math.log2(math.e)
KEY_CHUNK = 256
BOUND_SLACK = 1.0 + 2.0 ** -6
MIN_DENOMINATOR = 2.0 ** -60
W_BLK = 512
B_UF, B_ZF, B_MID, B_ZA = 0, 1, 2, 3
D_IN_PAD = 8 * W_BLK
TOKEN_TILE = 512
VMEM_LIMIT_BYTES = 56 * 1024 * 1024


def _const_spec(shape):
    nd = len(shape)
    return pl.BlockSpec(shape, lambda *_: (0,) * nd, pipeline_mode=pl.Buffered(1))


def _row_spec(rows, blk):
    return pl.BlockSpec((rows, D_MODEL), lambda *_: (blk, 0), pipeline_mode=pl.Buffered(1))


def _params(n_axes):
    return pltpu.CompilerParams(dimension_semantics=("arbitrary",) * n_axes,
                                vmem_limit_bytes=VMEM_LIMIT_BYTES)


def _rms(x):
    return x * lax.rsqrt(jnp.mean(x * x, axis=-1, keepdims=True) + EPS)


def _sigmoid(x):
    return 1.0 / (1.0 + jnp.exp(-x))


def _dot(a, b):
    return jnp.dot(a, b, preferred_element_type=F32)


def _dot_nt(a, b):
    return lax.dot_general(a, b, (((1,), (1,)), ((), ())), preferred_element_type=F32)


def _token_tiling(t, n, *, per_sequence):
    tm = TOKEN_TILE
    assert t % tm == 0 and (not per_sequence or n % tm == 0)
    per_seq = n // tm if per_sequence else None
    mod_row = (lambda i: 1 + i // per_seq) if per_sequence else (lambda i: 0)
    return tm, per_seq, mod_row


def _modulated(x_ref, mod_ref):
    mod = mod_ref[0]
    shift, scale = mod[:, 0:D_MODEL], mod[:, D_MODEL:2 * D_MODEL]
    return (_rms(x_ref[...]) * (1.0 + scale) + shift).astype(BF16)


def _adaln_kernel(c_ref, w_ref, b_ref, o_ref):
    c = c_ref[...]
    s = (c * _sigmoid(c)).astype(BF16)
    o_ref[...] = _dot(s, w_ref[...].astype(BF16)) + b_ref[...]


def _adaln(cond8, w_ada, b_ada):
    n_blk = 4
    bw = 3 * D_MODEL // n_blk
    return pl.pallas_call(
        _adaln_kernel,
        out_shape=jax.ShapeDtypeStruct((8, 3 * D_MODEL), F32),
        grid=(n_blk,),
        in_specs=[pl.BlockSpec((8, D_MODEL), lambda j: (0, 0)),
                  pl.BlockSpec((D_MODEL, bw), lambda j: (0, j)),
                  pl.BlockSpec((1, bw), lambda j: (0, j))],
        out_specs=pl.BlockSpec((8, bw), lambda j: (0, j)),
        compiler_params=_params(1),
        name="adaln",
    )(cond8, w_ada, b_ada)


def _rot_rows(w):
    q = QK_ROPE // 4
    return jnp.concatenate([-w[q:2 * q], w[0:q], -w[3 * q:4 * q], w[2 * q:3 * q]], axis=0)


def _pack_w_in_kernel(wt_ref, o_ref):
    lo = 2 * D_FOURIER + Q_RANK + KV_RANK
    o_ref[:lo] = wt_ref[:lo].astype(BF16)
    kr = wt_ref[lo:lo + QK_ROPE]
    zeros = jnp.zeros((QK_ROPE, D_MODEL), F32)
    o_ref[lo:lo + LANES] = jnp.concatenate([_rot_rows(kr), zeros, kr, zeros], axis=0).astype(BF16)
    o_ref[lo + LANES:] = wt_ref[lo + QK_ROPE:].astype(BF16)


def _pack_w_in(w_in_t):
    return pl.pallas_call(
        _pack_w_in_kernel,
        out_shape=jax.ShapeDtypeStruct((D_IN_PAD, D_MODEL), BF16),
        grid=(1,),
        in_specs=[_const_spec((D_IN, D_MODEL))],
        out_specs=_const_spec((D_IN_PAD, D_MODEL)),
        compiler_params=_params(1),
        name="pack_w_in",
    )(w_in_t)


def _sq_norms_row(xb):
    return _dot_nt(jnp.ones((8, HEAD_PAD), BF16), xb * xb)[0:1]


def _kv_outputs(ckvn_b, kr_b, w_k_ref, w_uvt_ref, ones_ref, k_ref, vt_ref, kmax_ref=None):
    tm = ckvn_b.shape[0]
    kp = _dot(jnp.concatenate([ckvn_b, kr_b], axis=1), w_k_ref[...])
    kmax = []
    for hh in range(N_HEADS):
        kh = kp[:, hh * HEAD_PAD:(hh + 1) * HEAD_PAD].astype(BF16)
        k_ref[hh] = kh
        if kmax_ref is not None:
            kmax.append(jnp.broadcast_to(jnp.max(_sq_norms_row(kh), axis=1, keepdims=True), (1, LANES)))
    ones = jnp.concatenate([ones_ref[...]] * (tm // LANES), axis=1)
    vt_ref[...] = (_dot_nt(w_uvt_ref[...], ckvn_b) + ones).astype(BF16)
    return jnp.concatenate(kmax, axis=0) if kmax else None


def _layer_in_kernel(*refs, rope, state, tiles_per_seq):
    (x_ref, mod_ref, w_uf_ref, w_mid_ref, qg_ref, kvg_ref, w_uq_ref, w_k_ref, w_uvt_ref, ones_ref,
     cs_ref), refs = refs[:11], refs[11:]
    if rope:
        (cos_ref, sin_ref), refs = refs[:2], refs[2:]
    (ucs_ref, q_ref, k_ref, vt_ref), refs = refs[:4], refs[4:]
    if state:
        ckvn_ref, krope_ref = refs
    if rope:
        qn2_ref, kmax_ref = refs

    h = _modulated(x_ref, mod_ref)

    u = _dot_nt(h, w_uf_ref[...]).astype(BF16)
    for g in range(N_GROUPS):
        r = _dot(u[:, g * GROUP:(g + 1) * GROUP], cs_ref[...])
        ucs_ref[:, g * GROUP:(g + 1) * GROUP] = r[:, :GROUP].astype(BF16)
        ucs_ref[:, D_FOURIER + g * GROUP:D_FOURIER + (g + 1) * GROUP] = r[:, GROUP:].astype(BF16)

    mid = _dot_nt(h, w_mid_ref[...])
    cq, ckv, kr = mid[:, :Q_RANK], mid[:, Q_RANK:Q_RANK + KV_RANK], mid[:, Q_RANK + KV_RANK:]
    cqn = (_rms(cq) * (qg_ref[...] * Q_SCALE)).astype(BF16)
    ckvn = _rms(ckv) * kvg_ref[...]
    if state:
        ckvn_ref[...] = ckvn
        krope_ref[...] = kr[:, ROPE_LANE:ROPE_LANE + QK_ROPE]
    n_q = N_HEADS * HEAD_PAD
    if rope:
        cos, sin = cos_ref[...], sin_ref[...]
        kr = kr * cos + pltpu.roll(kr, ROPE_LANE, 1) * sin
        qp = _dot(cqn, w_uq_ref[...])
    else:
        qp = _dot(cqn, w_uq_ref[:, :n_q])
    for hh in range(N_HEADS):
        qh = qp[:, hh * HEAD_PAD:(hh + 1) * HEAD_PAD]
        if rope:
            qh = qh * cos + qp[:, n_q + hh * HEAD_PAD:n_q + (hh + 1) * HEAD_PAD] * sin
        qh = qh.astype(BF16)
        q_ref[hh] = qh
        if rope:
            qn2_ref[hh:hh + 1, :] = _sq_norms_row(qh)
    kmax = _kv_outputs(ckvn.astype(BF16), kr.astype(BF16), w_k_ref, w_uvt_ref, ones_ref, k_ref, vt_ref,
                       kmax_ref if rope else None)
    if rope:
        first = pl.program_id(0) % tiles_per_seq == 0

        @pl.when(first)
        def _():
            kmax_ref[0] = kmax

        @pl.when(jnp.logical_not(first))
        def _():
            kmax_ref[0] = jnp.maximum(kmax, kmax_ref[0])


def _layer_in(x, mod3, w_in_p, wts, rope_tabs, *, n, state):
    t = x.shape[0]
    rope = rope_tabs is not None
    tm, per_seq, mod_row = _token_tiling(t, n, per_sequence=rope)
    tok = lambda w: pl.BlockSpec((tm, w), lambda i: (i, 0))
    in_specs = [tok(D_MODEL),
                pl.BlockSpec((1, 1, 3 * D_MODEL), lambda i: (mod_row(i), 0, 0)),
                _row_spec(W_BLK, B_UF), _row_spec(W_BLK, B_MID)]
    in_specs += [_const_spec(w.shape) for w in wts]
    args = [x, mod3, w_in_p, w_in_p, *wts]
    if rope:
        in_specs += [pl.BlockSpec((tm, HEAD_PAD), lambda i: (i % per_seq, 0))] * 2
        args += list(rope_tabs)
    head = pl.BlockSpec((N_HEADS, tm, HEAD_PAD), lambda i: (0, i, 0))
    out_specs = [tok(2 * D_FOURIER), head, head, pl.BlockSpec((VT_ROWS, tm), lambda i: (0, i))]
    out_shape = [jax.ShapeDtypeStruct((t, 2 * D_FOURIER), BF16),
                 jax.ShapeDtypeStruct((N_HEADS, t, HEAD_PAD), BF16),
                 jax.ShapeDtypeStruct((N_HEADS, t, HEAD_PAD), BF16),
                 jax.ShapeDtypeStruct((VT_ROWS, t), BF16)]
    if state:
        out_specs += [tok(KV_RANK), tok(QK_ROPE)]
        out_shape += [jax.ShapeDtypeStruct((t, KV_RANK), F32), jax.ShapeDtypeStruct((t, QK_ROPE), F32)]
    if rope:
        out_specs += [pl.BlockSpec((N_HEADS, tm), lambda i: (0, i)),
                      pl.BlockSpec((1, N_HEADS, LANES), lambda i: (i // per_seq, 0, 0))]
        out_shape += [jax.ShapeDtypeStruct((N_HEADS, t), F32),
                      jax.ShapeDtypeStruct((t // n, N_HEADS, LANES), F32)]
    return pl.pallas_call(
        functools.partial(_layer_in_kernel, rope=rope, state=state, tiles_per_seq=per_seq),
        out_shape=out_shape,
        grid=(t // tm,),
        in_specs=in_specs,
        out_specs=out_specs,
        compiler_params=_params(1),
        name="layer_in_latent" if rope else "layer_in_context",
    )(*args)


def _cache_kv_kernel(ckv_ref, kr_ref, w_k_ref, w_uvt_ref, ones_ref, k_ref, vt_ref, kmax_ref):
    kmax_ref[0] = _kv_outputs(ckv_ref[...].astype(BF16), kr_ref[...].astype(BF16), w_k_ref, w_uvt_ref,
                              ones_ref, k_ref, vt_ref, kmax_ref)


def _cache_kv(ckv, krope_pad, w_k, w_uvt, ones_col, *, m):
    t = ckv.shape[0]
    return pl.pallas_call(
        _cache_kv_kernel,
        out_shape=[jax.ShapeDtypeStruct((N_HEADS, t, HEAD_PAD), BF16),
                   jax.ShapeDtypeStruct((VT_ROWS, t), BF16),
                   jax.ShapeDtypeStruct((t // m, N_HEADS, LANES), F32)],
        grid=(t // m,),
        in_specs=[pl.BlockSpec((m, KV_RANK), lambda i: (i, 0)),
                  pl.BlockSpec((m, HEAD_PAD), lambda i: (i, 0)),
                  _const_spec(w_k.shape), _const_spec(w_uvt.shape), _const_spec(ones_col.shape)],
        out_specs=[pl.BlockSpec((N_HEADS, m, HEAD_PAD), lambda i: (0, i, 0)),
                   pl.BlockSpec((VT_ROWS, m), lambda i: (0, i)),
                   pl.BlockSpec((1, N_HEADS, LANES), lambda i: (i, 0, 0))],
        compiler_params=_params(1),
        name="cache_kv",
    )(ckv, krope_pad, w_k, w_uvt, ones_col)


def _pos_dft_kernel(cn_ref, sn_ref, ucs_ref, f_ref):
    f = _dot(cn_ref[...], ucs_ref[:, :D_FOURIER]) + _dot(sn_ref[...], ucs_ref[:, D_FOURIER:])
    f_ref[...] = f.astype(BF16)


def _pos_dft(cn, snn, ucs, *, n, tm):
    t = ucs.shape[0]
    per_seq = n // tm
    return pl.pallas_call(
        _pos_dft_kernel,
        out_shape=jax.ShapeDtypeStruct((t, D_FOURIER), BF16),
        grid=(per_seq, t // n),
        in_specs=[pl.BlockSpec((tm, n), lambda i, bi: (i, 0)),
                  pl.BlockSpec((tm, n), lambda i, bi: (i, 0)),
                  pl.BlockSpec((n, 2 * D_FOURIER), lambda i, bi: (bi, 0))],
        out_specs=pl.BlockSpec((tm, D_FOURIER), lambda i, bi: (bi * per_seq + i, 0)),
        compiler_params=_params(2),
        name=f"pos_dft_{n}",
    )(cn, snn, ucs)


def _attention_kernel(*refs, n_kv, bound):
    q_ref, refs = refs[0], refs[1:]
    k_refs, vt_refs, refs = refs[:n_kv], refs[n_kv:2 * n_kv], refs[2 * n_kv:]
    if bound:
        qn2_ref, kmax_refs, refs = refs[0], refs[1:1 + n_kv], refs[1 + n_kv:]
    o_ref, s_ref, pp_ref, ot_ref = refs
    tq = q_ref.shape[1]
    chunks, base = [], 0
    for j, k_ref in enumerate(k_refs):
        chunks += [(j, c, base + c) for c in range(0, k_ref.shape[1], KEY_CHUNK)]
        base += k_ref.shape[1]

    def finish(hh, ot):
        l = ot[V_DIM:V_DIM + 1]
        ot_ref[pl.ds(pl.multiple_of(hh * V_DIM, V_DIM), V_DIM), :] = ot[:V_DIM] / l
        return l

    def shifted_softmax_pv(score_head, pv_head):
        if score_head is not None:
            qh = q_ref[score_head]
            kmax = functools.reduce(jnp.maximum, [r[0, score_head:score_head + 1, :] for r in kmax_refs])
            shift = jnp.sqrt(qn2_ref[score_head:score_head + 1, :]
                             * jnp.concatenate([kmax] * (tq // LANES), axis=1)) * BOUND_SLACK
        ot = None
        for j, c, r in chunks:
            if score_head is not None:
                s = _dot_nt(k_refs[j][score_head, c:c + KEY_CHUNK, :], qh)
                pp_ref[score_head % 2, r:r + KEY_CHUNK, :] = jnp.exp2(s - shift).astype(BF16)
            if pv_head is not None:
                part = _dot(vt_refs[j][pv_head * V_ROWS:(pv_head + 1) * V_ROWS, c:c + KEY_CHUNK],
                            pp_ref[pv_head % 2, r:r + KEY_CHUNK, :])
                ot = part if ot is None else ot + part
        return finish(pv_head, ot) if pv_head is not None else None

    def exact_scores(hh, slot):
        qh = q_ref[hh]
        m8 = None
        for j, c, r in chunks:
            s = _dot_nt(k_refs[j][hh, c:c + KEY_CHUNK, :], qh)
            s_ref[slot, r:r + KEY_CHUNK, :] = s
            mc = jnp.max(s.reshape(KEY_CHUNK // 8, 8, tq), axis=0)
            m8 = mc if m8 is None else jnp.maximum(m8, mc)
        return jnp.max(m8, axis=0, keepdims=True)

    def exact_softmax_pv(hh, slot, m):
        for _, _, r in chunks:
            pp_ref[slot, r:r + KEY_CHUNK, :] = jnp.exp2(s_ref[slot, r:r + KEY_CHUNK, :] - m).astype(BF16)
        rows = pl.ds(pl.multiple_of(hh * V_ROWS, V_ROWS), V_ROWS)
        ot, base = None, 0
        for vt_ref in vt_refs:
            mk = vt_ref.shape[1]
            part = _dot(vt_ref[rows, :], pp_ref[slot, base:base + mk, :])
            ot = part if ot is None else ot + part
            base += mk
        finish(hh, ot)

    if not bound:
        maxima = [exact_scores(hh, hh) for hh in range(N_HEADS)]
        for hh in range(N_HEADS):
            exact_softmax_pv(hh, hh, maxima[hh])
    else:
        l_min = None
        for hh in range(N_HEADS + 1):
            l = shifted_softmax_pv(hh if hh < N_HEADS else None, hh - 1 if hh > 0 else None)
            if l is not None:
                l_min = l if l_min is None else jnp.minimum(l_min, l)

        @pl.when(jnp.logical_not(jnp.min(l_min) >= MIN_DENOMINATOR))
        def _():
            def body(hh, carry):
                exact_softmax_pv(hh, 0, exact_scores(hh, 0))
                return carry

            lax.fori_loop(0, N_HEADS, body, 0)

    o_ref[...] = ot_ref[...].T.astype(BF16)


def _attention(q, ks, vts, key_lens, *, n, tq, bound=None):
    t = q.shape[1]
    per_seq = n // tq
    m_tot = sum(key_lens)
    in_specs = [pl.BlockSpec((N_HEADS, tq, HEAD_PAD), lambda bi, i: (0, bi * per_seq + i, 0))]
    in_specs += [pl.BlockSpec((N_HEADS, m, HEAD_PAD), lambda bi, i: (0, bi, 0)) for m in key_lens]
    in_specs += [pl.BlockSpec((VT_ROWS, m), lambda bi, i: (0, bi)) for m in key_lens]
    args = [q, *ks, *vts]
    if bound is not None:
        qn2, kmaxes = bound
        in_specs += [pl.BlockSpec((N_HEADS, tq), lambda bi, i: (0, bi * per_seq + i))]
        in_specs += [pl.BlockSpec((1, N_HEADS, LANES), lambda bi, i: (bi, 0, 0)) for _ in kmaxes]
        args += [qn2, *kmaxes]
    s_slots, p_slots = (1, 2) if bound is not None else (N_HEADS, N_HEADS)
    return pl.pallas_call(
        functools.partial(_attention_kernel, n_kv=len(ks), bound=bound is not None),
        out_shape=jax.ShapeDtypeStruct((t, D_ATTN), BF16),
        grid=(t // n, per_seq),
        in_specs=in_specs,
        out_specs=pl.BlockSpec((tq, D_ATTN), lambda bi, i: (bi * per_seq + i, 0)),
        scratch_shapes=[pltpu.VMEM((s_slots, m_tot, tq), F32), pltpu.VMEM((p_slots, m_tot, tq), BF16),
                        pltpu.VMEM((D_ATTN, tq), F32)],
        compiler_params=_params(2),
        name=f"attention_{n}",
    )(*args)


def _merge_kernel(x_ref, mod_ref, f_ref, a_ref, w_zf_ref, w_za_ref, w_g_ref, w_f_ref, w_a_ref, w_o_ref,
                  g_ref, y_ref):
    h = _modulated(x_ref, mod_ref)
    zf = _dot_nt(h, w_zf_ref[...])
    y_f = _dot((f_ref[...] * (zf * _sigmoid(zf))).astype(BF16), w_f_ref[...])
    za = _dot_nt(h, w_za_ref[...])
    y_a = _dot((a_ref[...] * (za * _sigmoid(za))).astype(BF16), w_a_ref[...])
    merged = (_sigmoid(_dot_nt(h, w_g_ref[:D_MODEL])) * y_f
              + _sigmoid(_dot_nt(h, w_g_ref[D_MODEL:])) * y_a)
    gate = mod_ref[0][:, 2 * D_MODEL:]
    out = x_ref[...] + gate * _dot(merged.astype(BF16), w_o_ref[...])
    y_ref[...] = _rms(out) * g_ref[...]


def _merge(x, mod3, f, attn, w_in_p, w_f, w_a, w_o, g, *, n, latent):
    t = x.shape[0]
    tm, _, mod_row = _token_tiling(t, n, per_sequence=latent)
    tok = lambda w: pl.BlockSpec((tm, w), lambda i: (i, 0))
    return pl.pallas_call(
        _merge_kernel,
        out_shape=jax.ShapeDtypeStruct((t, D_MODEL), F32),
        grid=(t // tm,),
        in_specs=[tok(D_MODEL),
                  pl.BlockSpec((1, 1, 3 * D_MODEL), lambda i: (mod_row(i), 0, 0)),
                  tok(D_FOURIER), tok(D_ATTN),
                  _row_spec(W_BLK, B_ZF), _row_spec(W_BLK, B_ZA), _row_spec(2 * D_MODEL, 1),
                  _const_spec(w_f.shape), _const_spec(w_a.shape), _const_spec(w_o.shape),
                  _const_spec(g.shape)],
        out_specs=tok(D_MODEL),
        compiler_params=_params(1),
        name=f"merge_{n}",
    )(x, mod3, f, attn, w_in_p, w_in_p, w_in_p, w_f, w_a, w_o, g)


def _dft_tables(n):
    norm = 1.0 / math.sqrt(n)
    if n <= 256:
        k = jnp.arange(n, dtype=jnp.int32)
        ang = ((k[:, None] * k[None, :]) % n).astype(F32) * (2.0 * math.pi / n)
        return (jnp.cos(ang) * norm).astype(BF16), (jnp.sin(ang) * -norm).astype(BF16)
    lo = 64
    hi = n // lo
    pos = jnp.arange(n, dtype=jnp.int32)
    a1 = ((jnp.arange(hi, dtype=jnp.int32)[:, None] * pos[None, :]) % hi).astype(F32) * (2.0 * math.pi / hi)
    a0 = ((jnp.arange(lo, dtype=jnp.int32)[:, None] * pos[None, :]) % n).astype(F32) * (2.0 * math.pi / n)
    c1, s1 = jnp.cos(a1)[:, None, :], jnp.sin(a1)[:, None, :]
    c0, s0 = jnp.cos(a0)[None, :, :] * norm, jnp.sin(a0)[None, :, :] * norm
    cn = (c1 * c0 - s1 * s0).reshape(n, n)
    sn = (s1 * c0 + c1 * s0).reshape(n, n)
    return cn.astype(BF16), (-sn).astype(BF16)


def _channel_dft_table():
    c = jnp.arange(GROUP, dtype=jnp.int32)
    ang = ((c[:, None] * c[None, :]) % GROUP).astype(F32) * (2.0 * math.pi / GROUP)
    norm = 1.0 / math.sqrt(GROUP)
    return (jnp.concatenate([jnp.cos(ang), jnp.sin(ang)], axis=1) * norm).astype(BF16)


def _rope_tables(n):
    t = jnp.arange(n)
    row = (t // GRID_W).astype(F32)
    col = (t % GRID_W).astype(F32)
    half = QK_ROPE // 2
    inv = ROPE_THETA ** (-jnp.arange(0, half, 2, dtype=F32) / half)
    ar, ac = row[:, None] * inv, col[:, None] * inv
    ang = jnp.concatenate([ar, ar, ac, ac], axis=-1)
    pad = lambda a, fill: jnp.concatenate(
        [jnp.full((n, ROPE_LANE), fill, F32), a, jnp.full((n, HEAD_PAD - ROPE_LANE - QK_ROPE), fill, F32)], axis=1)
    return pad(jnp.cos(ang), 1.0), pad(jnp.sin(ang), 0.0)


def kernel(x_prompt, x_sample, cache_ckv, cache_krope, c, c_ctx, w_ada, b_ada, w_in, q_norm_g, w_uq,
           kv_norm_g, w_ukv, w_f_out, w_a_out, w_out, final_norm_g):
    assert w_in.shape[0] == 1
    b_ctx, n_ctx, _ = x_prompt.shape
    dec_b, n_lat, _ = x_sample.shape
    past = cache_ckv.shape[2]

    w_q3 = w_uq[0].reshape(Q_RANK, N_HEADS, QK_NOPE + QK_ROPE)
    head_pad = lambda a, left: jnp.pad(
        a, ((0, 0), (0, 0), (left, HEAD_PAD - left - a.shape[2]))).reshape(Q_RANK, N_HEADS * HEAD_PAD)
    rp = w_q3[:, :, QK_NOPE:].reshape(Q_RANK, N_HEADS, 2, 2, QK_ROPE // 4)
    w_q_rot = jnp.stack([-rp[:, :, :, 1], rp[:, :, :, 0]], axis=3).reshape(Q_RANK, N_HEADS, QK_ROPE)
    w_uq_p = jnp.concatenate([head_pad(w_q3, 0), head_pad(w_q_rot, ROPE_LANE)], axis=1).astype(BF16)
    w_kv3 = w_ukv[0].reshape(KV_RANK, N_HEADS, QK_NOPE + V_DIM)
    w_knope = jnp.pad(w_kv3[:, :, :QK_NOPE], ((0, 0), (0, 0), (0, HEAD_PAD - QK_NOPE)))
    place = np.zeros((HEAD_PAD, N_HEADS, HEAD_PAD), np.float32)
    for j in range(QK_ROPE):
        place[ROPE_LANE + j, :, ROPE_LANE + j] = 1.0
    w_k = jnp.concatenate([w_knope.reshape(KV_RANK, -1), jnp.asarray(place).reshape(HEAD_PAD, -1)],
                          axis=0).astype(BF16)
    w_uvt = jnp.pad(jnp.transpose(w_kv3[:, :, QK_NOPE:], (1, 2, 0)),
                    ((0, 0), (0, V_ROWS - V_DIM), (0, 0))).reshape(VT_ROWS, KV_RANK).astype(BF16)
    ones_col = np.zeros((N_HEADS, V_ROWS, LANES), np.float32)
    ones_col[:, V_DIM, :] = 1.0
    ones_col = jnp.asarray(ones_col.reshape(VT_ROWS, LANES))
    w_f = w_f_out[0].astype(BF16)
    w_a = w_a_out[0].astype(BF16)
    w_o = w_out[0].astype(BF16)
    qg = q_norm_g[0].reshape(1, Q_RANK)
    kvg = kv_norm_g[0].reshape(1, KV_RANK)
    fg = final_norm_g.reshape(1, D_MODEL)
    wts = (qg, kvg, w_uq_p, w_k, w_uvt, ones_col, _channel_dft_table())
    w_in_p = _pack_w_in(jnp.swapaxes(w_in[0], 0, 1))

    cond8 = jnp.concatenate([c_ctx[None, :], c, jnp.zeros((8 - 1 - dec_b, D_MODEL), F32)], axis=0)
    mod3 = _adaln(cond8, w_ada[0], b_ada[0].reshape(1, -1)).reshape(8, 1, 3 * D_MODEL)

    xp = x_prompt.reshape(b_ctx * n_ctx, D_MODEL)
    ucs, q, k, vt, state_ckv, state_krope = _layer_in(xp, mod3, w_in_p, wts, None, n=n_ctx, state=True)
    f = _pos_dft(*_dft_tables(n_ctx), ucs, n=n_ctx, tm=n_ctx)
    attn = _attention(q, [k], [vt], [n_ctx], n=n_ctx, tq=n_ctx)
    y_prompt = _merge(xp, mod3, f, attn, w_in_p, w_f, w_a, w_o, fg, n=n_ctx, latent=False)

    xs = x_sample.reshape(dec_b * n_lat, D_MODEL)
    ucs, q, k, vt, qn2, kmax = _layer_in(xs, mod3, w_in_p, wts, _rope_tables(n_lat), n=n_lat, state=False)
    f = _pos_dft(*_dft_tables(n_lat), ucs, n=n_lat, tm=1024)
    krope_pad = jnp.pad(cache_krope[:, 0], ((0, 0), (0, 0), (ROPE_LANE, HEAD_PAD - ROPE_LANE - QK_ROPE)))
    k_c, vt_c, kmax_c = _cache_kv(cache_ckv[:, 0].reshape(dec_b * past, KV_RANK),
                                  krope_pad.reshape(dec_b * past, HEAD_PAD), w_k, w_uvt, ones_col, m=past)
    attn = _attention(q, [k, k_c], [vt, vt_c], [n_lat, past], n=n_lat, tq=256, bound=(qn2, [kmax, kmax_c]))
    y_sample = _merge(xs, mod3, f, attn, w_in_p, w_f, w_a, w_o, fg, n=n_lat, latent=True)

    return (y_prompt.reshape(b_ctx, n_ctx, D_MODEL), y_sample.reshape(dec_b, n_lat, D_MODEL),
            state_ckv.reshape(b_ctx, 1, n_ctx, KV_RANK), state_krope.reshape(b_ctx, 1, n_ctx, QK_ROPE))
```

```python
import functools
import math

import jax
import jax.numpy as jnp
import numpy as np
from jax import lax
from jax.experimental import pallas as pl
from jax.experimental.pallas import tpu as pltpu

F32 = jnp.float32
BF16 = jnp.bfloat16

D_MODEL = 1024
GRID_W = 64
N_GROUPS = 4
GROUP = 128
D_FOURIER = N_GROUPS * GROUP
N_HEADS = 8
QK_NOPE = 64
QK_ROPE = 32
V_DIM = 64
Q_RANK = 256
KV_RANK = 128
D_ATTN = N_HEADS * V_DIM
D_IN = 2 * D_FOURIER + Q_RANK + KV_RANK + QK_ROPE + D_ATTN + 2 * D_MODEL
ROPE_THETA = 10000.0
EPS = 1e-6
LANES = 128
HEAD_PAD = LANES
ROPE_LANE = QK_NOPE
V_ROWS = V_DIM + 16
VT_ROWS = N_HEADS * V_ROWS
Q_SCALE = (QK_NOPE + QK_ROPE) ** -0.5 * math.log2(math.e)
KEY_CHUNK = 256
REV_BLOCK = 256
BOUND_SLACK = 1.0 + 2.0 ** -6
MIN_DENOMINATOR = 2.0 ** -60
W_BLK = 512
B_UF, B_ZF, B_MID, B_ZA = 0, 1, 2, 3
D_IN_PAD = 8 * W_BLK
TOKEN_TILE = 512
VMEM_LIMIT_BYTES = 56 * 1024 * 1024


def _const_spec(shape):
    nd = len(shape)
    return pl.BlockSpec(shape, lambda *_: (0,) * nd, pipeline_mode=pl.Buffered(1))


def _row_spec(rows, blk):
    return pl.BlockSpec((rows, D_MODEL), lambda *_: (blk, 0), pipeline_mode=pl.Buffered(1))


def _params(n_axes):
    return pltpu.CompilerParams(dimension_semantics=("arbitrary",) * n_axes,
                                vmem_limit_bytes=VMEM_LIMIT_BYTES)


def _rms(x):
    return x * lax.rsqrt(jnp.mean(x * x, axis=-1, keepdims=True) + EPS)


def _sigmoid(x):
    return 1.0 / (1.0 + jnp.exp(-x))


def _dot(a, b):
    return jnp.dot(a, b, preferred_element_type=F32)


def _dot_nt(a, b):
    return lax.dot_general(a, b, (((1,), (1,)), ((), ())), preferred_element_type=F32)


def _token_tiling(t, n, *, per_sequence):
    tm = TOKEN_TILE
    assert t % tm == 0 and (not per_sequence or n % tm == 0)
    per_seq = n // tm if per_sequence else None
    mod_row = (lambda i: 1 + i // per_seq) if per_sequence else (lambda i: 0)
    return tm, per_seq, mod_row


def _modulated(x_ref, mod_ref):
    mod = mod_ref[0]
    shift, scale = mod[:, 0:D_MODEL], mod[:, D_MODEL:2 * D_MODEL]
    return (_rms(x_ref[...]) * (1.0 + scale) + shift).astype(BF16)


def _adaln_kernel(c_ref, w_ref, b_ref, o_ref):
    c = c_ref[...]
    s = (c * _sigmoid(c)).astype(BF16)
    o_ref[...] = _dot(s, w_ref[...].astype(BF16)) + b_ref[...]


def _adaln(cond8, w_ada, b_ada):
    n_blk = 4
    bw = 3 * D_MODEL // n_blk
    return pl.pallas_call(
        _adaln_kernel,
        out_shape=jax.ShapeDtypeStruct((8, 3 * D_MODEL), F32),
        grid=(n_blk,),
        in_specs=[pl.BlockSpec((8, D_MODEL), lambda j: (0, 0)),
                  pl.BlockSpec((D_MODEL, bw), lambda j: (0, j)),
                  pl.BlockSpec((1, bw), lambda j: (0, j))],
        out_specs=pl.BlockSpec((8, bw), lambda j: (0, j)),
        compiler_params=_params(1),
        name="adaln",
    )(cond8, w_ada, b_ada)


def _rot_rows(w):
    q = QK_ROPE // 4
    return jnp.concatenate([-w[q:2 * q], w[0:q], -w[3 * q:4 * q], w[2 * q:3 * q]], axis=0)


def _pack_w_in_kernel(wt_ref, o_ref):
    lo = 2 * D_FOURIER + Q_RANK + KV_RANK
    o_ref[:lo] = wt_ref[:lo].astype(BF16)
    kr = wt_ref[lo:lo + QK_ROPE]
    zeros = jnp.zeros((QK_ROPE, D_MODEL), F32)
    o_ref[lo:lo + LANES] = jnp.concatenate([_rot_rows(kr), zeros, kr, zeros], axis=0).astype(BF16)
    o_ref[lo + LANES:] = wt_ref[lo + QK_ROPE:].astype(BF16)


def _pack_w_in(w_in_t):
    return pl.pallas_call(
        _pack_w_in_kernel,
        out_shape=jax.ShapeDtypeStruct((D_IN_PAD, D_MODEL), BF16),
        grid=(1,),
        in_specs=[_const_spec((D_IN, D_MODEL))],
        out_specs=_const_spec((D_IN_PAD, D_MODEL)),
        compiler_params=_params(1),
        name="pack_w_in",
    )(w_in_t)


def _sq_norms_row(xb):
    return _dot_nt(jnp.ones((8, HEAD_PAD), BF16), xb * xb)[0:1]


def _kv_outputs(ckvn_b, kr_b, w_k_ref, w_uvt_ref, ones_ref, k_ref, vt_ref, kmax_ref=None):
    tm = ckvn_b.shape[0]
    kp = _dot(jnp.concatenate([ckvn_b, kr_b], axis=1), w_k_ref[...])
    kmax = []
    for hh in range(N_HEADS):
        kh = kp[:, hh * HEAD_PAD:(hh + 1) * HEAD_PAD].astype(BF16)
        k_ref[hh] = kh
        if kmax_ref is not None:
            kmax.append(jnp.broadcast_to(jnp.max(_sq_norms_row(kh), axis=1, keepdims=True), (1, LANES)))
    ones = jnp.concatenate([ones_ref[...]] * (tm // LANES), axis=1)
    vt_ref[...] = (_dot_nt(w_uvt_ref[...], ckvn_b) + ones).astype(BF16)
    return jnp.concatenate(kmax, axis=0) if kmax else None


def _layer_in_kernel(*refs, rope, state, tiles_per_seq):
    (x_ref, mod_ref, w_uf_ref, w_mid_ref, qg_ref, kvg_ref, w_uq_ref, w_k_ref, w_uvt_ref, ones_ref,
     cs_ref), refs = refs[:11], refs[11:]
    if rope:
        (cos_ref, sin_ref), refs = refs[:2], refs[2:]
    (ucs_ref, q_ref, k_ref, vt_ref), refs = refs[:4], refs[4:]
    if state:
        ckvn_ref, krope_ref = refs
    if rope:
        qn2_ref, kmax_ref = refs

    h = _modulated(x_ref, mod_ref)

    u = _dot_nt(h, w_uf_ref[...]).astype(BF16)
    for g in range(N_GROUPS):
        r = _dot(u[:, g * GROUP:(g + 1) * GROUP], cs_ref[...])
        ucs_ref[:, g * GROUP:(g + 1) * GROUP] = r[:, :GROUP].astype(BF16)
        ucs_ref[:, D_FOURIER + g * GROUP:D_FOURIER + (g + 1) * GROUP] = r[:, GROUP:].astype(BF16)

    mid = _dot_nt(h, w_mid_ref[...])
    cq, ckv, kr = mid[:, :Q_RANK], mid[:, Q_RANK:Q_RANK + KV_RANK], mid[:, Q_RANK + KV_RANK:]
    cqn = (_rms(cq) * (qg_ref[...] * Q_SCALE)).astype(BF16)
    ckvn = _rms(ckv) * kvg_ref[...]
    if state:
        ckvn_ref[...] = ckvn
        krope_ref[...] = kr[:, ROPE_LANE:ROPE_LANE + QK_ROPE]
    n_q = N_HEADS * HEAD_PAD
    if rope:
        cos, sin = cos_ref[...], sin_ref[...]
        kr = kr * cos + pltpu.roll(kr, ROPE_LANE, 1) * sin
        qp = _dot(cqn, w_uq_ref[...])
    else:
        qp = _dot(cqn, w_uq_ref[:, :n_q])
    for hh in range(N_HEADS):
        qh = qp[:, hh * HEAD_PAD:(hh + 1) * HEAD_PAD]
        if rope:
            qh = qh * cos + qp[:, n_q + hh * HEAD_PAD:n_q + (hh + 1) * HEAD_PAD] * sin
        qh = qh.astype(BF16)
        q_ref[hh] = qh
        if rope:
            qn2_ref[hh:hh + 1, :] = _sq_norms_row(qh)
    kmax = _kv_outputs(ckvn.astype(BF16), kr.astype(BF16), w_k_ref, w_uvt_ref, ones_ref, k_ref, vt_ref,
                       kmax_ref if rope else None)
    if rope:
        first = pl.program_id(0) % tiles_per_seq == 0

        @pl.when(first)
        def _():
            kmax_ref[0] = kmax

        @pl.when(jnp.logical_not(first))
        def _():
            kmax_ref[0] = jnp.maximum(kmax, kmax_ref[0])


def _layer_in(x, mod3, w_in_p, wts, rope_tabs, *, n, state):
    t = x.shape[0]
    rope = rope_tabs is not None
    tm, per_seq, mod_row = _token_tiling(t, n, per_sequence=rope)
    tok = lambda w: pl.BlockSpec((tm, w), lambda i: (i, 0))
    in_specs = [tok(D_MODEL),
                pl.BlockSpec((1, 1, 3 * D_MODEL), lambda i: (mod_row(i), 0, 0)),
                _row_spec(W_BLK, B_UF), _row_spec(W_BLK, B_MID)]
    in_specs += [_const_spec(w.shape) for w in wts]
    args = [x, mod3, w_in_p, w_in_p, *wts]
    if rope:
        in_specs += [pl.BlockSpec((tm, HEAD_PAD), lambda i: (i % per_seq, 0))] * 2
        args += list(rope_tabs)
    head = pl.BlockSpec((N_HEADS, tm, HEAD_PAD), lambda i: (0, i, 0))
    out_specs = [tok(2 * D_FOURIER), head, head, pl.BlockSpec((VT_ROWS, tm), lambda i: (0, i))]
    out_shape = [jax.ShapeDtypeStruct((t, 2 * D_FOURIER), BF16),
                 jax.ShapeDtypeStruct((N_HEADS, t, HEAD_PAD), BF16),
                 jax.ShapeDtypeStruct((N_HEADS, t, HEAD_PAD), BF16),
                 jax.ShapeDtypeStruct((VT_ROWS, t), BF16)]
    if state:
        out_specs += [tok(KV_RANK), tok(QK_ROPE)]
        out_shape += [jax.ShapeDtypeStruct((t, KV_RANK), F32), jax.ShapeDtypeStruct((t, QK_ROPE), F32)]
    if rope:
        out_specs += [pl.BlockSpec((N_HEADS, tm), lambda i: (0, i)),
                      pl.BlockSpec((1, N_HEADS, LANES), lambda i: (i // per_seq, 0, 0))]
        out_shape += [jax.ShapeDtypeStruct((N_HEADS, t), F32),
                      jax.ShapeDtypeStruct((t // n, N_HEADS, LANES), F32)]
    return pl.pallas_call(
        functools.partial(_layer_in_kernel, rope=rope, state=state, tiles_per_seq=per_seq),
        out_shape=out_shape,
        grid=(t // tm,),
        in_specs=in_specs,
        out_specs=out_specs,
        compiler_params=_params(1),
        name="layer_in_latent" if rope else "layer_in_context",
    )(*args)


def _cache_kv_kernel(ckv_ref, kr_ref, w_k_ref, w_uvt_ref, ones_ref, k_ref, vt_ref, kmax_ref):
    kmax_ref[0] = _kv_outputs(ckv_ref[...].astype(BF16), kr_ref[...].astype(BF16), w_k_ref, w_uvt_ref,
                              ones_ref, k_ref, vt_ref, kmax_ref)


def _cache_kv(ckv, krope_pad, w_k, w_uvt, ones_col, *, m):
    t = ckv.shape[0]
    return pl.pallas_call(
        _cache_kv_kernel,
        out_shape=[jax.ShapeDtypeStruct((N_HEADS, t, HEAD_PAD), BF16),
                   jax.ShapeDtypeStruct((VT_ROWS, t), BF16),
                   jax.ShapeDtypeStruct((t // m, N_HEADS, LANES), F32)],
        grid=(t // m,),
        in_specs=[pl.BlockSpec((m, KV_RANK), lambda i: (i, 0)),
                  pl.BlockSpec((m, HEAD_PAD), lambda i: (i, 0)),
                  _const_spec(w_k.shape), _const_spec(w_uvt.shape), _const_spec(ones_col.shape)],
        out_specs=[pl.BlockSpec((N_HEADS, m, HEAD_PAD), lambda i: (0, i, 0)),
                   pl.BlockSpec((VT_ROWS, m), lambda i: (0, i)),
                   pl.BlockSpec((1, N_HEADS, LANES), lambda i: (i, 0, 0))],
        compiler_params=_params(1),
        name="cache_kv",
    )(ckv, krope_pad, w_k, w_uvt, ones_col)


def _pos_dft_direct_kernel(cn_ref, sn_ref, ucs_ref, f_ref):
    n = cn_ref.shape[0]
    for s in range(f_ref.shape[0] // n):
        rows = slice(s * n, (s + 1) * n)
        f = _dot(cn_ref[...], ucs_ref[rows, :D_FOURIER]) + _dot(sn_ref[...], ucs_ref[rows, D_FOURIER:])
        f_ref[rows] = f.astype(BF16)


def _pos_dft_half_kernel(cn_ref, sn_ref, rev_ref, ucs_ref, f_ref):
    half = f_ref.shape[0] // 2
    g = _dot(cn_ref[...], ucs_ref[:, :D_FOURIER])
    hn = _dot(sn_ref[...], ucs_ref[:, D_FOURIER:])
    f_ref[:half] = (g[:half] + hn[:half]).astype(BF16)
    mirrored = (g[1:half + 1] - hn[1:half + 1]).astype(BF16)
    blocks = half // REV_BLOCK
    for a in range(blocks):
        blk = mirrored[(blocks - 1 - a) * REV_BLOCK:(blocks - a) * REV_BLOCK]
        f_ref[half + a * REV_BLOCK:half + (a + 1) * REV_BLOCK] = _dot(rev_ref[...], blk).astype(BF16)


def _pos_dft(ucs, *, n):
    t = ucs.shape[0]
    out_shape = jax.ShapeDtypeStruct((t, D_FOURIER), BF16)
    if n <= REV_BLOCK:
        cn, snn = _dft_tables(n, n)
        seqs = min(4, t // n)
        return pl.pallas_call(
            _pos_dft_direct_kernel,
            out_shape=out_shape,
            grid=(t // (n * seqs),),
            in_specs=[_const_spec(cn.shape), _const_spec(snn.shape),
                      pl.BlockSpec((n * seqs, 2 * D_FOURIER), lambda i: (i, 0))],
            out_specs=pl.BlockSpec((n * seqs, D_FOURIER), lambda i: (i, 0)),
            compiler_params=_params(1),
            name=f"pos_dft_{n}",
        )(cn, snn, ucs)
    cn, snn = _dft_tables(n, n // 2 + 8)
    rev = jnp.asarray(np.eye(REV_BLOCK, dtype=np.float32)[::-1], dtype=BF16)
    return pl.pallas_call(
        _pos_dft_half_kernel,
        out_shape=out_shape,
        grid=(t // n,),
        in_specs=[_const_spec(cn.shape), _const_spec(snn.shape), _const_spec(rev.shape),
                  pl.BlockSpec((n, 2 * D_FOURIER), lambda i: (i, 0))],
        out_specs=pl.BlockSpec((n, D_FOURIER), lambda i: (i, 0)),
        compiler_params=_params(1),
        name=f"pos_dft_{n}",
    )(cn, snn, rev, ucs)


def _attention_kernel(*refs, n_kv, bound):
    q_ref, refs = refs[0], refs[1:]
    k_refs, vt_refs, refs = refs[:n_kv], refs[n_kv:2 * n_kv], refs[2 * n_kv:]
    if bound:
        qn2_ref, kmax_refs, refs = refs[0], refs[1:1 + n_kv], refs[1 + n_kv:]
    o_ref, s_ref, pp_ref, ot_ref = refs
    tq = q_ref.shape[1]
    chunks, base = [], 0
    for j, k_ref in enumerate(k_refs):
        chunks += [(j, c, base + c) for c in range(0, k_ref.shape[1], KEY_CHUNK)]
        base += k_ref.shape[1]

    def finish(hh, ot):
        l = ot[V_DIM:V_DIM + 1]
        ot_ref[pl.ds(pl.multiple_of(hh * V_DIM, V_DIM), V_DIM), :] = ot[:V_DIM] / l
        return l

    def shifted_softmax_pv(score_head, pv_head):
        if score_head is not None:
            qh = q_ref[score_head]
            kmax = functools.reduce(jnp.maximum, [r[0, score_head:score_head + 1, :] for r in kmax_refs])
            shift = jnp.sqrt(qn2_ref[score_head:score_head + 1, :]
                             * jnp.concatenate([kmax] * (tq // LANES), axis=1)) * BOUND_SLACK
        ot = None
        for j, c, r in chunks:
            if score_head is not None:
                s = _dot_nt(k_refs[j][score_head, c:c + KEY_CHUNK, :], qh)
                pp_ref[score_head % 2, r:r + KEY_CHUNK, :] = jnp.exp2(s - shift).astype(BF16)
            if pv_head is not None:
                part = _dot(vt_refs[j][pv_head * V_ROWS:(pv_head + 1) * V_ROWS, c:c + KEY_CHUNK],
                            pp_ref[pv_head % 2, r:r + KEY_CHUNK, :])
                ot = part if ot is None else ot + part
        return finish(pv_head, ot) if pv_head is not None else None

    def exact_scores(hh, slot):
        qh = q_ref[hh]
        m8 = None
        for j, c, r in chunks:
            s = _dot_nt(k_refs[j][hh, c:c + KEY_CHUNK, :], qh)
            s_ref[slot, r:r + KEY_CHUNK, :] = s
            mc = jnp.max(s.reshape(KEY_CHUNK // 8, 8, tq), axis=0)
            m8 = mc if m8 is None else jnp.maximum(m8, mc)
        return jnp.max(m8, axis=0, keepdims=True)

    def exact_softmax_pv(hh, slot, m):
        for _, _, r in chunks:
            pp_ref[slot, r:r + KEY_CHUNK, :] = jnp.exp2(s_ref[slot, r:r + KEY_CHUNK, :] - m).astype(BF16)
        rows = pl.ds(pl.multiple_of(hh * V_ROWS, V_ROWS), V_ROWS)
        ot, base = None, 0
        for vt_ref in vt_refs:
            mk = vt_ref.shape[1]
            part = _dot(vt_ref[rows, :], pp_ref[slot, base:base + mk, :])
            ot = part if ot is None else ot + part
            base += mk
        finish(hh, ot)

    if not bound:
        maxima = [exact_scores(hh, hh) for hh in range(N_HEADS)]
        for hh in range(N_HEADS):
            exact_softmax_pv(hh, hh, maxima[hh])
    else:
        l_min = None
        for hh in range(N_HEADS + 1):
            l = shifted_softmax_pv(hh if hh < N_HEADS else None, hh - 1 if hh > 0 else None)
            if l is not None:
                l_min = l if l_min is None else jnp.minimum(l_min, l)

        @pl.when(jnp.logical_not(jnp.min(l_min) >= MIN_DENOMINATOR))
        def _():
            def body(hh, carry):
                exact_softmax_pv(hh, 0, exact_scores(hh, 0))
                return carry

            lax.fori_loop(0, N_HEADS, body, 0)

    o_ref[...] = ot_ref[...].T.astype(BF16)


def _attention(q, ks, vts, key_lens, *, n, tq, bound=None):
    t = q.shape[1]
    per_seq = n // tq
    m_tot = sum(key_lens)
    in_specs = [pl.BlockSpec((N_HEADS, tq, HEAD_PAD), lambda bi, i: (0, bi * per_seq + i, 0))]
    in_specs += [pl.BlockSpec((N_HEADS, m, HEAD_PAD), lambda bi, i: (0, bi, 0)) for m in key_lens]
    in_specs += [pl.BlockSpec((VT_ROWS, m), lambda bi, i: (0, bi)) for m in key_lens]
    args = [q, *ks, *vts]
    if bound is not None:
        qn2, kmaxes = bound
        in_specs += [pl.BlockSpec((N_HEADS, tq), lambda bi, i: (0, bi * per_seq + i))]
        in_specs += [pl.BlockSpec((1, N_HEADS, LANES), lambda bi, i: (bi, 0, 0)) for _ in kmaxes]
        args += [qn2, *kmaxes]
    s_slots, p_slots = (1, 2) if bound is not None else (N_HEADS, N_HEADS)
    return pl.pallas_call(
        functools.partial(_attention_kernel, n_kv=len(ks), bound=bound is not None),
        out_shape=jax.ShapeDtypeStruct((t, D_ATTN), BF16),
        grid=(t // n, per_seq),
        in_specs=in_specs,
        out_specs=pl.BlockSpec((tq, D_ATTN), lambda bi, i: (bi * per_seq + i, 0)),
        scratch_shapes=[pltpu.VMEM((s_slots, m_tot, tq), F32), pltpu.VMEM((p_slots, m_tot, tq), BF16),
                        pltpu.VMEM((D_ATTN, tq), F32)],
        compiler_params=_params(2),
        name=f"attention_{n}",
    )(*args)


def _merge_kernel(x_ref, mod_ref, f_ref, a_ref, w_zf_ref, w_za_ref, w_g_ref, w_f_ref, w_a_ref, w_o_ref,
                  g_ref, y_ref):
    h = _modulated(x_ref, mod_ref)
    zf = _dot_nt(h, w_zf_ref[...])
    y_f = _dot((f_ref[...] * (zf * _sigmoid(zf))).astype(BF16), w_f_ref[...])
    za = _dot_nt(h, w_za_ref[...])
    y_a = _dot((a_ref[...] * (za * _sigmoid(za))).astype(BF16), w_a_ref[...])
    merged = (_sigmoid(_dot_nt(h, w_g_ref[:D_MODEL])) * y_f
              + _sigmoid(_dot_nt(h, w_g_ref[D_MODEL:])) * y_a)
    gate = mod_ref[0][:, 2 * D_MODEL:]
    out = x_ref[...] + gate * _dot(merged.astype(BF16), w_o_ref[...])
    y_ref[...] = _rms(out) * g_ref[...]


def _merge(x, mod3, f, attn, w_in_p, w_f, w_a, w_o, g, *, n, latent):
    t = x.shape[0]
    tm, _, mod_row = _token_tiling(t, n, per_sequence=latent)
    tok = lambda w: pl.BlockSpec((tm, w), lambda i: (i, 0))
    return pl.pallas_call(
        _merge_kernel,
        out_shape=jax.ShapeDtypeStruct((t, D_MODEL), F32),
        grid=(t // tm,),
        in_specs=[tok(D_MODEL),
                  pl.BlockSpec((1, 1, 3 * D_MODEL), lambda i: (mod_row(i), 0, 0)),
                  tok(D_FOURIER), tok(D_ATTN),
                  _row_spec(W_BLK, B_ZF), _row_spec(W_BLK, B_ZA), _row_spec(2 * D_MODEL, 1),
                  _const_spec(w_f.shape), _const_spec(w_a.shape), _const_spec(w_o.shape),
                  _const_spec(g.shape)],
        out_specs=tok(D_MODEL),
        compiler_params=_params(1),
        name=f"merge_{n}",
    )(x, mod3, f, attn, w_in_p, w_in_p, w_in_p, w_f, w_a, w_o, g)


def _dft_tables(n, rows):
    norm = 1.0 / math.sqrt(n)
    if n <= 256:
        k = jnp.arange(n, dtype=jnp.int32)
        ang = ((k[:rows, None] * k[None, :]) % n).astype(F32) * (2.0 * math.pi / n)
        return (jnp.cos(ang) * norm).astype(BF16), (jnp.sin(ang) * -norm).astype(BF16)
    lo = 64
    period = n // lo
    hi = -(-rows // lo)
    pos = jnp.arange(n, dtype=jnp.int32)
    a1 = ((jnp.arange(hi, dtype=jnp.int32)[:, None] * pos[None, :]) % period).astype(F32) * (2.0 * math.pi / period)
    a0 = ((jnp.arange(lo, dtype=jnp.int32)[:, None] * pos[None, :]) % n).astype(F32) * (2.0 * math.pi / n)
    c1, s1 = jnp.cos(a1)[:, None, :], jnp.sin(a1)[:, None, :]
    c0, s0 = jnp.cos(a0)[None, :, :] * norm, jnp.sin(a0)[None, :, :] * norm
    cn = (c1 * c0 - s1 * s0).reshape(hi * lo, n)[:rows]
    sn = (s1 * c0 + c1 * s0).reshape(hi * lo, n)[:rows]
    return cn.astype(BF16), (-sn).astype(BF16)


def _channel_dft_table():
    c = jnp.arange(GROUP, dtype=jnp.int32)
    ang = ((c[:, None] * c[None, :]) % GROUP).astype(F32) * (2.0 * math.pi / GROUP)
    norm = 1.0 / math.sqrt(GROUP)
    return (jnp.concatenate([jnp.cos(ang), jnp.sin(ang)], axis=1) * norm).astype(BF16)


def _rope_tables(n):
    t = jnp.arange(n)
    row = (t // GRID_W).astype(F32)
    col = (t % GRID_W).astype(F32)
    half = QK_ROPE // 2
    inv = ROPE_THETA ** (-jnp.arange(0, half, 2, dtype=F32) / half)
    ar, ac = row[:, None] * inv, col[:, None] * inv
    ang = jnp.concatenate([ar, ar, ac, ac], axis=-1)
    pad = lambda a, fill: jnp.concatenate(
        [jnp.full((n, ROPE_LANE), fill, F32), a, jnp.full((n, HEAD_PAD - ROPE_LANE - QK_ROPE), fill, F32)], axis=1)
    return pad(jnp.cos(ang), 1.0), pad(jnp.sin(ang), 0.0)


def kernel(x_prompt, x_sample, cache_ckv, cache_krope, c, c_ctx, w_ada, b_ada, w_in, q_norm_g, w_uq,
           kv_norm_g, w_ukv, w_f_out, w_a_out, w_out, final_norm_g):
    assert w_in.shape[0] == 1
    b_ctx, n_ctx, _ = x_prompt.shape
    dec_b, n_lat, _ = x_sample.shape
    past = cache_ckv.shape[2]

    w_q3 = w_uq[0].reshape(Q_RANK, N_HEADS, QK_NOPE + QK_ROPE)
    head_pad = lambda a, left: jnp.pad(
        a, ((0, 0), (0, 0), (left, HEAD_PAD - left - a.shape[2]))).reshape(Q_RANK, N_HEADS * HEAD_PAD)
    rp = w_q3[:, :, QK_NOPE:].reshape(Q_RANK, N_HEADS, 2, 2, QK_ROPE // 4)
    w_q_rot = jnp.stack([-rp[:, :, :, 1], rp[:, :, :, 0]], axis=3).reshape(Q_RANK, N_HEADS, QK_ROPE)
    w_uq_p = jnp.concatenate([head_pad(w_q3, 0), head_pad(w_q_rot, ROPE_LANE)], axis=1).astype(BF16)
    w_kv3 = w_ukv[0].reshape(KV_RANK, N_HEADS, QK_NOPE + V_DIM)
    w_knope = jnp.pad(w_kv3[:, :, :QK_NOPE], ((0, 0), (0, 0), (0, HEAD_PAD - QK_NOPE)))
    place = np.zeros((HEAD_PAD, N_HEADS, HEAD_PAD), np.float32)
    for j in range(QK_ROPE):
        place[ROPE_LANE + j, :, ROPE_LANE + j] = 1.0
    w_k = jnp.concatenate([w_knope.reshape(KV_RANK, -1), jnp.asarray(place).reshape(HEAD_PAD, -1)],
                          axis=0).astype(BF16)
    w_uvt = jnp.pad(jnp.transpose(w_kv3[:, :, QK_NOPE:], (1, 2, 0)),
                    ((0, 0), (0, V_ROWS - V_DIM), (0, 0))).reshape(VT_ROWS, KV_RANK).astype(BF16)
    ones_col = np.zeros((N_HEADS, V_ROWS, LANES), np.float32)
    ones_col[:, V_DIM, :] = 1.0
    ones_col = jnp.asarray(ones_col.reshape(VT_ROWS, LANES))
    w_f = w_f_out[0].astype(BF16)
    w_a = w_a_out[0].astype(BF16)
    w_o = w_out[0].astype(BF16)
    qg = q_norm_g[0].reshape(1, Q_RANK)
    kvg = kv_norm_g[0].reshape(1, KV_RANK)
    fg = final_norm_g.reshape(1, D_MODEL)
    wts = (qg, kvg, w_uq_p, w_k, w_uvt, ones_col, _channel_dft_table())
    w_in_p = _pack_w_in(jnp.swapaxes(w_in[0], 0, 1))

    cond8 = jnp.concatenate([c_ctx[None, :], c, jnp.zeros((8 - 1 - dec_b, D_MODEL), F32)], axis=0)
    mod3 = _adaln(cond8, w_ada[0], b_ada[0].reshape(1, -1)).reshape(8, 1, 3 * D_MODEL)

    xp = x_prompt.reshape(b_ctx * n_ctx, D_MODEL)
    ucs, q, k, vt, state_ckv, state_krope = _layer_in(xp, mod3, w_in_p, wts, None, n=n_ctx, state=True)
    f = _pos_dft(ucs, n=n_ctx)
    attn = _attention(q, [k], [vt], [n_ctx], n=n_ctx, tq=n_ctx)
    y_prompt = _merge(xp, mod3, f, attn, w_in_p, w_f, w_a, w_o, fg, n=n_ctx, latent=False)

    xs = x_sample.reshape(dec_b * n_lat, D_MODEL)
    ucs, q, k, vt, qn2, kmax = _layer_in(xs, mod3, w_in_p, wts, _rope_tables(n_lat), n=n_lat, state=False)
    f = _pos_dft(ucs, n=n_lat)
    krope_pad = jnp.pad(cache_krope[:, 0], ((0, 0), (0, 0), (ROPE_LANE, HEAD_PAD - ROPE_LANE - QK_ROPE)))
    k_c, vt_c, kmax_c = _cache_kv(cache_ckv[:, 0].reshape(dec_b * past, KV_RANK),
                                  krope_pad.reshape(dec_b * past, HEAD_PAD), w_k, w_uvt, ones_col, m=past)
    attn = _attention(q, [k, k_c], [vt, vt_c], [n_lat, past], n=n_lat, tq=256, bound=(qn2, [kmax, kmax_c]))
    y_sample = _merge(xs, mod3, f, attn, w_in_p, w_f, w_a, w_o, fg, n=n_lat, latent=True)

    return (y_prompt.reshape(b_ctx, n_ctx, D_MODEL), y_sample.reshape(dec_b, n_lat, D_MODEL),
            state_ckv.reshape(b_ctx, 1, n_ctx, KV_RANK), state_krope.reshape(b_ctx, 1, n_ctx, QK_ROPE))
```

```python
import functools
import math

import jax
import jax.numpy as jnp
import numpy as np
from jax import lax
from jax.experimental import pallas as pl
from jax.experimental.pallas import tpu as pltpu

F32 = jnp.float32
BF16 = jnp.bfloat16

D_MODEL = 1024
GRID_W = 64
N_GROUPS = 4
GROUP = 128
D_FOURIER = N_GROUPS * GROUP
N_HEADS = 8
QK_NOPE = 64
QK_ROPE = 32
V_DIM = 64
Q_RANK = 256
KV_RANK = 128
D_ATTN = N_HEADS * V_DIM
D_IN = 2 * D_FOURIER + Q_RANK + KV_RANK + QK_ROPE + D_ATTN + 2 * D_MODEL
ROPE_THETA = 10000.0
EPS = 1e-6
LANES = 128
HEAD_PAD = LANES
ROPE_LANE = QK_NOPE
V_ROWS = V_DIM + 16
VT_ROWS = N_HEADS * V_ROWS
Q_SCALE = (QK_NOPE + QK_ROPE) ** -0.5 * math.log2(math.e)
KEY_CHUNK = 256
REV_BLOCK = 256
BOUND_SLACK = 1.0 + 2.0 ** -6
MIN_DENOMINATOR = 2.0 ** -60
W_BLK = 512
B_UF, B_ZF, B_MID, B_ZA = 0, 1, 2, 3
D_IN_PAD = 8 * W_BLK
TOKEN_TILE = 1024
VMEM_LIMIT_BYTES = 56 * 1024 * 1024


def _const_spec(shape):
    nd = len(shape)
    return pl.BlockSpec(shape, lambda *_: (0,) * nd, pipeline_mode=pl.Buffered(1))


def _row_spec(rows, blk):
    return pl.BlockSpec((rows, D_MODEL), lambda *_: (blk, 0), pipeline_mode=pl.Buffered(1))


def _params(n_axes):
    return pltpu.CompilerParams(dimension_semantics=("arbitrary",) * n_axes,
                                vmem_limit_bytes=VMEM_LIMIT_BYTES)


def _rms(x):
    return x * lax.rsqrt(jnp.mean(x * x, axis=-1, keepdims=True) + EPS)


def _sigmoid(x):
    return 1.0 / (1.0 + jnp.exp(-x))


def _dot(a, b):
    return jnp.dot(a, b, preferred_element_type=F32)


def _dot_nt(a, b):
    return lax.dot_general(a, b, (((1,), (1,)), ((), ())), preferred_element_type=F32)


def _token_tiling(t, n, *, per_sequence):
    tm = TOKEN_TILE
    assert t % tm == 0 and (not per_sequence or n % tm == 0)
    per_seq = n // tm if per_sequence else None
    mod_row = (lambda i: 1 + i // per_seq) if per_sequence else (lambda i: 0)
    return tm, per_seq, mod_row


def _modulated(x_ref, mod_ref):
    mod = mod_ref[0]
    shift, scale = mod[:, 0:D_MODEL], mod[:, D_MODEL:2 * D_MODEL]
    return (_rms(x_ref[...]) * (1.0 + scale) + shift).astype(BF16)


def _adaln_kernel(c_ref, w_ref, b_ref, o_ref):
    c = c_ref[...]
    s = (c * _sigmoid(c)).astype(BF16)
    o_ref[...] = _dot(s, w_ref[...].astype(BF16)) + b_ref[...]


def _adaln(cond8, w_ada, b_ada):
    n_blk = 4
    bw = 3 * D_MODEL // n_blk
    return pl.pallas_call(
        _adaln_kernel,
        out_shape=jax.ShapeDtypeStruct((8, 3 * D_MODEL), F32),
        grid=(n_blk,),
        in_specs=[pl.BlockSpec((8, D_MODEL), lambda j: (0, 0)),
                  pl.BlockSpec((D_MODEL, bw), lambda j: (0, j)),
                  pl.BlockSpec((1, bw), lambda j: (0, j))],
        out_specs=pl.BlockSpec((8, bw), lambda j: (0, j)),
        compiler_params=_params(1),
        name="adaln",
    )(cond8, w_ada, b_ada)


def _rot_rows(w):
    q = QK_ROPE // 4
    return jnp.concatenate([-w[q:2 * q], w[0:q], -w[3 * q:4 * q], w[2 * q:3 * q]], axis=0)


def _pack_w_in_kernel(wt_ref, o_ref):
    lo = 2 * D_FOURIER + Q_RANK + KV_RANK
    o_ref[:lo] = wt_ref[:lo].astype(BF16)
    kr = wt_ref[lo:lo + QK_ROPE]
    zeros = jnp.zeros((QK_ROPE, D_MODEL), F32)
    o_ref[lo:lo + LANES] = jnp.concatenate([_rot_rows(kr), zeros, kr, zeros], axis=0).astype(BF16)
    o_ref[lo + LANES:] = wt_ref[lo + QK_ROPE:].astype(BF16)


def _pack_w_in(w_in_t):
    return pl.pallas_call(
        _pack_w_in_kernel,
        out_shape=jax.ShapeDtypeStruct((D_IN_PAD, D_MODEL), BF16),
        grid=(1,),
        in_specs=[_const_spec((D_IN, D_MODEL))],
        out_specs=_const_spec((D_IN_PAD, D_MODEL)),
        compiler_params=_params(1),
        name="pack_w_in",
    )(w_in_t)


def _sq_norms_row(xb):
    return _dot_nt(jnp.ones((8, HEAD_PAD), BF16), xb * xb)[0:1]


def _kv_outputs(ckvn_b, kr_b, w_k_ref, w_uvt_ref, ones_ref, k_ref, vt_ref, kmax_ref=None):
    tm = ckvn_b.shape[0]
    kp = _dot(jnp.concatenate([ckvn_b, kr_b], axis=1), w_k_ref[...])
    kmax = []
    for hh in range(N_HEADS):
        kh = kp[:, hh * HEAD_PAD:(hh + 1) * HEAD_PAD].astype(BF16)
        k_ref[hh] = kh
        if kmax_ref is not None:
            kmax.append(jnp.broadcast_to(jnp.max(_sq_norms_row(kh), axis=1, keepdims=True), (1, LANES)))
    ones = jnp.concatenate([ones_ref[...]] * (tm // LANES), axis=1)
    vt_ref[...] = (_dot_nt(w_uvt_ref[...], ckvn_b) + ones).astype(BF16)
    return jnp.concatenate(kmax, axis=0) if kmax else None


def _layer_in_kernel(*refs, rope, state, tiles_per_seq):
    (x_ref, mod_ref, w_uf_ref, w_mid_ref, qg_ref, kvg_ref, w_uq_ref, w_k_ref, w_uvt_ref, ones_ref,
     cs_ref), refs = refs[:11], refs[11:]
    if rope:
        (cos_ref, sin_ref), refs = refs[:2], refs[2:]
    (ucs_ref, q_ref, k_ref, vt_ref), refs = refs[:4], refs[4:]
    if state:
        ckvn_ref, krope_ref = refs
    if rope:
        qn2_ref, kmax_ref = refs

    h = _modulated(x_ref, mod_ref)

    u = _dot_nt(h, w_uf_ref[...]).astype(BF16)
    for g in range(N_GROUPS):
        r = _dot(u[:, g * GROUP:(g + 1) * GROUP], cs_ref[...])
        ucs_ref[:, g * GROUP:(g + 1) * GROUP] = r[:, :GROUP].astype(BF16)
        ucs_ref[:, D_FOURIER + g * GROUP:D_FOURIER + (g + 1) * GROUP] = r[:, GROUP:].astype(BF16)

    mid = _dot_nt(h, w_mid_ref[...])
    cq, ckv, kr = mid[:, :Q_RANK], mid[:, Q_RANK:Q_RANK + KV_RANK], mid[:, Q_RANK + KV_RANK:]
    cqn = (_rms(cq) * (qg_ref[...] * Q_SCALE)).astype(BF16)
    ckvn = _rms(ckv) * kvg_ref[...]
    if state:
        ckvn_ref[...] = ckvn
        krope_ref[...] = kr[:, ROPE_LANE:ROPE_LANE + QK_ROPE]
    n_q = N_HEADS * HEAD_PAD
    if rope:
        cos, sin = cos_ref[...], sin_ref[...]
        kr = kr * cos + pltpu.roll(kr, ROPE_LANE, 1) * sin
        qp = _dot(cqn, w_uq_ref[...])
    else:
        qp = _dot(cqn, w_uq_ref[:, :n_q])
    for hh in range(N_HEADS):
        qh = qp[:, hh * HEAD_PAD:(hh + 1) * HEAD_PAD]
        if rope:
            qh = qh * cos + qp[:, n_q + hh * HEAD_PAD:n_q + (hh + 1) * HEAD_PAD] * sin
        qh = qh.astype(BF16)
        q_ref[hh] = qh
        if rope:
            qn2_ref[hh:hh + 1, :] = _sq_norms_row(qh)
    kmax = _kv_outputs(ckvn.astype(BF16), kr.astype(BF16), w_k_ref, w_uvt_ref, ones_ref, k_ref, vt_ref,
                       kmax_ref if rope else None)
    if rope:
        first = pl.program_id(0) % tiles_per_seq == 0

        @pl.when(first)
        def _():
            kmax_ref[0] = kmax

        @pl.when(jnp.logical_not(first))
        def _():
            kmax_ref[0] = jnp.maximum(kmax, kmax_ref[0])


def _layer_in(x, mod3, w_in_p, wts, rope_tabs, *, n, state):
    t = x.shape[0]
    rope = rope_tabs is not None
    tm, per_seq, mod_row = _token_tiling(t, n, per_sequence=rope)
    tok = lambda w: pl.BlockSpec((tm, w), lambda i: (i, 0))
    in_specs = [tok(D_MODEL),
                pl.BlockSpec((1, 1, 3 * D_MODEL), lambda i: (mod_row(i), 0, 0)),
                _row_spec(W_BLK, B_UF), _row_spec(W_BLK, B_MID)]
    in_specs += [_const_spec(w.shape) for w in wts]
    args = [x, mod3, w_in_p, w_in_p, *wts]
    if rope:
        in_specs += [pl.BlockSpec((tm, HEAD_PAD), lambda i: (i % per_seq, 0))] * 2
        args += list(rope_tabs)
    head = pl.BlockSpec((N_HEADS, tm, HEAD_PAD), lambda i: (0, i, 0))
    out_specs = [tok(2 * D_FOURIER), head, head, pl.BlockSpec((VT_ROWS, tm), lambda i: (0, i))]
    out_shape = [jax.ShapeDtypeStruct((t, 2 * D_FOURIER), BF16),
                 jax.ShapeDtypeStruct((N_HEADS, t, HEAD_PAD), BF16),
                 jax.ShapeDtypeStruct((N_HEADS, t, HEAD_PAD), BF16),
                 jax.ShapeDtypeStruct((VT_ROWS, t), BF16)]
    if state:
        out_specs += [tok(KV_RANK), tok(QK_ROPE)]
        out_shape += [jax.ShapeDtypeStruct((t, KV_RANK), F32), jax.ShapeDtypeStruct((t, QK_ROPE), F32)]
    if rope:
        out_specs += [pl.BlockSpec((N_HEADS, tm), lambda i: (0, i)),
                      pl.BlockSpec((1, N_HEADS, LANES), lambda i: (i // per_seq, 0, 0))]
        out_shape += [jax.ShapeDtypeStruct((N_HEADS, t), F32),
                      jax.ShapeDtypeStruct((t // n, N_HEADS, LANES), F32)]
    return pl.pallas_call(
        functools.partial(_layer_in_kernel, rope=rope, state=state, tiles_per_seq=per_seq),
        out_shape=out_shape,
        grid=(t // tm,),
        in_specs=in_specs,
        out_specs=out_specs,
        compiler_params=_params(1),
        name="layer_in_latent" if rope else "layer_in_context",
    )(*args)


def _cache_kv_kernel(ckv_ref, kr_ref, w_k_ref, w_uvt_ref, ones_ref, k_ref, vt_ref, kmax_ref):
    kmax_ref[0] = _kv_outputs(ckv_ref[...].astype(BF16), kr_ref[...].astype(BF16), w_k_ref, w_uvt_ref,
                              ones_ref, k_ref, vt_ref, kmax_ref)


def _cache_kv(ckv, krope_pad, w_k, w_uvt, ones_col, *, m):
    t = ckv.shape[0]
    return pl.pallas_call(
        _cache_kv_kernel,
        out_shape=[jax.ShapeDtypeStruct((N_HEADS, t, HEAD_PAD), BF16),
                   jax.ShapeDtypeStruct((VT_ROWS, t), BF16),
                   jax.ShapeDtypeStruct((t // m, N_HEADS, LANES), F32)],
        grid=(t // m,),
        in_specs=[pl.BlockSpec((m, KV_RANK), lambda i: (i, 0)),
                  pl.BlockSpec((m, HEAD_PAD), lambda i: (i, 0)),
                  _const_spec(w_k.shape), _const_spec(w_uvt.shape), _const_spec(ones_col.shape)],
        out_specs=[pl.BlockSpec((N_HEADS, m, HEAD_PAD), lambda i: (0, i, 0)),
                   pl.BlockSpec((VT_ROWS, m), lambda i: (0, i)),
                   pl.BlockSpec((1, N_HEADS, LANES), lambda i: (i, 0, 0))],
        compiler_params=_params(1),
        name="cache_kv",
    )(ckv, krope_pad, w_k, w_uvt, ones_col)


def _pos_dft_direct_kernel(cn_ref, sn_ref, ucs_ref, f_ref):
    n = cn_ref.shape[0]
    for s in range(f_ref.shape[0] // n):
        rows = slice(s * n, (s + 1) * n)
        f = _dot(cn_ref[...], ucs_ref[rows, :D_FOURIER]) + _dot(sn_ref[...], ucs_ref[rows, D_FOURIER:])
        f_ref[rows] = f.astype(BF16)


def _pos_dft_half_kernel(cn_ref, sn_ref, rev_ref, ucs_ref, f_ref):
    half = f_ref.shape[0] // 2
    g = _dot(cn_ref[...], ucs_ref[:, :D_FOURIER])
    hn = _dot(sn_ref[...], ucs_ref[:, D_FOURIER:])
    f_ref[:half] = (g[:half] + hn[:half]).astype(BF16)
    mirrored = (g[1:half + 1] - hn[1:half + 1]).astype(BF16)
    blocks = half // REV_BLOCK
    for a in range(blocks):
        blk = mirrored[(blocks - 1 - a) * REV_BLOCK:(blocks - a) * REV_BLOCK]
        f_ref[half + a * REV_BLOCK:half + (a + 1) * REV_BLOCK] = _dot(rev_ref[...], blk).astype(BF16)


def _pos_dft(ucs, *, n):
    t = ucs.shape[0]
    out_shape = jax.ShapeDtypeStruct((t, D_FOURIER), BF16)
    if n <= REV_BLOCK:
        cn, snn = _dft_tables(n, n)
        seqs = min(4, t // n)
        return pl.pallas_call(
            _pos_dft_direct_kernel,
            out_shape=out_shape,
            grid=(t // (n * seqs),),
            in_specs=[_const_spec(cn.shape), _const_spec(snn.shape),
                      pl.BlockSpec((n * seqs, 2 * D_FOURIER), lambda i: (i, 0))],
            out_specs=pl.BlockSpec((n * seqs, D_FOURIER), lambda i: (i, 0)),
            compiler_params=_params(1),
            name=f"pos_dft_{n}",
        )(cn, snn, ucs)
    cn, snn = _dft_tables(n, n // 2 + 8)
    rev = jnp.asarray(np.eye(REV_BLOCK, dtype=np.float32)[::-1], dtype=BF16)
    return pl.pallas_call(
        _pos_dft_half_kernel,
        out_shape=out_shape,
        grid=(t // n,),
        in_specs=[_const_spec(cn.shape), _const_spec(snn.shape), _const_spec(rev.shape),
                  pl.BlockSpec((n, 2 * D_FOURIER), lambda i: (i, 0))],
        out_specs=pl.BlockSpec((n, D_FOURIER), lambda i: (i, 0)),
        compiler_params=_params(1),
        name=f"pos_dft_{n}",
    )(cn, snn, rev, ucs)


def _attention_kernel(*refs, n_kv, bound):
    q_ref, refs = refs[0], refs[1:]
    k_refs, vt_refs, refs = refs[:n_kv], refs[n_kv:2 * n_kv], refs[2 * n_kv:]
    if bound:
        qn2_ref, kmax_refs, refs = refs[0], refs[1:1 + n_kv], refs[1 + n_kv:]
    o_ref, s_ref, pp_ref, ot_ref = refs
    tq = q_ref.shape[1]
    chunks, base = [], 0
    for j, k_ref in enumerate(k_refs):
        chunks += [(j, c, base + c) for c in range(0, k_ref.shape[1], KEY_CHUNK)]
        base += k_ref.shape[1]

    def finish(hh, ot):
        l = ot[V_DIM:V_DIM + 1]
        ot_ref[pl.ds(pl.multiple_of(hh * V_DIM, V_DIM), V_DIM), :] = ot[:V_DIM] / l
        return l

    def shifted_softmax_pv(score_head, pv_head):
        if score_head is not None:
            qh = q_ref[score_head]
            kmax = functools.reduce(jnp.maximum, [r[0, score_head:score_head + 1, :] for r in kmax_refs])
            shift = jnp.sqrt(qn2_ref[score_head:score_head + 1, :]
                             * jnp.concatenate([kmax] * (tq // LANES), axis=1)) * BOUND_SLACK
        ot = None
        for j, c, r in chunks:
            if score_head is not None:
                s = _dot_nt(k_refs[j][score_head, c:c + KEY_CHUNK, :], qh)
                pp_ref[score_head % 2, r:r + KEY_CHUNK, :] = jnp.exp2(s - shift).astype(BF16)
            if pv_head is not None:
                part = _dot(vt_refs[j][pv_head * V_ROWS:(pv_head + 1) * V_ROWS, c:c + KEY_CHUNK],
                            pp_ref[pv_head % 2, r:r + KEY_CHUNK, :])
                ot = part if ot is None else ot + part
        return finish(pv_head, ot) if pv_head is not None else None

    def exact_scores(hh, slot):
        qh = q_ref[hh]
        m8 = None
        for j, c, r in chunks:
            s = _dot_nt(k_refs[j][hh, c:c + KEY_CHUNK, :], qh)
            s_ref[slot, r:r + KEY_CHUNK, :] = s
            mc = jnp.max(s.reshape(KEY_CHUNK // 8, 8, tq), axis=0)
            m8 = mc if m8 is None else jnp.maximum(m8, mc)
        return jnp.max(m8, axis=0, keepdims=True)

    def exact_softmax_pv(hh, slot, m):
        for _, _, r in chunks:
            pp_ref[slot, r:r + KEY_CHUNK, :] = jnp.exp2(s_ref[slot, r:r + KEY_CHUNK, :] - m).astype(BF16)
        rows = pl.ds(pl.multiple_of(hh * V_ROWS, V_ROWS), V_ROWS)
        ot, base = None, 0
        for vt_ref in vt_refs:
            mk = vt_ref.shape[1]
            part = _dot(vt_ref[rows, :], pp_ref[slot, base:base + mk, :])
            ot = part if ot is None else ot + part
            base += mk
        finish(hh, ot)

    if not bound:
        maxima = [exact_scores(hh, hh) for hh in range(N_HEADS)]
        for hh in range(N_HEADS):
            exact_softmax_pv(hh, hh, maxima[hh])
    else:
        l_min = None
        for hh in range(N_HEADS + 1):
            l = shifted_softmax_pv(hh if hh < N_HEADS else None, hh - 1 if hh > 0 else None)
            if l is not None:
                l_min = l if l_min is None else jnp.minimum(l_min, l)

        @pl.when(jnp.logical_not(jnp.min(l_min) >= MIN_DENOMINATOR))
        def _():
            def body(hh, carry):
                exact_softmax_pv(hh, 0, exact_scores(hh, 0))
                return carry

            lax.fori_loop(0, N_HEADS, body, 0)

    o_ref[...] = ot_ref[...].T.astype(BF16)


def _attention(q, ks, vts, key_lens, *, n, tq, bound=None):
    t = q.shape[1]
    per_seq = n // tq
    m_tot = sum(key_lens)
    in_specs = [pl.BlockSpec((N_HEADS, tq, HEAD_PAD), lambda bi, i: (0, bi * per_seq + i, 0))]
    in_specs += [pl.BlockSpec((N_HEADS, m, HEAD_PAD), lambda bi, i: (0, bi, 0)) for m in key_lens]
    in_specs += [pl.BlockSpec((VT_ROWS, m), lambda bi, i: (0, bi)) for m in key_lens]
    args = [q, *ks, *vts]
    if bound is not None:
        qn2, kmaxes = bound
        in_specs += [pl.BlockSpec((N_HEADS, tq), lambda bi, i: (0, bi * per_seq + i))]
        in_specs += [pl.BlockSpec((1, N_HEADS, LANES), lambda bi, i: (bi, 0, 0)) for _ in kmaxes]
        args += [qn2, *kmaxes]
    s_slots, p_slots = (1, 2) if bound is not None else (N_HEADS, N_HEADS)
    return pl.pallas_call(
        functools.partial(_attention_kernel, n_kv=len(ks), bound=bound is not None),
        out_shape=jax.ShapeDtypeStruct((t, D_ATTN), BF16),
        grid=(t // n, per_seq),
        in_specs=in_specs,
        out_specs=pl.BlockSpec((tq, D_ATTN), lambda bi, i: (bi * per_seq + i, 0)),
        scratch_shapes=[pltpu.VMEM((s_slots, m_tot, tq), F32), pltpu.VMEM((p_slots, m_tot, tq), BF16),
                        pltpu.VMEM((D_ATTN, tq), F32)],
        compiler_params=_params(2),
        name=f"attention_{n}",
    )(*args)


def _merge_kernel(x_ref, mod_ref, f_ref, a_ref, w_zf_ref, w_za_ref, w_g_ref, w_f_ref, w_a_ref, w_o_ref,
                  g_ref, y_ref):
    h = _modulated(x_ref, mod_ref)
    zf = _dot_nt(h, w_zf_ref[...])
    y_f = _dot((f_ref[...] * (zf * _sigmoid(zf))).astype(BF16), w_f_ref[...])
    za = _dot_nt(h, w_za_ref[...])
    y_a = _dot((a_ref[...] * (za * _sigmoid(za))).astype(BF16), w_a_ref[...])
    merged = (_sigmoid(_dot_nt(h, w_g_ref[:D_MODEL])) * y_f
              + _sigmoid(_dot_nt(h, w_g_ref[D_MODEL:])) * y_a)
    gate = mod_ref[0][:, 2 * D_MODEL:]
    out = x_ref[...] + gate * _dot(merged.astype(BF16), w_o_ref[...])
    y_ref[...] = _rms(out) * g_ref[...]


def _merge(x, mod3, f, attn, w_in_p, w_f, w_a, w_o, g, *, n, latent):
    t = x.shape[0]
    tm, _, mod_row = _token_tiling(t, n, per_sequence=latent)
    tok = lambda w: pl.BlockSpec((tm, w), lambda i: (i, 0))
    return pl.pallas_call(
        _merge_kernel,
        out_shape=jax.ShapeDtypeStruct((t, D_MODEL), F32),
        grid=(t // tm,),
        in_specs=[tok(D_MODEL),
                  pl.BlockSpec((1, 1, 3 * D_MODEL), lambda i: (mod_row(i), 0, 0)),
                  tok(D_FOURIER), tok(D_ATTN),
                  _row_spec(W_BLK, B_ZF), _row_spec(W_BLK, B_ZA), _row_spec(2 * D_MODEL, 1),
                  _const_spec(w_f.shape), _const_spec(w_a.shape), _const_spec(w_o.shape),
                  _const_spec(g.shape)],
        out_specs=tok(D_MODEL),
        compiler_params=_params(1),
        name=f"merge_{n}",
    )(x, mod3, f, attn, w_in_p, w_in_p, w_in_p, w_f, w_a, w_o, g)


def _dft_tables(n, rows):
    norm = 1.0 / math.sqrt(n)
    if n <= 256:
        k = jnp.arange(n, dtype=jnp.int32)
        ang = ((k[:rows, None] * k[None, :]) % n).astype(F32) * (2.0 * math.pi / n)
        return (jnp.cos(ang) * norm).astype(BF16), (jnp.sin(ang) * -norm).astype(BF16)
    lo = 64
    period = n // lo
    hi = -(-rows // lo)
    pos = jnp.arange(n, dtype=jnp.int32)
    a1 = ((jnp.arange(hi, dtype=jnp.int32)[:, None] * pos[None, :]) % period).astype(F32) * (2.0 * math.pi / period)
    a0 = ((jnp.arange(lo, dtype=jnp.int32)[:, None] * pos[None, :]) % n).astype(F32) * (2.0 * math.pi / n)
    c1, s1 = jnp.cos(a1)[:, None, :], jnp.sin(a1)[:, None, :]
    c0, s0 = jnp.cos(a0)[None, :, :] * norm, jnp.sin(a0)[None, :, :] * norm
    cn = (c1 * c0 - s1 * s0).reshape(hi * lo, n)[:rows]
    sn = (s1 * c0 + c1 * s0).reshape(hi * lo, n)[:rows]
    return cn.astype(BF16), (-sn).astype(BF16)


def _channel_dft_table():
    c = jnp.arange(GROUP, dtype=jnp.int32)
    ang = ((c[:, None] * c[None, :]) % GROUP).astype(F32) * (2.0 * math.pi / GROUP)
    norm = 1.0 / math.sqrt(GROUP)
    return (jnp.concatenate([jnp.cos(ang), jnp.sin(ang)], axis=1) * norm).astype(BF16)


def _rope_tables(n):
    t = jnp.arange(n)
    row = (t // GRID_W).astype(F32)
    col = (t % GRID_W).astype(F32)
    half = QK_ROPE // 2
    inv = ROPE_THETA ** (-jnp.arange(0, half, 2, dtype=F32) / half)
    ar, ac = row[:, None] * inv, col[:, None] * inv
    ang = jnp.concatenate([ar, ar, ac, ac], axis=-1)
    pad = lambda a, fill: jnp.concatenate(
        [jnp.full((n, ROPE_LANE), fill, F32), a, jnp.full((n, HEAD_PAD - ROPE_LANE - QK_ROPE), fill, F32)], axis=1)
    return pad(jnp.cos(ang), 1.0), pad(jnp.sin(ang), 0.0)


def kernel(x_prompt, x_sample, cache_ckv, cache_krope, c, c_ctx, w_ada, b_ada, w_in, q_norm_g, w_uq,
           kv_norm_g, w_ukv, w_f_out, w_a_out, w_out, final_norm_g):
    assert w_in.shape[0] == 1
    b_ctx, n_ctx, _ = x_prompt.shape
    dec_b, n_lat, _ = x_sample.shape
    past = cache_ckv.shape[2]

    w_q3 = w_uq[0].reshape(Q_RANK, N_HEADS, QK_NOPE + QK_ROPE)
    head_pad = lambda a, left: jnp.pad(
        a, ((0, 0), (0, 0), (left, HEAD_PAD - left - a.shape[2]))).reshape(Q_RANK, N_HEADS * HEAD_PAD)
    rp = w_q3[:, :, QK_NOPE:].reshape(Q_RANK, N_HEADS, 2, 2, QK_ROPE // 4)
    w_q_rot = jnp.stack([-rp[:, :, :, 1], rp[:, :, :, 0]], axis=3).reshape(Q_RANK, N_HEADS, QK_ROPE)
    w_uq_p = jnp.concatenate([head_pad(w_q3, 0), head_pad(w_q_rot, ROPE_LANE)], axis=1).astype(BF16)
    w_kv3 = w_ukv[0].reshape(KV_RANK, N_HEADS, QK_NOPE + V_DIM)
    w_knope = jnp.pad(w_kv3[:, :, :QK_NOPE], ((0, 0), (0, 0), (0, HEAD_PAD - QK_NOPE)))
    place = np.zeros((HEAD_PAD, N_HEADS, HEAD_PAD), np.float32)
    for j in range(QK_ROPE):
        place[ROPE_LANE + j, :, ROPE_LANE + j] = 1.0
    w_k = jnp.concatenate([w_knope.reshape(KV_RANK, -1), jnp.asarray(place).reshape(HEAD_PAD, -1)],
                          axis=0).astype(BF16)
    w_uvt = jnp.pad(jnp.transpose(w_kv3[:, :, QK_NOPE:], (1, 2, 0)),
                    ((0, 0), (0, V_ROWS - V_DIM), (0, 0))).reshape(VT_ROWS, KV_RANK).astype(BF16)
    ones_col = np.zeros((N_HEADS, V_ROWS, LANES), np.float32)
    ones_col[:, V_DIM, :] = 1.0
    ones_col = jnp.asarray(ones_col.reshape(VT_ROWS, LANES))
    w_f = w_f_out[0].astype(BF16)
    w_a = w_a_out[0].astype(BF16)
    w_o = w_out[0].astype(BF16)
    qg = q_norm_g[0].reshape(1, Q_RANK)
    kvg = kv_norm_g[0].reshape(1, KV_RANK)
    fg = final_norm_g.reshape(1, D_MODEL)
    wts = (qg, kvg, w_uq_p, w_k, w_uvt, ones_col, _channel_dft_table())
    w_in_p = _pack_w_in(jnp.swapaxes(w_in[0], 0, 1))

    cond8 = jnp.concatenate([c_ctx[None, :], c, jnp.zeros((8 - 1 - dec_b, D_MODEL), F32)], axis=0)
    mod3 = _adaln(cond8, w_ada[0], b_ada[0].reshape(1, -1)).reshape(8, 1, 3 * D_MODEL)

    xp = x_prompt.reshape(b_ctx * n_ctx, D_MODEL)
    ucs, q, k, vt, state_ckv, state_krope = _layer_in(xp, mod3, w_in_p, wts, None, n=n_ctx, state=True)
    f = _pos_dft(ucs, n=n_ctx)
    attn = _attention(q, [k], [vt], [n_ctx], n=n_ctx, tq=n_ctx)
    y_prompt = _merge(xp, mod3, f, attn, w_in_p, w_f, w_a, w_o, fg, n=n_ctx, latent=False)

    xs = x_sample.reshape(dec_b * n_lat, D_MODEL)
    ucs, q, k, vt, qn2, kmax = _layer_in(xs, mod3, w_in_p, wts, _rope_tables(n_lat), n=n_lat, state=False)
    f = _pos_dft(ucs, n=n_lat)
    krope_pad = jnp.pad(cache_krope[:, 0], ((0, 0), (0, 0), (ROPE_LANE, HEAD_PAD - ROPE_LANE - QK_ROPE)))
    k_c, vt_c, kmax_c = _cache_kv(cache_ckv[:, 0].reshape(dec_b * past, KV_RANK),
                                  krope_pad.reshape(dec_b * past, HEAD_PAD), w_k, w_uvt, ones_col, m=past)
    attn = _attention(q, [k, k_c], [vt, vt_c], [n_lat, past], n=n_lat, tq=512, bound=(qn2, [kmax, kmax_c]))
    y_sample = _merge(xs, mod3, f, attn, w_in_p, w_f, w_a, w_o, fg, n=n_lat, latent=True)

    return (y_prompt.reshape(b_ctx, n_ctx, D_MODEL), y_sample.reshape(dec_b, n_lat, D_MODEL),
            state_ckv.reshape(b_ctx, 1, n_ctx, KV_RANK), state_krope.reshape(b_ctx, 1, n_ctx, QK_ROPE))
```

```python
import functools
import math

import jax
import jax.numpy as jnp
import numpy as np
from jax import lax
from jax.experimental import pallas as pl
from jax.experimental.pallas import tpu as pltpu

F32 = jnp.float32
BF16 = jnp.bfloat16

D_MODEL = 1024
GRID_W = 64
N_GROUPS = 4
GROUP = 128
D_FOURIER = N_GROUPS * GROUP
N_HEADS = 8
QK_NOPE = 64
QK_ROPE = 32
V_DIM = 64
Q_RANK = 256
KV_RANK = 128
D_ATTN = N_HEADS * V_DIM
D_IN = 2 * D_FOURIER + Q_RANK + KV_RANK + QK_ROPE + D_ATTN + 2 * D_MODEL
ROPE_THETA = 10000.0
EPS = 1e-6
LANES = 128
HEAD_PAD = LANES
ROPE_LANE = QK_NOPE
V_ROWS = V_DIM + 16
VT_ROWS = N_HEADS * V_ROWS
Q_SCALE = (QK_NOPE + QK_ROPE) ** -0.5 * math.log2(math.e)
KEY_CHUNK = 256
REV_BLOCK = 256
MIN_DENOMINATOR = 2.0 ** -60
MAX_DENOMINATOR = 2.0 ** 60
W_BLK = 512
B_UF, B_ZF, B_MID, B_ZA = 0, 1, 2, 3
D_IN_PAD = 8 * W_BLK
TOKEN_TILE = 1024
VMEM_LIMIT_BYTES = 56 * 1024 * 1024


def _const_spec(shape):
    nd = len(shape)
    return pl.BlockSpec(shape, lambda *_: (0,) * nd, pipeline_mode=pl.Buffered(1))


def _row_spec(rows, blk):
    return pl.BlockSpec((rows, D_MODEL), lambda *_: (blk, 0), pipeline_mode=pl.Buffered(1))


def _params(n_axes):
    return pltpu.CompilerParams(dimension_semantics=("arbitrary",) * n_axes,
                                vmem_limit_bytes=VMEM_LIMIT_BYTES)


def _rms(x):
    return x * lax.rsqrt(jnp.mean(x * x, axis=-1, keepdims=True) + EPS)


def _sigmoid(x):
    return 1.0 / (1.0 + jnp.exp(-x))


def _dot(a, b):
    return jnp.dot(a, b, preferred_element_type=F32)


def _dot_nt(a, b):
    return lax.dot_general(a, b, (((1,), (1,)), ((), ())), preferred_element_type=F32)


def _token_tiling(t, n, *, per_sequence):
    tm = TOKEN_TILE
    assert t % tm == 0 and (not per_sequence or n % tm == 0)
    per_seq = n // tm if per_sequence else None
    mod_row = (lambda i: 1 + i // per_seq) if per_sequence else (lambda i: 0)
    return tm, per_seq, mod_row


def _modulated(x_ref, mod_ref):
    mod = mod_ref[0]
    shift, scale = mod[:, 0:D_MODEL], mod[:, D_MODEL:2 * D_MODEL]
    return (_rms(x_ref[...]) * (1.0 + scale) + shift).astype(BF16)


def _adaln_kernel(c_ref, w_ref, b_ref, o_ref):
    c = c_ref[...]
    s = (c * _sigmoid(c)).astype(BF16)
    o_ref[...] = _dot(s, w_ref[...].astype(BF16)) + b_ref[...]


def _adaln(cond8, w_ada, b_ada):
    n_blk = 4
    bw = 3 * D_MODEL // n_blk
    return pl.pallas_call(
        _adaln_kernel,
        out_shape=jax.ShapeDtypeStruct((8, 3 * D_MODEL), F32),
        grid=(n_blk,),
        in_specs=[pl.BlockSpec((8, D_MODEL), lambda j: (0, 0)),
                  pl.BlockSpec((D_MODEL, bw), lambda j: (0, j)),
                  pl.BlockSpec((1, bw), lambda j: (0, j))],
        out_specs=pl.BlockSpec((8, bw), lambda j: (0, j)),
        compiler_params=_params(1),
        name="adaln",
    )(cond8, w_ada, b_ada)


def _rot_rows(w):
    q = QK_ROPE // 4
    return jnp.concatenate([-w[q:2 * q], w[0:q], -w[3 * q:4 * q], w[2 * q:3 * q]], axis=0)


def _pack_w_in_kernel(wt_ref, o_ref):
    lo = 2 * D_FOURIER + Q_RANK + KV_RANK
    o_ref[:lo] = wt_ref[:lo].astype(BF16)
    kr = wt_ref[lo:lo + QK_ROPE]
    zeros = jnp.zeros((QK_ROPE, D_MODEL), F32)
    o_ref[lo:lo + LANES] = jnp.concatenate([_rot_rows(kr), zeros, kr, zeros], axis=0).astype(BF16)
    o_ref[lo + LANES:] = wt_ref[lo + QK_ROPE:].astype(BF16)


def _pack_w_in(w_in_t):
    return pl.pallas_call(
        _pack_w_in_kernel,
        out_shape=jax.ShapeDtypeStruct((D_IN_PAD, D_MODEL), BF16),
        grid=(1,),
        in_specs=[_const_spec((D_IN, D_MODEL))],
        out_specs=_const_spec((D_IN_PAD, D_MODEL)),
        compiler_params=_params(1),
        name="pack_w_in",
    )(w_in_t)


def _kv_outputs(ckvn_b, kr_b, w_k_ref, w_uvt_ref, ones_ref, k_ref, vt_ref):
    tm = ckvn_b.shape[0]
    kp = _dot(jnp.concatenate([ckvn_b, kr_b], axis=1), w_k_ref[...])
    for hh in range(N_HEADS):
        k_ref[hh] = kp[:, hh * HEAD_PAD:(hh + 1) * HEAD_PAD].astype(BF16)
    ones = jnp.concatenate([ones_ref[...]] * (tm // LANES), axis=1)
    vt_ref[...] = (_dot_nt(w_uvt_ref[...], ckvn_b) + ones).astype(BF16)


def _layer_in_kernel(*refs, rope, state):
    (x_ref, mod_ref, w_uf_ref, w_mid_ref, qg_ref, kvg_ref, w_uq_ref, w_k_ref, w_uvt_ref, ones_ref,
     cs_ref), refs = refs[:11], refs[11:]
    if rope:
        (cos_ref, sin_ref), refs = refs[:2], refs[2:]
    (ucs_ref, q_ref, k_ref, vt_ref), refs = refs[:4], refs[4:]
    if state:
        ckvn_ref, krope_ref = refs

    h = _modulated(x_ref, mod_ref)

    u = _dot_nt(h, w_uf_ref[...]).astype(BF16)
    for g in range(N_GROUPS):
        r = _dot(u[:, g * GROUP:(g + 1) * GROUP], cs_ref[...])
        ucs_ref[:, g * GROUP:(g + 1) * GROUP] = r[:, :GROUP].astype(BF16)
        ucs_ref[:, D_FOURIER + g * GROUP:D_FOURIER + (g + 1) * GROUP] = r[:, GROUP:].astype(BF16)

    mid = _dot_nt(h, w_mid_ref[...])
    cq, ckv, kr = mid[:, :Q_RANK], mid[:, Q_RANK:Q_RANK + KV_RANK], mid[:, Q_RANK + KV_RANK:]
    cqn = (_rms(cq) * (qg_ref[...] * Q_SCALE)).astype(BF16)
    ckvn = _rms(ckv) * kvg_ref[...]
    if state:
        ckvn_ref[...] = ckvn
        krope_ref[...] = kr[:, ROPE_LANE:ROPE_LANE + QK_ROPE]
    n_q = N_HEADS * HEAD_PAD
    if rope:
        cos, sin = cos_ref[...], sin_ref[...]
        kr = kr * cos + pltpu.roll(kr, ROPE_LANE, 1) * sin
        qp = _dot(cqn, w_uq_ref[...])
    else:
        qp = _dot(cqn, w_uq_ref[:, :n_q])
    for hh in range(N_HEADS):
        qh = qp[:, hh * HEAD_PAD:(hh + 1) * HEAD_PAD]
        if rope:
            qh = qh * cos + qp[:, n_q + hh * HEAD_PAD:n_q + (hh + 1) * HEAD_PAD] * sin
        q_ref[hh] = qh.astype(BF16)
    _kv_outputs(ckvn.astype(BF16), kr.astype(BF16), w_k_ref, w_uvt_ref, ones_ref, k_ref, vt_ref)


def _layer_in(x, mod3, w_in_p, wts, rope_tabs, *, n, state):
    t = x.shape[0]
    rope = rope_tabs is not None
    tm, per_seq, mod_row = _token_tiling(t, n, per_sequence=rope)
    tok = lambda w: pl.BlockSpec((tm, w), lambda i: (i, 0))
    in_specs = [tok(D_MODEL),
                pl.BlockSpec((1, 1, 3 * D_MODEL), lambda i: (mod_row(i), 0, 0)),
                _row_spec(W_BLK, B_UF), _row_spec(W_BLK, B_MID)]
    in_specs += [_const_spec(w.shape) for w in wts]
    args = [x, mod3, w_in_p, w_in_p, *wts]
    if rope:
        in_specs += [pl.BlockSpec((tm, HEAD_PAD), lambda i: (i % per_seq, 0))] * 2
        args += list(rope_tabs)
    head = pl.BlockSpec((N_HEADS, tm, HEAD_PAD), lambda i: (0, i, 0))
    out_specs = [tok(2 * D_FOURIER), head, head, pl.BlockSpec((VT_ROWS, tm), lambda i: (0, i))]
    out_shape = [jax.ShapeDtypeStruct((t, 2 * D_FOURIER), BF16),
                 jax.ShapeDtypeStruct((N_HEADS, t, HEAD_PAD), BF16),
                 jax.ShapeDtypeStruct((N_HEADS, t, HEAD_PAD), BF16),
                 jax.ShapeDtypeStruct((VT_ROWS, t), BF16)]
    if state:
        out_specs += [tok(KV_RANK), tok(QK_ROPE)]
        out_shape += [jax.ShapeDtypeStruct((t, KV_RANK), F32), jax.ShapeDtypeStruct((t, QK_ROPE), F32)]
    return pl.pallas_call(
        functools.partial(_layer_in_kernel, rope=rope, state=state),
        out_shape=out_shape,
        grid=(t // tm,),
        in_specs=in_specs,
        out_specs=out_specs,
        compiler_params=_params(1),
        name="layer_in_latent" if rope else "layer_in_context",
    )(*args)


def _cache_kv_kernel(ckv_ref, kr_ref, w_k_ref, w_uvt_ref, ones_ref, k_ref, vt_ref):
    _kv_outputs(ckv_ref[...].astype(BF16), kr_ref[...].astype(BF16), w_k_ref, w_uvt_ref, ones_ref,
                k_ref, vt_ref)


def _cache_kv(ckv, krope_pad, w_k, w_uvt, ones_col):
    t = ckv.shape[0]
    out_shape = [jax.ShapeDtypeStruct((N_HEADS, t, HEAD_PAD), BF16), jax.ShapeDtypeStruct((VT_ROWS, t), BF16)]
    return pl.pallas_call(
        _cache_kv_kernel,
        out_shape=out_shape,
        grid=(1,),
        in_specs=[_const_spec(a.shape) for a in (ckv, krope_pad, w_k, w_uvt, ones_col)],
        out_specs=[_const_spec(s.shape) for s in out_shape],
        compiler_params=_params(1),
        name="cache_kv",
    )(ckv, krope_pad, w_k, w_uvt, ones_col)


def _pos_dft_direct_kernel(cn_ref, sn_ref, ucs_ref, f_ref):
    n = cn_ref.shape[0]
    for s in range(f_ref.shape[0] // n):
        rows = slice(s * n, (s + 1) * n)
        f = _dot(cn_ref[...], ucs_ref[rows, :D_FOURIER]) + _dot(sn_ref[...], ucs_ref[rows, D_FOURIER:])
        f_ref[rows] = f.astype(BF16)


def _pos_dft_half_kernel(cn_ref, sn_ref, rev_ref, ucs_ref, f_ref):
    half = f_ref.shape[0] // 2
    g = _dot(cn_ref[...], ucs_ref[:, :D_FOURIER])
    hn = _dot(sn_ref[...], ucs_ref[:, D_FOURIER:])
    f_ref[:half] = (g[:half] + hn[:half]).astype(BF16)
    mirrored = (g[1:half + 1] - hn[1:half + 1]).astype(BF16)
    blocks = half // REV_BLOCK
    for a in range(blocks):
        blk = mirrored[(blocks - 1 - a) * REV_BLOCK:(blocks - a) * REV_BLOCK]
        f_ref[half + a * REV_BLOCK:half + (a + 1) * REV_BLOCK] = _dot(rev_ref[...], blk).astype(BF16)


def _pos_dft(ucs, *, n):
    t = ucs.shape[0]
    out_shape = jax.ShapeDtypeStruct((t, D_FOURIER), BF16)
    if n <= REV_BLOCK:
        cn, snn = _dft_tables(n, n)
        seqs = min(4, t // n)
        return pl.pallas_call(
            _pos_dft_direct_kernel,
            out_shape=out_shape,
            grid=(t // (n * seqs),),
            in_specs=[_const_spec(cn.shape), _const_spec(snn.shape),
                      pl.BlockSpec((n * seqs, 2 * D_FOURIER), lambda i: (i, 0))],
            out_specs=pl.BlockSpec((n * seqs, D_FOURIER), lambda i: (i, 0)),
            compiler_params=_params(1),
            name=f"pos_dft_{n}",
        )(cn, snn, ucs)
    cn, snn = _dft_tables(n, n // 2 + 8)
    rev = jnp.asarray(np.eye(REV_BLOCK, dtype=np.float32)[::-1], dtype=BF16)
    return pl.pallas_call(
        _pos_dft_half_kernel,
        out_shape=out_shape,
        grid=(t // n,),
        in_specs=[_const_spec(cn.shape), _const_spec(snn.shape), _const_spec(rev.shape),
                  pl.BlockSpec((n, 2 * D_FOURIER), lambda i: (i, 0))],
        out_specs=pl.BlockSpec((n, D_FOURIER), lambda i: (i, 0)),
        compiler_params=_params(1),
        name=f"pos_dft_{n}",
    )(cn, snn, rev, ucs)


def _attention_kernel(*refs, n_kv, single_pass):
    q_ref, refs = refs[0], refs[1:]
    k_refs, vt_refs, (o_ref, s_ref, pp_ref, ot_ref) = refs[:n_kv], refs[n_kv:2 * n_kv], refs[2 * n_kv:]
    tq = q_ref.shape[1]
    chunks, base = [], 0
    for j, k_ref in enumerate(k_refs):
        chunks += [(j, c, base + c) for c in range(0, k_ref.shape[1], KEY_CHUNK)]
        base += k_ref.shape[1]

    def finish(hh, ot):
        l = ot[V_DIM:V_DIM + 1]
        ot_ref[pl.ds(pl.multiple_of(hh * V_DIM, V_DIM), V_DIM), :] = ot[:V_DIM] / l
        return l

    def first_chunk_scores(hh):
        j, c, _ = chunks[0]
        s = _dot_nt(k_refs[j][hh, c:c + KEY_CHUNK, :], q_ref[hh])
        s_ref[0, hh * KEY_CHUNK:(hh + 1) * KEY_CHUNK, :] = s
        return jnp.max(s, axis=0, keepdims=True)

    def shifted_softmax_pv(score_head, pv_head, shifts):
        if score_head is not None:
            qh = q_ref[score_head]
            shift = shifts[score_head]
        ot = None
        for idx, (j, c, r) in enumerate(chunks):
            if score_head is not None:
                s = (s_ref[0, score_head * KEY_CHUNK:(score_head + 1) * KEY_CHUNK, :] if idx == 0 else
                     _dot_nt(k_refs[j][score_head, c:c + KEY_CHUNK, :], qh))
                pp_ref[score_head % 2, r:r + KEY_CHUNK, :] = jnp.exp2(s - shift).astype(BF16)
            if pv_head is not None:
                part = _dot(vt_refs[j][pv_head * V_ROWS:(pv_head + 1) * V_ROWS, c:c + KEY_CHUNK],
                            pp_ref[pv_head % 2, r:r + KEY_CHUNK, :])
                ot = part if ot is None else ot + part
        return finish(pv_head, ot) if pv_head is not None else None

    def exact_scores(hh, slot):
        qh = q_ref[hh]
        m8 = None
        for j, c, r in chunks:
            s = _dot_nt(k_refs[j][hh, c:c + KEY_CHUNK, :], qh)
            s_ref[slot, r:r + KEY_CHUNK, :] = s
            mc = jnp.max(s.reshape(KEY_CHUNK // 8, 8, tq), axis=0)
            m8 = mc if m8 is None else jnp.maximum(m8, mc)
        return jnp.max(m8, axis=0, keepdims=True)

    def exact_softmax_pv(hh, slot, m):
        for _, _, r in chunks:
            pp_ref[slot, r:r + KEY_CHUNK, :] = jnp.exp2(s_ref[slot, r:r + KEY_CHUNK, :] - m).astype(BF16)
        rows = pl.ds(pl.multiple_of(hh * V_ROWS, V_ROWS), V_ROWS)
        ot, base = None, 0
        for vt_ref in vt_refs:
            mk = vt_ref.shape[1]
            part = _dot(vt_ref[rows, :], pp_ref[slot, base:base + mk, :])
            ot = part if ot is None else ot + part
            base += mk
        finish(hh, ot)

    if not single_pass:
        maxima = [exact_scores(hh, hh) for hh in range(N_HEADS)]
        for hh in range(N_HEADS):
            exact_softmax_pv(hh, hh, maxima[hh])
    else:
        l_min = l_max = None
        shifts = [first_chunk_scores(hh) for hh in range(N_HEADS)]
        for hh in range(N_HEADS + 1):
            l = shifted_softmax_pv(hh if hh < N_HEADS else None, hh - 1 if hh > 0 else None, shifts)
            if l is not None:
                l_min = l if l_min is None else jnp.minimum(l_min, l)
                l_max = l if l_max is None else jnp.maximum(l_max, l)
        trusted = jnp.logical_and(jnp.min(l_min) >= MIN_DENOMINATOR, jnp.max(l_max) <= MAX_DENOMINATOR)

        @pl.when(jnp.logical_not(trusted))
        def _():
            def body(hh, carry):
                exact_softmax_pv(hh, 0, exact_scores(hh, 0))
                return carry

            lax.fori_loop(0, N_HEADS, body, 0)

    o_ref[...] = ot_ref[...].T.astype(BF16)


def _attention(q, ks, vts, key_lens, *, n, tq):
    t = q.shape[1]
    per_seq = n // tq
    m_tot = sum(key_lens)
    in_specs = [pl.BlockSpec((N_HEADS, tq, HEAD_PAD), lambda bi, i: (0, bi * per_seq + i, 0))]
    in_specs += [pl.BlockSpec((N_HEADS, m, HEAD_PAD), lambda bi, i: (0, bi, 0)) for m in key_lens]
    in_specs += [pl.BlockSpec((VT_ROWS, m), lambda bi, i: (0, bi)) for m in key_lens]
    args = [q, *ks, *vts]
    single_pass = m_tot > 2 * KEY_CHUNK
    s_slots, p_slots = (1, 2) if single_pass else (N_HEADS, N_HEADS)
    return pl.pallas_call(
        functools.partial(_attention_kernel, n_kv=len(ks), single_pass=single_pass),
        out_shape=jax.ShapeDtypeStruct((t, D_ATTN), BF16),
        grid=(t // n, per_seq),
        in_specs=in_specs,
        out_specs=pl.BlockSpec((tq, D_ATTN), lambda bi, i: (bi * per_seq + i, 0)),
        scratch_shapes=[pltpu.VMEM((s_slots, m_tot, tq), F32), pltpu.VMEM((p_slots, m_tot, tq), BF16),
                        pltpu.VMEM((D_ATTN, tq), F32)],
        compiler_params=_params(2),
        name=f"attention_{n}",
    )(*args)


def _merge_kernel(x_ref, mod_ref, f_ref, a_ref, w_zf_ref, w_za_ref, w_g_ref, w_f_ref, w_a_ref, w_o_ref,
                  g_ref, y_ref):
    h = _modulated(x_ref, mod_ref)
    zf = _dot_nt(h, w_zf_ref[...])
    y_f = _dot((f_ref[...] * (zf * _sigmoid(zf))).astype(BF16), w_f_ref[...])
    za = _dot_nt(h, w_za_ref[...])
    y_a = _dot((a_ref[...] * (za * _sigmoid(za))).astype(BF16), w_a_ref[...])
    merged = (_sigmoid(_dot_nt(h, w_g_ref[:D_MODEL])) * y_f
              + _sigmoid(_dot_nt(h, w_g_ref[D_MODEL:])) * y_a)
    gate = mod_ref[0][:, 2 * D_MODEL:]
    out = x_ref[...] + gate * _dot(merged.astype(BF16), w_o_ref[...])
    y_ref[...] = _rms(out) * g_ref[...]


def _merge(x, mod3, f, attn, w_in_p, w_f, w_a, w_o, g, *, n, latent):
    t = x.shape[0]
    tm, _, mod_row = _token_tiling(t, n, per_sequence=latent)
    tok = lambda w: pl.BlockSpec((tm, w), lambda i: (i, 0))
    return pl.pallas_call(
        _merge_kernel,
        out_shape=jax.ShapeDtypeStruct((t, D_MODEL), F32),
        grid=(t // tm,),
        in_specs=[tok(D_MODEL),
                  pl.BlockSpec((1, 1, 3 * D_MODEL), lambda i: (mod_row(i), 0, 0)),
                  tok(D_FOURIER), tok(D_ATTN),
                  _row_spec(W_BLK, B_ZF), _row_spec(W_BLK, B_ZA), _row_spec(2 * D_MODEL, 1),
                  _const_spec(w_f.shape), _const_spec(w_a.shape), _const_spec(w_o.shape),
                  _const_spec(g.shape)],
        out_specs=tok(D_MODEL),
        compiler_params=_params(1),
        name=f"merge_{n}",
    )(x, mod3, f, attn, w_in_p, w_in_p, w_in_p, w_f, w_a, w_o, g)


def _dft_tables(n, rows):
    norm = 1.0 / math.sqrt(n)
    if n <= 256:
        k = jnp.arange(n, dtype=jnp.int32)
        ang = ((k[:rows, None] * k[None, :]) % n).astype(F32) * (2.0 * math.pi / n)
        return (jnp.cos(ang) * norm).astype(BF16), (jnp.sin(ang) * -norm).astype(BF16)
    lo = 64
    period = n // lo
    hi = -(-rows // lo)
    pos = jnp.arange(n, dtype=jnp.int32)
    a1 = ((jnp.arange(hi, dtype=jnp.int32)[:, None] * pos[None, :]) % period).astype(F32) * (2.0 * math.pi / period)
    a0 = ((jnp.arange(lo, dtype=jnp.int32)[:, None] * pos[None, :]) % n).astype(F32) * (2.0 * math.pi / n)
    c1, s1 = jnp.cos(a1)[:, None, :], jnp.sin(a1)[:, None, :]
    c0, s0 = jnp.cos(a0)[None, :, :] * norm, jnp.sin(a0)[None, :, :] * norm
    cn = (c1 * c0 - s1 * s0).reshape(hi * lo, n)[:rows]
    sn = (s1 * c0 + c1 * s0).reshape(hi * lo, n)[:rows]
    return cn.astype(BF16), (-sn).astype(BF16)


def _channel_dft_table():
    c = jnp.arange(GROUP, dtype=jnp.int32)
    ang = ((c[:, None] * c[None, :]) % GROUP).astype(F32) * (2.0 * math.pi / GROUP)
    norm = 1.0 / math.sqrt(GROUP)
    return (jnp.concatenate([jnp.cos(ang), jnp.sin(ang)], axis=1) * norm).astype(BF16)


def _rope_tables(n):
    t = jnp.arange(n)
    row = (t // GRID_W).astype(F32)
    col = (t % GRID_W).astype(F32)
    half = QK_ROPE // 2
    inv = ROPE_THETA ** (-jnp.arange(0, half, 2, dtype=F32) / half)
    ar, ac = row[:, None] * inv, col[:, None] * inv
    ang = jnp.concatenate([ar, ar, ac, ac], axis=-1)
    pad = lambda a, fill: jnp.concatenate(
        [jnp.full((n, ROPE_LANE), fill, F32), a, jnp.full((n, HEAD_PAD - ROPE_LANE - QK_ROPE), fill, F32)], axis=1)
    return pad(jnp.cos(ang), 1.0), pad(jnp.sin(ang), 0.0)


def kernel(x_prompt, x_sample, cache_ckv, cache_krope, c, c_ctx, w_ada, b_ada, w_in, q_norm_g, w_uq,
           kv_norm_g, w_ukv, w_f_out, w_a_out, w_out, final_norm_g):
    assert w_in.shape[0] == 1
    b_ctx, n_ctx, _ = x_prompt.shape
    dec_b, n_lat, _ = x_sample.shape
    past = cache_ckv.shape[2]

    w_q3 = w_uq[0].reshape(Q_RANK, N_HEADS, QK_NOPE + QK_ROPE)
    head_pad = lambda a, left: jnp.pad(
        a, ((0, 0), (0, 0), (left, HEAD_PAD - left - a.shape[2]))).reshape(Q_RANK, N_HEADS * HEAD_PAD)
    rp = w_q3[:, :, QK_NOPE:].reshape(Q_RANK, N_HEADS, 2, 2, QK_ROPE // 4)
    w_q_rot = jnp.stack([-rp[:, :, :, 1], rp[:, :, :, 0]], axis=3).reshape(Q_RANK, N_HEADS, QK_ROPE)
    w_uq_p = jnp.concatenate([head_pad(w_q3, 0), head_pad(w_q_rot, ROPE_LANE)], axis=1).astype(BF16)
    w_kv3 = w_ukv[0].reshape(KV_RANK, N_HEADS, QK_NOPE + V_DIM)
    w_knope = jnp.pad(w_kv3[:, :, :QK_NOPE], ((0, 0), (0, 0), (0, HEAD_PAD - QK_NOPE)))
    place = np.zeros((HEAD_PAD, N_HEADS, HEAD_PAD), np.float32)
    for j in range(QK_ROPE):
        place[ROPE_LANE + j, :, ROPE_LANE + j] = 1.0
    w_k = jnp.concatenate([w_knope.reshape(KV_RANK, -1), jnp.asarray(place).reshape(HEAD_PAD, -1)],
                          axis=0).astype(BF16)
    w_uvt = jnp.pad(jnp.transpose(w_kv3[:, :, QK_NOPE:], (1, 2, 0)),
                    ((0, 0), (0, V_ROWS - V_DIM), (0, 0))).reshape(VT_ROWS, KV_RANK).astype(BF16)
    ones_col = np.zeros((N_HEADS, V_ROWS, LANES), np.float32)
    ones_col[:, V_DIM, :] = 1.0
    ones_col = jnp.asarray(ones_col.reshape(VT_ROWS, LANES))
    w_f = w_f_out[0].astype(BF16)
    w_a = w_a_out[0].astype(BF16)
    w_o = w_out[0].astype(BF16)
    qg = q_norm_g[0].reshape(1, Q_RANK)
    kvg = kv_norm_g[0].reshape(1, KV_RANK)
    fg = final_norm_g.reshape(1, D_MODEL)
    wts = (qg, kvg, w_uq_p, w_k, w_uvt, ones_col, _channel_dft_table())
    w_in_p = _pack_w_in(jnp.swapaxes(w_in[0], 0, 1))

    cond8 = jnp.concatenate([c_ctx[None, :], c, jnp.zeros((8 - 1 - dec_b, D_MODEL), F32)], axis=0)
    mod3 = _adaln(cond8, w_ada[0], b_ada[0].reshape(1, -1)).reshape(8, 1, 3 * D_MODEL)

    xp = x_prompt.reshape(b_ctx * n_ctx, D_MODEL)
    ucs, q, k, vt, state_ckv, state_krope = _layer_in(xp, mod3, w_in_p, wts, None, n=n_ctx, state=True)
    f = _pos_dft(ucs, n=n_ctx)
    attn = _attention(q, [k], [vt], [n_ctx], n=n_ctx, tq=n_ctx)
    y_prompt = _merge(xp, mod3, f, attn, w_in_p, w_f, w_a, w_o, fg, n=n_ctx, latent=False)

    xs = x_sample.reshape(dec_b * n_lat, D_MODEL)
    ucs, q, k, vt = _layer_in(xs, mod3, w_in_p, wts, _rope_tables(n_lat), n=n_lat, state=False)
    f = _pos_dft(ucs, n=n_lat)
    krope_pad = jnp.pad(cache_krope[:, 0], ((0, 0), (0, 0), (ROPE_LANE, HEAD_PAD - ROPE_LANE - QK_ROPE)))
    k_c, vt_c = _cache_kv(cache_ckv[:, 0].reshape(dec_b * past, KV_RANK),
                          krope_pad.reshape(dec_b * past, HEAD_PAD), w_k, w_uvt, ones_col)
    attn = _attention(q, [k, k_c], [vt, vt_c], [n_lat, past], n=n_lat, tq=256)
    y_sample = _merge(xs, mod3, f, attn, w_in_p, w_f, w_a, w_o, fg, n=n_lat, latent=True)

    return (y_prompt.reshape(b_ctx, n_ctx, D_MODEL), y_sample.reshape(dec_b, n_lat, D_MODEL),
            state_ckv.reshape(b_ctx, 1, n_ctx, KV_RANK), state_krope.reshape(b_ctx, 1, n_ctx, QK_ROPE))
```

```python
import functools
import math

import jax
import jax.numpy as jnp
import numpy as np
from jax import lax
from jax.experimental import pallas as pl
from jax.experimental.pallas import tpu as pltpu

F32 = jnp.float32
BF16 = jnp.bfloat16

D_MODEL = 1024
GRID_W = 64
N_GROUPS = 4
GROUP = 128
D_FOURIER = N_GROUPS * GROUP
N_HEADS = 8
QK_NOPE = 64
QK_ROPE = 32
V_DIM = 64
Q_RANK = 256
KV_RANK = 128
D_ATTN = N_HEADS * V_DIM
D_IN = 2 * D_FOURIER + Q_RANK + KV_RANK + QK_ROPE + D_ATTN + 2 * D_MODEL
ROPE_THETA = 10000.0
EPS = 1e-6
LANES = 128
HEAD_PAD = LANES
ROPE_LANE = QK_NOPE
V_ROWS = V_DIM + 16
VT_ROWS = N_HEADS * V_ROWS
Q_SCALE = (QK_NOPE + QK_ROPE) ** -0.5 * math.log2(math.e)
KEY_CHUNK = 256
REV_BLOCK = 256
DFT_SPLIT = 64
MIN_DENOMINATOR = 2.0 ** -60
MAX_DENOMINATOR = 2.0 ** 60
W_BLK = 512
B_UF, B_ZF, B_MID, B_ZA = 0, 1, 2, 3
D_IN_PAD = 8 * W_BLK
TOKEN_TILE = 1024
VMEM_LIMIT_BYTES = 56 * 1024 * 1024


def _const_spec(shape):
    nd = len(shape)
    return pl.BlockSpec(shape, lambda *_: (0,) * nd, pipeline_mode=pl.Buffered(1))


def _row_spec(rows, blk):
    return pl.BlockSpec((rows, D_MODEL), lambda *_: (blk, 0), pipeline_mode=pl.Buffered(1))


def _params(n_axes):
    return pltpu.CompilerParams(dimension_semantics=("arbitrary",) * n_axes,
                                vmem_limit_bytes=VMEM_LIMIT_BYTES)


def _rms(x):
    return x * lax.rsqrt(jnp.mean(x * x, axis=-1, keepdims=True) + EPS)


def _sigmoid(x):
    return 1.0 / (1.0 + jnp.exp(-x))


def _dot(a, b):
    return jnp.dot(a, b, preferred_element_type=F32)


def _dot_nt(a, b):
    return lax.dot_general(a, b, (((1,), (1,)), ((), ())), preferred_element_type=F32)


def _token_tiling(t, n, *, per_sequence):
    tm = TOKEN_TILE
    assert t % tm == 0 and (not per_sequence or n % tm == 0)
    per_seq = n // tm if per_sequence else None
    mod_row = (lambda i: 1 + i // per_seq) if per_sequence else (lambda i: 0)
    return tm, per_seq, mod_row


def _modulated(x_ref, mod_ref):
    mod = mod_ref[0]
    shift, scale = mod[:, 0:D_MODEL], mod[:, D_MODEL:2 * D_MODEL]
    return (_rms(x_ref[...]) * (1.0 + scale) + shift).astype(BF16)


def _adaln_kernel(c_ref, w_ref, b_ref, o_ref):
    c = c_ref[...]
    s = (c * _sigmoid(c)).astype(BF16)
    o_ref[...] = _dot(s, w_ref[...].astype(BF16)) + b_ref[...]


def _adaln(cond8, w_ada, b_ada):
    n_blk = 4
    bw = 3 * D_MODEL // n_blk
    return pl.pallas_call(
        _adaln_kernel,
        out_shape=jax.ShapeDtypeStruct((8, 3 * D_MODEL), F32),
        grid=(n_blk,),
        in_specs=[pl.BlockSpec((8, D_MODEL), lambda j: (0, 0)),
                  pl.BlockSpec((D_MODEL, bw), lambda j: (0, j)),
                  pl.BlockSpec((1, bw), lambda j: (0, j))],
        out_specs=pl.BlockSpec((8, bw), lambda j: (0, j)),
        compiler_params=_params(1),
        name="adaln",
    )(cond8, w_ada, b_ada)


def _rot_rows(w):
    q = QK_ROPE // 4
    return jnp.concatenate([-w[q:2 * q], w[0:q], -w[3 * q:4 * q], w[2 * q:3 * q]], axis=0)


def _pack_w_in_kernel(wt_ref, o_ref):
    lo = 2 * D_FOURIER + Q_RANK + KV_RANK
    o_ref[:lo] = wt_ref[:lo].astype(BF16)
    kr = wt_ref[lo:lo + QK_ROPE]
    zeros = jnp.zeros((QK_ROPE, D_MODEL), F32)
    o_ref[lo:lo + LANES] = jnp.concatenate([_rot_rows(kr), zeros, kr, zeros], axis=0).astype(BF16)
    o_ref[lo + LANES:] = wt_ref[lo + QK_ROPE:].astype(BF16)


def _pack_w_in(w_in_t):
    return pl.pallas_call(
        _pack_w_in_kernel,
        out_shape=jax.ShapeDtypeStruct((D_IN_PAD, D_MODEL), BF16),
        grid=(1,),
        in_specs=[_const_spec((D_IN, D_MODEL))],
        out_specs=_const_spec((D_IN_PAD, D_MODEL)),
        compiler_params=_params(1),
        name="pack_w_in",
    )(w_in_t)


def _kv_outputs(ckvn_b, kr_b, w_k_ref, w_uvt_ref, ones_ref, k_ref, vt_ref):
    tm = ckvn_b.shape[0]
    kp = _dot(jnp.concatenate([ckvn_b, kr_b], axis=1), w_k_ref[...])
    for hh in range(N_HEADS):
        k_ref[hh] = kp[:, hh * HEAD_PAD:(hh + 1) * HEAD_PAD].astype(BF16)
    ones = jnp.concatenate([ones_ref[...]] * (tm // LANES), axis=1)
    vt_ref[...] = (_dot_nt(w_uvt_ref[...], ckvn_b) + ones).astype(BF16)


def _layer_in_kernel(*refs, rope, state):
    (x_ref, mod_ref, w_uf_ref, w_mid_ref, qg_ref, kvg_ref, w_uq_ref, w_k_ref, w_uvt_ref, ones_ref,
     cs_ref), refs = refs[:11], refs[11:]
    if rope:
        (cos_ref, sin_ref), refs = refs[:2], refs[2:]
    (ucs_ref, q_ref, k_ref, vt_ref), refs = refs[:4], refs[4:]
    if state:
        ckvn_ref, krope_ref = refs

    h = _modulated(x_ref, mod_ref)

    u = _dot_nt(h, w_uf_ref[...]).astype(BF16)
    for g in range(N_GROUPS):
        r = _dot(u[:, g * GROUP:(g + 1) * GROUP], cs_ref[...])
        ucs_ref[:, g * GROUP:(g + 1) * GROUP] = r[:, :GROUP].astype(BF16)
        ucs_ref[:, D_FOURIER + g * GROUP:D_FOURIER + (g + 1) * GROUP] = r[:, GROUP:].astype(BF16)

    mid = _dot_nt(h, w_mid_ref[...])
    cq, ckv, kr = mid[:, :Q_RANK], mid[:, Q_RANK:Q_RANK + KV_RANK], mid[:, Q_RANK + KV_RANK:]
    cqn = (_rms(cq) * (qg_ref[...] * Q_SCALE)).astype(BF16)
    ckvn = _rms(ckv) * kvg_ref[...]
    if state:
        ckvn_ref[...] = ckvn
        krope_ref[...] = kr[:, ROPE_LANE:ROPE_LANE + QK_ROPE]
    n_q = N_HEADS * HEAD_PAD
    if rope:
        cos, sin = cos_ref[...], sin_ref[...]
        kr = kr * cos + pltpu.roll(kr, ROPE_LANE, 1) * sin
        qp = _dot(cqn, w_uq_ref[...])
    else:
        qp = _dot(cqn, w_uq_ref[:, :n_q])
    for hh in range(N_HEADS):
        qh = qp[:, hh * HEAD_PAD:(hh + 1) * HEAD_PAD]
        if rope:
            qh = qh * cos + qp[:, n_q + hh * HEAD_PAD:n_q + (hh + 1) * HEAD_PAD] * sin
        q_ref[hh] = qh.astype(BF16)
    _kv_outputs(ckvn.astype(BF16), kr.astype(BF16), w_k_ref, w_uvt_ref, ones_ref, k_ref, vt_ref)


def _layer_in(x, mod3, w_in_p, wts, rope_tabs, *, n, state):
    t = x.shape[0]
    rope = rope_tabs is not None
    tm, per_seq, mod_row = _token_tiling(t, n, per_sequence=rope)
    tok = lambda w: pl.BlockSpec((tm, w), lambda i: (i, 0))
    in_specs = [tok(D_MODEL),
                pl.BlockSpec((1, 1, 3 * D_MODEL), lambda i: (mod_row(i), 0, 0)),
                _row_spec(W_BLK, B_UF), _row_spec(W_BLK, B_MID)]
    in_specs += [_const_spec(w.shape) for w in wts]
    args = [x, mod3, w_in_p, w_in_p, *wts]
    if rope:
        in_specs += [pl.BlockSpec((tm, HEAD_PAD), lambda i: (i % per_seq, 0))] * 2
        args += list(rope_tabs)
    head = pl.BlockSpec((N_HEADS, tm, HEAD_PAD), lambda i: (0, i, 0))
    out_specs = [tok(2 * D_FOURIER), head, head, pl.BlockSpec((VT_ROWS, tm), lambda i: (0, i))]
    out_shape = [jax.ShapeDtypeStruct((t, 2 * D_FOURIER), BF16),
                 jax.ShapeDtypeStruct((N_HEADS, t, HEAD_PAD), BF16),
                 jax.ShapeDtypeStruct((N_HEADS, t, HEAD_PAD), BF16),
                 jax.ShapeDtypeStruct((VT_ROWS, t), BF16)]
    if state:
        out_specs += [tok(KV_RANK), tok(QK_ROPE)]
        out_shape += [jax.ShapeDtypeStruct((t, KV_RANK), F32), jax.ShapeDtypeStruct((t, QK_ROPE), F32)]
    return pl.pallas_call(
        functools.partial(_layer_in_kernel, rope=rope, state=state),
        out_shape=out_shape,
        grid=(t // tm,),
        in_specs=in_specs,
        out_specs=out_specs,
        compiler_params=_params(1),
        name="layer_in_latent" if rope else "layer_in_context",
    )(*args)


def _cache_kv_kernel(ckv_ref, kr_ref, w_k_ref, w_uvt_ref, ones_ref, k_ref, vt_ref):
    _kv_outputs(ckv_ref[...].astype(BF16), kr_ref[...].astype(BF16), w_k_ref, w_uvt_ref, ones_ref,
                k_ref, vt_ref)


def _cache_kv(ckv, krope_pad, w_k, w_uvt, ones_col):
    t = ckv.shape[0]
    out_shape = [jax.ShapeDtypeStruct((N_HEADS, t, HEAD_PAD), BF16), jax.ShapeDtypeStruct((VT_ROWS, t), BF16)]
    return pl.pallas_call(
        _cache_kv_kernel,
        out_shape=out_shape,
        grid=(1,),
        in_specs=[_const_spec(a.shape) for a in (ckv, krope_pad, w_k, w_uvt, ones_col)],
        out_specs=[_const_spec(s.shape) for s in out_shape],
        compiler_params=_params(1),
        name="cache_kv",
    )(ckv, krope_pad, w_k, w_uvt, ones_col)


def _pos_dft_direct_kernel(cn_ref, sn_ref, ucs_ref, f_ref):
    n = cn_ref.shape[0]
    for s in range(f_ref.shape[0] // n):
        rows = slice(s * n, (s + 1) * n)
        f = _dot(cn_ref[...], ucs_ref[rows, :D_FOURIER]) + _dot(sn_ref[...], ucs_ref[rows, D_FOURIER:])
        f_ref[rows] = f.astype(BF16)


def _pos_dft_half_kernel(c1_ref, s1_ref, c0_ref, s0_ref, rev_ref, ucs_ref, f_ref, cn_ref, sn_ref):
    half = f_ref.shape[0] // 2

    @pl.when(pl.program_id(0) == 0)
    def _():
        c0, s0 = c0_ref[...], s0_ref[...]
        for k1 in range(c1_ref.shape[0]):
            rows = slice(k1 * DFT_SPLIT, min((k1 + 1) * DFT_SPLIT, cn_ref.shape[0]))
            m = rows.stop - rows.start
            c1, s1 = c1_ref[k1:k1 + 1, :], s1_ref[k1:k1 + 1, :]
            cn_ref[rows] = (c1 * c0[:m] - s1 * s0[:m]).astype(BF16)
            sn_ref[rows] = (-(s1 * c0[:m] + c1 * s0[:m])).astype(BF16)

    g = _dot(cn_ref[...], ucs_ref[:, :D_FOURIER])
    hn = _dot(sn_ref[...], ucs_ref[:, D_FOURIER:])
    f_ref[:half] = (g[:half] + hn[:half]).astype(BF16)
    mirrored = (g[1:half + 1] - hn[1:half + 1]).astype(BF16)
    blocks = half // REV_BLOCK
    for a in range(blocks):
        blk = mirrored[(blocks - 1 - a) * REV_BLOCK:(blocks - a) * REV_BLOCK]
        f_ref[half + a * REV_BLOCK:half + (a + 1) * REV_BLOCK] = _dot(rev_ref[...], blk).astype(BF16)


def _pos_dft(ucs, *, n):
    t = ucs.shape[0]
    out_shape = jax.ShapeDtypeStruct((t, D_FOURIER), BF16)
    if n <= REV_BLOCK:
        cn, snn = _dft_tables(n)
        seqs = min(4, t // n)
        return pl.pallas_call(
            _pos_dft_direct_kernel,
            out_shape=out_shape,
            grid=(t // (n * seqs),),
            in_specs=[_const_spec(cn.shape), _const_spec(snn.shape),
                      pl.BlockSpec((n * seqs, 2 * D_FOURIER), lambda i: (i, 0))],
            out_specs=pl.BlockSpec((n * seqs, D_FOURIER), lambda i: (i, 0)),
            compiler_params=_params(1),
            name=f"pos_dft_{n}",
        )(cn, snn, ucs)
    rows = n // 2 + 8
    factors = _dft_factors(n, rows)
    rev = jnp.asarray(np.eye(REV_BLOCK, dtype=np.float32)[::-1], dtype=BF16)
    return pl.pallas_call(
        _pos_dft_half_kernel,
        out_shape=out_shape,
        grid=(t // n,),
        in_specs=[_const_spec(a.shape) for a in (*factors, rev)]
        + [pl.BlockSpec((n, 2 * D_FOURIER), lambda i: (i, 0))],
        out_specs=pl.BlockSpec((n, D_FOURIER), lambda i: (i, 0)),
        scratch_shapes=[pltpu.VMEM((rows, n), BF16), pltpu.VMEM((rows, n), BF16)],
        compiler_params=_params(1),
        name=f"pos_dft_{n}",
    )(*factors, rev, ucs)


def _attention_kernel(*refs, n_kv, single_pass, seqs):
    q_ref, refs = refs[0], refs[1:]
    k_refs, vt_refs, (o_ref, s_ref, pp_ref, ot_ref) = refs[:n_kv], refs[n_kv:2 * n_kv], refs[2 * n_kv:]
    tq = q_ref.shape[1] // seqs
    key_lens = [k_ref.shape[1] // seqs for k_ref in k_refs]
    chunks, base = [], 0
    for j, m in enumerate(key_lens):
        chunks += [(j, c, base + c) for c in range(0, m, KEY_CHUNK)]
        base += m

    def finish(hh, ot, seq=0):
        l = ot[V_DIM:V_DIM + 1]
        ot_ref[seq, pl.ds(pl.multiple_of(hh * V_DIM, V_DIM), V_DIM), :] = ot[:V_DIM] / l
        return l

    def first_chunk_scores(hh):
        j, c, _ = chunks[0]
        s = _dot_nt(k_refs[j][hh, c:c + KEY_CHUNK, :], q_ref[hh])
        s_ref[0, hh * KEY_CHUNK:(hh + 1) * KEY_CHUNK, :] = s
        return jnp.max(s, axis=0, keepdims=True)

    def shifted_softmax_pv(score_head, pv_head, shifts):
        if score_head is not None:
            qh = q_ref[score_head]
            shift = shifts[score_head]
        ot = None
        for idx, (j, c, r) in enumerate(chunks):
            if score_head is not None:
                s = (s_ref[0, score_head * KEY_CHUNK:(score_head + 1) * KEY_CHUNK, :] if idx == 0 else
                     _dot_nt(k_refs[j][score_head, c:c + KEY_CHUNK, :], qh))
                pp_ref[score_head % 2, r:r + KEY_CHUNK, :] = jnp.exp2(s - shift).astype(BF16)
            if pv_head is not None:
                part = _dot(vt_refs[j][pv_head * V_ROWS:(pv_head + 1) * V_ROWS, c:c + KEY_CHUNK],
                            pp_ref[pv_head % 2, r:r + KEY_CHUNK, :])
                ot = part if ot is None else ot + part
        return finish(pv_head, ot) if pv_head is not None else None

    def exact_scores(hh, slot, seq=0):
        qh = q_ref[hh, seq * tq:(seq + 1) * tq, :]
        m8 = None
        for j, c, r in chunks:
            k0 = seq * key_lens[j] + c
            s = _dot_nt(k_refs[j][hh, k0:k0 + KEY_CHUNK, :], qh)
            s_ref[slot, r:r + KEY_CHUNK, :] = s
            mc = jnp.max(s.reshape(KEY_CHUNK // 8, 8, tq), axis=0)
            m8 = mc if m8 is None else jnp.maximum(m8, mc)
        return jnp.max(m8, axis=0, keepdims=True)

    def exact_softmax_pv(hh, slot, m, seq=0):
        for _, _, r in chunks:
            pp_ref[slot, r:r + KEY_CHUNK, :] = jnp.exp2(s_ref[slot, r:r + KEY_CHUNK, :] - m).astype(BF16)
        rows = pl.ds(pl.multiple_of(hh * V_ROWS, V_ROWS), V_ROWS)
        ot, base = None, 0
        for j, vt_ref in enumerate(vt_refs):
            mk = key_lens[j]
            part = _dot(vt_ref[rows, seq * mk:(seq + 1) * mk], pp_ref[slot, base:base + mk, :])
            ot = part if ot is None else ot + part
            base += mk
        finish(hh, ot, seq)

    if not single_pass:
        work = [(seq, hh) for seq in range(seqs) for hh in range(N_HEADS)]
        maxima = [exact_scores(hh, slot, seq) for slot, (seq, hh) in enumerate(work)]
        for slot, (seq, hh) in enumerate(work):
            exact_softmax_pv(hh, slot, maxima[slot], seq)
    else:
        l_min = l_max = None
        shifts = [first_chunk_scores(hh) for hh in range(N_HEADS)]
        for hh in range(N_HEADS + 1):
            l = shifted_softmax_pv(hh if hh < N_HEADS else None, hh - 1 if hh > 0 else None, shifts)
            if l is not None:
                l_min = l if l_min is None else jnp.minimum(l_min, l)
                l_max = l if l_max is None else jnp.maximum(l_max, l)
        trusted = jnp.logical_and(jnp.min(l_min) >= MIN_DENOMINATOR, jnp.max(l_max) <= MAX_DENOMINATOR)

        @pl.when(jnp.logical_not(trusted))
        def _():
            def body(hh, carry):
                exact_softmax_pv(hh, 0, exact_scores(hh, 0))
                return carry

            lax.fori_loop(0, N_HEADS, body, 0)

    for seq in range(seqs):
        o_ref[seq * tq:(seq + 1) * tq, :] = ot_ref[seq].T.astype(BF16)


def _attention(q, ks, vts, key_lens, *, n, tq):
    t = q.shape[1]
    per_seq = n // tq
    m_tot = sum(key_lens)
    single_pass = m_tot > 2 * KEY_CHUNK
    seqs = 1 if single_pass else min(4, t // n)
    assert single_pass or per_seq == 1
    s_slots, p_slots = (1, 2) if single_pass else (seqs * N_HEADS,) * 2
    in_specs = [pl.BlockSpec((N_HEADS, seqs * tq, HEAD_PAD), lambda bi, i: (0, bi * per_seq + i, 0))]
    in_specs += [pl.BlockSpec((N_HEADS, seqs * m, HEAD_PAD), lambda bi, i: (0, bi, 0)) for m in key_lens]
    in_specs += [pl.BlockSpec((VT_ROWS, seqs * m), lambda bi, i: (0, bi)) for m in key_lens]
    return pl.pallas_call(
        functools.partial(_attention_kernel, n_kv=len(ks), single_pass=single_pass, seqs=seqs),
        out_shape=jax.ShapeDtypeStruct((t, D_ATTN), BF16),
        grid=(t // (n * seqs), per_seq),
        in_specs=in_specs,
        out_specs=pl.BlockSpec((seqs * tq, D_ATTN), lambda bi, i: (bi * per_seq + i, 0)),
        scratch_shapes=[pltpu.VMEM((s_slots, m_tot, tq), F32), pltpu.VMEM((p_slots, m_tot, tq), BF16),
                        pltpu.VMEM((seqs, D_ATTN, tq), F32)],
        compiler_params=_params(2),
        name=f"attention_{n}",
    )(q, *ks, *vts)


def _merge_kernel(x_ref, mod_ref, f_ref, a_ref, w_zf_ref, w_za_ref, w_g_ref, w_f_ref, w_a_ref, w_o_ref,
                  g_ref, y_ref):
    h = _modulated(x_ref, mod_ref)
    zf = _dot_nt(h, w_zf_ref[...])
    y_f = _dot((f_ref[...] * (zf * _sigmoid(zf))).astype(BF16), w_f_ref[...])
    za = _dot_nt(h, w_za_ref[...])
    y_a = _dot((a_ref[...] * (za * _sigmoid(za))).astype(BF16), w_a_ref[...])
    merged = (_sigmoid(_dot_nt(h, w_g_ref[:D_MODEL])) * y_f
              + _sigmoid(_dot_nt(h, w_g_ref[D_MODEL:])) * y_a)
    gate = mod_ref[0][:, 2 * D_MODEL:]
    out = x_ref[...] + gate * _dot(merged.astype(BF16), w_o_ref[...])
    y_ref[...] = _rms(out) * g_ref[...]


def _merge(x, mod3, f, attn, w_in_p, w_f, w_a, w_o, g, *, n, latent):
    t = x.shape[0]
    tm, _, mod_row = _token_tiling(t, n, per_sequence=latent)
    tok = lambda w: pl.BlockSpec((tm, w), lambda i: (i, 0))
    return pl.pallas_call(
        _merge_kernel,
        out_shape=jax.ShapeDtypeStruct((t, D_MODEL), F32),
        grid=(t // tm,),
        in_specs=[tok(D_MODEL),
                  pl.BlockSpec((1, 1, 3 * D_MODEL), lambda i: (mod_row(i), 0, 0)),
                  tok(D_FOURIER), tok(D_ATTN),
                  _row_spec(W_BLK, B_ZF), _row_spec(W_BLK, B_ZA), _row_spec(2 * D_MODEL, 1),
                  _const_spec(w_f.shape), _const_spec(w_a.shape), _const_spec(w_o.shape),
                  _const_spec(g.shape)],
        out_specs=tok(D_MODEL),
        compiler_params=_params(1),
        name=f"merge_{n}",
    )(x, mod3, f, attn, w_in_p, w_in_p, w_in_p, w_f, w_a, w_o, g)


def _dft_tables(n):
    norm = 1.0 / math.sqrt(n)
    k = jnp.arange(n, dtype=jnp.int32)
    ang = ((k[:, None] * k[None, :]) % n).astype(F32) * (2.0 * math.pi / n)
    return (jnp.cos(ang) * norm).astype(BF16), (jnp.sin(ang) * -norm).astype(BF16)


def _dft_factors(n, rows):
    norm = 1.0 / math.sqrt(n)
    period = n // DFT_SPLIT
    pos = jnp.arange(n, dtype=jnp.int32)
    k1 = jnp.arange(-(-rows // DFT_SPLIT), dtype=jnp.int32)
    k0 = jnp.arange(DFT_SPLIT, dtype=jnp.int32)
    a1 = ((k1[:, None] * pos[None, :]) % period).astype(F32) * (2.0 * math.pi / period)
    a0 = ((k0[:, None] * pos[None, :]) % n).astype(F32) * (2.0 * math.pi / n)
    return jnp.cos(a1), jnp.sin(a1), jnp.cos(a0) * norm, jnp.sin(a0) * norm


def _channel_dft_table():
    c = jnp.arange(GROUP, dtype=jnp.int32)
    ang = ((c[:, None] * c[None, :]) % GROUP).astype(F32) * (2.0 * math.pi / GROUP)
    norm = 1.0 / math.sqrt(GROUP)
    return (jnp.concatenate([jnp.cos(ang), jnp.sin(ang)], axis=1) * norm).astype(BF16)


def _rope_tables(n):
    t = jnp.arange(n)
    row = (t // GRID_W).astype(F32)
    col = (t % GRID_W).astype(F32)
    half = QK_ROPE // 2
    inv = ROPE_THETA ** (-jnp.arange(0, half, 2, dtype=F32) / half)
    ar, ac = row[:, None] * inv, col[:, None] * inv
    ang = jnp.concatenate([ar, ar, ac, ac], axis=-1)
    pad = lambda a, fill: jnp.concatenate(
        [jnp.full((n, ROPE_LANE), fill, F32), a, jnp.full((n, HEAD_PAD - ROPE_LANE - QK_ROPE), fill, F32)], axis=1)
    return pad(jnp.cos(ang), 1.0), pad(jnp.sin(ang), 0.0)


def kernel(x_prompt, x_sample, cache_ckv, cache_krope, c, c_ctx, w_ada, b_ada, w_in, q_norm_g, w_uq,
           kv_norm_g, w_ukv, w_f_out, w_a_out, w_out, final_norm_g):
    assert w_in.shape[0] == 1
    b_ctx, n_ctx, _ = x_prompt.shape
    dec_b, n_lat, _ = x_sample.shape
    past = cache_ckv.shape[2]

    w_q3 = w_uq[0].reshape(Q_RANK, N_HEADS, QK_NOPE + QK_ROPE)
    head_pad = lambda a, left: jnp.pad(
        a, ((0, 0), (0, 0), (left, HEAD_PAD - left - a.shape[2]))).reshape(Q_RANK, N_HEADS * HEAD_PAD)
    rp = w_q3[:, :, QK_NOPE:].reshape(Q_RANK, N_HEADS, 2, 2, QK_ROPE // 4)
    w_q_rot = jnp.stack([-rp[:, :, :, 1], rp[:, :, :, 0]], axis=3).reshape(Q_RANK, N_HEADS, QK_ROPE)
    w_uq_p = jnp.concatenate([head_pad(w_q3, 0), head_pad(w_q_rot, ROPE_LANE)], axis=1).astype(BF16)
    w_kv3 = w_ukv[0].reshape(KV_RANK, N_HEADS, QK_NOPE + V_DIM)
    w_knope = jnp.pad(w_kv3[:, :, :QK_NOPE], ((0, 0), (0, 0), (0, HEAD_PAD - QK_NOPE)))
    place = np.zeros((HEAD_PAD, N_HEADS, HEAD_PAD), np.float32)
    for j in range(QK_ROPE):
        place[ROPE_LANE + j, :, ROPE_LANE + j] = 1.0
    w_k = jnp.concatenate([w_knope.reshape(KV_RANK, -1), jnp.asarray(place).reshape(HEAD_PAD, -1)],
                          axis=0).astype(BF16)
    w_uvt = jnp.pad(jnp.transpose(w_kv3[:, :, QK_NOPE:], (1, 2, 0)),
                    ((0, 0), (0, V_ROWS - V_DIM), (0, 0))).reshape(VT_ROWS, KV_RANK).astype(BF16)
    ones_col = np.zeros((N_HEADS, V_ROWS, LANES), np.float32)
    ones_col[:, V_DIM, :] = 1.0
    ones_col = jnp.asarray(ones_col.reshape(VT_ROWS, LANES))
    w_f = w_f_out[0].astype(BF16)
    w_a = w_a_out[0].astype(BF16)
    w_o = w_out[0].astype(BF16)
    qg = q_norm_g[0].reshape(1, Q_RANK)
    kvg = kv_norm_g[0].reshape(1, KV_RANK)
    fg = final_norm_g.reshape(1, D_MODEL)
    wts = (qg, kvg, w_uq_p, w_k, w_uvt, ones_col, _channel_dft_table())
    w_in_p = _pack_w_in(jnp.swapaxes(w_in[0], 0, 1))

    cond8 = jnp.concatenate([c_ctx[None, :], c, jnp.zeros((8 - 1 - dec_b, D_MODEL), F32)], axis=0)
    mod3 = _adaln(cond8, w_ada[0], b_ada[0].reshape(1, -1)).reshape(8, 1, 3 * D_MODEL)

    xp = x_prompt.reshape(b_ctx * n_ctx, D_MODEL)
    ucs, q, k, vt, state_ckv, state_krope = _layer_in(xp, mod3, w_in_p, wts, None, n=n_ctx, state=True)
    f = _pos_dft(ucs, n=n_ctx)
    attn = _attention(q, [k], [vt], [n_ctx], n=n_ctx, tq=n_ctx)
    y_prompt = _merge(xp, mod3, f, attn, w_in_p, w_f, w_a, w_o, fg, n=n_ctx, latent=False)

    xs = x_sample.reshape(dec_b * n_lat, D_MODEL)
    ucs, q, k, vt = _layer_in(xs, mod3, w_in_p, wts, _rope_tables(n_lat), n=n_lat, state=False)
    f = _pos_dft(ucs, n=n_lat)
    krope_pad = jnp.pad(cache_krope[:, 0], ((0, 0), (0, 0), (ROPE_LANE, HEAD_PAD - ROPE_LANE - QK_ROPE)))
    k_c, vt_c = _cache_kv(cache_ckv[:, 0].reshape(dec_b * past, KV_RANK),
                          krope_pad.reshape(dec_b * past, HEAD_PAD), w_k, w_uvt, ones_col)
    attn = _attention(q, [k, k_c], [vt, vt_c], [n_lat, past], n=n_lat, tq=256)
    y_sample = _merge(xs, mod3, f, attn, w_in_p, w_f, w_a, w_o, fg, n=n_lat, latent=True)

    return (y_prompt.reshape(b_ctx, n_ctx, D_MODEL), y_sample.reshape(dec_b, n_lat, D_MODEL),
            state_ckv.reshape(b_ctx, 1, n_ctx, KV_RANK), state_krope.reshape(b_ctx, 1, n_ctx, QK_ROPE))
```

```python
import functools
import math

import jax
import jax.numpy as jnp
import numpy as np
from jax import lax
from jax.experimental import pallas as pl
from jax.experimental.pallas import tpu as pltpu

F32 = jnp.float32
BF16 = jnp.bfloat16

D_MODEL = 1024
GRID_W = 64
N_GROUPS = 4
GROUP = 128
D_FOURIER = N_GROUPS * GROUP
N_HEADS = 8
QK_NOPE = 64
QK_ROPE = 32
V_DIM = 64
Q_RANK = 256
KV_RANK = 128
D_ATTN = N_HEADS * V_DIM
D_IN = 2 * D_FOURIER + Q_RANK + KV_RANK + QK_ROPE + D_ATTN + 2 * D_MODEL
ROPE_THETA = 10000.0
EPS = 1e-6
LANES = 128
HEAD_PAD = LANES
ROPE_LANE = QK_NOPE
V_ROWS = V_DIM + 16
VT_ROWS = N_HEADS * V_ROWS
Q_SCALE = (QK_NOPE + QK_ROPE) ** -0.5 * math.log2(math.e)
KEY_CHUNK = 256
REV_BLOCK = 256
DFT_SPLIT = 64
MIN_DENOMINATOR = 2.0 ** -60
MAX_DENOMINATOR = 2.0 ** 60
W_BLK = 512
B_UF, B_ZF, B_MID, B_ZA = 0, 1, 2, 3
D_IN_PAD = 8 * W_BLK
TOKEN_TILE = 1024
LAYER_IN_SUB_TILE = TOKEN_TILE
MERGE_SUB_TILE = TOKEN_TILE // 2
VMEM_LIMIT_BYTES = 56 * 1024 * 1024


def _const_spec(shape):
    nd = len(shape)
    return pl.BlockSpec(shape, lambda *_: (0,) * nd, pipeline_mode=pl.Buffered(1))


def _row_spec(rows, blk):
    return pl.BlockSpec((rows, D_MODEL), lambda *_: (blk, 0), pipeline_mode=pl.Buffered(1))


def _params(n_axes):
    return pltpu.CompilerParams(dimension_semantics=("arbitrary",) * n_axes,
                                vmem_limit_bytes=VMEM_LIMIT_BYTES)


def _rms(x):
    return x * lax.rsqrt(jnp.mean(x * x, axis=-1, keepdims=True) + EPS)


def _sigmoid(x):
    return 0.5 * jnp.tanh(0.5 * x) + 0.5


def _dot(a, b):
    return jnp.dot(a, b, preferred_element_type=F32)


def _dot_nt(a, b):
    return lax.dot_general(a, b, (((1,), (1,)), ((), ())), preferred_element_type=F32)


def _token_tiling(t, n, *, per_sequence):
    tm = TOKEN_TILE
    assert t % tm == 0 and (not per_sequence or n % tm == 0)
    per_seq = n // tm if per_sequence else None
    mod_row = (lambda i: 1 + i // per_seq) if per_sequence else (lambda i: 0)
    return tm, per_seq, mod_row


def _sub_tiles(rows, step):
    return [slice(r, r + step) for r in range(0, rows, step)]


def _modulated(x, mod_ref):
    mod = mod_ref[0]
    shift, scale = mod[:, 0:D_MODEL], mod[:, D_MODEL:2 * D_MODEL]
    return (_rms(x) * (1.0 + scale) + shift).astype(BF16)


def _adaln_kernel(c_ref, w_ref, b_ref, o_ref):
    c = c_ref[...]
    s = (c * _sigmoid(c)).astype(BF16)
    o_ref[...] = _dot(s, w_ref[...].astype(BF16)) + b_ref[...]


def _adaln(cond8, w_ada, b_ada):
    n_blk = 4
    bw = 3 * D_MODEL // n_blk
    return pl.pallas_call(
        _adaln_kernel,
        out_shape=jax.ShapeDtypeStruct((8, 3 * D_MODEL), F32),
        grid=(n_blk,),
        in_specs=[pl.BlockSpec((8, D_MODEL), lambda j: (0, 0)),
                  pl.BlockSpec((D_MODEL, bw), lambda j: (0, j)),
                  pl.BlockSpec((1, bw), lambda j: (0, j))],
        out_specs=pl.BlockSpec((8, bw), lambda j: (0, j)),
        compiler_params=_params(1),
        name="adaln",
    )(cond8, w_ada, b_ada)


def _rot_rows(w):
    q = QK_ROPE // 4
    return jnp.concatenate([-w[q:2 * q], w[0:q], -w[3 * q:4 * q], w[2 * q:3 * q]], axis=0)


def _pack_w_in_kernel(wt_ref, o_ref):
    lo = 2 * D_FOURIER + Q_RANK + KV_RANK
    o_ref[:lo] = wt_ref[:lo].astype(BF16)
    kr = wt_ref[lo:lo + QK_ROPE]
    zeros = jnp.zeros((QK_ROPE, D_MODEL), F32)
    o_ref[lo:lo + LANES] = jnp.concatenate([_rot_rows(kr), zeros, kr, zeros], axis=0).astype(BF16)
    o_ref[lo + LANES:] = wt_ref[lo + QK_ROPE:].astype(BF16)


def _pack_w_in(w_in_t):
    return pl.pallas_call(
        _pack_w_in_kernel,
        out_shape=jax.ShapeDtypeStruct((D_IN_PAD, D_MODEL), BF16),
        grid=(1,),
        in_specs=[_const_spec((D_IN, D_MODEL))],
        out_specs=_const_spec((D_IN_PAD, D_MODEL)),
        compiler_params=_params(1),
        name="pack_w_in",
    )(w_in_t)


def _kv_outputs(ckvn_b, kr_b, w_k_ref, w_uvt_ref, ones_ref, k_ref, vt_ref, rows):
    tm = ckvn_b.shape[0]
    kp = _dot(jnp.concatenate([ckvn_b, kr_b], axis=1), w_k_ref[...])
    for hh in range(N_HEADS):
        k_ref[hh, rows] = kp[:, hh * HEAD_PAD:(hh + 1) * HEAD_PAD].astype(BF16)
    ones = jnp.concatenate([ones_ref[...]] * (tm // LANES), axis=1)
    vt_ref[:, rows] = (_dot_nt(w_uvt_ref[...], ckvn_b) + ones).astype(BF16)


def _layer_in_kernel(*refs, rope, state):
    (x_ref, mod_ref, w_uf_ref, w_mid_ref, qg_ref, kvg_ref, w_uq_ref, w_k_ref, w_uvt_ref, ones_ref,
     cs_ref), refs = refs[:11], refs[11:]
    if rope:
        (cos_ref, sin_ref), refs = refs[:2], refs[2:]
    (ucs_ref, q_ref, k_ref, vt_ref), refs = refs[:4], refs[4:]
    if state:
        ckvn_ref, krope_ref = refs

    for rows in _sub_tiles(x_ref.shape[0], LAYER_IN_SUB_TILE):
        h = _modulated(x_ref[rows], mod_ref)

        u = _dot_nt(h, w_uf_ref[...]).astype(BF16)
        for g in range(N_GROUPS):
            r = _dot(u[:, g * GROUP:(g + 1) * GROUP], cs_ref[...])
            ucs_ref[rows, g * GROUP:(g + 1) * GROUP] = r[:, :GROUP].astype(BF16)
            ucs_ref[rows, D_FOURIER + g * GROUP:D_FOURIER + (g + 1) * GROUP] = r[:, GROUP:].astype(BF16)

        mid = _dot_nt(h, w_mid_ref[...])
        cq, ckv, kr = mid[:, :Q_RANK], mid[:, Q_RANK:Q_RANK + KV_RANK], mid[:, Q_RANK + KV_RANK:]
        cqn = (_rms(cq) * (qg_ref[...] * Q_SCALE)).astype(BF16)
        ckvn = _rms(ckv) * kvg_ref[...]
        if state:
            ckvn_ref[rows] = ckvn
            krope_ref[rows] = kr[:, ROPE_LANE:ROPE_LANE + QK_ROPE]
        n_q = N_HEADS * HEAD_PAD
        if rope:
            cos, sin = cos_ref[rows], sin_ref[rows]
            kr = kr * cos + pltpu.roll(kr, ROPE_LANE, 1) * sin
            qp = _dot(cqn, w_uq_ref[...])
        else:
            qp = _dot(cqn, w_uq_ref[:, :n_q])
        for hh in range(N_HEADS):
            qh = qp[:, hh * HEAD_PAD:(hh + 1) * HEAD_PAD]
            if rope:
                qh = qh * cos + qp[:, n_q + hh * HEAD_PAD:n_q + (hh + 1) * HEAD_PAD] * sin
            q_ref[hh, rows] = qh.astype(BF16)
        _kv_outputs(ckvn.astype(BF16), kr.astype(BF16), w_k_ref, w_uvt_ref, ones_ref, k_ref, vt_ref, rows)


def _layer_in(x, mod3, w_in_p, wts, rope_tabs, *, n, state):
    t = x.shape[0]
    rope = rope_tabs is not None
    tm, per_seq, mod_row = _token_tiling(t, n, per_sequence=rope)
    tok = lambda w: pl.BlockSpec((tm, w), lambda i: (i, 0))
    in_specs = [tok(D_MODEL),
                pl.BlockSpec((1, 1, 3 * D_MODEL), lambda i: (mod_row(i), 0, 0)),
                _row_spec(W_BLK, B_UF), _row_spec(W_BLK, B_MID)]
    in_specs += [_const_spec(w.shape) for w in wts]
    args = [x, mod3, w_in_p, w_in_p, *wts]
    if rope:
        in_specs += [pl.BlockSpec((tm, HEAD_PAD), lambda i: (i % per_seq, 0))] * 2
        args += list(rope_tabs)
    head = pl.BlockSpec((N_HEADS, tm, HEAD_PAD), lambda i: (0, i, 0))
    out_specs = [tok(2 * D_FOURIER), head, head, pl.BlockSpec((VT_ROWS, tm), lambda i: (0, i))]
    out_shape = [jax.ShapeDtypeStruct((t, 2 * D_FOURIER), BF16),
                 jax.ShapeDtypeStruct((N_HEADS, t, HEAD_PAD), BF16),
                 jax.ShapeDtypeStruct((N_HEADS, t, HEAD_PAD), BF16),
                 jax.ShapeDtypeStruct((VT_ROWS, t), BF16)]
    if state:
        out_specs += [tok(KV_RANK), tok(QK_ROPE)]
        out_shape += [jax.ShapeDtypeStruct((t, KV_RANK), F32), jax.ShapeDtypeStruct((t, QK_ROPE), F32)]
    return pl.pallas_call(
        functools.partial(_layer_in_kernel, rope=rope, state=state),
        out_shape=out_shape,
        grid=(t // tm,),
        in_specs=in_specs,
        out_specs=out_specs,
        compiler_params=_params(1),
        name="layer_in_latent" if rope else "layer_in_context",
    )(*args)


def _cache_kv_kernel(ckv_ref, kr_ref, w_k_ref, w_uvt_ref, ones_ref, k_ref, vt_ref):
    _kv_outputs(ckv_ref[...].astype(BF16), kr_ref[...].astype(BF16), w_k_ref, w_uvt_ref, ones_ref,
                k_ref, vt_ref, slice(None))


def _cache_kv(ckv, krope_pad, w_k, w_uvt, ones_col):
    t = ckv.shape[0]
    out_shape = [jax.ShapeDtypeStruct((N_HEADS, t, HEAD_PAD), BF16), jax.ShapeDtypeStruct((VT_ROWS, t), BF16)]
    return pl.pallas_call(
        _cache_kv_kernel,
        out_shape=out_shape,
        grid=(1,),
        in_specs=[_const_spec(a.shape) for a in (ckv, krope_pad, w_k, w_uvt, ones_col)],
        out_specs=[_const_spec(s.shape) for s in out_shape],
        compiler_params=_params(1),
        name="cache_kv",
    )(ckv, krope_pad, w_k, w_uvt, ones_col)


def _pos_dft_direct_kernel(cn_ref, sn_ref, ucs_ref, f_ref):
    n = cn_ref.shape[0]
    for s in range(f_ref.shape[0] // n):
        rows = slice(s * n, (s + 1) * n)
        f = _dot(cn_ref[...], ucs_ref[rows, :D_FOURIER]) + _dot(sn_ref[...], ucs_ref[rows, D_FOURIER:])
        f_ref[rows] = f.astype(BF16)


def _pos_dft_half_kernel(c1_ref, s1_ref, c0_ref, s0_ref, rev_ref, ucs_ref, f_ref, cn_ref, sn_ref):
    half = f_ref.shape[0] // 2

    @pl.when(pl.program_id(0) == 0)
    def _():
        c0, s0 = c0_ref[...], s0_ref[...]
        for k1 in range(c1_ref.shape[0]):
            rows = slice(k1 * DFT_SPLIT, min((k1 + 1) * DFT_SPLIT, cn_ref.shape[0]))
            m = rows.stop - rows.start
            c1, s1 = c1_ref[k1:k1 + 1, :], s1_ref[k1:k1 + 1, :]
            cn_ref[rows] = (c1 * c0[:m] - s1 * s0[:m]).astype(BF16)
            sn_ref[rows] = (-(s1 * c0[:m] + c1 * s0[:m])).astype(BF16)

    g = _dot(cn_ref[...], ucs_ref[:, :D_FOURIER])
    hn = _dot(sn_ref[...], ucs_ref[:, D_FOURIER:])
    f_ref[:half] = (g[:half] + hn[:half]).astype(BF16)
    mirrored = (g[1:half + 1] - hn[1:half + 1]).astype(BF16)
    blocks = half // REV_BLOCK
    for a in range(blocks):
        blk = mirrored[(blocks - 1 - a) * REV_BLOCK:(blocks - a) * REV_BLOCK]
        f_ref[half + a * REV_BLOCK:half + (a + 1) * REV_BLOCK] = _dot(rev_ref[...], blk).astype(BF16)


def _pos_dft(ucs, *, n):
    t = ucs.shape[0]
    out_shape = jax.ShapeDtypeStruct((t, D_FOURIER), BF16)
    if n <= REV_BLOCK:
        cn, snn = _dft_tables(n)
        seqs = min(4, t // n)
        return pl.pallas_call(
            _pos_dft_direct_kernel,
            out_shape=out_shape,
            grid=(t // (n * seqs),),
            in_specs=[_const_spec(cn.shape), _const_spec(snn.shape),
                      pl.BlockSpec((n * seqs, 2 * D_FOURIER), lambda i: (i, 0))],
            out_specs=pl.BlockSpec((n * seqs, D_FOURIER), lambda i: (i, 0)),
            compiler_params=_params(1),
            name=f"pos_dft_{n}",
        )(cn, snn, ucs)
    rows = n // 2 + 8
    factors = _dft_factors(n, rows)
    rev = jnp.asarray(np.eye(REV_BLOCK, dtype=np.float32)[::-1], dtype=BF16)
    return pl.pallas_call(
        _pos_dft_half_kernel,
        out_shape=out_shape,
        grid=(t // n,),
        in_specs=[_const_spec(a.shape) for a in (*factors, rev)]
        + [pl.BlockSpec((n, 2 * D_FOURIER), lambda i: (i, 0))],
        out_specs=pl.BlockSpec((n, D_FOURIER), lambda i: (i, 0)),
        scratch_shapes=[pltpu.VMEM((rows, n), BF16), pltpu.VMEM((rows, n), BF16)],
        compiler_params=_params(1),
        name=f"pos_dft_{n}",
    )(*factors, rev, ucs)


def _attention_kernel(*refs, n_kv, single_pass, seqs):
    q_ref, refs = refs[0], refs[1:]
    k_refs, vt_refs, (o_ref, s_ref, pp_ref, ot_ref) = refs[:n_kv], refs[n_kv:2 * n_kv], refs[2 * n_kv:]
    tq = q_ref.shape[1] // seqs
    key_lens = [k_ref.shape[1] // seqs for k_ref in k_refs]
    chunks, base = [], 0
    for j, m in enumerate(key_lens):
        chunks += [(j, c, base + c) for c in range(0, m, KEY_CHUNK)]
        base += m

    def finish(hh, ot, seq=0):
        l = ot[V_DIM:V_DIM + 1]
        ot_ref[seq, pl.ds(pl.multiple_of(hh * V_DIM, V_DIM), V_DIM), :] = ot[:V_DIM] / l
        return l

    def first_chunk_scores(hh):
        j, c, _ = chunks[0]
        s = _dot_nt(k_refs[j][hh, c:c + KEY_CHUNK, :], q_ref[hh])
        s_ref[0, hh * KEY_CHUNK:(hh + 1) * KEY_CHUNK, :] = s
        return jnp.max(s, axis=0, keepdims=True)

    def shifted_softmax_pv(score_head, pv_head, shifts):
        if score_head is not None:
            qh = q_ref[score_head]
            shift = shifts[score_head]
        ot = None
        for idx, (j, c, r) in enumerate(chunks):
            if score_head is not None:
                s = (s_ref[0, score_head * KEY_CHUNK:(score_head + 1) * KEY_CHUNK, :] if idx == 0 else
                     _dot_nt(k_refs[j][score_head, c:c + KEY_CHUNK, :], qh))
                pp_ref[score_head % 2, r:r + KEY_CHUNK, :] = jnp.exp2(s - shift).astype(BF16)
            if pv_head is not None:
                part = _dot(vt_refs[j][pv_head * V_ROWS:(pv_head + 1) * V_ROWS, c:c + KEY_CHUNK],
                            pp_ref[pv_head % 2, r:r + KEY_CHUNK, :])
                ot = part if ot is None else ot + part
        return finish(pv_head, ot) if pv_head is not None else None

    def exact_scores(hh, slot, seq=0):
        qh = q_ref[hh, seq * tq:(seq + 1) * tq, :]
        m8 = None
        for j, c, r in chunks:
            k0 = seq * key_lens[j] + c
            s = _dot_nt(k_refs[j][hh, k0:k0 + KEY_CHUNK, :], qh)
            s_ref[slot, r:r + KEY_CHUNK, :] = s
            mc = jnp.max(s.reshape(KEY_CHUNK // 8, 8, tq), axis=0)
            m8 = mc if m8 is None else jnp.maximum(m8, mc)
        return jnp.max(m8, axis=0, keepdims=True)

    def exact_softmax_pv(hh, slot, m, seq=0):
        for _, _, r in chunks:
            pp_ref[slot, r:r + KEY_CHUNK, :] = jnp.exp2(s_ref[slot, r:r + KEY_CHUNK, :] - m).astype(BF16)
        rows = pl.ds(pl.multiple_of(hh * V_ROWS, V_ROWS), V_ROWS)
        ot, base = None, 0
        for j, vt_ref in enumerate(vt_refs):
            mk = key_lens[j]
            part = _dot(vt_ref[rows, seq * mk:(seq + 1) * mk], pp_ref[slot, base:base + mk, :])
            ot = part if ot is None else ot + part
            base += mk
        finish(hh, ot, seq)

    if not single_pass:
        work = [(seq, hh) for seq in range(seqs) for hh in range(N_HEADS)]
        maxima = [exact_scores(hh, slot, seq) for slot, (seq, hh) in enumerate(work)]
        for slot, (seq, hh) in enumerate(work):
            exact_softmax_pv(hh, slot, maxima[slot], seq)
    else:
        l_min = l_max = None
        shifts = [first_chunk_scores(hh) for hh in range(N_HEADS)]
        for hh in range(N_HEADS + 1):
            l = shifted_softmax_pv(hh if hh < N_HEADS else None, hh - 1 if hh > 0 else None, shifts)
            if l is not None:
                l_min = l if l_min is None else jnp.minimum(l_min, l)
                l_max = l if l_max is None else jnp.maximum(l_max, l)
        trusted = jnp.logical_and(jnp.min(l_min) >= MIN_DENOMINATOR, jnp.max(l_max) <= MAX_DENOMINATOR)

        @pl.when(jnp.logical_not(trusted))
        def _():
            def body(hh, carry):
                exact_softmax_pv(hh, 0, exact_scores(hh, 0))
                return carry

            lax.fori_loop(0, N_HEADS, body, 0)

    for seq in range(seqs):
        o_ref[seq * tq:(seq + 1) * tq, :] = ot_ref[seq].T.astype(BF16)


def _attention(q, ks, vts, key_lens, *, n, tq):
    t = q.shape[1]
    per_seq = n // tq
    m_tot = sum(key_lens)
    single_pass = m_tot > 2 * KEY_CHUNK
    seqs = 1 if single_pass else min(4, t // n)
    assert single_pass or per_seq == 1
    s_slots, p_slots = (1, 2) if single_pass else (seqs * N_HEADS,) * 2
    in_specs = [pl.BlockSpec((N_HEADS, seqs * tq, HEAD_PAD), lambda bi, i: (0, bi * per_seq + i, 0))]
    in_specs += [pl.BlockSpec((N_HEADS, seqs * m, HEAD_PAD), lambda bi, i: (0, bi, 0)) for m in key_lens]
    in_specs += [pl.BlockSpec((VT_ROWS, seqs * m), lambda bi, i: (0, bi)) for m in key_lens]
    return pl.pallas_call(
        functools.partial(_attention_kernel, n_kv=len(ks), single_pass=single_pass, seqs=seqs),
        out_shape=jax.ShapeDtypeStruct((t, D_ATTN), BF16),
        grid=(t // (n * seqs), per_seq),
        in_specs=in_specs,
        out_specs=pl.BlockSpec((seqs * tq, D_ATTN), lambda bi, i: (bi * per_seq + i, 0)),
        scratch_shapes=[pltpu.VMEM((s_slots, m_tot, tq), F32), pltpu.VMEM((p_slots, m_tot, tq), BF16),
                        pltpu.VMEM((seqs, D_ATTN, tq), F32)],
        compiler_params=_params(2),
        name=f"attention_{n}",
    )(q, *ks, *vts)


def _merge_kernel(x_ref, mod_ref, f_ref, a_ref, w_zf_ref, w_za_ref, w_g_ref, w_f_ref, w_a_ref, w_o_ref,
                  g_ref, y_ref):
    gate = mod_ref[0][:, 2 * D_MODEL:]
    for rows in _sub_tiles(x_ref.shape[0], MERGE_SUB_TILE):
        x = x_ref[rows]
        h = _modulated(x, mod_ref)
        zf = _dot_nt(h, w_zf_ref[...])
        y_f = _dot((f_ref[rows] * (zf * _sigmoid(zf))).astype(BF16), w_f_ref[...])
        za = _dot_nt(h, w_za_ref[...])
        y_a = _dot((a_ref[rows] * (za * _sigmoid(za))).astype(BF16), w_a_ref[...])
        merged = (_sigmoid(_dot_nt(h, w_g_ref[:D_MODEL])) * y_f
                  + _sigmoid(_dot_nt(h, w_g_ref[D_MODEL:])) * y_a)
        out = x + gate * _dot(merged.astype(BF16), w_o_ref[...])
        y_ref[rows] = _rms(out) * g_ref[...]


def _merge(x, mod3, f, attn, w_in_p, w_f, w_a, w_o, g, *, n, latent):
    t = x.shape[0]
    tm, _, mod_row = _token_tiling(t, n, per_sequence=latent)
    tok = lambda w: pl.BlockSpec((tm, w), lambda i: (i, 0))
    return pl.pallas_call(
        _merge_kernel,
        out_shape=jax.ShapeDtypeStruct((t, D_MODEL), F32),
        grid=(t // tm,),
        in_specs=[tok(D_MODEL),
                  pl.BlockSpec((1, 1, 3 * D_MODEL), lambda i: (mod_row(i), 0, 0)),
                  tok(D_FOURIER), tok(D_ATTN),
                  _row_spec(W_BLK, B_ZF), _row_spec(W_BLK, B_ZA), _row_spec(2 * D_MODEL, 1),
                  _const_spec(w_f.shape), _const_spec(w_a.shape), _const_spec(w_o.shape),
                  _const_spec(g.shape)],
        out_specs=tok(D_MODEL),
        compiler_params=_params(1),
        name=f"merge_{n}",
    )(x, mod3, f, attn, w_in_p, w_in_p, w_in_p, w_f, w_a, w_o, g)


def _dft_tables(n):
    norm = 1.0 / math.sqrt(n)
    k = jnp.arange(n, dtype=jnp.int32)
    ang = ((k[:, None] * k[None, :]) % n).astype(F32) * (2.0 * math.pi / n)
    return (jnp.cos(ang) * norm).astype(BF16), (jnp.sin(ang) * -norm).astype(BF16)


def _dft_factors(n, rows):
    norm = 1.0 / math.sqrt(n)
    period = n // DFT_SPLIT
    pos = jnp.arange(n, dtype=jnp.int32)
    k1 = jnp.arange(-(-rows // DFT_SPLIT), dtype=jnp.int32)
    k0 = jnp.arange(DFT_SPLIT, dtype=jnp.int32)
    a1 = ((k1[:, None] * pos[None, :]) % period).astype(F32) * (2.0 * math.pi / period)
    a0 = ((k0[:, None] * pos[None, :]) % n).astype(F32) * (2.0 * math.pi / n)
    return jnp.cos(a1), jnp.sin(a1), jnp.cos(a0) * norm, jnp.sin(a0) * norm


def _channel_dft_table():
    c = jnp.arange(GROUP, dtype=jnp.int32)
    ang = ((c[:, None] * c[None, :]) % GROUP).astype(F32) * (2.0 * math.pi / GROUP)
    norm = 1.0 / math.sqrt(GROUP)
    return (jnp.concatenate([jnp.cos(ang), jnp.sin(ang)], axis=1) * norm).astype(BF16)


def _rope_tables(n):
    t = jnp.arange(n)
    row = (t // GRID_W).astype(F32)
    col = (t % GRID_W).astype(F32)
    half = QK_ROPE // 2
    inv = ROPE_THETA ** (-jnp.arange(0, half, 2, dtype=F32) / half)
    ar, ac = row[:, None] * inv, col[:, None] * inv
    ang = jnp.concatenate([ar, ar, ac, ac], axis=-1)
    pad = lambda a, fill: jnp.concatenate(
        [jnp.full((n, ROPE_LANE), fill, F32), a, jnp.full((n, HEAD_PAD - ROPE_LANE - QK_ROPE), fill, F32)], axis=1)
    return pad(jnp.cos(ang), 1.0), pad(jnp.sin(ang), 0.0)


def kernel(x_prompt, x_sample, cache_ckv, cache_krope, c, c_ctx, w_ada, b_ada, w_in, q_norm_g, w_uq,
           kv_norm_g, w_ukv, w_f_out, w_a_out, w_out, final_norm_g):
    assert w_in.shape[0] == 1
    b_ctx, n_ctx, _ = x_prompt.shape
    dec_b, n_lat, _ = x_sample.shape
    past = cache_ckv.shape[2]

    w_q3 = w_uq[0].reshape(Q_RANK, N_HEADS, QK_NOPE + QK_ROPE)
    head_pad = lambda a, left: jnp.pad(
        a, ((0, 0), (0, 0), (left, HEAD_PAD - left - a.shape[2]))).reshape(Q_RANK, N_HEADS * HEAD_PAD)
    rp = w_q3[:, :, QK_NOPE:].reshape(Q_RANK, N_HEADS, 2, 2, QK_ROPE // 4)
    w_q_rot = jnp.stack([-rp[:, :, :, 1], rp[:, :, :, 0]], axis=3).reshape(Q_RANK, N_HEADS, QK_ROPE)
    w_uq_p = jnp.concatenate([head_pad(w_q3, 0), head_pad(w_q_rot, ROPE_LANE)], axis=1).astype(BF16)
    w_kv3 = w_ukv[0].reshape(KV_RANK, N_HEADS, QK_NOPE + V_DIM)
    w_knope = jnp.pad(w_kv3[:, :, :QK_NOPE], ((0, 0), (0, 0), (0, HEAD_PAD - QK_NOPE)))
    place = np.zeros((HEAD_PAD, N_HEADS, HEAD_PAD), np.float32)
    for j in range(QK_ROPE):
        place[ROPE_LANE + j, :, ROPE_LANE + j] = 1.0
    w_k = jnp.concatenate([w_knope.reshape(KV_RANK, -1), jnp.asarray(place).reshape(HEAD_PAD, -1)],
                          axis=0).astype(BF16)
    w_uvt = jnp.pad(jnp.transpose(w_kv3[:, :, QK_NOPE:], (1, 2, 0)),
                    ((0, 0), (0, V_ROWS - V_DIM), (0, 0))).reshape(VT_ROWS, KV_RANK).astype(BF16)
    ones_col = np.zeros((N_HEADS, V_ROWS, LANES), np.float32)
    ones_col[:, V_DIM, :] = 1.0
    ones_col = jnp.asarray(ones_col.reshape(VT_ROWS, LANES))
    w_f = w_f_out[0].astype(BF16)
    w_a = w_a_out[0].astype(BF16)
    w_o = w_out[0].astype(BF16)
    qg = q_norm_g[0].reshape(1, Q_RANK)
    kvg = kv_norm_g[0].reshape(1, KV_RANK)
    fg = final_norm_g.reshape(1, D_MODEL)
    wts = (qg, kvg, w_uq_p, w_k, w_uvt, ones_col, _channel_dft_table())
    w_in_p = _pack_w_in(jnp.swapaxes(w_in[0], 0, 1))

    cond8 = jnp.concatenate([c_ctx[None, :], c, jnp.zeros((8 - 1 - dec_b, D_MODEL), F32)], axis=0)
    mod3 = _adaln(cond8, w_ada[0], b_ada[0].reshape(1, -1)).reshape(8, 1, 3 * D_MODEL)

    xp = x_prompt.reshape(b_ctx * n_ctx, D_MODEL)
    ucs, q, k, vt, state_ckv, state_krope = _layer_in(xp, mod3, w_in_p, wts, None, n=n_ctx, state=True)
    f = _pos_dft(ucs, n=n_ctx)
    attn = _attention(q, [k], [vt], [n_ctx], n=n_ctx, tq=n_ctx)
    y_prompt = _merge(xp, mod3, f, attn, w_in_p, w_f, w_a, w_o, fg, n=n_ctx, latent=False)

    xs = x_sample.reshape(dec_b * n_lat, D_MODEL)
    ucs, q, k, vt = _layer_in(xs, mod3, w_in_p, wts, _rope_tables(n_lat), n=n_lat, state=False)
    f = _pos_dft(ucs, n=n_lat)
    krope_pad = jnp.pad(cache_krope[:, 0], ((0, 0), (0, 0), (ROPE_LANE, HEAD_PAD - ROPE_LANE - QK_ROPE)))
    k_c, vt_c = _cache_kv(cache_ckv[:, 0].reshape(dec_b * past, KV_RANK),
                          krope_pad.reshape(dec_b * past, HEAD_PAD), w_k, w_uvt, ones_col)
    attn = _attention(q, [k, k_c], [vt, vt_c], [n_lat, past], n=n_lat, tq=256)
    y_sample = _merge(xs, mod3, f, attn, w_in_p, w_f, w_a, w_o, fg, n=n_lat, latent=True)

    return (y_prompt.reshape(b_ctx, n_ctx, D_MODEL), y_sample.reshape(dec_b, n_lat, D_MODEL),
            state_ckv.reshape(b_ctx, 1, n_ctx, KV_RANK), state_krope.reshape(b_ctx, 1, n_ctx, QK_ROPE))
```

```python
import functools
import math

import jax
import jax.numpy as jnp
import numpy as np
from jax import lax
from jax.experimental import pallas as pl
from jax.experimental.pallas import tpu as pltpu

F32 = jnp.float32
BF16 = jnp.bfloat16

D_MODEL = 1024
GRID_W = 64
N_GROUPS = 4
GROUP = 128
D_FOURIER = N_GROUPS * GROUP
N_HEADS = 8
QK_NOPE = 64
QK_ROPE = 32
V_DIM = 64
Q_RANK = 256
KV_RANK = 128
D_ATTN = N_HEADS * V_DIM
D_IN = 2 * D_FOURIER + Q_RANK + KV_RANK + QK_ROPE + D_ATTN + 2 * D_MODEL
ROPE_THETA = 10000.0
EPS = 1e-6
LANES = 128
HEAD_PAD = LANES
ROPE_LANE = QK_NOPE
V_ROWS = V_DIM + 16
VT_ROWS = N_HEADS * V_ROWS
Q_SCALE = (QK_NOPE + QK_ROPE) ** -0.5 * math.log2(math.e)
KEY_CHUNK = 256
Q_TILE = 256
REV_BLOCK = 256
DFT_SPLIT = 64
MIN_DENOMINATOR = 2.0 ** -60
MAX_DENOMINATOR = 2.0 ** 60
W_BLK = 512
B_UF, B_ZF, B_MID, B_ZA = 0, 1, 2, 3
D_IN_PAD = 8 * W_BLK
TOKEN_TILE = 1024
LAYER_IN_SUB_TILE = TOKEN_TILE
MERGE_SUB_TILE = TOKEN_TILE // 2
VMEM_LIMIT_BYTES = 56 * 1024 * 1024


def _const_spec(shape):
    nd = len(shape)
    return pl.BlockSpec(shape, lambda *_: (0,) * nd, pipeline_mode=pl.Buffered(1))


def _row_spec(rows, blk):
    return pl.BlockSpec((rows, D_MODEL), lambda *_: (blk, 0), pipeline_mode=pl.Buffered(1))


def _params(n_axes):
    return pltpu.CompilerParams(dimension_semantics=("arbitrary",) * n_axes,
                                vmem_limit_bytes=VMEM_LIMIT_BYTES)


def _rms(x):
    return x * lax.rsqrt(jnp.mean(x * x, axis=-1, keepdims=True) + EPS)


def _sigmoid(x):
    return 0.5 * jnp.tanh(0.5 * x) + 0.5


def _dot(a, b):
    return jnp.dot(a, b, preferred_element_type=F32)


def _dot_nt(a, b):
    return lax.dot_general(a, b, (((1,), (1,)), ((), ())), preferred_element_type=F32)


def _token_tiling(t, n, *, per_sequence):
    tm = TOKEN_TILE
    assert t % tm == 0 and (not per_sequence or n % tm == 0)
    per_seq = n // tm if per_sequence else None
    mod_row = (lambda i: 1 + i // per_seq) if per_sequence else (lambda i: 0)
    return tm, per_seq, mod_row


def _sub_tiles(rows, step):
    return [slice(r, r + step) for r in range(0, rows, step)]


def _modulated(x, mod_ref):
    mod = mod_ref[0]
    shift, scale = mod[:, 0:D_MODEL], mod[:, D_MODEL:2 * D_MODEL]
    return (_rms(x) * (1.0 + scale) + shift).astype(BF16)


def _adaln_kernel(c_ref, w_ref, b_ref, o_ref):
    c = c_ref[...]
    s = (c * _sigmoid(c)).astype(BF16)
    o_ref[...] = _dot(s, w_ref[...].astype(BF16)) + b_ref[...]


def _adaln(cond8, w_ada, b_ada):
    n_blk = 4
    bw = 3 * D_MODEL // n_blk
    return pl.pallas_call(
        _adaln_kernel,
        out_shape=jax.ShapeDtypeStruct((8, 3 * D_MODEL), F32),
        grid=(n_blk,),
        in_specs=[pl.BlockSpec((8, D_MODEL), lambda j: (0, 0)),
                  pl.BlockSpec((D_MODEL, bw), lambda j: (0, j)),
                  pl.BlockSpec((1, bw), lambda j: (0, j))],
        out_specs=pl.BlockSpec((8, bw), lambda j: (0, j)),
        compiler_params=_params(1),
        name="adaln",
    )(cond8, w_ada, b_ada)


def _rot_rows(w):
    q = QK_ROPE // 4
    return jnp.concatenate([-w[q:2 * q], w[0:q], -w[3 * q:4 * q], w[2 * q:3 * q]], axis=0)


def _pack_w_in_kernel(wt_ref, o_ref):
    lo = 2 * D_FOURIER + Q_RANK + KV_RANK
    o_ref[:lo] = wt_ref[:lo].astype(BF16)
    kr = wt_ref[lo:lo + QK_ROPE]
    zeros = jnp.zeros((QK_ROPE, D_MODEL), F32)
    o_ref[lo:lo + LANES] = jnp.concatenate([_rot_rows(kr), zeros, kr, zeros], axis=0).astype(BF16)
    o_ref[lo + LANES:] = wt_ref[lo + QK_ROPE:].astype(BF16)


def _pack_w_in(w_in_t):
    return pl.pallas_call(
        _pack_w_in_kernel,
        out_shape=jax.ShapeDtypeStruct((D_IN_PAD, D_MODEL), BF16),
        grid=(1,),
        in_specs=[_const_spec((D_IN, D_MODEL))],
        out_specs=_const_spec((D_IN_PAD, D_MODEL)),
        compiler_params=_params(1),
        name="pack_w_in",
    )(w_in_t)


def _kv_outputs(ckvn_b, kr_b, w_k_ref, w_uvt_ref, ones_ref, k_ref, vt_ref, rows):
    tm = ckvn_b.shape[0]
    kp = _dot(jnp.concatenate([ckvn_b, kr_b], axis=1), w_k_ref[...])
    for hh in range(N_HEADS):
        k_ref[hh, rows] = kp[:, hh * HEAD_PAD:(hh + 1) * HEAD_PAD].astype(BF16)
    ones = jnp.concatenate([ones_ref[...]] * (tm // LANES), axis=1)
    vt_ref[:, rows] = (_dot_nt(w_uvt_ref[...], ckvn_b) + ones).astype(BF16)


def _layer_in_kernel(*refs, rope, state):
    (x_ref, mod_ref, w_uf_ref, w_mid_ref, qg_ref, kvg_ref, w_uq_ref, w_k_ref, w_uvt_ref, ones_ref,
     cs_ref), refs = refs[:11], refs[11:]
    if rope:
        (cos_ref, sin_ref), refs = refs[:2], refs[2:]
    (ucs_ref, q_ref, k_ref, vt_ref), refs = refs[:4], refs[4:]
    if state:
        ckvn_ref, krope_ref = refs

    for rows in _sub_tiles(x_ref.shape[0], LAYER_IN_SUB_TILE):
        h = _modulated(x_ref[rows], mod_ref)

        u = _dot_nt(h, w_uf_ref[...]).astype(BF16)
        for g in range(N_GROUPS):
            r = _dot(u[:, g * GROUP:(g + 1) * GROUP], cs_ref[...])
            ucs_ref[rows, g * GROUP:(g + 1) * GROUP] = r[:, :GROUP].astype(BF16)
            ucs_ref[rows, D_FOURIER + g * GROUP:D_FOURIER + (g + 1) * GROUP] = r[:, GROUP:].astype(BF16)

        mid = _dot_nt(h, w_mid_ref[...])
        cq, ckv, kr = mid[:, :Q_RANK], mid[:, Q_RANK:Q_RANK + KV_RANK], mid[:, Q_RANK + KV_RANK:]
        cqn = (_rms(cq) * (qg_ref[...] * Q_SCALE)).astype(BF16)
        ckvn = _rms(ckv) * kvg_ref[...]
        if state:
            ckvn_ref[rows] = ckvn
            krope_ref[rows] = kr[:, ROPE_LANE:ROPE_LANE + QK_ROPE]
        n_q = N_HEADS * HEAD_PAD
        if rope:
            cos, sin = cos_ref[rows], sin_ref[rows]
            kr = kr * cos + pltpu.roll(kr, ROPE_LANE, 1) * sin
            qp = _dot(cqn, w_uq_ref[...])
        else:
            qp = _dot(cqn, w_uq_ref[:, :n_q])
        for hh in range(N_HEADS):
            qh = qp[:, hh * HEAD_PAD:(hh + 1) * HEAD_PAD]
            if rope:
                qh = qh * cos + qp[:, n_q + hh * HEAD_PAD:n_q + (hh + 1) * HEAD_PAD] * sin
            q_ref[hh, rows] = qh.astype(BF16)
        _kv_outputs(ckvn.astype(BF16), kr.astype(BF16), w_k_ref, w_uvt_ref, ones_ref, k_ref, vt_ref, rows)


def _layer_in(x, mod3, w_in_p, wts, rope_tabs, *, n, state):
    t = x.shape[0]
    rope = rope_tabs is not None
    tm, per_seq, mod_row = _token_tiling(t, n, per_sequence=rope)
    tok = lambda w: pl.BlockSpec((tm, w), lambda i: (i, 0))
    in_specs = [tok(D_MODEL),
                pl.BlockSpec((1, 1, 3 * D_MODEL), lambda i: (mod_row(i), 0, 0)),
                _row_spec(W_BLK, B_UF), _row_spec(W_BLK, B_MID)]
    in_specs += [_const_spec(w.shape) for w in wts]
    args = [x, mod3, w_in_p, w_in_p, *wts]
    if rope:
        in_specs += [pl.BlockSpec((tm, HEAD_PAD), lambda i: (i % per_seq, 0))] * 2
        args += list(rope_tabs)
    head = pl.BlockSpec((N_HEADS, tm, HEAD_PAD), lambda i: (0, i, 0))
    out_specs = [tok(2 * D_FOURIER), head, head, pl.BlockSpec((VT_ROWS, tm), lambda i: (0, i))]
    out_shape = [jax.ShapeDtypeStruct((t, 2 * D_FOURIER), BF16),
                 jax.ShapeDtypeStruct((N_HEADS, t, HEAD_PAD), BF16),
                 jax.ShapeDtypeStruct((N_HEADS, t, HEAD_PAD), BF16),
                 jax.ShapeDtypeStruct((VT_ROWS, t), BF16)]
    if state:
        out_specs += [tok(KV_RANK), tok(QK_ROPE)]
        out_shape += [jax.ShapeDtypeStruct((t, KV_RANK), F32), jax.ShapeDtypeStruct((t, QK_ROPE), F32)]
    return pl.pallas_call(
        functools.partial(_layer_in_kernel, rope=rope, state=state),
        out_shape=out_shape,
        grid=(t // tm,),
        in_specs=in_specs,
        out_specs=out_specs,
        compiler_params=_params(1),
        name="layer_in_latent" if rope else "layer_in_context",
    )(*args)


def _cache_kv_kernel(ckv_ref, kr_ref, w_k_ref, w_uvt_ref, ones_ref, k_ref, vt_ref):
    _kv_outputs(ckv_ref[...].astype(BF16), kr_ref[...].astype(BF16), w_k_ref, w_uvt_ref, ones_ref,
                k_ref, vt_ref, slice(None))


def _cache_kv(ckv, krope_pad, w_k, w_uvt, ones_col):
    t = ckv.shape[0]
    out_shape = [jax.ShapeDtypeStruct((N_HEADS, t, HEAD_PAD), BF16), jax.ShapeDtypeStruct((VT_ROWS, t), BF16)]
    return pl.pallas_call(
        _cache_kv_kernel,
        out_shape=out_shape,
        grid=(1,),
        in_specs=[_const_spec(a.shape) for a in (ckv, krope_pad, w_k, w_uvt, ones_col)],
        out_specs=[_const_spec(s.shape) for s in out_shape],
        compiler_params=_params(1),
        name="cache_kv",
    )(ckv, krope_pad, w_k, w_uvt, ones_col)


def _pos_dft_direct_kernel(cn_ref, sn_ref, ucs_ref, f_ref):
    n = cn_ref.shape[0]
    for s in range(f_ref.shape[0] // n):
        rows = slice(s * n, (s + 1) * n)
        f = _dot(cn_ref[...], ucs_ref[rows, :D_FOURIER]) + _dot(sn_ref[...], ucs_ref[rows, D_FOURIER:])
        f_ref[rows] = f.astype(BF16)


def _pos_dft_half_kernel(c1_ref, s1_ref, c0_ref, s0_ref, rev_ref, ucs_ref, f_ref, cn_ref, sn_ref):
    half = f_ref.shape[0] // 2

    @pl.when(pl.program_id(0) == 0)
    def _():
        c0, s0 = c0_ref[...], s0_ref[...]
        for k1 in range(c1_ref.shape[0]):
            rows = slice(k1 * DFT_SPLIT, min((k1 + 1) * DFT_SPLIT, cn_ref.shape[0]))
            m = rows.stop - rows.start
            c1, s1 = c1_ref[k1:k1 + 1, :], s1_ref[k1:k1 + 1, :]
            cn_ref[rows] = (c1 * c0[:m] - s1 * s0[:m]).astype(BF16)
            sn_ref[rows] = (-(s1 * c0[:m] + c1 * s0[:m])).astype(BF16)

    g = _dot(cn_ref[...], ucs_ref[:, :D_FOURIER])
    hn = _dot(sn_ref[...], ucs_ref[:, D_FOURIER:])
    f_ref[:half] = (g[:half] + hn[:half]).astype(BF16)
    mirrored = (g[1:half + 1] - hn[1:half + 1]).astype(BF16)
    blocks = half // REV_BLOCK
    for a in range(blocks):
        blk = mirrored[(blocks - 1 - a) * REV_BLOCK:(blocks - a) * REV_BLOCK]
        f_ref[half + a * REV_BLOCK:half + (a + 1) * REV_BLOCK] = _dot(rev_ref[...], blk).astype(BF16)


def _pos_dft(ucs, *, n):
    t = ucs.shape[0]
    out_shape = jax.ShapeDtypeStruct((t, D_FOURIER), BF16)
    if n <= REV_BLOCK:
        cn, snn = _dft_tables(n)
        seqs = min(4, t // n)
        return pl.pallas_call(
            _pos_dft_direct_kernel,
            out_shape=out_shape,
            grid=(t // (n * seqs),),
            in_specs=[_const_spec(cn.shape), _const_spec(snn.shape),
                      pl.BlockSpec((n * seqs, 2 * D_FOURIER), lambda i: (i, 0))],
            out_specs=pl.BlockSpec((n * seqs, D_FOURIER), lambda i: (i, 0)),
            compiler_params=_params(1),
            name=f"pos_dft_{n}",
        )(cn, snn, ucs)
    rows = n // 2 + 8
    factors = _dft_factors(n, rows)
    rev = jnp.asarray(np.eye(REV_BLOCK, dtype=np.float32)[::-1], dtype=BF16)
    return pl.pallas_call(
        _pos_dft_half_kernel,
        out_shape=out_shape,
        grid=(t // n,),
        in_specs=[_const_spec(a.shape) for a in (*factors, rev)]
        + [pl.BlockSpec((n, 2 * D_FOURIER), lambda i: (i, 0))],
        out_specs=pl.BlockSpec((n, D_FOURIER), lambda i: (i, 0)),
        scratch_shapes=[pltpu.VMEM((rows, n), BF16), pltpu.VMEM((rows, n), BF16)],
        compiler_params=_params(1),
        name=f"pos_dft_{n}",
    )(*factors, rev, ucs)


def _attention_kernel(*refs, n_kv, single_pass, units):
    q_ref, refs = refs[0], refs[1:]
    k_refs, vt_refs, (o_ref, s_ref, pp_ref, ot_ref) = refs[:n_kv], refs[n_kv:2 * n_kv], refs[2 * n_kv:]
    tq = q_ref.shape[1] // units
    key_lens = [k_ref.shape[1] // (1 if single_pass else units) for k_ref in k_refs]
    chunks, base = [], 0
    for j, m in enumerate(key_lens):
        chunks += [(j, c, base + c) for c in range(0, m, KEY_CHUNK)]
        base += m

    def q_block(hh, u):
        return q_ref[hh, u * tq:(u + 1) * tq, :]

    def finish(hh, ot, u):
        l = ot[V_DIM:V_DIM + 1]
        ot_ref[u, pl.ds(pl.multiple_of(hh * V_DIM, V_DIM), V_DIM), :] = ot[:V_DIM] / l
        return l

    def first_chunk_scores(hh, u):
        j, c, _ = chunks[0]
        s = _dot_nt(k_refs[j][hh, c:c + KEY_CHUNK, :], q_block(hh, u))
        s_ref[u, hh * KEY_CHUNK:(hh + 1) * KEY_CHUNK, :] = s
        return jnp.max(s, axis=0, keepdims=True)

    def shifted_softmax_pv(score_head, pv_head, shifts, u):
        if score_head is not None:
            qh = q_block(score_head, u)
            shift = shifts[score_head]
        ot = None
        for idx, (j, c, r) in enumerate(chunks):
            if score_head is not None:
                s = (s_ref[u, score_head * KEY_CHUNK:(score_head + 1) * KEY_CHUNK, :] if idx == 0 else
                     _dot_nt(k_refs[j][score_head, c:c + KEY_CHUNK, :], qh))
                pp_ref[2 * u + score_head % 2, r:r + KEY_CHUNK, :] = jnp.exp2(s - shift).astype(BF16)
            if pv_head is not None:
                part = _dot(vt_refs[j][pv_head * V_ROWS:(pv_head + 1) * V_ROWS, c:c + KEY_CHUNK],
                            pp_ref[2 * u + pv_head % 2, r:r + KEY_CHUNK, :])
                ot = part if ot is None else ot + part
        return finish(pv_head, ot, u) if pv_head is not None else None

    def exact_scores(hh, slot, u, kseq):
        qh = q_block(hh, u)
        m8 = None
        for j, c, r in chunks:
            k0 = kseq * key_lens[j] + c
            s = _dot_nt(k_refs[j][hh, k0:k0 + KEY_CHUNK, :], qh)
            s_ref[slot, r:r + KEY_CHUNK, :] = s
            mc = jnp.max(s.reshape(KEY_CHUNK // 8, 8, tq), axis=0)
            m8 = mc if m8 is None else jnp.maximum(m8, mc)
        return jnp.max(m8, axis=0, keepdims=True)

    def exact_softmax_pv(hh, s_slot, p_slot, m, u, kseq):
        for _, _, r in chunks:
            pp_ref[p_slot, r:r + KEY_CHUNK, :] = jnp.exp2(s_ref[s_slot, r:r + KEY_CHUNK, :] - m).astype(BF16)
        rows = pl.ds(pl.multiple_of(hh * V_ROWS, V_ROWS), V_ROWS)
        ot, base = None, 0
        for j, vt_ref in enumerate(vt_refs):
            mk = key_lens[j]
            part = _dot(vt_ref[rows, kseq * mk:(kseq + 1) * mk], pp_ref[p_slot, base:base + mk, :])
            ot = part if ot is None else ot + part
            base += mk
        finish(hh, ot, u)

    if not single_pass:
        work = [(u, hh) for u in range(units) for hh in range(N_HEADS)]
        maxima = [exact_scores(hh, slot, u, u) for slot, (u, hh) in enumerate(work)]
        for slot, (u, hh) in enumerate(work):
            exact_softmax_pv(hh, slot, slot, maxima[slot], u, u)
    else:
        trusted = []
        for u in range(units):
            l_min = l_max = None
            shifts = [first_chunk_scores(hh, u) for hh in range(N_HEADS)]
            for hh in range(N_HEADS + 1):
                l = shifted_softmax_pv(hh if hh < N_HEADS else None, hh - 1 if hh > 0 else None, shifts, u)
                if l is not None:
                    l_min = l if l_min is None else jnp.minimum(l_min, l)
                    l_max = l if l_max is None else jnp.maximum(l_max, l)
            trusted.append(jnp.logical_and(jnp.min(l_min) >= MIN_DENOMINATOR,
                                           jnp.max(l_max) <= MAX_DENOMINATOR))

        for u in range(units):
            @pl.when(jnp.logical_not(trusted[u]))
            def _(u=u):
                def body(hh, carry):
                    exact_softmax_pv(hh, u, 2 * u, exact_scores(hh, u, u, 0), u, 0)
                    return carry

                lax.fori_loop(0, N_HEADS, body, 0)

    for u in range(units):
        o_ref[u * tq:(u + 1) * tq, :] = ot_ref[u].T.astype(BF16)


def _attention(q, ks, vts, key_lens, *, n):
    t = q.shape[1]
    tq = Q_TILE
    m_tot = sum(key_lens)
    single_pass = m_tot > 2 * KEY_CHUNK
    if single_pass:
        units = 2
        steps = n // (units * tq)
        kv_rows = 1
        s_slots, p_slots = units, 2 * units
    else:
        assert n == tq
        units = min(4, t // n)
        steps = 1
        kv_rows = units
        s_slots = p_slots = units * N_HEADS
    in_specs = [pl.BlockSpec((N_HEADS, units * tq, HEAD_PAD), lambda bi, i: (0, bi * steps + i, 0))]
    in_specs += [pl.BlockSpec((N_HEADS, kv_rows * m, HEAD_PAD), lambda bi, i: (0, bi, 0)) for m in key_lens]
    in_specs += [pl.BlockSpec((VT_ROWS, kv_rows * m), lambda bi, i: (0, bi)) for m in key_lens]
    return pl.pallas_call(
        functools.partial(_attention_kernel, n_kv=len(ks), single_pass=single_pass, units=units),
        out_shape=jax.ShapeDtypeStruct((t, D_ATTN), BF16),
        grid=(t // (n * kv_rows), steps),
        in_specs=in_specs,
        out_specs=pl.BlockSpec((units * tq, D_ATTN), lambda bi, i: (bi * steps + i, 0)),
        scratch_shapes=[pltpu.VMEM((s_slots, m_tot, tq), F32), pltpu.VMEM((p_slots, m_tot, tq), BF16),
                        pltpu.VMEM((units, D_ATTN, tq), F32)],
        compiler_params=_params(2),
        name=f"attention_{n}",
    )(q, *ks, *vts)


def _merge_kernel(x_ref, mod_ref, f_ref, a_ref, w_zf_ref, w_za_ref, w_g_ref, w_f_ref, w_a_ref, w_o_ref,
                  g_ref, y_ref):
    gate = mod_ref[0][:, 2 * D_MODEL:]
    for rows in _sub_tiles(x_ref.shape[0], MERGE_SUB_TILE):
        x = x_ref[rows]
        h = _modulated(x, mod_ref)
        zf = _dot_nt(h, w_zf_ref[...])
        y_f = _dot((f_ref[rows] * (zf * _sigmoid(zf))).astype(BF16), w_f_ref[...])
        za = _dot_nt(h, w_za_ref[...])
        y_a = _dot((a_ref[rows] * (za * _sigmoid(za))).astype(BF16), w_a_ref[...])
        merged = (_sigmoid(_dot_nt(h, w_g_ref[:D_MODEL])) * y_f
                  + _sigmoid(_dot_nt(h, w_g_ref[D_MODEL:])) * y_a)
        out = x + gate * _dot(merged.astype(BF16), w_o_ref[...])
        y_ref[rows] = _rms(out) * g_ref[...]


def _merge(x, mod3, f, attn, w_in_p, w_f, w_a, w_o, g, *, n, latent):
    t = x.shape[0]
    tm, _, mod_row = _token_tiling(t, n, per_sequence=latent)
    tok = lambda w: pl.BlockSpec((tm, w), lambda i: (i, 0))
    return pl.pallas_call(
        _merge_kernel,
        out_shape=jax.ShapeDtypeStruct((t, D_MODEL), F32),
        grid=(t // tm,),
        in_specs=[tok(D_MODEL),
                  pl.BlockSpec((1, 1, 3 * D_MODEL), lambda i: (mod_row(i), 0, 0)),
                  tok(D_FOURIER), tok(D_ATTN),
                  _row_spec(W_BLK, B_ZF), _row_spec(W_BLK, B_ZA), _row_spec(2 * D_MODEL, 1),
                  _const_spec(w_f.shape), _const_spec(w_a.shape), _const_spec(w_o.shape),
                  _const_spec(g.shape)],
        out_specs=tok(D_MODEL),
        compiler_params=_params(1),
        name=f"merge_{n}",
    )(x, mod3, f, attn, w_in_p, w_in_p, w_in_p, w_f, w_a, w_o, g)


def _dft_tables(n):
    norm = 1.0 / math.sqrt(n)
    k = jnp.arange(n, dtype=jnp.int32)
    ang = ((k[:, None] * k[None, :]) % n).astype(F32) * (2.0 * math.pi / n)
    return (jnp.cos(ang) * norm).astype(BF16), (jnp.sin(ang) * -norm).astype(BF16)


def _dft_factors(n, rows):
    norm = 1.0 / math.sqrt(n)
    period = n // DFT_SPLIT
    pos = jnp.arange(n, dtype=jnp.int32)
    k1 = jnp.arange(-(-rows // DFT_SPLIT), dtype=jnp.int32)
    k0 = jnp.arange(DFT_SPLIT, dtype=jnp.int32)
    a1 = ((k1[:, None] * pos[None, :]) % period).astype(F32) * (2.0 * math.pi / period)
    a0 = ((k0[:, None] * pos[None, :]) % n).astype(F32) * (2.0 * math.pi / n)
    return jnp.cos(a1), jnp.sin(a1), jnp.cos(a0) * norm, jnp.sin(a0) * norm


def _channel_dft_table():
    c = jnp.arange(GROUP, dtype=jnp.int32)
    ang = ((c[:, None] * c[None, :]) % GROUP).astype(F32) * (2.0 * math.pi / GROUP)
    norm = 1.0 / math.sqrt(GROUP)
    return (jnp.concatenate([jnp.cos(ang), jnp.sin(ang)], axis=1) * norm).astype(BF16)


def _rope_tables(n):
    t = jnp.arange(n)
    row = (t // GRID_W).astype(F32)
    col = (t % GRID_W).astype(F32)
    half = QK_ROPE // 2
    inv = ROPE_THETA ** (-jnp.arange(0, half, 2, dtype=F32) / half)
    ar, ac = row[:, None] * inv, col[:, None] * inv
    ang = jnp.concatenate([ar, ar, ac, ac], axis=-1)
    pad = lambda a, fill: jnp.concatenate(
        [jnp.full((n, ROPE_LANE), fill, F32), a, jnp.full((n, HEAD_PAD - ROPE_LANE - QK_ROPE), fill, F32)], axis=1)
    return pad(jnp.cos(ang), 1.0), pad(jnp.sin(ang), 0.0)


def kernel(x_prompt, x_sample, cache_ckv, cache_krope, c, c_ctx, w_ada, b_ada, w_in, q_norm_g, w_uq,
           kv_norm_g, w_ukv, w_f_out, w_a_out, w_out, final_norm_g):
    assert w_in.shape[0] == 1
    b_ctx, n_ctx, _ = x_prompt.shape
    dec_b, n_lat, _ = x_sample.shape
    past = cache_ckv.shape[2]

    w_q3 = w_uq[0].reshape(Q_RANK, N_HEADS, QK_NOPE + QK_ROPE)
    head_pad = lambda a, left: jnp.pad(
        a, ((0, 0), (0, 0), (left, HEAD_PAD - left - a.shape[2]))).reshape(Q_RANK, N_HEADS * HEAD_PAD)
    rp = w_q3[:, :, QK_NOPE:].reshape(Q_RANK, N_HEADS, 2, 2, QK_ROPE // 4)
    w_q_rot = jnp.stack([-rp[:, :, :, 1], rp[:, :, :, 0]], axis=3).reshape(Q_RANK, N_HEADS, QK_ROPE)
    w_uq_p = jnp.concatenate([head_pad(w_q3, 0), head_pad(w_q_rot, ROPE_LANE)], axis=1).astype(BF16)
    w_kv3 = w_ukv[0].reshape(KV_RANK, N_HEADS, QK_NOPE + V_DIM)
    w_knope = jnp.pad(w_kv3[:, :, :QK_NOPE], ((0, 0), (0, 0), (0, HEAD_PAD - QK_NOPE)))
    place = np.zeros((HEAD_PAD, N_HEADS, HEAD_PAD), np.float32)
    for j in range(QK_ROPE):
        place[ROPE_LANE + j, :, ROPE_LANE + j] = 1.0
    w_k = jnp.concatenate([w_knope.reshape(KV_RANK, -1), jnp.asarray(place).reshape(HEAD_PAD, -1)],
                          axis=0).astype(BF16)
    w_uvt = jnp.pad(jnp.transpose(w_kv3[:, :, QK_NOPE:], (1, 2, 0)),
                    ((0, 0), (0, V_ROWS - V_DIM), (0, 0))).reshape(VT_ROWS, KV_RANK).astype(BF16)
    ones_col = np.zeros((N_HEADS, V_ROWS, LANES), np.float32)
    ones_col[:, V_DIM, :] = 1.0
    ones_col = jnp.asarray(ones_col.reshape(VT_ROWS, LANES))
    w_f = w_f_out[0].astype(BF16)
    w_a = w_a_out[0].astype(BF16)
    w_o = w_out[0].astype(BF16)
    qg = q_norm_g[0].reshape(1, Q_RANK)
    kvg = kv_norm_g[0].reshape(1, KV_RANK)
    fg = final_norm_g.reshape(1, D_MODEL)
    wts = (qg, kvg, w_uq_p, w_k, w_uvt, ones_col, _channel_dft_table())
    w_in_p = _pack_w_in(jnp.swapaxes(w_in[0], 0, 1))

    cond8 = jnp.concatenate([c_ctx[None, :], c, jnp.zeros((8 - 1 - dec_b, D_MODEL), F32)], axis=0)
    mod3 = _adaln(cond8, w_ada[0], b_ada[0].reshape(1, -1)).reshape(8, 1, 3 * D_MODEL)

    xp = x_prompt.reshape(b_ctx * n_ctx, D_MODEL)
    ucs, q, k, vt, state_ckv, state_krope = _layer_in(xp, mod3, w_in_p, wts, None, n=n_ctx, state=True)
    f = _pos_dft(ucs, n=n_ctx)
    attn = _attention(q, [k], [vt], [n_ctx], n=n_ctx)
    y_prompt = _merge(xp, mod3, f, attn, w_in_p, w_f, w_a, w_o, fg, n=n_ctx, latent=False)

    xs = x_sample.reshape(dec_b * n_lat, D_MODEL)
    ucs, q, k, vt = _layer_in(xs, mod3, w_in_p, wts, _rope_tables(n_lat), n=n_lat, state=False)
    f = _pos_dft(ucs, n=n_lat)
    krope_pad = jnp.pad(cache_krope[:, 0], ((0, 0), (0, 0), (ROPE_LANE, HEAD_PAD - ROPE_LANE - QK_ROPE)))
    k_c, vt_c = _cache_kv(cache_ckv[:, 0].reshape(dec_b * past, KV_RANK),
                          krope_pad.reshape(dec_b * past, HEAD_PAD), w_k, w_uvt, ones_col)
    attn = _attention(q, [k, k_c], [vt, vt_c], [n_lat, past], n=n_lat)
    y_sample = _merge(xs, mod3, f, attn, w_in_p, w_f, w_a, w_o, fg, n=n_lat, latent=True)

    return (y_prompt.reshape(b_ctx, n_ctx, D_MODEL), y_sample.reshape(dec_b, n_lat, D_MODEL),
            state_ckv.reshape(b_ctx, 1, n_ctx, KV_RANK), state_krope.reshape(b_ctx, 1, n_ctx, QK_ROPE))
```

```python
import functools
import math

import jax
import jax.numpy as jnp
import numpy as np
from jax import lax
from jax.experimental import pallas as pl
from jax.experimental.pallas import tpu as pltpu

F32 = jnp.float32
BF16 = jnp.bfloat16

D_MODEL = 1024
GRID_W = 64
N_GROUPS = 4
GROUP = 128
D_FOURIER = N_GROUPS * GROUP
N_HEADS = 8
QK_NOPE = 64
QK_ROPE = 32
V_DIM = 64
Q_RANK = 256
KV_RANK = 128
D_ATTN = N_HEADS * V_DIM
D_IN = 2 * D_FOURIER + Q_RANK + KV_RANK + QK_ROPE + D_ATTN + 2 * D_MODEL
ROPE_THETA = 10000.0
EPS = 1e-6
LANES = 128
HEAD_PAD = LANES
ROPE_LANE = QK_NOPE
V_ROWS = V_DIM + 16
VT_ROWS = N_HEADS * V_ROWS
Q_SCALE = (QK_NOPE + QK_ROPE) ** -0.5 * math.log2(math.e)
KEY_CHUNK = 256
Q_TILE = 256
REV_BLOCK = 256
DFT_SPLIT = 64
MIN_DENOMINATOR = 2.0 ** -60
MAX_DENOMINATOR = 2.0 ** 60
W_BLK = 512
B_UF, B_ZF, B_MID, B_ZA = 0, 1, 2, 3
D_IN_PAD = 8 * W_BLK
TOKEN_TILE = 1024
LAYER_IN_SUB_TILE = TOKEN_TILE
MERGE_SUB_TILE = TOKEN_TILE // 2
VMEM_LIMIT_BYTES = 56 * 1024 * 1024


def _const_spec(shape):
    nd = len(shape)
    return pl.BlockSpec(shape, lambda *_: (0,) * nd, pipeline_mode=pl.Buffered(1))


def _row_spec(rows, blk):
    return pl.BlockSpec((rows, D_MODEL), lambda *_: (blk, 0), pipeline_mode=pl.Buffered(1))


def _params(n_axes):
    return pltpu.CompilerParams(dimension_semantics=("arbitrary",) * n_axes,
                                vmem_limit_bytes=VMEM_LIMIT_BYTES)


def _rms(x):
    return x * lax.rsqrt(jnp.mean(x * x, axis=-1, keepdims=True) + EPS)


def _sigmoid(x):
    return 0.5 * jnp.tanh(0.5 * x) + 0.5


def _dot(a, b):
    return jnp.dot(a, b, preferred_element_type=F32)


def _dot_nt(a, b):
    return lax.dot_general(a, b, (((1,), (1,)), ((), ())), preferred_element_type=F32)


def _token_tiling(t, n, *, per_sequence):
    tm = TOKEN_TILE
    assert t % tm == 0 and (not per_sequence or n % tm == 0)
    per_seq = n // tm if per_sequence else None
    mod_row = (lambda i: 1 + i // per_seq) if per_sequence else (lambda i: 0)
    return tm, per_seq, mod_row


def _sub_tiles(rows, step):
    return [slice(r, r + step) for r in range(0, rows, step)]


def _modulated(x, mod_ref):
    mod = mod_ref[0]
    shift, scale = mod[:, 0:D_MODEL], mod[:, D_MODEL:2 * D_MODEL]
    return (_rms(x) * (1.0 + scale) + shift).astype(BF16)


def _adaln_kernel(c_ref, w_ref, b_ref, o_ref):
    c = c_ref[...]
    s = (c * _sigmoid(c)).astype(BF16)
    o_ref[...] = _dot(s, w_ref[...].astype(BF16)) + b_ref[...]


def _adaln(cond8, w_ada, b_ada):
    n_blk = 4
    bw = 3 * D_MODEL // n_blk
    return pl.pallas_call(
        _adaln_kernel,
        out_shape=jax.ShapeDtypeStruct((8, 3 * D_MODEL), F32),
        grid=(n_blk,),
        in_specs=[pl.BlockSpec((8, D_MODEL), lambda j: (0, 0)),
                  pl.BlockSpec((D_MODEL, bw), lambda j: (0, j)),
                  pl.BlockSpec((1, bw), lambda j: (0, j))],
        out_specs=pl.BlockSpec((8, bw), lambda j: (0, j)),
        compiler_params=_params(1),
        name="adaln",
    )(cond8, w_ada, b_ada)


def _rot_rows(w):
    q = QK_ROPE // 4
    return jnp.concatenate([-w[q:2 * q], w[0:q], -w[3 * q:4 * q], w[2 * q:3 * q]], axis=0)


def _pack_w_in_kernel(wt_ref, o_ref):
    lo = 2 * D_FOURIER + Q_RANK + KV_RANK
    o_ref[:lo] = wt_ref[:lo].astype(BF16)
    kr = wt_ref[lo:lo + QK_ROPE]
    zeros = jnp.zeros((QK_ROPE, D_MODEL), F32)
    o_ref[lo:lo + LANES] = jnp.concatenate([_rot_rows(kr), zeros, kr, zeros], axis=0).astype(BF16)
    o_ref[lo + LANES:] = wt_ref[lo + QK_ROPE:].astype(BF16)


def _pack_w_in(w_in_t):
    return pl.pallas_call(
        _pack_w_in_kernel,
        out_shape=jax.ShapeDtypeStruct((D_IN_PAD, D_MODEL), BF16),
        grid=(1,),
        in_specs=[_const_spec((D_IN, D_MODEL))],
        out_specs=_const_spec((D_IN_PAD, D_MODEL)),
        compiler_params=_params(1),
        name="pack_w_in",
    )(w_in_t)


def _kv_outputs(ckvn_b, kr_b, w_k_ref, w_uvt_ref, ones_ref, k_ref, vt_ref, rows):
    tm = ckvn_b.shape[0]
    kp = _dot(jnp.concatenate([ckvn_b, kr_b], axis=1), w_k_ref[...])
    for hh in range(N_HEADS):
        k_ref[hh, rows] = kp[:, hh * HEAD_PAD:(hh + 1) * HEAD_PAD].astype(BF16)
    ones = jnp.concatenate([ones_ref[...]] * (tm // LANES), axis=1)
    vt_ref[:, rows] = (_dot_nt(w_uvt_ref[...], ckvn_b) + ones).astype(BF16)


def _layer_in_kernel(*refs, rope, state):
    (x_ref, mod_ref, w_uf_ref, w_mid_ref, qg_ref, kvg_ref, w_uq_ref, w_k_ref, w_uvt_ref, ones_ref,
     cs_ref), refs = refs[:11], refs[11:]
    if rope:
        (cos_ref, sin_ref), refs = refs[:2], refs[2:]
    (ucs_ref, q_ref, k_ref, vt_ref), refs = refs[:4], refs[4:]
    if state:
        ckvn_ref, krope_ref = refs

    for rows in _sub_tiles(x_ref.shape[0], LAYER_IN_SUB_TILE):
        h = _modulated(x_ref[rows], mod_ref)

        u = _dot_nt(h, w_uf_ref[...]).astype(BF16)
        for g in range(N_GROUPS):
            r = _dot(u[:, g * GROUP:(g + 1) * GROUP], cs_ref[...])
            ucs_ref[rows, g * GROUP:(g + 1) * GROUP] = r[:, :GROUP].astype(BF16)
            ucs_ref[rows, D_FOURIER + g * GROUP:D_FOURIER + (g + 1) * GROUP] = r[:, GROUP:].astype(BF16)

        mid = _dot_nt(h, w_mid_ref[...])
        cq, ckv, kr = mid[:, :Q_RANK], mid[:, Q_RANK:Q_RANK + KV_RANK], mid[:, Q_RANK + KV_RANK:]
        cqn = (_rms(cq) * (qg_ref[...] * Q_SCALE)).astype(BF16)
        ckvn = _rms(ckv) * kvg_ref[...]
        if state:
            ckvn_ref[rows] = ckvn
            krope_ref[rows] = kr[:, ROPE_LANE:ROPE_LANE + QK_ROPE]
        n_q = N_HEADS * HEAD_PAD
        if rope:
            cos, sin = cos_ref[rows], sin_ref[rows]
            kr = kr * cos + pltpu.roll(kr, ROPE_LANE, 1) * sin
            qp = _dot(cqn, w_uq_ref[...])
        else:
            qp = _dot(cqn, w_uq_ref[:, :n_q])
        for hh in range(N_HEADS):
            qh = qp[:, hh * HEAD_PAD:(hh + 1) * HEAD_PAD]
            if rope:
                qh = qh * cos + qp[:, n_q + hh * HEAD_PAD:n_q + (hh + 1) * HEAD_PAD] * sin
            q_ref[hh, rows] = qh.astype(BF16)
        _kv_outputs(ckvn.astype(BF16), kr.astype(BF16), w_k_ref, w_uvt_ref, ones_ref, k_ref, vt_ref, rows)


def _layer_in(x, mod3, w_in_p, wts, rope_tabs, *, n, state):
    t = x.shape[0]
    rope = rope_tabs is not None
    tm, per_seq, mod_row = _token_tiling(t, n, per_sequence=rope)
    tok = lambda w: pl.BlockSpec((tm, w), lambda i: (i, 0))
    in_specs = [tok(D_MODEL),
                pl.BlockSpec((1, 1, 3 * D_MODEL), lambda i: (mod_row(i), 0, 0)),
                _row_spec(W_BLK, B_UF), _row_spec(W_BLK, B_MID)]
    in_specs += [_const_spec(w.shape) for w in wts]
    args = [x, mod3, w_in_p, w_in_p, *wts]
    if rope:
        in_specs += [pl.BlockSpec((tm, HEAD_PAD), lambda i: (i % per_seq, 0))] * 2
        args += list(rope_tabs)
    head = pl.BlockSpec((N_HEADS, tm, HEAD_PAD), lambda i: (0, i, 0))
    out_specs = [tok(2 * D_FOURIER), head, head, pl.BlockSpec((VT_ROWS, tm), lambda i: (0, i))]
    out_shape = [jax.ShapeDtypeStruct((t, 2 * D_FOURIER), BF16),
                 jax.ShapeDtypeStruct((N_HEADS, t, HEAD_PAD), BF16),
                 jax.ShapeDtypeStruct((N_HEADS, t, HEAD_PAD), BF16),
                 jax.ShapeDtypeStruct((VT_ROWS, t), BF16)]
    if state:
        out_specs += [tok(KV_RANK), tok(QK_ROPE)]
        out_shape += [jax.ShapeDtypeStruct((t, KV_RANK), F32), jax.ShapeDtypeStruct((t, QK_ROPE), F32)]
    return pl.pallas_call(
        functools.partial(_layer_in_kernel, rope=rope, state=state),
        out_shape=out_shape,
        grid=(t // tm,),
        in_specs=in_specs,
        out_specs=out_specs,
        compiler_params=_params(1),
        name="layer_in_latent" if rope else "layer_in_context",
    )(*args)


def _cache_kv_kernel(ckv_ref, kr_ref, w_k_ref, w_uvt_ref, ones_ref, k_ref, vt_ref):
    _kv_outputs(ckv_ref[...].astype(BF16), kr_ref[...].astype(BF16), w_k_ref, w_uvt_ref, ones_ref,
                k_ref, vt_ref, slice(None))


def _cache_kv(ckv, krope_pad, w_k, w_uvt, ones_col):
    t = ckv.shape[0]
    out_shape = [jax.ShapeDtypeStruct((N_HEADS, t, HEAD_PAD), BF16), jax.ShapeDtypeStruct((VT_ROWS, t), BF16)]
    return pl.pallas_call(
        _cache_kv_kernel,
        out_shape=out_shape,
        grid=(1,),
        in_specs=[_const_spec(a.shape) for a in (ckv, krope_pad, w_k, w_uvt, ones_col)],
        out_specs=[_const_spec(s.shape) for s in out_shape],
        compiler_params=_params(1),
        name="cache_kv",
    )(ckv, krope_pad, w_k, w_uvt, ones_col)


def _pos_dft_direct_kernel(cn_ref, sn_ref, ucs_ref, f_ref):
    n = cn_ref.shape[0]
    for s in range(f_ref.shape[0] // n):
        rows = slice(s * n, (s + 1) * n)
        f = _dot(cn_ref[...], ucs_ref[rows, :D_FOURIER]) + _dot(sn_ref[...], ucs_ref[rows, D_FOURIER:])
        f_ref[rows] = f.astype(BF16)


def _pos_dft_half_kernel(c1_ref, s1_ref, c0_ref, s0_ref, rev_ref, ucs_ref, f_ref, cn_ref, sn_ref):
    half = f_ref.shape[0] // 2

    @pl.when(pl.program_id(0) == 0)
    def _():
        c0, s0 = c0_ref[...], s0_ref[...]
        for k1 in range(c1_ref.shape[0]):
            rows = slice(k1 * DFT_SPLIT, min((k1 + 1) * DFT_SPLIT, cn_ref.shape[0]))
            m = rows.stop - rows.start
            c1, s1 = c1_ref[k1:k1 + 1, :], s1_ref[k1:k1 + 1, :]
            cn_ref[rows] = (c1 * c0[:m] - s1 * s0[:m]).astype(BF16)
            sn_ref[rows] = (-(s1 * c0[:m] + c1 * s0[:m])).astype(BF16)

    g = _dot(cn_ref[...], ucs_ref[:, :D_FOURIER])
    hn = _dot(sn_ref[...], ucs_ref[:, D_FOURIER:])
    f_ref[:half] = (g[:half] + hn[:half]).astype(BF16)
    mirrored = (g[1:half + 1] - hn[1:half + 1]).astype(BF16)
    blocks = half // REV_BLOCK
    for a in range(blocks):
        blk = mirrored[(blocks - 1 - a) * REV_BLOCK:(blocks - a) * REV_BLOCK]
        f_ref[half + a * REV_BLOCK:half + (a + 1) * REV_BLOCK] = _dot(rev_ref[...], blk).astype(BF16)


def _pos_dft(ucs, *, n):
    t = ucs.shape[0]
    out_shape = jax.ShapeDtypeStruct((t, D_FOURIER), BF16)
    if n <= REV_BLOCK:
        cn, snn = _dft_tables(n)
        seqs = min(4, t // n)
        return pl.pallas_call(
            _pos_dft_direct_kernel,
            out_shape=out_shape,
            grid=(t // (n * seqs),),
            in_specs=[_const_spec(cn.shape), _const_spec(snn.shape),
                      pl.BlockSpec((n * seqs, 2 * D_FOURIER), lambda i: (i, 0))],
            out_specs=pl.BlockSpec((n * seqs, D_FOURIER), lambda i: (i, 0)),
            compiler_params=_params(1),
            name=f"pos_dft_{n}",
        )(cn, snn, ucs)
    rows = n // 2 + 8
    factors = _dft_factors(n, rows)
    rev = jnp.asarray(np.eye(REV_BLOCK, dtype=np.float32)[::-1], dtype=BF16)
    return pl.pallas_call(
        _pos_dft_half_kernel,
        out_shape=out_shape,
        grid=(t // n,),
        in_specs=[_const_spec(a.shape) for a in (*factors, rev)]
        + [pl.BlockSpec((n, 2 * D_FOURIER), lambda i: (i, 0))],
        out_specs=pl.BlockSpec((n, D_FOURIER), lambda i: (i, 0)),
        scratch_shapes=[pltpu.VMEM((rows, n), BF16), pltpu.VMEM((rows, n), BF16)],
        compiler_params=_params(1),
        name=f"pos_dft_{n}",
    )(*factors, rev, ucs)


def _attention_kernel(*refs, n_kv, single_pass, units):
    q_ref, refs = refs[0], refs[1:]
    k_refs, vt_refs, refs = refs[:n_kv], refs[n_kv:2 * n_kv], refs[2 * n_kv:]
    o_ref, s_ref, pp_ref, ot_ref = refs[-4:]
    tq = q_ref.shape[1] // units
    key_lens = [k_ref.shape[1] // (1 if single_pass else units) for k_ref in k_refs]
    chunks, base = [], 0
    for j, m in enumerate(key_lens):
        chunks += [(j, c, base + c) for c in range(0, m, KEY_CHUNK)]
        base += m

    def q_block(hh, u):
        return q_ref[hh, u * tq:(u + 1) * tq, :]

    def finish(hh, ot, u):
        l = ot[V_DIM:V_DIM + 1]
        ot_ref[u, pl.ds(pl.multiple_of(hh * V_DIM, V_DIM), V_DIM), :] = ot[:V_DIM] / l
        return l

    def first_chunk_scores(hh, u):
        j, c, _ = chunks[0]
        s = _dot_nt(k_refs[j][hh, c:c + KEY_CHUNK, :], q_block(hh, u))
        s_ref[u, hh * KEY_CHUNK:(hh + 1) * KEY_CHUNK, :] = s
        return jnp.max(s, axis=0, keepdims=True)

    def shifted_softmax_pv(score_head, pv_head, shifts, u):
        if score_head is not None:
            qh = q_block(score_head, u)
            shift = shifts[score_head]
        ot = None
        for idx, (j, c, r) in enumerate(chunks):
            if score_head is not None:
                s = (s_ref[u, score_head * KEY_CHUNK:(score_head + 1) * KEY_CHUNK, :] if idx == 0 else
                     _dot_nt(k_refs[j][score_head, c:c + KEY_CHUNK, :], qh))
                pp_ref[2 * u + score_head % 2, r:r + KEY_CHUNK, :] = jnp.exp2(s - shift).astype(BF16)
            if pv_head is not None:
                part = _dot(vt_refs[j][pv_head * V_ROWS:(pv_head + 1) * V_ROWS, c:c + KEY_CHUNK],
                            pp_ref[2 * u + pv_head % 2, r:r + KEY_CHUNK, :])
                ot = part if ot is None else ot + part
        return finish(pv_head, ot, u) if pv_head is not None else None

    def exact_scores(hh, slot, u, kseq):
        qh = q_block(hh, u)
        m8 = None
        for j, c, r in chunks:
            k0 = kseq * key_lens[j] + c
            s = _dot_nt(k_refs[j][hh, k0:k0 + KEY_CHUNK, :], qh)
            s_ref[slot, r:r + KEY_CHUNK, :] = s
            mc = jnp.max(s.reshape(KEY_CHUNK // 8, 8, tq), axis=0)
            m8 = mc if m8 is None else jnp.maximum(m8, mc)
        return jnp.max(m8, axis=0, keepdims=True)

    def exact_softmax_pv(hh, s_slot, p_slot, m, u, kseq):
        for _, _, r in chunks:
            pp_ref[p_slot, r:r + KEY_CHUNK, :] = jnp.exp2(s_ref[s_slot, r:r + KEY_CHUNK, :] - m).astype(BF16)
        rows = pl.ds(pl.multiple_of(hh * V_ROWS, V_ROWS), V_ROWS)
        ot, base = None, 0
        for j, vt_ref in enumerate(vt_refs):
            mk = key_lens[j]
            part = _dot(vt_ref[rows, kseq * mk:(kseq + 1) * mk], pp_ref[p_slot, base:base + mk, :])
            ot = part if ot is None else ot + part
            base += mk
        finish(hh, ot, u)

    if not single_pass:
        work = [(u, hh) for u in range(units) for hh in range(N_HEADS)]
        maxima = [exact_scores(hh, slot, u, u) for slot, (u, hh) in enumerate(work)]
        for slot, (u, hh) in enumerate(work):
            exact_softmax_pv(hh, slot, slot, maxima[slot], u, u)
    else:
        trusted = []
        for u in range(units):
            l_min = l_max = None
            shifts = [first_chunk_scores(hh, u) for hh in range(N_HEADS)]
            for hh in range(N_HEADS + 1):
                l = shifted_softmax_pv(hh if hh < N_HEADS else None, hh - 1 if hh > 0 else None, shifts, u)
                if l is not None:
                    l_min = l if l_min is None else jnp.minimum(l_min, l)
                    l_max = l if l_max is None else jnp.maximum(l_max, l)
            trusted.append(jnp.logical_and(jnp.min(l_min) >= MIN_DENOMINATOR,
                                           jnp.max(l_max) <= MAX_DENOMINATOR))

        for u in range(units):
            @pl.when(jnp.logical_not(trusted[u]))
            def _(u=u):
                def body(hh, carry):
                    exact_softmax_pv(hh, u, 2 * u, exact_scores(hh, u, u, 0), u, 0)
                    return carry

                lax.fori_loop(0, N_HEADS, body, 0)

    for u in range(units):
        o_ref[u * tq:(u + 1) * tq, :] = ot_ref[u].T.astype(BF16)


def _attention(q, ks, vts, key_lens, *, n, after=None):
    t = q.shape[1]
    tq = Q_TILE
    m_tot = sum(key_lens)
    single_pass = m_tot > 2 * KEY_CHUNK
    if single_pass:
        units = 2
        steps = n // (units * tq)
        kv_rows = 1
        s_slots, p_slots = units, 2 * units
    else:
        assert n == tq
        units = min(4, t // n)
        steps = 1
        kv_rows = units
        s_slots = p_slots = units * N_HEADS
    in_specs = [pl.BlockSpec((N_HEADS, units * tq, HEAD_PAD), lambda bi, i: (0, bi * steps + i, 0))]
    in_specs += [pl.BlockSpec((N_HEADS, kv_rows * m, HEAD_PAD), lambda bi, i: (0, bi, 0)) for m in key_lens]
    in_specs += [pl.BlockSpec((VT_ROWS, kv_rows * m), lambda bi, i: (0, bi)) for m in key_lens]
    args = [q, *ks, *vts]
    if after is not None:
        in_specs.append(pl.BlockSpec((8, after.shape[1]), lambda bi, i: (0, 0)))
        args.append(after)
    return pl.pallas_call(
        functools.partial(_attention_kernel, n_kv=len(ks), single_pass=single_pass, units=units),
        out_shape=jax.ShapeDtypeStruct((t, D_ATTN), BF16),
        grid=(t // (n * kv_rows), steps),
        in_specs=in_specs,
        out_specs=pl.BlockSpec((units * tq, D_ATTN), lambda bi, i: (bi * steps + i, 0)),
        scratch_shapes=[pltpu.VMEM((s_slots, m_tot, tq), F32), pltpu.VMEM((p_slots, m_tot, tq), BF16),
                        pltpu.VMEM((units, D_ATTN, tq), F32)],
        compiler_params=_params(2),
        name=f"attention_{n}",
    )(*args)


def _merge_kernel(x_ref, mod_ref, f_ref, a_ref, w_zf_ref, w_za_ref, w_g_ref, w_f_ref, w_a_ref, w_o_ref,
                  g_ref, y_ref):
    gate = mod_ref[0][:, 2 * D_MODEL:]
    for rows in _sub_tiles(x_ref.shape[0], MERGE_SUB_TILE):
        x = x_ref[rows]
        h = _modulated(x, mod_ref)
        zf = _dot_nt(h, w_zf_ref[...])
        y_f = _dot((f_ref[rows] * (zf * _sigmoid(zf))).astype(BF16), w_f_ref[...])
        za = _dot_nt(h, w_za_ref[...])
        y_a = _dot((a_ref[rows] * (za * _sigmoid(za))).astype(BF16), w_a_ref[...])
        merged = (_sigmoid(_dot_nt(h, w_g_ref[:D_MODEL])) * y_f
                  + _sigmoid(_dot_nt(h, w_g_ref[D_MODEL:])) * y_a)
        out = x + gate * _dot(merged.astype(BF16), w_o_ref[...])
        y_ref[rows] = _rms(out) * g_ref[...]


def _merge(x, mod3, f, attn, w_in_p, w_f, w_a, w_o, g, *, n, latent):
    t = x.shape[0]
    tm, _, mod_row = _token_tiling(t, n, per_sequence=latent)
    tok = lambda w: pl.BlockSpec((tm, w), lambda i: (i, 0))
    return pl.pallas_call(
        _merge_kernel,
        out_shape=jax.ShapeDtypeStruct((t, D_MODEL), F32),
        grid=(t // tm,),
        in_specs=[tok(D_MODEL),
                  pl.BlockSpec((1, 1, 3 * D_MODEL), lambda i: (mod_row(i), 0, 0)),
                  tok(D_FOURIER), tok(D_ATTN),
                  _row_spec(W_BLK, B_ZF), _row_spec(W_BLK, B_ZA), _row_spec(2 * D_MODEL, 1),
                  _const_spec(w_f.shape), _const_spec(w_a.shape), _const_spec(w_o.shape),
                  _const_spec(g.shape)],
        out_specs=tok(D_MODEL),
        compiler_params=_params(1),
        name=f"merge_{n}",
    )(x, mod3, f, attn, w_in_p, w_in_p, w_in_p, w_f, w_a, w_o, g)


def _dft_tables(n):
    norm = 1.0 / math.sqrt(n)
    k = jnp.arange(n, dtype=jnp.int32)
    ang = ((k[:, None] * k[None, :]) % n).astype(F32) * (2.0 * math.pi / n)
    return (jnp.cos(ang) * norm).astype(BF16), (jnp.sin(ang) * -norm).astype(BF16)


def _dft_factors(n, rows):
    norm = 1.0 / math.sqrt(n)
    period = n // DFT_SPLIT
    pos = jnp.arange(n, dtype=jnp.int32)
    k1 = jnp.arange(-(-rows // DFT_SPLIT), dtype=jnp.int32)
    k0 = jnp.arange(DFT_SPLIT, dtype=jnp.int32)
    a1 = ((k1[:, None] * pos[None, :]) % period).astype(F32) * (2.0 * math.pi / period)
    a0 = ((k0[:, None] * pos[None, :]) % n).astype(F32) * (2.0 * math.pi / n)
    return jnp.cos(a1), jnp.sin(a1), jnp.cos(a0) * norm, jnp.sin(a0) * norm


def _channel_dft_table():
    c = jnp.arange(GROUP, dtype=jnp.int32)
    ang = ((c[:, None] * c[None, :]) % GROUP).astype(F32) * (2.0 * math.pi / GROUP)
    norm = 1.0 / math.sqrt(GROUP)
    return (jnp.concatenate([jnp.cos(ang), jnp.sin(ang)], axis=1) * norm).astype(BF16)


def _rope_tables(n):
    t = jnp.arange(n)
    row = (t // GRID_W).astype(F32)
    col = (t % GRID_W).astype(F32)
    half = QK_ROPE // 2
    inv = ROPE_THETA ** (-jnp.arange(0, half, 2, dtype=F32) / half)
    ar, ac = row[:, None] * inv, col[:, None] * inv
    ang = jnp.concatenate([ar, ar, ac, ac], axis=-1)
    pad = lambda a, fill: jnp.concatenate(
        [jnp.full((n, ROPE_LANE), fill, F32), a, jnp.full((n, HEAD_PAD - ROPE_LANE - QK_ROPE), fill, F32)], axis=1)
    return pad(jnp.cos(ang), 1.0), pad(jnp.sin(ang), 0.0)


def kernel(x_prompt, x_sample, cache_ckv, cache_krope, c, c_ctx, w_ada, b_ada, w_in, q_norm_g, w_uq,
           kv_norm_g, w_ukv, w_f_out, w_a_out, w_out, final_norm_g):
    assert w_in.shape[0] == 1
    b_ctx, n_ctx, _ = x_prompt.shape
    dec_b, n_lat, _ = x_sample.shape
    past = cache_ckv.shape[2]

    w_q3 = w_uq[0].reshape(Q_RANK, N_HEADS, QK_NOPE + QK_ROPE)
    head_pad = lambda a, left: jnp.pad(
        a, ((0, 0), (0, 0), (left, HEAD_PAD - left - a.shape[2]))).reshape(Q_RANK, N_HEADS * HEAD_PAD)
    rp = w_q3[:, :, QK_NOPE:].reshape(Q_RANK, N_HEADS, 2, 2, QK_ROPE // 4)
    w_q_rot = jnp.stack([-rp[:, :, :, 1], rp[:, :, :, 0]], axis=3).reshape(Q_RANK, N_HEADS, QK_ROPE)
    w_uq_p = jnp.concatenate([head_pad(w_q3, 0), head_pad(w_q_rot, ROPE_LANE)], axis=1).astype(BF16)
    w_kv3 = w_ukv[0].reshape(KV_RANK, N_HEADS, QK_NOPE + V_DIM)
    w_knope = jnp.pad(w_kv3[:, :, :QK_NOPE], ((0, 0), (0, 0), (0, HEAD_PAD - QK_NOPE)))
    place = np.zeros((HEAD_PAD, N_HEADS, HEAD_PAD), np.float32)
    for j in range(QK_ROPE):
        place[ROPE_LANE + j, :, ROPE_LANE + j] = 1.0
    w_k = jnp.concatenate([w_knope.reshape(KV_RANK, -1), jnp.asarray(place).reshape(HEAD_PAD, -1)],
                          axis=0).astype(BF16)
    w_uvt = jnp.pad(jnp.transpose(w_kv3[:, :, QK_NOPE:], (1, 2, 0)),
                    ((0, 0), (0, V_ROWS - V_DIM), (0, 0))).reshape(VT_ROWS, KV_RANK).astype(BF16)
    ones_col = np.zeros((N_HEADS, V_ROWS, LANES), np.float32)
    ones_col[:, V_DIM, :] = 1.0
    ones_col = jnp.asarray(ones_col.reshape(VT_ROWS, LANES))
    w_f = w_f_out[0].astype(BF16)
    w_a = w_a_out[0].astype(BF16)
    w_o = w_out[0].astype(BF16)
    qg = q_norm_g[0].reshape(1, Q_RANK)
    kvg = kv_norm_g[0].reshape(1, KV_RANK)
    fg = final_norm_g.reshape(1, D_MODEL)
    wts = (qg, kvg, w_uq_p, w_k, w_uvt, ones_col, _channel_dft_table())
    w_in_p = _pack_w_in(jnp.swapaxes(w_in[0], 0, 1))

    cond8 = jnp.concatenate([c_ctx[None, :], c, jnp.zeros((8 - 1 - dec_b, D_MODEL), F32)], axis=0)
    mod3 = _adaln(cond8, w_ada[0], b_ada[0].reshape(1, -1)).reshape(8, 1, 3 * D_MODEL)

    xp = x_prompt.reshape(b_ctx * n_ctx, D_MODEL)
    ucs, q, k, vt, state_ckv, state_krope = _layer_in(xp, mod3, w_in_p, wts, None, n=n_ctx, state=True)
    f = _pos_dft(ucs, n=n_ctx)
    attn = _attention(q, [k], [vt], [n_ctx], n=n_ctx)
    y_prompt = _merge(xp, mod3, f, attn, w_in_p, w_f, w_a, w_o, fg, n=n_ctx, latent=False)

    xs = x_sample.reshape(dec_b * n_lat, D_MODEL)
    ucs, q, k, vt = _layer_in(xs, mod3, w_in_p, wts, _rope_tables(n_lat), n=n_lat, state=False)
    f = _pos_dft(ucs, n=n_lat)
    krope_pad = jnp.pad(cache_krope[:, 0], ((0, 0), (0, 0), (ROPE_LANE, HEAD_PAD - ROPE_LANE - QK_ROPE)))
    k_c, vt_c = _cache_kv(cache_ckv[:, 0].reshape(dec_b * past, KV_RANK),
                          krope_pad.reshape(dec_b * past, HEAD_PAD), w_k, w_uvt, ones_col)
    attn = _attention(q, [k, k_c], [vt, vt_c], [n_lat, past], n=n_lat, after=f)
    y_sample = _merge(xs, mod3, f, attn, w_in_p, w_f, w_a, w_o, fg, n=n_lat, latent=True)

    return (y_prompt.reshape(b_ctx, n_ctx, D_MODEL), y_sample.reshape(dec_b, n_lat, D_MODEL),
            state_ckv.reshape(b_ctx, 1, n_ctx, KV_RANK), state_krope.reshape(b_ctx, 1, n_ctx, QK_ROPE))
```

```python
import functools
import math

import jax
import jax.numpy as jnp
import numpy as np
from jax import lax
from jax.experimental import pallas as pl
from jax.experimental.pallas import tpu as pltpu

F32 = jnp.float32
BF16 = jnp.bfloat16

D_MODEL = 1024
GRID_W = 64
N_GROUPS = 4
GROUP = 128
D_FOURIER = N_GROUPS * GROUP
N_HEADS = 8
QK_NOPE = 64
QK_ROPE = 32
V_DIM = 64
Q_RANK = 256
KV_RANK = 128
D_ATTN = N_HEADS * V_DIM
D_IN = 2 * D_FOURIER + Q_RANK + KV_RANK + QK_ROPE + D_ATTN + 2 * D_MODEL
ROPE_THETA = 10000.0
EPS = 1e-6
LANES = 128
HEAD_PAD = LANES
ROPE_LANE = QK_NOPE
V_ROWS = V_DIM + 16
VT_ROWS = N_HEADS * V_ROWS
Q_SCALE = (QK_NOPE + QK_ROPE) ** -0.5 * math.log2(math.e)
KEY_CHUNK = 256
Q_TILE = 256
REV_BLOCK = 256
DFT_SPLIT = 64
MIN_DENOMINATOR = 2.0 ** -60
MAX_DENOMINATOR = 2.0 ** 60
W_BLK = 512
B_UF, B_ZF, B_MID, B_ZA = 0, 1, 2, 3
D_IN_PAD = 8 * W_BLK
TOKEN_TILE = 1024
LAYER_IN_SUB_TILE = TOKEN_TILE
MERGE_SUB_TILE = TOKEN_TILE // 2
VMEM_LIMIT_BYTES = 56 * 1024 * 1024


def _const_spec(shape):
    nd = len(shape)
    return pl.BlockSpec(shape, lambda *_: (0,) * nd, pipeline_mode=pl.Buffered(1))


def _row_spec(rows, blk):
    return pl.BlockSpec((rows, D_MODEL), lambda *_: (blk, 0), pipeline_mode=pl.Buffered(1))


def _params(n_axes):
    return pltpu.CompilerParams(dimension_semantics=("arbitrary",) * n_axes,
                                vmem_limit_bytes=VMEM_LIMIT_BYTES)


def _rms(x):
    return x * lax.rsqrt(jnp.mean(x * x, axis=-1, keepdims=True) + EPS)


def _sigmoid(x):
    return 0.5 * jnp.tanh(0.5 * x) + 0.5


def _dot(a, b):
    return jnp.dot(a, b, preferred_element_type=F32)


def _dot_nt(a, b):
    return lax.dot_general(a, b, (((1,), (1,)), ((), ())), preferred_element_type=F32)


def _token_tiling(t, n, *, per_sequence):
    tm = TOKEN_TILE
    assert t % tm == 0 and (not per_sequence or n % tm == 0)
    per_seq = n // tm if per_sequence else None
    mod_row = (lambda i: 1 + i // per_seq) if per_sequence else (lambda i: 0)
    return tm, per_seq, mod_row


def _sub_tiles(rows, step):
    return [slice(r, r + step) for r in range(0, rows, step)]


def _modulated(x, mod_ref):
    mod = mod_ref[0]
    shift, scale = mod[:, 0:D_MODEL], mod[:, D_MODEL:2 * D_MODEL]
    return (_rms(x) * (1.0 + scale) + shift).astype(BF16)


def _adaln_kernel(c_ref, w_ref, b_ref, o_ref):
    c = c_ref[...]
    s = (c * _sigmoid(c)).astype(BF16)
    o_ref[...] = _dot(s, w_ref[...].astype(BF16)) + b_ref[...]


def _adaln(cond8, w_ada, b_ada):
    n_blk = 4
    bw = 3 * D_MODEL // n_blk
    return pl.pallas_call(
        _adaln_kernel,
        out_shape=jax.ShapeDtypeStruct((8, 3 * D_MODEL), F32),
        grid=(n_blk,),
        in_specs=[pl.BlockSpec((8, D_MODEL), lambda j: (0, 0)),
                  pl.BlockSpec((D_MODEL, bw), lambda j: (0, j)),
                  pl.BlockSpec((1, bw), lambda j: (0, j))],
        out_specs=pl.BlockSpec((8, bw), lambda j: (0, j)),
        compiler_params=_params(1),
        name="adaln",
    )(cond8, w_ada, b_ada)


def _rot_rows(w):
    q = QK_ROPE // 4
    return jnp.concatenate([-w[q:2 * q], w[0:q], -w[3 * q:4 * q], w[2 * q:3 * q]], axis=0)


def _pack_w_in_kernel(wt_ref, o_ref):
    lo = 2 * D_FOURIER + Q_RANK + KV_RANK
    o_ref[:lo] = wt_ref[:lo].astype(BF16)
    kr = wt_ref[lo:lo + QK_ROPE]
    zeros = jnp.zeros((QK_ROPE, D_MODEL), F32)
    o_ref[lo:lo + LANES] = jnp.concatenate([_rot_rows(kr), zeros, kr, zeros], axis=0).astype(BF16)
    o_ref[lo + LANES:] = wt_ref[lo + QK_ROPE:].astype(BF16)


def _pack_w_in(w_in_t):
    return pl.pallas_call(
        _pack_w_in_kernel,
        out_shape=jax.ShapeDtypeStruct((D_IN_PAD, D_MODEL), BF16),
        grid=(1,),
        in_specs=[_const_spec((D_IN, D_MODEL))],
        out_specs=_const_spec((D_IN_PAD, D_MODEL)),
        compiler_params=_params(1),
        name="pack_w_in",
    )(w_in_t)


def _kv_outputs(ckvn_b, kr_b, w_k_ref, w_uvt_ref, ones_ref, k_ref, vt_ref, rows):
    tm = ckvn_b.shape[0]
    kp = _dot(jnp.concatenate([ckvn_b, kr_b], axis=1), w_k_ref[...])
    for hh in range(N_HEADS):
        k_ref[hh, rows] = kp[:, hh * HEAD_PAD:(hh + 1) * HEAD_PAD].astype(BF16)
    ones = jnp.concatenate([ones_ref[...]] * (tm // LANES), axis=1)
    vt_ref[:, rows] = (_dot_nt(w_uvt_ref[...], ckvn_b) + ones).astype(BF16)


def _layer_in_kernel(*refs, rope, state):
    (x_ref, mod_ref, w_uf_ref, w_mid_ref, qg_ref, kvg_ref, w_uq_ref, w_k_ref, w_uvt_ref, ones_ref,
     cs_ref), refs = refs[:11], refs[11:]
    if rope:
        (cos_ref, sin_ref), refs = refs[:2], refs[2:]
    (ucs_ref, q_ref, k_ref, vt_ref), refs = refs[:4], refs[4:]
    if state:
        ckvn_ref, krope_ref = refs

    for rows in _sub_tiles(x_ref.shape[0], LAYER_IN_SUB_TILE):
        h = _modulated(x_ref[rows], mod_ref)

        u = _dot_nt(h, w_uf_ref[...]).astype(BF16)
        for g in range(N_GROUPS):
            r = _dot(u[:, g * GROUP:(g + 1) * GROUP], cs_ref[...])
            ucs_ref[rows, g * GROUP:(g + 1) * GROUP] = r[:, :GROUP].astype(BF16)
            ucs_ref[rows, D_FOURIER + g * GROUP:D_FOURIER + (g + 1) * GROUP] = r[:, GROUP:].astype(BF16)

        mid = _dot_nt(h, w_mid_ref[...])
        cq, ckv, kr = mid[:, :Q_RANK], mid[:, Q_RANK:Q_RANK + KV_RANK], mid[:, Q_RANK + KV_RANK:]
        cqn = (_rms(cq) * (qg_ref[...] * Q_SCALE)).astype(BF16)
        ckvn = _rms(ckv) * kvg_ref[...]
        if state:
            ckvn_ref[rows] = ckvn
            krope_ref[rows] = kr[:, ROPE_LANE:ROPE_LANE + QK_ROPE]
        n_q = N_HEADS * HEAD_PAD
        if rope:
            cos, sin = cos_ref[rows], sin_ref[rows]
            kr = kr * cos + pltpu.roll(kr, ROPE_LANE, 1) * sin
            qp = _dot(cqn, w_uq_ref[...])
        else:
            qp = _dot(cqn, w_uq_ref[:, :n_q])
        for hh in range(N_HEADS):
            qh = qp[:, hh * HEAD_PAD:(hh + 1) * HEAD_PAD]
            if rope:
                qh = qh * cos + qp[:, n_q + hh * HEAD_PAD:n_q + (hh + 1) * HEAD_PAD] * sin
            q_ref[hh, rows] = qh.astype(BF16)
        _kv_outputs(ckvn.astype(BF16), kr.astype(BF16), w_k_ref, w_uvt_ref, ones_ref, k_ref, vt_ref, rows)


def _layer_in(x, mod3, w_in_p, wts, rope_tabs, *, n, state):
    t = x.shape[0]
    rope = rope_tabs is not None
    tm, per_seq, mod_row = _token_tiling(t, n, per_sequence=rope)
    tok = lambda w: pl.BlockSpec((tm, w), lambda i: (i, 0))
    in_specs = [tok(D_MODEL),
                pl.BlockSpec((1, 1, 3 * D_MODEL), lambda i: (mod_row(i), 0, 0)),
                _row_spec(W_BLK, B_UF), _row_spec(W_BLK, B_MID)]
    in_specs += [_const_spec(w.shape) for w in wts]
    args = [x, mod3, w_in_p, w_in_p, *wts]
    if rope:
        in_specs += [pl.BlockSpec((tm, HEAD_PAD), lambda i: (i % per_seq, 0))] * 2
        args += list(rope_tabs)
    head = pl.BlockSpec((N_HEADS, tm, HEAD_PAD), lambda i: (0, i, 0))
    out_specs = [tok(2 * D_FOURIER), head, head, pl.BlockSpec((VT_ROWS, tm), lambda i: (0, i))]
    out_shape = [jax.ShapeDtypeStruct((t, 2 * D_FOURIER), BF16),
                 jax.ShapeDtypeStruct((N_HEADS, t, HEAD_PAD), BF16),
                 jax.ShapeDtypeStruct((N_HEADS, t, HEAD_PAD), BF16),
                 jax.ShapeDtypeStruct((VT_ROWS, t), BF16)]
    if state:
        out_specs += [tok(KV_RANK), tok(QK_ROPE)]
        out_shape += [jax.ShapeDtypeStruct((t, KV_RANK), F32), jax.ShapeDtypeStruct((t, QK_ROPE), F32)]
    return pl.pallas_call(
        functools.partial(_layer_in_kernel, rope=rope, state=state),
        out_shape=out_shape,
        grid=(t // tm,),
        in_specs=in_specs,
        out_specs=out_specs,
        compiler_params=_params(1),
        name="layer_in_latent" if rope else "layer_in_context",
    )(*args)


def _cache_kv_kernel(ckv_ref, kr_ref, w_k_ref, w_uvt_ref, ones_ref, k_ref, vt_ref):
    _kv_outputs(ckv_ref[...].astype(BF16), kr_ref[...].astype(BF16), w_k_ref, w_uvt_ref, ones_ref,
                k_ref, vt_ref, slice(None))


def _cache_kv(ckv, krope_pad, w_k, w_uvt, ones_col):
    t = ckv.shape[0]
    out_shape = [jax.ShapeDtypeStruct((N_HEADS, t, HEAD_PAD), BF16), jax.ShapeDtypeStruct((VT_ROWS, t), BF16)]
    return pl.pallas_call(
        _cache_kv_kernel,
        out_shape=out_shape,
        grid=(1,),
        in_specs=[_const_spec(a.shape) for a in (ckv, krope_pad, w_k, w_uvt, ones_col)],
        out_specs=[_const_spec(s.shape) for s in out_shape],
        compiler_params=_params(1),
        name="cache_kv",
    )(ckv, krope_pad, w_k, w_uvt, ones_col)


def _pos_dft_direct_kernel(cn_ref, sn_ref, ucs_ref, f_ref):
    n = cn_ref.shape[0]
    for s in range(f_ref.shape[0] // n):
        rows = slice(s * n, (s + 1) * n)
        f = _dot(cn_ref[...], ucs_ref[rows, :D_FOURIER]) + _dot(sn_ref[...], ucs_ref[rows, D_FOURIER:])
        f_ref[rows] = f.astype(BF16)


def _pos_dft_half_kernel(c1_ref, s1_ref, c0_ref, s0_ref, rev_ref, rev1_ref, ucs_ref, f_ref, cn_ref, sn_ref):
    n = f_ref.shape[0]
    half = n // 2
    blocks = half // REV_BLOCK

    @pl.when(pl.program_id(0) == 0)
    def _():
        c0, s0 = c0_ref[...], s0_ref[...]
        for k1 in range(c1_ref.shape[0]):
            rows = slice(k1 * DFT_SPLIT, min((k1 + 1) * DFT_SPLIT, cn_ref.shape[0]))
            m = rows.stop - rows.start
            c1, s1 = c1_ref[k1:k1 + 1, :], s1_ref[k1:k1 + 1, :]
            cn_ref[rows] = (c1 * c0[:m] - s1 * s0[:m]).astype(BF16)
            sn_ref[rows] = (-(s1 * c0[:m] + c1 * s0[:m])).astype(BF16)

    def folded(cols, sign):
        first_row = lax.broadcasted_iota(jnp.int32, (REV_BLOCK, D_FOURIER), 0) == 0
        out = []
        for b in range(blocks):
            m = _dot(rev1_ref[...], ucs_ref[n - (b + 1) * REV_BLOCK:n - b * REV_BLOCK, cols])
            if b > 0:
                m = jnp.where(first_row, ucs_ref[n - b * REV_BLOCK:n - b * REV_BLOCK + 1, cols].astype(F32), m)
            out.append((ucs_ref[b * REV_BLOCK:(b + 1) * REV_BLOCK, cols] + sign * m).astype(BF16))
        return jnp.concatenate(out, axis=0)

    e = folded(slice(0, D_FOURIER), 1.0)
    o = folded(slice(D_FOURIER, 2 * D_FOURIER), -1.0)
    odd_k = (lax.broadcasted_iota(jnp.int32, (cn_ref.shape[0], D_FOURIER), 0) & 1) == 1
    mid = ucs_ref[half:half + 1, :D_FOURIER].astype(F32) * (1.0 / math.sqrt(n))
    g = _dot(cn_ref[...], e) + jnp.where(odd_k, -mid, mid)
    hn = _dot(sn_ref[...], o)
    f_ref[:half] = (g[:half] + hn[:half]).astype(BF16)
    mirrored = (g[1:half + 1] - hn[1:half + 1]).astype(BF16)
    for a in range(blocks):
        blk = mirrored[(blocks - 1 - a) * REV_BLOCK:(blocks - a) * REV_BLOCK]
        f_ref[half + a * REV_BLOCK:half + (a + 1) * REV_BLOCK] = _dot(rev_ref[...], blk).astype(BF16)


def _pos_dft(ucs, *, n):
    t = ucs.shape[0]
    out_shape = jax.ShapeDtypeStruct((t, D_FOURIER), BF16)
    if n <= REV_BLOCK:
        cn, snn = _dft_tables(n)
        seqs = min(4, t // n)
        return pl.pallas_call(
            _pos_dft_direct_kernel,
            out_shape=out_shape,
            grid=(t // (n * seqs),),
            in_specs=[_const_spec(cn.shape), _const_spec(snn.shape),
                      pl.BlockSpec((n * seqs, 2 * D_FOURIER), lambda i: (i, 0))],
            out_specs=pl.BlockSpec((n * seqs, D_FOURIER), lambda i: (i, 0)),
            compiler_params=_params(1),
            name=f"pos_dft_{n}",
        )(cn, snn, ucs)
    rows = n // 2 + 8
    consts = _dft_factors(n, rows, n // 2)
    anti = np.eye(REV_BLOCK, dtype=np.float32)[::-1]
    shifted = np.roll(anti, 1, axis=1)
    shifted[0] = 0.0
    consts += (jnp.asarray(anti, dtype=BF16), jnp.asarray(shifted, dtype=BF16))
    return pl.pallas_call(
        _pos_dft_half_kernel,
        out_shape=out_shape,
        grid=(t // n,),
        in_specs=[_const_spec(a.shape) for a in consts]
        + [pl.BlockSpec((n, 2 * D_FOURIER), lambda i: (i, 0))],
        out_specs=pl.BlockSpec((n, D_FOURIER), lambda i: (i, 0)),
        scratch_shapes=[pltpu.VMEM((rows, n // 2), BF16), pltpu.VMEM((rows, n // 2), BF16)],
        compiler_params=_params(1),
        name=f"pos_dft_{n}",
    )(*consts, ucs)


def _attention_kernel(*refs, n_kv, single_pass, units):
    q_ref, refs = refs[0], refs[1:]
    k_refs, vt_refs, refs = refs[:n_kv], refs[n_kv:2 * n_kv], refs[2 * n_kv:]
    o_ref, s_ref, pp_ref, ot_ref = refs[-4:]
    tq = q_ref.shape[1] // units
    key_lens = [k_ref.shape[1] // (1 if single_pass else units) for k_ref in k_refs]
    chunks, base = [], 0
    for j, m in enumerate(key_lens):
        chunks += [(j, c, base + c) for c in range(0, m, KEY_CHUNK)]
        base += m

    def q_block(hh, u):
        return q_ref[hh, u * tq:(u + 1) * tq, :]

    def finish(hh, ot, u):
        l = ot[V_DIM:V_DIM + 1]
        ot_ref[u, pl.ds(pl.multiple_of(hh * V_DIM, V_DIM), V_DIM), :] = ot[:V_DIM] / l
        return l

    def first_chunk_scores(hh, u):
        j, c, _ = chunks[0]
        s = _dot_nt(k_refs[j][hh, c:c + KEY_CHUNK, :], q_block(hh, u))
        s_ref[u, hh * KEY_CHUNK:(hh + 1) * KEY_CHUNK, :] = s
        return jnp.max(s, axis=0, keepdims=True)

    def shifted_softmax_pv(score_head, pv_head, shifts, u):
        if score_head is not None:
            qh = q_block(score_head, u)
            shift = shifts[score_head]
        ot = None
        for idx, (j, c, r) in enumerate(chunks):
            if score_head is not None:
                s = (s_ref[u, score_head * KEY_CHUNK:(score_head + 1) * KEY_CHUNK, :] if idx == 0 else
                     _dot_nt(k_refs[j][score_head, c:c + KEY_CHUNK, :], qh))
                pp_ref[2 * u + score_head % 2, r:r + KEY_CHUNK, :] = jnp.exp2(s - shift).astype(BF16)
            if pv_head is not None:
                part = _dot(vt_refs[j][pv_head * V_ROWS:(pv_head + 1) * V_ROWS, c:c + KEY_CHUNK],
                            pp_ref[2 * u + pv_head % 2, r:r + KEY_CHUNK, :])
                ot = part if ot is None else ot + part
        return finish(pv_head, ot, u) if pv_head is not None else None

    def exact_scores(hh, slot, u, kseq):
        qh = q_block(hh, u)
        m8 = None
        for j, c, r in chunks:
            k0 = kseq * key_lens[j] + c
            s = _dot_nt(k_refs[j][hh, k0:k0 + KEY_CHUNK, :], qh)
            s_ref[slot, r:r + KEY_CHUNK, :] = s
            mc = jnp.max(s.reshape(KEY_CHUNK // 8, 8, tq), axis=0)
            m8 = mc if m8 is None else jnp.maximum(m8, mc)
        return jnp.max(m8, axis=0, keepdims=True)

    def exact_softmax_pv(hh, s_slot, p_slot, m, u, kseq):
        for _, _, r in chunks:
            pp_ref[p_slot, r:r + KEY_CHUNK, :] = jnp.exp2(s_ref[s_slot, r:r + KEY_CHUNK, :] - m).astype(BF16)
        rows = pl.ds(pl.multiple_of(hh * V_ROWS, V_ROWS), V_ROWS)
        ot, base = None, 0
        for j, vt_ref in enumerate(vt_refs):
            mk = key_lens[j]
            part = _dot(vt_ref[rows, kseq * mk:(kseq + 1) * mk], pp_ref[p_slot, base:base + mk, :])
            ot = part if ot is None else ot + part
            base += mk
        finish(hh, ot, u)

    if not single_pass:
        work = [(u, hh) for u in range(units) for hh in range(N_HEADS)]
        maxima = [exact_scores(hh, slot, u, u) for slot, (u, hh) in enumerate(work)]
        for slot, (u, hh) in enumerate(work):
            exact_softmax_pv(hh, slot, slot, maxima[slot], u, u)
    else:
        trusted = []
        for u in range(units):
            l_min = l_max = None
            shifts = [first_chunk_scores(hh, u) for hh in range(N_HEADS)]
            for hh in range(N_HEADS + 1):
                l = shifted_softmax_pv(hh if hh < N_HEADS else None, hh - 1 if hh > 0 else None, shifts, u)
                if l is not None:
                    l_min = l if l_min is None else jnp.minimum(l_min, l)
                    l_max = l if l_max is None else jnp.maximum(l_max, l)
            trusted.append(jnp.logical_and(jnp.min(l_min) >= MIN_DENOMINATOR,
                                           jnp.max(l_max) <= MAX_DENOMINATOR))

        for u in range(units):
            @pl.when(jnp.logical_not(trusted[u]))
            def _(u=u):
                def body(hh, carry):
                    exact_softmax_pv(hh, u, 2 * u, exact_scores(hh, u, u, 0), u, 0)
                    return carry

                lax.fori_loop(0, N_HEADS, body, 0)

    for u in range(units):
        o_ref[u * tq:(u + 1) * tq, :] = ot_ref[u].T.astype(BF16)


def _attention(q, ks, vts, key_lens, *, n, after=None):
    t = q.shape[1]
    tq = Q_TILE
    m_tot = sum(key_lens)
    single_pass = m_tot > 2 * KEY_CHUNK
    if single_pass:
        units = 4
        steps = n // (units * tq)
        kv_rows = 1
        s_slots, p_slots = units, 2 * units
    else:
        assert n == tq
        units = min(4, t // n)
        steps = 1
        kv_rows = units
        s_slots = p_slots = units * N_HEADS
    in_specs = [pl.BlockSpec((N_HEADS, units * tq, HEAD_PAD), lambda bi, i: (0, bi * steps + i, 0))]
    in_specs += [pl.BlockSpec((N_HEADS, kv_rows * m, HEAD_PAD), lambda bi, i: (0, bi, 0)) for m in key_lens]
    in_specs += [pl.BlockSpec((VT_ROWS, kv_rows * m), lambda bi, i: (0, bi)) for m in key_lens]
    args = [q, *ks, *vts]
    if after is not None:
        in_specs.append(pl.BlockSpec((8, after.shape[1]), lambda bi, i: (0, 0)))
        args.append(after)
    return pl.pallas_call(
        functools.partial(_attention_kernel, n_kv=len(ks), single_pass=single_pass, units=units),
        out_shape=jax.ShapeDtypeStruct((t, D_ATTN), BF16),
        grid=(t // (n * kv_rows), steps),
        in_specs=in_specs,
        out_specs=pl.BlockSpec((units * tq, D_ATTN), lambda bi, i: (bi * steps + i, 0)),
        scratch_shapes=[pltpu.VMEM((s_slots, m_tot, tq), F32), pltpu.VMEM((p_slots, m_tot, tq), BF16),
                        pltpu.VMEM((units, D_ATTN, tq), F32)],
        compiler_params=_params(2),
        name=f"attention_{n}",
    )(*args)


def _merge_kernel(x_ref, mod_ref, f_ref, a_ref, w_zf_ref, w_za_ref, w_g_ref, w_f_ref, w_a_ref, w_o_ref,
                  g_ref, y_ref):
    gate = mod_ref[0][:, 2 * D_MODEL:]
    for rows in _sub_tiles(x_ref.shape[0], MERGE_SUB_TILE):
        x = x_ref[rows]
        h = _modulated(x, mod_ref)
        zf = _dot_nt(h, w_zf_ref[...])
        y_f = _dot((f_ref[rows] * (zf * _sigmoid(zf))).astype(BF16), w_f_ref[...])
        za = _dot_nt(h, w_za_ref[...])
        y_a = _dot((a_ref[rows] * (za * _sigmoid(za))).astype(BF16), w_a_ref[...])
        merged = (_sigmoid(_dot_nt(h, w_g_ref[:D_MODEL])) * y_f
                  + _sigmoid(_dot_nt(h, w_g_ref[D_MODEL:])) * y_a)
        out = x + gate * _dot(merged.astype(BF16), w_o_ref[...])
        y_ref[rows] = _rms(out) * g_ref[...]


def _merge(x, mod3, f, attn, w_in_p, w_f, w_a, w_o, g, *, n, latent):
    t = x.shape[0]
    tm, _, mod_row = _token_tiling(t, n, per_sequence=latent)
    tok = lambda w: pl.BlockSpec((tm, w), lambda i: (i, 0))
    return pl.pallas_call(
        _merge_kernel,
        out_shape=jax.ShapeDtypeStruct((t, D_MODEL), F32),
        grid=(t // tm,),
        in_specs=[tok(D_MODEL),
                  pl.BlockSpec((1, 1, 3 * D_MODEL), lambda i: (mod_row(i), 0, 0)),
                  tok(D_FOURIER), tok(D_ATTN),
                  _row_spec(W_BLK, B_ZF), _row_spec(W_BLK, B_ZA), _row_spec(2 * D_MODEL, 1),
                  _const_spec(w_f.shape), _const_spec(w_a.shape), _const_spec(w_o.shape),
                  _const_spec(g.shape)],
        out_specs=tok(D_MODEL),
        compiler_params=_params(1),
        name=f"merge_{n}",
    )(x, mod3, f, attn, w_in_p, w_in_p, w_in_p, w_f, w_a, w_o, g)


def _dft_tables(n):
    norm = 1.0 / math.sqrt(n)
    k = jnp.arange(n, dtype=jnp.int32)
    ang = ((k[:, None] * k[None, :]) % n).astype(F32) * (2.0 * math.pi / n)
    return (jnp.cos(ang) * norm).astype(BF16), (jnp.sin(ang) * -norm).astype(BF16)


def _dft_factors(n, rows, n_pos):
    norm = 1.0 / math.sqrt(n)
    period = n // DFT_SPLIT
    pos = jnp.arange(n_pos, dtype=jnp.int32)
    k1 = jnp.arange(-(-rows // DFT_SPLIT), dtype=jnp.int32)
    k0 = jnp.arange(DFT_SPLIT, dtype=jnp.int32)
    a1 = ((k1[:, None] * pos[None, :]) % period).astype(F32) * (2.0 * math.pi / period)
    a0 = ((k0[:, None] * pos[None, :]) % n).astype(F32) * (2.0 * math.pi / n)
    return jnp.cos(a1), jnp.sin(a1), jnp.cos(a0) * norm, jnp.sin(a0) * norm


def _channel_dft_table():
    c = jnp.arange(GROUP, dtype=jnp.int32)
    ang = ((c[:, None] * c[None, :]) % GROUP).astype(F32) * (2.0 * math.pi / GROUP)
    norm = 1.0 / math.sqrt(GROUP)
    return (jnp.concatenate([jnp.cos(ang), jnp.sin(ang)], axis=1) * norm).astype(BF16)


def _rope_tables(n):
    t = jnp.arange(n)
    row = (t // GRID_W).astype(F32)
    col = (t % GRID_W).astype(F32)
    half = QK_ROPE // 2
    inv = ROPE_THETA ** (-jnp.arange(0, half, 2, dtype=F32) / half)
    ar, ac = row[:, None] * inv, col[:, None] * inv
    ang = jnp.concatenate([ar, ar, ac, ac], axis=-1)
    pad = lambda a, fill: jnp.concatenate(
        [jnp.full((n, ROPE_LANE), fill, F32), a, jnp.full((n, HEAD_PAD - ROPE_LANE - QK_ROPE), fill, F32)], axis=1)
    return pad(jnp.cos(ang), 1.0), pad(jnp.sin(ang), 0.0)


def kernel(x_prompt, x_sample, cache_ckv, cache_krope, c, c_ctx, w_ada, b_ada, w_in, q_norm_g, w_uq,
           kv_norm_g, w_ukv, w_f_out, w_a_out, w_out, final_norm_g):
    assert w_in.shape[0] == 1
    b_ctx, n_ctx, _ = x_prompt.shape
    dec_b, n_lat, _ = x_sample.shape
    past = cache_ckv.shape[2]

    w_q3 = w_uq[0].reshape(Q_RANK, N_HEADS, QK_NOPE + QK_ROPE)
    head_pad = lambda a, left: jnp.pad(
        a, ((0, 0), (0, 0), (left, HEAD_PAD - left - a.shape[2]))).reshape(Q_RANK, N_HEADS * HEAD_PAD)
    rp = w_q3[:, :, QK_NOPE:].reshape(Q_RANK, N_HEADS, 2, 2, QK_ROPE // 4)
    w_q_rot = jnp.stack([-rp[:, :, :, 1], rp[:, :, :, 0]], axis=3).reshape(Q_RANK, N_HEADS, QK_ROPE)
    w_uq_p = jnp.concatenate([head_pad(w_q3, 0), head_pad(w_q_rot, ROPE_LANE)], axis=1).astype(BF16)
    w_kv3 = w_ukv[0].reshape(KV_RANK, N_HEADS, QK_NOPE + V_DIM)
    w_knope = jnp.pad(w_kv3[:, :, :QK_NOPE], ((0, 0), (0, 0), (0, HEAD_PAD - QK_NOPE)))
    place = np.zeros((HEAD_PAD, N_HEADS, HEAD_PAD), np.float32)
    for j in range(QK_ROPE):
        place[ROPE_LANE + j, :, ROPE_LANE + j] = 1.0
    w_k = jnp.concatenate([w_knope.reshape(KV_RANK, -1), jnp.asarray(place).reshape(HEAD_PAD, -1)],
                          axis=0).astype(BF16)
    w_uvt = jnp.pad(jnp.transpose(w_kv3[:, :, QK_NOPE:], (1, 2, 0)),
                    ((0, 0), (0, V_ROWS - V_DIM), (0, 0))).reshape(VT_ROWS, KV_RANK).astype(BF16)
    ones_col = np.zeros((N_HEADS, V_ROWS, LANES), np.float32)
    ones_col[:, V_DIM, :] = 1.0
    ones_col = jnp.asarray(ones_col.reshape(VT_ROWS, LANES))
    w_f = w_f_out[0].astype(BF16)
    w_a = w_a_out[0].astype(BF16)
    w_o = w_out[0].astype(BF16)
    qg = q_norm_g[0].reshape(1, Q_RANK)
    kvg = kv_norm_g[0].reshape(1, KV_RANK)
    fg = final_norm_g.reshape(1, D_MODEL)
    wts = (qg, kvg, w_uq_p, w_k, w_uvt, ones_col, _channel_dft_table())
    w_in_p = _pack_w_in(jnp.swapaxes(w_in[0], 0, 1))

    cond8 = jnp.concatenate([c_ctx[None, :], c, jnp.zeros((8 - 1 - dec_b, D_MODEL), F32)], axis=0)
    mod3 = _adaln(cond8, w_ada[0], b_ada[0].reshape(1, -1)).reshape(8, 1, 3 * D_MODEL)

    xp = x_prompt.reshape(b_ctx * n_ctx, D_MODEL)
    ucs, q, k, vt, state_ckv, state_krope = _layer_in(xp, mod3, w_in_p, wts, None, n=n_ctx, state=True)
    f = _pos_dft(ucs, n=n_ctx)
    attn = _attention(q, [k], [vt], [n_ctx], n=n_ctx)
    y_prompt = _merge(xp, mod3, f, attn, w_in_p, w_f, w_a, w_o, fg, n=n_ctx, latent=False)

    xs = x_sample.reshape(dec_b * n_lat, D_MODEL)
    ucs, q, k, vt = _layer_in(xs, mod3, w_in_p, wts, _rope_tables(n_lat), n=n_lat, state=False)
    f = _pos_dft(ucs, n=n_lat)
    krope_pad = jnp.pad(cache_krope[:, 0], ((0, 0), (0, 0), (ROPE_LANE, HEAD_PAD - ROPE_LANE - QK_ROPE)))
    k_c, vt_c = _cache_kv(cache_ckv[:, 0].reshape(dec_b * past, KV_RANK),
                          krope_pad.reshape(dec_b * past, HEAD_PAD), w_k, w_uvt, ones_col)
    attn = _attention(q, [k, k_c], [vt, vt_c], [n_lat, past], n=n_lat, after=f)
    y_sample = _merge(xs, mod3, f, attn, w_in_p, w_f, w_a, w_o, fg, n=n_lat, latent=True)

    return (y_prompt.reshape(b_ctx, n_ctx, D_MODEL), y_sample.reshape(dec_b, n_lat, D_MODEL),
            state_ckv.reshape(b_ctx, 1, n_ctx, KV_RANK), state_krope.reshape(b_ctx, 1, n_ctx, QK_ROPE))
```

```python
import functools
import math

import jax
import jax.numpy as jnp
import numpy as np
from jax import lax
from jax.experimental import pallas as pl
from jax.experimental.pallas import tpu as pltpu

F32 = jnp.float32
BF16 = jnp.bfloat16

D_MODEL = 1024
GRID_W = 64
N_GROUPS = 4
GROUP = 128
D_FOURIER = N_GROUPS * GROUP
N_HEADS = 8
QK_NOPE = 64
QK_ROPE = 32
V_DIM = 64
Q_RANK = 256
KV_RANK = 128
D_ATTN = N_HEADS * V_DIM
D_IN = 2 * D_FOURIER + Q_RANK + KV_RANK + QK_ROPE + D_ATTN + 2 * D_MODEL
ROPE_THETA = 10000.0
EPS = 1e-6
LANES = 128
HEAD_PAD = LANES
ROPE_LANE = QK_NOPE
V_ROWS = V_DIM + 16
VT_ROWS = N_HEADS * V_ROWS
Q_SCALE = (QK_NOPE + QK_ROPE) ** -0.5 * math.log2(math.e)
KEY_CHUNK = 256
Q_TILE = 256
REV_BLOCK = 256
DFT_SPLIT = 64
MIN_DENOMINATOR = 2.0 ** -60
MAX_DENOMINATOR = 2.0 ** 60
W_BLK = 512
B_UF, B_ZF, B_MID, B_ZA = 0, 1, 2, 3
D_IN_PAD = 8 * W_BLK
TOKEN_TILE = 1024
LAYER_IN_SUB_TILE = TOKEN_TILE
MERGE_SUB_TILE = TOKEN_TILE // 2
VMEM_LIMIT_BYTES = 56 * 1024 * 1024


def _const_spec(shape):
    nd = len(shape)
    return pl.BlockSpec(shape, lambda *_: (0,) * nd, pipeline_mode=pl.Buffered(1))


def _row_spec(rows, blk):
    return pl.BlockSpec((rows, D_MODEL), lambda *_: (blk, 0), pipeline_mode=pl.Buffered(1))


def _params(n_axes):
    return pltpu.CompilerParams(dimension_semantics=("arbitrary",) * n_axes,
                                vmem_limit_bytes=VMEM_LIMIT_BYTES)


def _rms(x):
    return x * lax.rsqrt(jnp.mean(x * x, axis=-1, keepdims=True) + EPS)


def _sigmoid(x):
    return 0.5 * jnp.tanh(0.5 * x) + 0.5


def _dot(a, b):
    return jnp.dot(a, b, preferred_element_type=F32)


def _dot_nt(a, b):
    return lax.dot_general(a, b, (((1,), (1,)), ((), ())), preferred_element_type=F32)


def _token_tiling(t, n, *, per_sequence):
    tm = TOKEN_TILE
    assert t % tm == 0 and (not per_sequence or n % tm == 0)
    per_seq = n // tm if per_sequence else None
    mod_row = (lambda i: 1 + i // per_seq) if per_sequence else (lambda i: 0)
    return tm, per_seq, mod_row


def _sub_tiles(rows, step):
    return [slice(r, r + step) for r in range(0, rows, step)]


def _modulated(x, mod_ref):
    mod = mod_ref[0]
    shift, scale = mod[:, 0:D_MODEL], mod[:, D_MODEL:2 * D_MODEL]
    return (_rms(x) * (1.0 + scale) + shift).astype(BF16)


def _adaln_kernel(c_ref, w_ref, b_ref, o_ref):
    c = c_ref[...]
    s = (c * _sigmoid(c)).astype(BF16)
    mod = _dot(s, w_ref[...].astype(BF16)) + b_ref[...]
    for r in range(mod.shape[0]):
        o_ref[r] = mod[r:r + 1]


def _adaln(cond8, w_ada, b_ada):
    n_blk = 4
    bw = 3 * D_MODEL // n_blk
    return pl.pallas_call(
        _adaln_kernel,
        out_shape=jax.ShapeDtypeStruct((8, 1, 3 * D_MODEL), F32),
        grid=(n_blk,),
        in_specs=[pl.BlockSpec((8, D_MODEL), lambda j: (0, 0)),
                  pl.BlockSpec((D_MODEL, bw), lambda j: (0, j)),
                  pl.BlockSpec((1, bw), lambda j: (0, j))],
        out_specs=pl.BlockSpec((8, 1, bw), lambda j: (0, 0, j)),
        compiler_params=_params(1),
        name="adaln",
    )(cond8, w_ada, b_ada)


def _rot_rows(w):
    q = QK_ROPE // 4
    return jnp.concatenate([-w[q:2 * q], w[0:q], -w[3 * q:4 * q], w[2 * q:3 * q]], axis=0)


def _pack_w_in_kernel(wt_ref, o_ref):
    lo = 2 * D_FOURIER + Q_RANK + KV_RANK
    o_ref[:lo] = wt_ref[:lo].astype(BF16)
    kr = wt_ref[lo:lo + QK_ROPE]
    zeros = jnp.zeros((QK_ROPE, D_MODEL), F32)
    o_ref[lo:lo + LANES] = jnp.concatenate([_rot_rows(kr), zeros, kr, zeros], axis=0).astype(BF16)
    o_ref[lo + LANES:] = wt_ref[lo + QK_ROPE:].astype(BF16)


def _pack_w_in(w_in_t):
    return pl.pallas_call(
        _pack_w_in_kernel,
        out_shape=jax.ShapeDtypeStruct((D_IN_PAD, D_MODEL), BF16),
        grid=(1,),
        in_specs=[_const_spec((D_IN, D_MODEL))],
        out_specs=_const_spec((D_IN_PAD, D_MODEL)),
        compiler_params=_params(1),
        name="pack_w_in",
    )(w_in_t)


def _kv_outputs(ckvn_b, kr_b, w_k_ref, w_uvt_ref, ones_ref, k_ref, vt_ref, rows):
    tm = ckvn_b.shape[0]
    kp = _dot(jnp.concatenate([ckvn_b, kr_b], axis=1), w_k_ref[...])
    for hh in range(N_HEADS):
        k_ref[hh, rows] = kp[:, hh * HEAD_PAD:(hh + 1) * HEAD_PAD].astype(BF16)
    ones = jnp.concatenate([ones_ref[...]] * (tm // LANES), axis=1)
    vt_ref[:, rows] = (_dot_nt(w_uvt_ref[...], ckvn_b) + ones).astype(BF16)


def _layer_in_kernel(*refs, rope, state):
    (x_ref, mod_ref, w_uf_ref, w_mid_ref, qg_ref, kvg_ref, w_uq_ref, w_k_ref, w_uvt_ref, ones_ref,
     cs_ref), refs = refs[:11], refs[11:]
    if rope:
        (cos_ref, sin_ref), refs = refs[:2], refs[2:]
    (ucs_ref, q_ref, k_ref, vt_ref), refs = refs[:4], refs[4:]
    if state:
        ckvn_ref, krope_ref = refs

    for rows in _sub_tiles(x_ref.shape[0], LAYER_IN_SUB_TILE):
        h = _modulated(x_ref[rows], mod_ref)

        mid = _dot_nt(h, w_mid_ref[...])
        cq, ckv, kr = mid[:, :Q_RANK], mid[:, Q_RANK:Q_RANK + KV_RANK], mid[:, Q_RANK + KV_RANK:]
        cqn = (_rms(cq) * (qg_ref[...] * Q_SCALE)).astype(BF16)
        ckvn = _rms(ckv) * kvg_ref[...]
        if state:
            ckvn_ref[rows] = ckvn
            krope_ref[rows] = kr[:, ROPE_LANE:ROPE_LANE + QK_ROPE]
        n_q = N_HEADS * HEAD_PAD
        if rope:
            cos, sin = cos_ref[rows], sin_ref[rows]
            kr = kr * cos + pltpu.roll(kr, ROPE_LANE, 1) * sin
            qp = _dot(cqn, w_uq_ref[...])
        else:
            qp = _dot(cqn, w_uq_ref[:, :n_q])
        for hh in range(N_HEADS):
            qh = qp[:, hh * HEAD_PAD:(hh + 1) * HEAD_PAD]
            if rope:
                qh = qh * cos + qp[:, n_q + hh * HEAD_PAD:n_q + (hh + 1) * HEAD_PAD] * sin
            q_ref[hh, rows] = qh.astype(BF16)
        _kv_outputs(ckvn.astype(BF16), kr.astype(BF16), w_k_ref, w_uvt_ref, ones_ref, k_ref, vt_ref, rows)

        u = _dot_nt(h, w_uf_ref[...]).astype(BF16)
        for g in range(N_GROUPS):
            r = _dot(u[:, g * GROUP:(g + 1) * GROUP], cs_ref[...])
            ucs_ref[rows, g * GROUP:(g + 1) * GROUP] = r[:, :GROUP].astype(BF16)
            ucs_ref[rows, D_FOURIER + g * GROUP:D_FOURIER + (g + 1) * GROUP] = r[:, GROUP:].astype(BF16)


def _layer_in(x, mod3, w_in_p, wts, rope_tabs, *, n, state):
    t = x.shape[0]
    rope = rope_tabs is not None
    tm, per_seq, mod_row = _token_tiling(t, n, per_sequence=rope)
    tok = lambda w: pl.BlockSpec((tm, w), lambda i: (i, 0))
    in_specs = [tok(D_MODEL),
                pl.BlockSpec((1, 1, 3 * D_MODEL), lambda i: (mod_row(i), 0, 0)),
                _row_spec(W_BLK, B_UF), _row_spec(W_BLK, B_MID)]
    in_specs += [_const_spec(w.shape) for w in wts]
    args = [x, mod3, w_in_p, w_in_p, *wts]
    if rope:
        in_specs += [pl.BlockSpec((tm, HEAD_PAD), lambda i: (i % per_seq, 0))] * 2
        args += list(rope_tabs)
    head = pl.BlockSpec((N_HEADS, tm, HEAD_PAD), lambda i: (0, i, 0))
    out_specs = [tok(2 * D_FOURIER), head, head, pl.BlockSpec((VT_ROWS, tm), lambda i: (0, i))]
    out_shape = [jax.ShapeDtypeStruct((t, 2 * D_FOURIER), BF16),
                 jax.ShapeDtypeStruct((N_HEADS, t, HEAD_PAD), BF16),
                 jax.ShapeDtypeStruct((N_HEADS, t, HEAD_PAD), BF16),
                 jax.ShapeDtypeStruct((VT_ROWS, t), BF16)]
    if state:
        out_specs += [tok(KV_RANK), tok(QK_ROPE)]
        out_shape += [jax.ShapeDtypeStruct((t, KV_RANK), F32), jax.ShapeDtypeStruct((t, QK_ROPE), F32)]
    return pl.pallas_call(
        functools.partial(_layer_in_kernel, rope=rope, state=state),
        out_shape=out_shape,
        grid=(t // tm,),
        in_specs=in_specs,
        out_specs=out_specs,
        compiler_params=_params(1),
        name="layer_in_latent" if rope else "layer_in_context",
    )(*args)


def _cache_kv_kernel(ckv_ref, kr_ref, w_k_ref, w_uvt_ref, ones_ref, k_ref, vt_ref):
    _kv_outputs(ckv_ref[...].astype(BF16), kr_ref[...].astype(BF16), w_k_ref, w_uvt_ref, ones_ref,
                k_ref, vt_ref, slice(None))


def _cache_kv(ckv, krope_pad, w_k, w_uvt, ones_col):
    t = ckv.shape[0]
    out_shape = [jax.ShapeDtypeStruct((N_HEADS, t, HEAD_PAD), BF16), jax.ShapeDtypeStruct((VT_ROWS, t), BF16)]
    return pl.pallas_call(
        _cache_kv_kernel,
        out_shape=out_shape,
        grid=(1,),
        in_specs=[_const_spec(a.shape) for a in (ckv, krope_pad, w_k, w_uvt, ones_col)],
        out_specs=[_const_spec(s.shape) for s in out_shape],
        compiler_params=_params(1),
        name="cache_kv",
    )(ckv, krope_pad, w_k, w_uvt, ones_col)


def _pos_dft_direct_kernel(cn_ref, sn_ref, ucs_ref, f_ref):
    n = cn_ref.shape[0]
    for s in range(f_ref.shape[0] // n):
        rows = slice(s * n, (s + 1) * n)
        f = _dot(cn_ref[...], ucs_ref[rows, :D_FOURIER]) + _dot(sn_ref[...], ucs_ref[rows, D_FOURIER:])
        f_ref[rows] = f.astype(BF16)


def _pos_dft_half_kernel(c1_ref, s1_ref, c0_ref, s0_ref, rev_ref, rev1_ref, ucs_ref, f_ref, cn_ref, sn_ref):
    n = f_ref.shape[0]
    half = n // 2
    blocks = half // REV_BLOCK

    @pl.when(pl.program_id(0) == 0)
    def _():
        c0, s0 = c0_ref[...], s0_ref[...]
        for k1 in range(c1_ref.shape[0]):
            rows = slice(k1 * DFT_SPLIT, min((k1 + 1) * DFT_SPLIT, cn_ref.shape[0]))
            m = rows.stop - rows.start
            c1, s1 = c1_ref[k1:k1 + 1, :], s1_ref[k1:k1 + 1, :]
            cn_ref[rows] = (c1 * c0[:m] - s1 * s0[:m]).astype(BF16)
            sn_ref[rows] = (-(s1 * c0[:m] + c1 * s0[:m])).astype(BF16)

    def folded(cols, sign):
        first_row = lax.broadcasted_iota(jnp.int32, (REV_BLOCK, D_FOURIER), 0) == 0
        out = []
        for b in range(blocks):
            m = _dot(rev1_ref[...], ucs_ref[n - (b + 1) * REV_BLOCK:n - b * REV_BLOCK, cols])
            if b > 0:
                m = jnp.where(first_row, ucs_ref[n - b * REV_BLOCK:n - b * REV_BLOCK + 1, cols].astype(F32), m)
            out.append((ucs_ref[b * REV_BLOCK:(b + 1) * REV_BLOCK, cols] + sign * m).astype(BF16))
        return jnp.concatenate(out, axis=0)

    e = folded(slice(0, D_FOURIER), 1.0)
    o = folded(slice(D_FOURIER, 2 * D_FOURIER), -1.0)
    odd_k = (lax.broadcasted_iota(jnp.int32, (cn_ref.shape[0], D_FOURIER), 0) & 1) == 1
    mid = ucs_ref[half:half + 1, :D_FOURIER].astype(F32) * (1.0 / math.sqrt(n))
    g = _dot(cn_ref[...], e) + jnp.where(odd_k, -mid, mid)
    hn = _dot(sn_ref[...], o)
    f_ref[:half] = (g[:half] + hn[:half]).astype(BF16)
    mirrored = (g[1:half + 1] - hn[1:half + 1]).astype(BF16)
    for a in range(blocks):
        blk = mirrored[(blocks - 1 - a) * REV_BLOCK:(blocks - a) * REV_BLOCK]
        f_ref[half + a * REV_BLOCK:half + (a + 1) * REV_BLOCK] = _dot(rev_ref[...], blk).astype(BF16)


def _pos_dft(ucs, *, n):
    t = ucs.shape[0]
    out_shape = jax.ShapeDtypeStruct((t, D_FOURIER), BF16)
    if n <= REV_BLOCK:
        cn, snn = _dft_tables(n)
        seqs = min(8, t // n)
        return pl.pallas_call(
            _pos_dft_direct_kernel,
            out_shape=out_shape,
            grid=(t // (n * seqs),),
            in_specs=[_const_spec(cn.shape), _const_spec(snn.shape),
                      pl.BlockSpec((n * seqs, 2 * D_FOURIER), lambda i: (i, 0))],
            out_specs=pl.BlockSpec((n * seqs, D_FOURIER), lambda i: (i, 0)),
            compiler_params=_params(1),
            name=f"pos_dft_{n}",
        )(cn, snn, ucs)
    rows = n // 2 + 8
    consts = _dft_factors(n, rows, n // 2)
    anti = np.eye(REV_BLOCK, dtype=np.float32)[::-1]
    shifted = np.roll(anti, 1, axis=1)
    shifted[0] = 0.0
    consts += (jnp.asarray(anti, dtype=BF16), jnp.asarray(shifted, dtype=BF16))
    return pl.pallas_call(
        _pos_dft_half_kernel,
        out_shape=out_shape,
        grid=(t // n,),
        in_specs=[_const_spec(a.shape) for a in consts]
        + [pl.BlockSpec((n, 2 * D_FOURIER), lambda i: (i, 0))],
        out_specs=pl.BlockSpec((n, D_FOURIER), lambda i: (i, 0)),
        scratch_shapes=[pltpu.VMEM((rows, n // 2), BF16), pltpu.VMEM((rows, n // 2), BF16)],
        compiler_params=_params(1),
        name=f"pos_dft_{n}",
    )(*consts, ucs)


def _attention_kernel(*refs, n_kv, single_pass, units):
    q_ref, refs = refs[0], refs[1:]
    k_refs, vt_refs, refs = refs[:n_kv], refs[n_kv:2 * n_kv], refs[2 * n_kv:]
    o_ref, s_ref, pp_ref, ot_ref = refs[-4:]
    tq = q_ref.shape[1] // units
    key_lens = [k_ref.shape[1] // (1 if single_pass else units) for k_ref in k_refs]
    chunks, base = [], 0
    for j, m in enumerate(key_lens):
        chunks += [(j, c, base + c) for c in range(0, m, KEY_CHUNK)]
        base += m

    def q_block(hh, u):
        return q_ref[hh, u * tq:(u + 1) * tq, :]

    def finish(hh, ot, u):
        l = ot[V_DIM:V_DIM + 1]
        ot_ref[u, pl.ds(pl.multiple_of(hh * V_DIM, V_DIM), V_DIM), :] = ot[:V_DIM] / l
        return l

    def first_chunk_scores(hh, u):
        j, c, _ = chunks[0]
        s = _dot_nt(k_refs[j][hh, c:c + KEY_CHUNK, :], q_block(hh, u))
        s_ref[u, hh * KEY_CHUNK:(hh + 1) * KEY_CHUNK, :] = s
        return jnp.max(s, axis=0, keepdims=True)

    def shifted_softmax_pv(score_head, pv_head, shifts, u):
        if score_head is not None:
            qh = q_block(score_head, u)
            shift = shifts[score_head]
        ot = None
        for idx, (j, c, r) in enumerate(chunks):
            if score_head is not None:
                s = (s_ref[u, score_head * KEY_CHUNK:(score_head + 1) * KEY_CHUNK, :] if idx == 0 else
                     _dot_nt(k_refs[j][score_head, c:c + KEY_CHUNK, :], qh))
                pp_ref[2 * u + score_head % 2, r:r + KEY_CHUNK, :] = jnp.exp2(s - shift).astype(BF16)
            if pv_head is not None:
                part = _dot(vt_refs[j][pv_head * V_ROWS:(pv_head + 1) * V_ROWS, c:c + KEY_CHUNK],
                            pp_ref[2 * u + pv_head % 2, r:r + KEY_CHUNK, :])
                ot = part if ot is None else ot + part
        return finish(pv_head, ot, u) if pv_head is not None else None

    def exact_scores(hh, slot, u, kseq):
        qh = q_block(hh, u)
        m8 = None
        for j, c, r in chunks:
            k0 = kseq * key_lens[j] + c
            s = _dot_nt(k_refs[j][hh, k0:k0 + KEY_CHUNK, :], qh)
            s_ref[slot, r:r + KEY_CHUNK, :] = s
            mc = jnp.max(s.reshape(KEY_CHUNK // 8, 8, tq), axis=0)
            m8 = mc if m8 is None else jnp.maximum(m8, mc)
        return jnp.max(m8, axis=0, keepdims=True)

    def exact_softmax_pv(hh, s_slot, p_slot, m, u, kseq):
        for _, _, r in chunks:
            pp_ref[p_slot, r:r + KEY_CHUNK, :] = jnp.exp2(s_ref[s_slot, r:r + KEY_CHUNK, :] - m).astype(BF16)
        rows = pl.ds(pl.multiple_of(hh * V_ROWS, V_ROWS), V_ROWS)
        ot, base = None, 0
        for j, vt_ref in enumerate(vt_refs):
            mk = key_lens[j]
            part = _dot(vt_ref[rows, kseq * mk:(kseq + 1) * mk], pp_ref[p_slot, base:base + mk, :])
            ot = part if ot is None else ot + part
            base += mk
        finish(hh, ot, u)

    if not single_pass:
        work = [(u, hh) for u in range(units) for hh in range(N_HEADS)]
        maxima = [exact_scores(hh, slot, u, u) for slot, (u, hh) in enumerate(work)]
        for slot, (u, hh) in enumerate(work):
            exact_softmax_pv(hh, slot, slot, maxima[slot], u, u)
    else:
        trusted = []
        for u in range(units):
            l_min = l_max = None
            shifts = [first_chunk_scores(hh, u) for hh in range(N_HEADS)]
            for hh in range(N_HEADS + 1):
                l = shifted_softmax_pv(hh if hh < N_HEADS else None, hh - 1 if hh > 0 else None, shifts, u)
                if l is not None:
                    l_min = l if l_min is None else jnp.minimum(l_min, l)
                    l_max = l if l_max is None else jnp.maximum(l_max, l)
            trusted.append(jnp.logical_and(jnp.min(l_min) >= MIN_DENOMINATOR,
                                           jnp.max(l_max) <= MAX_DENOMINATOR))

        for u in range(units):
            @pl.when(jnp.logical_not(trusted[u]))
            def _(u=u):
                def body(hh, carry):
                    exact_softmax_pv(hh, u, 2 * u, exact_scores(hh, u, u, 0), u, 0)
                    return carry

                lax.fori_loop(0, N_HEADS, body, 0)

    for u in range(units):
        o_ref[u * tq:(u + 1) * tq, :] = ot_ref[u].T.astype(BF16)


def _attention(q, ks, vts, key_lens, *, n, after=None):
    t = q.shape[1]
    tq = Q_TILE
    m_tot = sum(key_lens)
    single_pass = m_tot > 2 * KEY_CHUNK
    if single_pass:
        units = 4
        steps = n // (units * tq)
        kv_rows = 1
        s_slots, p_slots = units, 2 * units
    else:
        assert n == tq
        units = min(4, t // n)
        steps = 1
        kv_rows = units
        s_slots = p_slots = units * N_HEADS
    in_specs = [pl.BlockSpec((N_HEADS, units * tq, HEAD_PAD), lambda bi, i: (0, bi * steps + i, 0))]
    in_specs += [pl.BlockSpec((N_HEADS, kv_rows * m, HEAD_PAD), lambda bi, i: (0, bi, 0)) for m in key_lens]
    in_specs += [pl.BlockSpec((VT_ROWS, kv_rows * m), lambda bi, i: (0, bi)) for m in key_lens]
    args = [q, *ks, *vts]
    if after is not None:
        in_specs.append(pl.BlockSpec((8, after.shape[1]), lambda bi, i: (0, 0)))
        args.append(after)
    return pl.pallas_call(
        functools.partial(_attention_kernel, n_kv=len(ks), single_pass=single_pass, units=units),
        out_shape=jax.ShapeDtypeStruct((t, D_ATTN), BF16),
        grid=(t // (n * kv_rows), steps),
        in_specs=in_specs,
        out_specs=pl.BlockSpec((units * tq, D_ATTN), lambda bi, i: (bi * steps + i, 0)),
        scratch_shapes=[pltpu.VMEM((s_slots, m_tot, tq), F32), pltpu.VMEM((p_slots, m_tot, tq), BF16),
                        pltpu.VMEM((units, D_ATTN, tq), F32)],
        compiler_params=_params(2),
        name=f"attention_{n}",
    )(*args)


def _merge_kernel(x_ref, mod_ref, f_ref, a_ref, w_zf_ref, w_za_ref, w_g_ref, w_f_ref, w_a_ref, w_o_ref,
                  g_ref, y_ref):
    gate = mod_ref[0][:, 2 * D_MODEL:]
    for rows in _sub_tiles(x_ref.shape[0], MERGE_SUB_TILE):
        x = x_ref[rows]
        h = _modulated(x, mod_ref)
        zf = _dot_nt(h, w_zf_ref[...])
        y_f = _dot((f_ref[rows] * (zf * _sigmoid(zf))).astype(BF16), w_f_ref[...])
        za = _dot_nt(h, w_za_ref[...])
        y_a = _dot((a_ref[rows] * (za * _sigmoid(za))).astype(BF16), w_a_ref[...])
        merged = (_sigmoid(_dot_nt(h, w_g_ref[:D_MODEL])) * y_f
                  + _sigmoid(_dot_nt(h, w_g_ref[D_MODEL:])) * y_a)
        out = x + gate * _dot(merged.astype(BF16), w_o_ref[...])
        y_ref[rows] = _rms(out) * g_ref[...]


def _merge(x, mod3, f, attn, w_in_p, w_f, w_a, w_o, g, *, n, latent):
    t = x.shape[0]
    tm, _, mod_row = _token_tiling(t, n, per_sequence=latent)
    tok = lambda w: pl.BlockSpec((tm, w), lambda i: (i, 0))
    return pl.pallas_call(
        _merge_kernel,
        out_shape=jax.ShapeDtypeStruct((t, D_MODEL), F32),
        grid=(t // tm,),
        in_specs=[tok(D_MODEL),
                  pl.BlockSpec((1, 1, 3 * D_MODEL), lambda i: (mod_row(i), 0, 0)),
                  tok(D_FOURIER), tok(D_ATTN),
                  _row_spec(W_BLK, B_ZF), _row_spec(W_BLK, B_ZA), _row_spec(2 * D_MODEL, 1),
                  _const_spec(w_f.shape), _const_spec(w_a.shape), _const_spec(w_o.shape),
                  _const_spec(g.shape)],
        out_specs=tok(D_MODEL),
        compiler_params=_params(1),
        name=f"merge_{n}",
    )(x, mod3, f, attn, w_in_p, w_in_p, w_in_p, w_f, w_a, w_o, g)


def _dft_tables(n):
    norm = 1.0 / math.sqrt(n)
    k = jnp.arange(n, dtype=jnp.int32)
    ang = ((k[:, None] * k[None, :]) % n).astype(F32) * (2.0 * math.pi / n)
    return (jnp.cos(ang) * norm).astype(BF16), (jnp.sin(ang) * -norm).astype(BF16)


def _dft_factors(n, rows, n_pos):
    norm = 1.0 / math.sqrt(n)
    period = n // DFT_SPLIT
    pos = jnp.arange(n_pos, dtype=jnp.int32)
    k1 = jnp.arange(-(-rows // DFT_SPLIT), dtype=jnp.int32)
    k0 = jnp.arange(DFT_SPLIT, dtype=jnp.int32)
    a1 = ((k1[:, None] * pos[None, :]) % period).astype(F32) * (2.0 * math.pi / period)
    a0 = ((k0[:, None] * pos[None, :]) % n).astype(F32) * (2.0 * math.pi / n)
    return jnp.cos(a1), jnp.sin(a1), jnp.cos(a0) * norm, jnp.sin(a0) * norm


def _channel_dft_table():
    c = jnp.arange(GROUP, dtype=jnp.int32)
    ang = ((c[:, None] * c[None, :]) % GROUP).astype(F32) * (2.0 * math.pi / GROUP)
    norm = 1.0 / math.sqrt(GROUP)
    return (jnp.concatenate([jnp.cos(ang), jnp.sin(ang)], axis=1) * norm).astype(BF16)


def _rope_tables(n):
    t = jnp.arange(n)
    row = (t // GRID_W).astype(F32)
    col = (t % GRID_W).astype(F32)
    half = QK_ROPE // 2
    inv = ROPE_THETA ** (-jnp.arange(0, half, 2, dtype=F32) / half)
    ar, ac = row[:, None] * inv, col[:, None] * inv
    ang = jnp.concatenate([ar, ar, ac, ac], axis=-1)
    pad = lambda a, fill: jnp.concatenate(
        [jnp.full((n, ROPE_LANE), fill, F32), a, jnp.full((n, HEAD_PAD - ROPE_LANE - QK_ROPE), fill, F32)], axis=1)
    return pad(jnp.cos(ang), 1.0), pad(jnp.sin(ang), 0.0)


def kernel(x_prompt, x_sample, cache_ckv, cache_krope, c, c_ctx, w_ada, b_ada, w_in, q_norm_g, w_uq,
           kv_norm_g, w_ukv, w_f_out, w_a_out, w_out, final_norm_g):
    assert w_in.shape[0] == 1
    b_ctx, n_ctx, _ = x_prompt.shape
    dec_b, n_lat, _ = x_sample.shape
    past = cache_ckv.shape[2]

    w_q3 = w_uq[0].reshape(Q_RANK, N_HEADS, QK_NOPE + QK_ROPE)
    head_pad = lambda a, left: jnp.pad(
        a, ((0, 0), (0, 0), (left, HEAD_PAD - left - a.shape[2]))).reshape(Q_RANK, N_HEADS * HEAD_PAD)
    rp = w_q3[:, :, QK_NOPE:].reshape(Q_RANK, N_HEADS, 2, 2, QK_ROPE // 4)
    w_q_rot = jnp.stack([-rp[:, :, :, 1], rp[:, :, :, 0]], axis=3).reshape(Q_RANK, N_HEADS, QK_ROPE)
    w_uq_p = jnp.concatenate([head_pad(w_q3, 0), head_pad(w_q_rot, ROPE_LANE)], axis=1).astype(BF16)
    w_kv3 = w_ukv[0].reshape(KV_RANK, N_HEADS, QK_NOPE + V_DIM)
    w_knope = jnp.pad(w_kv3[:, :, :QK_NOPE], ((0, 0), (0, 0), (0, HEAD_PAD - QK_NOPE)))
    place = np.zeros((HEAD_PAD, N_HEADS, HEAD_PAD), np.float32)
    for j in range(QK_ROPE):
        place[ROPE_LANE + j, :, ROPE_LANE + j] = 1.0
    w_k = jnp.concatenate([w_knope.reshape(KV_RANK, -1), jnp.asarray(place).reshape(HEAD_PAD, -1)],
                          axis=0).astype(BF16)
    w_uvt = jnp.pad(jnp.transpose(w_kv3[:, :, QK_NOPE:], (1, 2, 0)),
                    ((0, 0), (0, V_ROWS - V_DIM), (0, 0))).reshape(VT_ROWS, KV_RANK).astype(BF16)
    ones_col = np.zeros((N_HEADS, V_ROWS, LANES), np.float32)
    ones_col[:, V_DIM, :] = 1.0
    ones_col = jnp.asarray(ones_col.reshape(VT_ROWS, LANES))
    w_f = w_f_out[0].astype(BF16)
    w_a = w_a_out[0].astype(BF16)
    w_o = w_out[0].astype(BF16)
    qg = q_norm_g[0].reshape(1, Q_RANK)
    kvg = kv_norm_g[0].reshape(1, KV_RANK)
    fg = final_norm_g.reshape(1, D_MODEL)
    wts = (qg, kvg, w_uq_p, w_k, w_uvt, ones_col, _channel_dft_table())
    w_in_p = _pack_w_in(jnp.swapaxes(w_in[0], 0, 1))

    cond8 = jnp.concatenate([c_ctx[None, :], c, jnp.zeros((8 - 1 - dec_b, D_MODEL), F32)], axis=0)
    mod3 = _adaln(cond8, w_ada[0], b_ada[0].reshape(1, -1))

    xp = x_prompt.reshape(b_ctx * n_ctx, D_MODEL)
    ucs, q, k, vt, state_ckv, state_krope = _layer_in(xp, mod3, w_in_p, wts, None, n=n_ctx, state=True)
    f = _pos_dft(ucs, n=n_ctx)
    attn = _attention(q, [k], [vt], [n_ctx], n=n_ctx, after=f)
    y_prompt = _merge(xp, mod3, f, attn, w_in_p, w_f, w_a, w_o, fg, n=n_ctx, latent=False)

    xs = x_sample.reshape(dec_b * n_lat, D_MODEL)
    ucs, q, k, vt = _layer_in(xs, mod3, w_in_p, wts, _rope_tables(n_lat), n=n_lat, state=False)
    f = _pos_dft(ucs, n=n_lat)
    krope_pad = jnp.pad(cache_krope[:, 0], ((0, 0), (0, 0), (ROPE_LANE, HEAD_PAD - ROPE_LANE - QK_ROPE)))
    k_c, vt_c = _cache_kv(cache_ckv[:, 0].reshape(dec_b * past, KV_RANK),
                          krope_pad.reshape(dec_b * past, HEAD_PAD), w_k, w_uvt, ones_col)
    attn = _attention(q, [k, k_c], [vt, vt_c], [n_lat, past], n=n_lat, after=f)
    y_sample = _merge(xs, mod3, f, attn, w_in_p, w_f, w_a, w_o, fg, n=n_lat, latent=True)

    return (y_prompt.reshape(b_ctx, n_ctx, D_MODEL), y_sample.reshape(dec_b, n_lat, D_MODEL),
            state_ckv.reshape(b_ctx, 1, n_ctx, KV_RANK), state_krope.reshape(b_ctx, 1, n_ctx, QK_ROPE))
```

```python
import functools
import math

import jax
import jax.numpy as jnp
import numpy as np
from jax import lax
from jax.experimental import pallas as pl
from jax.experimental.pallas import tpu as pltpu

F32 = jnp.float32
BF16 = jnp.bfloat16

D_MODEL = 1024
GRID_W = 64
N_GROUPS = 4
GROUP = 128
D_FOURIER = N_GROUPS * GROUP
N_HEADS = 8
QK_NOPE = 64
QK_ROPE = 32
V_DIM = 64
Q_RANK = 256
KV_RANK = 128
D_ATTN = N_HEADS * V_DIM
D_IN = 2 * D_FOURIER + Q_RANK + KV_RANK + QK_ROPE + D_ATTN + 2 * D_MODEL
ROPE_THETA = 10000.0
EPS = 1e-6
LANES = 128
HEAD_PAD = LANES
ROPE_LANE = QK_NOPE
V_ROWS = V_DIM + 16
VT_ROWS = N_HEADS * V_ROWS
Q_SCALE = (QK_NOPE + QK_ROPE) ** -0.5 * math.log2(math.e)
KEY_CHUNK = 256
Q_TILE = 256
REV_BLOCK = 256
DFT_SPLIT = 64
MIN_DENOMINATOR = 2.0 ** -60
MAX_DENOMINATOR = 2.0 ** 60
W_BLK = 512
B_UF, B_ZF, B_MID, B_ZA = 0, 1, 2, 3
D_IN_PAD = 8 * W_BLK
TOKEN_TILE = 1024
LAYER_IN_SUB_TILE = TOKEN_TILE
MERGE_SUB_TILE = TOKEN_TILE // 2
VMEM_LIMIT_BYTES = 56 * 1024 * 1024


def _const_spec(shape):
    nd = len(shape)
    return pl.BlockSpec(shape, lambda *_: (0,) * nd, pipeline_mode=pl.Buffered(1))


def _row_spec(rows, blk):
    return pl.BlockSpec((rows, D_MODEL), lambda *_: (blk, 0), pipeline_mode=pl.Buffered(1))


def _params(n_axes):
    return pltpu.CompilerParams(dimension_semantics=("arbitrary",) * n_axes,
                                vmem_limit_bytes=VMEM_LIMIT_BYTES)


def _rms(x):
    return x * lax.rsqrt(jnp.mean(x * x, axis=-1, keepdims=True) + EPS)


def _sigmoid(x):
    return 0.5 * jnp.tanh(0.5 * x) + 0.5


def _dot(a, b):
    return jnp.dot(a, b, preferred_element_type=F32)


def _dot_nt(a, b):
    return lax.dot_general(a, b, (((1,), (1,)), ((), ())), preferred_element_type=F32)


def _token_tiling(t, n, *, per_sequence):
    tm = TOKEN_TILE
    assert t % tm == 0 and (not per_sequence or n % tm == 0)
    per_seq = n // tm if per_sequence else None
    mod_row = (lambda i: 1 + i // per_seq) if per_sequence else (lambda i: 0)
    return tm, per_seq, mod_row


def _sub_tiles(rows, step):
    return [slice(r, r + step) for r in range(0, rows, step)]


def _modulated(x, mod_ref):
    mod = mod_ref[0]
    shift, scale = mod[:, 0:D_MODEL], mod[:, D_MODEL:2 * D_MODEL]
    return (_rms(x) * (1.0 + scale) + shift).astype(BF16)


def _adaln_kernel(c_ref, w_ref, b_ref, o_ref):
    c = c_ref[...]
    s = (c * _sigmoid(c)).astype(BF16)
    mod = _dot(s, w_ref[...].astype(BF16)) + b_ref[...]
    for r in range(mod.shape[0]):
        o_ref[r] = mod[r:r + 1]


def _adaln(cond8, w_ada, b_ada):
    n_blk = 4
    bw = 3 * D_MODEL // n_blk
    return pl.pallas_call(
        _adaln_kernel,
        out_shape=jax.ShapeDtypeStruct((8, 1, 3 * D_MODEL), F32),
        grid=(n_blk,),
        in_specs=[pl.BlockSpec((8, D_MODEL), lambda j: (0, 0)),
                  pl.BlockSpec((D_MODEL, bw), lambda j: (0, j)),
                  pl.BlockSpec((1, bw), lambda j: (0, j))],
        out_specs=pl.BlockSpec((8, 1, bw), lambda j: (0, 0, j)),
        compiler_params=_params(1),
        name="adaln",
    )(cond8, w_ada, b_ada)


def _rot_rows(w):
    q = QK_ROPE // 4
    return jnp.concatenate([-w[q:2 * q], w[0:q], -w[3 * q:4 * q], w[2 * q:3 * q]], axis=0)


def _pack_w_in_kernel(wt_ref, o_ref):
    lo = 2 * D_FOURIER + Q_RANK + KV_RANK
    o_ref[:lo] = wt_ref[:lo].astype(BF16)
    kr = wt_ref[lo:lo + QK_ROPE]
    zeros = jnp.zeros((QK_ROPE, D_MODEL), F32)
    o_ref[lo:lo + LANES] = jnp.concatenate([_rot_rows(kr), zeros, kr, zeros], axis=0).astype(BF16)
    o_ref[lo + LANES:] = wt_ref[lo + QK_ROPE:].astype(BF16)


def _pack_w_in(w_in_t):
    return pl.pallas_call(
        _pack_w_in_kernel,
        out_shape=jax.ShapeDtypeStruct((D_IN_PAD, D_MODEL), BF16),
        grid=(1,),
        in_specs=[_const_spec((D_IN, D_MODEL))],
        out_specs=_const_spec((D_IN_PAD, D_MODEL)),
        compiler_params=_params(1),
        name="pack_w_in",
    )(w_in_t)


def _kv_outputs(ckvn_b, kr_b, w_k_ref, w_uvt_ref, ones_ref, k_ref, vt_ref, rows):
    tm = ckvn_b.shape[0]
    kp = _dot(jnp.concatenate([ckvn_b, kr_b], axis=1), w_k_ref[...])
    for hh in range(N_HEADS):
        k_ref[hh, rows] = kp[:, hh * HEAD_PAD:(hh + 1) * HEAD_PAD].astype(BF16)
    ones = jnp.concatenate([ones_ref[...]] * (tm // LANES), axis=1)
    vt_ref[:, rows] = (_dot_nt(w_uvt_ref[...], ckvn_b) + ones).astype(BF16)


def _layer_in_kernel(*refs, rope, state):
    (x_ref, mod_ref, w_uf_ref, w_mid_ref, qg_ref, kvg_ref, w_uq_ref, w_k_ref, w_uvt_ref, ones_ref,
     cs_ref), refs = refs[:11], refs[11:]
    if rope:
        (cos_ref, sin_ref), refs = refs[:2], refs[2:]
    (ucs_ref, q_ref, k_ref, vt_ref), refs = refs[:4], refs[4:]
    if state:
        ckvn_ref, krope_ref = refs

    for rows in _sub_tiles(x_ref.shape[0], LAYER_IN_SUB_TILE):
        h = _modulated(x_ref[rows], mod_ref)

        mid = _dot_nt(h, w_mid_ref[...])
        cq, ckv, kr = mid[:, :Q_RANK], mid[:, Q_RANK:Q_RANK + KV_RANK], mid[:, Q_RANK + KV_RANK:]
        cqn = (_rms(cq) * (qg_ref[...] * Q_SCALE)).astype(BF16)
        ckvn = _rms(ckv) * kvg_ref[...]
        if state:
            ckvn_ref[rows] = ckvn
            n = krope_ref.shape[2]
            kr_t = kr.T[ROPE_LANE:ROPE_LANE + QK_ROPE]
            for s in range(kr.shape[0] // n):
                krope_ref[rows.start // n + s] = kr_t[:, s * n:(s + 1) * n]
        n_q = N_HEADS * HEAD_PAD
        if rope:
            cos, sin = cos_ref[rows], sin_ref[rows]
            kr = kr * cos + pltpu.roll(kr, ROPE_LANE, 1) * sin
            qp = _dot(cqn, w_uq_ref[...])
        else:
            qp = _dot(cqn, w_uq_ref[:, :n_q])
        for hh in range(N_HEADS):
            qh = qp[:, hh * HEAD_PAD:(hh + 1) * HEAD_PAD]
            if rope:
                qh = qh * cos + qp[:, n_q + hh * HEAD_PAD:n_q + (hh + 1) * HEAD_PAD] * sin
            q_ref[hh, rows] = qh.astype(BF16)
        _kv_outputs(ckvn.astype(BF16), kr.astype(BF16), w_k_ref, w_uvt_ref, ones_ref, k_ref, vt_ref, rows)

        u = _dot_nt(h, w_uf_ref[...]).astype(BF16)
        for g in range(N_GROUPS):
            r = _dot(u[:, g * GROUP:(g + 1) * GROUP], cs_ref[...])
            ucs_ref[rows, g * GROUP:(g + 1) * GROUP] = r[:, :GROUP].astype(BF16)
            ucs_ref[rows, D_FOURIER + g * GROUP:D_FOURIER + (g + 1) * GROUP] = r[:, GROUP:].astype(BF16)


def _layer_in(x, mod3, w_in_p, wts, rope_tabs, *, n, state):
    t = x.shape[0]
    rope = rope_tabs is not None
    tm, per_seq, mod_row = _token_tiling(t, n, per_sequence=rope)
    tok = lambda w: pl.BlockSpec((tm, w), lambda i: (i, 0))
    in_specs = [tok(D_MODEL),
                pl.BlockSpec((1, 1, 3 * D_MODEL), lambda i: (mod_row(i), 0, 0)),
                _row_spec(W_BLK, B_UF), _row_spec(W_BLK, B_MID)]
    in_specs += [_const_spec(w.shape) for w in wts]
    args = [x, mod3, w_in_p, w_in_p, *wts]
    if rope:
        in_specs += [pl.BlockSpec((tm, HEAD_PAD), lambda i: (i % per_seq, 0))] * 2
        args += list(rope_tabs)
    head = pl.BlockSpec((N_HEADS, tm, HEAD_PAD), lambda i: (0, i, 0))
    out_specs = [tok(2 * D_FOURIER), head, head, pl.BlockSpec((VT_ROWS, tm), lambda i: (0, i))]
    out_shape = [jax.ShapeDtypeStruct((t, 2 * D_FOURIER), BF16),
                 jax.ShapeDtypeStruct((N_HEADS, t, HEAD_PAD), BF16),
                 jax.ShapeDtypeStruct((N_HEADS, t, HEAD_PAD), BF16),
                 jax.ShapeDtypeStruct((VT_ROWS, t), BF16)]
    if state:
        out_specs += [tok(KV_RANK), pl.BlockSpec((tm // n, QK_ROPE, n), lambda i: (i, 0, 0))]
        out_shape += [jax.ShapeDtypeStruct((t, KV_RANK), F32), jax.ShapeDtypeStruct((t // n, QK_ROPE, n), F32)]
    return pl.pallas_call(
        functools.partial(_layer_in_kernel, rope=rope, state=state),
        out_shape=out_shape,
        grid=(t // tm,),
        in_specs=in_specs,
        out_specs=out_specs,
        compiler_params=_params(1),
        name="layer_in_latent" if rope else "layer_in_context",
    )(*args)


def _cache_kv_kernel(ckv_ref, krt_ref, w_k_ref, w_uvt_ref, ones_ref, k_ref, vt_ref):
    seqs, _, m = krt_ref.shape
    above = jnp.zeros((ROPE_LANE, m), F32)
    below = jnp.zeros((HEAD_PAD - ROPE_LANE - QK_ROPE, m), F32)
    kr = jnp.concatenate([jnp.concatenate([above, krt_ref[s], below], axis=0).T for s in range(seqs)], axis=0)
    _kv_outputs(ckv_ref[...].astype(BF16), kr.astype(BF16), w_k_ref, w_uvt_ref, ones_ref,
                k_ref, vt_ref, slice(None))


def _cache_kv(ckv, krope_t, w_k, w_uvt, ones_col):
    t = ckv.shape[0]
    out_shape = [jax.ShapeDtypeStruct((N_HEADS, t, HEAD_PAD), BF16), jax.ShapeDtypeStruct((VT_ROWS, t), BF16)]
    return pl.pallas_call(
        _cache_kv_kernel,
        out_shape=out_shape,
        grid=(1,),
        in_specs=[_const_spec(a.shape) for a in (ckv, krope_t, w_k, w_uvt, ones_col)],
        out_specs=[_const_spec(s.shape) for s in out_shape],
        compiler_params=_params(1),
        name="cache_kv",
    )(ckv, krope_t, w_k, w_uvt, ones_col)


def _pos_dft_direct_kernel(cn_ref, sn_ref, ucs_ref, f_ref):
    n = cn_ref.shape[0]
    for s in range(f_ref.shape[0] // n):
        rows = slice(s * n, (s + 1) * n)
        f = _dot(cn_ref[...], ucs_ref[rows, :D_FOURIER]) + _dot(sn_ref[...], ucs_ref[rows, D_FOURIER:])
        f_ref[rows] = f.astype(BF16)


def _pos_dft_half_kernel(c1_ref, s1_ref, c0_ref, s0_ref, rev_ref, rev1_ref, ucs_ref, f_ref, cn_ref, sn_ref):
    n = f_ref.shape[0]
    half = n // 2
    blocks = half // REV_BLOCK

    @pl.when(pl.program_id(0) == 0)
    def _():
        c0, s0 = c0_ref[...], s0_ref[...]
        for k1 in range(c1_ref.shape[0]):
            rows = slice(k1 * DFT_SPLIT, min((k1 + 1) * DFT_SPLIT, cn_ref.shape[0]))
            m = rows.stop - rows.start
            c1, s1 = c1_ref[k1:k1 + 1, :], s1_ref[k1:k1 + 1, :]
            cn_ref[rows] = (c1 * c0[:m] - s1 * s0[:m]).astype(BF16)
            sn_ref[rows] = (-(s1 * c0[:m] + c1 * s0[:m])).astype(BF16)

    def folded(cols, sign):
        first_row = lax.broadcasted_iota(jnp.int32, (REV_BLOCK, D_FOURIER), 0) == 0
        out = []
        for b in range(blocks):
            m = _dot(rev1_ref[...], ucs_ref[n - (b + 1) * REV_BLOCK:n - b * REV_BLOCK, cols])
            if b > 0:
                m = jnp.where(first_row, ucs_ref[n - b * REV_BLOCK:n - b * REV_BLOCK + 1, cols].astype(F32), m)
            out.append((ucs_ref[b * REV_BLOCK:(b + 1) * REV_BLOCK, cols] + sign * m).astype(BF16))
        return jnp.concatenate(out, axis=0)

    e = folded(slice(0, D_FOURIER), 1.0)
    o = folded(slice(D_FOURIER, 2 * D_FOURIER), -1.0)
    odd_k = (lax.broadcasted_iota(jnp.int32, (cn_ref.shape[0], D_FOURIER), 0) & 1) == 1
    mid = ucs_ref[half:half + 1, :D_FOURIER].astype(F32) * (1.0 / math.sqrt(n))
    g = _dot(cn_ref[...], e) + jnp.where(odd_k, -mid, mid)
    hn = _dot(sn_ref[...], o)
    f_ref[:half] = (g[:half] + hn[:half]).astype(BF16)
    mirrored = (g[1:half + 1] - hn[1:half + 1]).astype(BF16)
    for a in range(blocks):
        blk = mirrored[(blocks - 1 - a) * REV_BLOCK:(blocks - a) * REV_BLOCK]
        f_ref[half + a * REV_BLOCK:half + (a + 1) * REV_BLOCK] = _dot(rev_ref[...], blk).astype(BF16)


def _pos_dft(ucs, *, n):
    t = ucs.shape[0]
    out_shape = jax.ShapeDtypeStruct((t, D_FOURIER), BF16)
    if n <= REV_BLOCK:
        cn, snn = _dft_tables(n)
        seqs = min(8, t // n)
        return pl.pallas_call(
            _pos_dft_direct_kernel,
            out_shape=out_shape,
            grid=(t // (n * seqs),),
            in_specs=[_const_spec(cn.shape), _const_spec(snn.shape),
                      pl.BlockSpec((n * seqs, 2 * D_FOURIER), lambda i: (i, 0))],
            out_specs=pl.BlockSpec((n * seqs, D_FOURIER), lambda i: (i, 0)),
            compiler_params=_params(1),
            name=f"pos_dft_{n}",
        )(cn, snn, ucs)
    rows = n // 2 + 8
    consts = _dft_factors(n, rows, n // 2)
    anti = np.eye(REV_BLOCK, dtype=np.float32)[::-1]
    shifted = np.roll(anti, 1, axis=1)
    shifted[0] = 0.0
    consts += (jnp.asarray(anti, dtype=BF16), jnp.asarray(shifted, dtype=BF16))
    return pl.pallas_call(
        _pos_dft_half_kernel,
        out_shape=out_shape,
        grid=(t // n,),
        in_specs=[_const_spec(a.shape) for a in consts]
        + [pl.BlockSpec((n, 2 * D_FOURIER), lambda i: (i, 0))],
        out_specs=pl.BlockSpec((n, D_FOURIER), lambda i: (i, 0)),
        scratch_shapes=[pltpu.VMEM((rows, n // 2), BF16), pltpu.VMEM((rows, n // 2), BF16)],
        compiler_params=_params(1),
        name=f"pos_dft_{n}",
    )(*consts, ucs)


def _attention_kernel(*refs, n_kv, single_pass, units):
    q_ref, refs = refs[0], refs[1:]
    k_refs, vt_refs, refs = refs[:n_kv], refs[n_kv:2 * n_kv], refs[2 * n_kv:]
    o_ref, s_ref, pp_ref, ot_ref = refs[-4:]
    tq = q_ref.shape[1] // units
    key_lens = [k_ref.shape[1] // (1 if single_pass else units) for k_ref in k_refs]
    chunks, base = [], 0
    for j, m in enumerate(key_lens):
        chunks += [(j, c, base + c) for c in range(0, m, KEY_CHUNK)]
        base += m

    def q_block(hh, u):
        return q_ref[hh, u * tq:(u + 1) * tq, :]

    def finish(hh, ot, u):
        l = ot[V_DIM:V_DIM + 1]
        ot_ref[u, pl.ds(pl.multiple_of(hh * V_DIM, V_DIM), V_DIM), :] = ot[:V_DIM] / l
        return l

    def first_chunk_scores(hh, u):
        j, c, _ = chunks[0]
        s = _dot_nt(k_refs[j][hh, c:c + KEY_CHUNK, :], q_block(hh, u))
        s_ref[u, hh * KEY_CHUNK:(hh + 1) * KEY_CHUNK, :] = s
        return jnp.max(s, axis=0, keepdims=True)

    def shifted_softmax_pv(score_head, pv_head, shifts, u):
        if score_head is not None:
            qh = q_block(score_head, u)
            shift = shifts[score_head]
        ot = None
        for idx, (j, c, r) in enumerate(chunks):
            if score_head is not None:
                s = (s_ref[u, score_head * KEY_CHUNK:(score_head + 1) * KEY_CHUNK, :] if idx == 0 else
                     _dot_nt(k_refs[j][score_head, c:c + KEY_CHUNK, :], qh))
                pp_ref[2 * u + score_head % 2, r:r + KEY_CHUNK, :] = jnp.exp2(s - shift).astype(BF16)
            if pv_head is not None:
                part = _dot(vt_refs[j][pv_head * V_ROWS:(pv_head + 1) * V_ROWS, c:c + KEY_CHUNK],
                            pp_ref[2 * u + pv_head % 2, r:r + KEY_CHUNK, :])
                ot = part if ot is None else ot + part
        return finish(pv_head, ot, u) if pv_head is not None else None

    def exact_scores(hh, slot, u, kseq):
        qh = q_block(hh, u)
        m8 = None
        for j, c, r in chunks:
            k0 = kseq * key_lens[j] + c
            s = _dot_nt(k_refs[j][hh, k0:k0 + KEY_CHUNK, :], qh)
            s_ref[slot, r:r + KEY_CHUNK, :] = s
            mc = jnp.max(s.reshape(KEY_CHUNK // 8, 8, tq), axis=0)
            m8 = mc if m8 is None else jnp.maximum(m8, mc)
        return jnp.max(m8, axis=0, keepdims=True)

    def exact_softmax_pv(hh, s_slot, p_slot, m, u, kseq):
        for _, _, r in chunks:
            pp_ref[p_slot, r:r + KEY_CHUNK, :] = jnp.exp2(s_ref[s_slot, r:r + KEY_CHUNK, :] - m).astype(BF16)
        rows = pl.ds(pl.multiple_of(hh * V_ROWS, V_ROWS), V_ROWS)
        ot, base = None, 0
        for j, vt_ref in enumerate(vt_refs):
            mk = key_lens[j]
            part = _dot(vt_ref[rows, kseq * mk:(kseq + 1) * mk], pp_ref[p_slot, base:base + mk, :])
            ot = part if ot is None else ot + part
            base += mk
        finish(hh, ot, u)

    if not single_pass:
        work = [(u, hh) for u in range(units) for hh in range(N_HEADS)]
        maxima = [exact_scores(hh, slot, u, u) for slot, (u, hh) in enumerate(work)]
        for slot, (u, hh) in enumerate(work):
            exact_softmax_pv(hh, slot, slot, maxima[slot], u, u)
    else:
        trusted = []
        for u in range(units):
            l_min = l_max = None
            shifts = [first_chunk_scores(hh, u) for hh in range(N_HEADS)]
            for hh in range(N_HEADS + 1):
                l = shifted_softmax_pv(hh if hh < N_HEADS else None, hh - 1 if hh > 0 else None, shifts, u)
                if l is not None:
                    l_min = l if l_min is None else jnp.minimum(l_min, l)
                    l_max = l if l_max is None else jnp.maximum(l_max, l)
            trusted.append(jnp.logical_and(jnp.min(l_min) >= MIN_DENOMINATOR,
                                           jnp.max(l_max) <= MAX_DENOMINATOR))

        for u in range(units):
            @pl.when(jnp.logical_not(trusted[u]))
            def _(u=u):
                def body(hh, carry):
                    exact_softmax_pv(hh, u, 2 * u, exact_scores(hh, u, u, 0), u, 0)
                    return carry

                lax.fori_loop(0, N_HEADS, body, 0)

    for u in range(units):
        o_ref[u * tq:(u + 1) * tq, :] = ot_ref[u].T.astype(BF16)


def _attention(q, ks, vts, key_lens, *, n, after=None):
    t = q.shape[1]
    tq = Q_TILE
    m_tot = sum(key_lens)
    single_pass = m_tot > 2 * KEY_CHUNK
    if single_pass:
        units = 4
        steps = n // (units * tq)
        kv_rows = 1
        s_slots, p_slots = units, 2 * units
    else:
        assert n == tq
        units = min(4, t // n)
        steps = 1
        kv_rows = units
        s_slots = p_slots = units * N_HEADS
    in_specs = [pl.BlockSpec((N_HEADS, units * tq, HEAD_PAD), lambda bi, i: (0, bi * steps + i, 0))]
    in_specs += [pl.BlockSpec((N_HEADS, kv_rows * m, HEAD_PAD), lambda bi, i: (0, bi, 0)) for m in key_lens]
    in_specs += [pl.BlockSpec((VT_ROWS, kv_rows * m), lambda bi, i: (0, bi)) for m in key_lens]
    args = [q, *ks, *vts]
    if after is not None:
        in_specs.append(pl.BlockSpec((8, after.shape[1]), lambda bi, i: (0, 0)))
        args.append(after)
    return pl.pallas_call(
        functools.partial(_attention_kernel, n_kv=len(ks), single_pass=single_pass, units=units),
        out_shape=jax.ShapeDtypeStruct((t, D_ATTN), BF16),
        grid=(t // (n * kv_rows), steps),
        in_specs=in_specs,
        out_specs=pl.BlockSpec((units * tq, D_ATTN), lambda bi, i: (bi * steps + i, 0)),
        scratch_shapes=[pltpu.VMEM((s_slots, m_tot, tq), F32), pltpu.VMEM((p_slots, m_tot, tq), BF16),
                        pltpu.VMEM((units, D_ATTN, tq), F32)],
        compiler_params=_params(2),
        name=f"attention_{n}",
    )(*args)


def _merge_kernel(x_ref, mod_ref, f_ref, a_ref, w_zf_ref, w_za_ref, w_g_ref, w_f_ref, w_a_ref, w_o_ref,
                  g_ref, y_ref):
    gate = mod_ref[0][:, 2 * D_MODEL:]
    for rows in _sub_tiles(x_ref.shape[0], MERGE_SUB_TILE):
        x = x_ref[rows]
        h = _modulated(x, mod_ref)
        zf = _dot_nt(h, w_zf_ref[...])
        y_f = _dot((f_ref[rows] * (zf * _sigmoid(zf))).astype(BF16), w_f_ref[...])
        za = _dot_nt(h, w_za_ref[...])
        y_a = _dot((a_ref[rows] * (za * _sigmoid(za))).astype(BF16), w_a_ref[...])
        merged = (_sigmoid(_dot_nt(h, w_g_ref[:D_MODEL])) * y_f
                  + _sigmoid(_dot_nt(h, w_g_ref[D_MODEL:])) * y_a)
        out = x + gate * _dot(merged.astype(BF16), w_o_ref[...])
        y_ref[rows] = _rms(out) * g_ref[...]


def _merge(x, mod3, f, attn, w_in_p, w_f, w_a, w_o, g, *, n, latent):
    t = x.shape[0]
    tm, _, mod_row = _token_tiling(t, n, per_sequence=latent)
    tok = lambda w: pl.BlockSpec((tm, w), lambda i: (i, 0))
    return pl.pallas_call(
        _merge_kernel,
        out_shape=jax.ShapeDtypeStruct((t, D_MODEL), F32),
        grid=(t // tm,),
        in_specs=[tok(D_MODEL),
                  pl.BlockSpec((1, 1, 3 * D_MODEL), lambda i: (mod_row(i), 0, 0)),
                  tok(D_FOURIER), tok(D_ATTN),
                  _row_spec(W_BLK, B_ZF), _row_spec(W_BLK, B_ZA), _row_spec(2 * D_MODEL, 1),
                  _const_spec(w_f.shape), _const_spec(w_a.shape), _const_spec(w_o.shape),
                  _const_spec(g.shape)],
        out_specs=tok(D_MODEL),
        compiler_params=_params(1),
        name=f"merge_{n}",
    )(x, mod3, f, attn, w_in_p, w_in_p, w_in_p, w_f, w_a, w_o, g)


def _dft_tables(n):
    norm = 1.0 / math.sqrt(n)
    k = jnp.arange(n, dtype=jnp.int32)
    ang = ((k[:, None] * k[None, :]) % n).astype(F32) * (2.0 * math.pi / n)
    return (jnp.cos(ang) * norm).astype(BF16), (jnp.sin(ang) * -norm).astype(BF16)


def _dft_factors(n, rows, n_pos):
    norm = 1.0 / math.sqrt(n)
    period = n // DFT_SPLIT
    pos = jnp.arange(n_pos, dtype=jnp.int32)
    k1 = jnp.arange(-(-rows // DFT_SPLIT), dtype=jnp.int32)
    k0 = jnp.arange(DFT_SPLIT, dtype=jnp.int32)
    a1 = ((k1[:, None] * pos[None, :]) % period).astype(F32) * (2.0 * math.pi / period)
    a0 = ((k0[:, None] * pos[None, :]) % n).astype(F32) * (2.0 * math.pi / n)
    return jnp.cos(a1), jnp.sin(a1), jnp.cos(a0) * norm, jnp.sin(a0) * norm


def _channel_dft_table():
    c = jnp.arange(GROUP, dtype=jnp.int32)
    ang = ((c[:, None] * c[None, :]) % GROUP).astype(F32) * (2.0 * math.pi / GROUP)
    norm = 1.0 / math.sqrt(GROUP)
    return (jnp.concatenate([jnp.cos(ang), jnp.sin(ang)], axis=1) * norm).astype(BF16)


def _rope_tables(n):
    t = jnp.arange(n)
    row = (t // GRID_W).astype(F32)
    col = (t % GRID_W).astype(F32)
    half = QK_ROPE // 2
    inv = ROPE_THETA ** (-jnp.arange(0, half, 2, dtype=F32) / half)
    ar, ac = row[:, None] * inv, col[:, None] * inv
    ang = jnp.concatenate([ar, ar, ac, ac], axis=-1)
    pad = lambda a, fill: jnp.concatenate(
        [jnp.full((n, ROPE_LANE), fill, F32), a, jnp.full((n, HEAD_PAD - ROPE_LANE - QK_ROPE), fill, F32)], axis=1)
    return pad(jnp.cos(ang), 1.0), pad(jnp.sin(ang), 0.0)


def kernel(x_prompt, x_sample, cache_ckv, cache_krope, c, c_ctx, w_ada, b_ada, w_in, q_norm_g, w_uq,
           kv_norm_g, w_ukv, w_f_out, w_a_out, w_out, final_norm_g):
    assert w_in.shape[0] == 1
    b_ctx, n_ctx, _ = x_prompt.shape
    dec_b, n_lat, _ = x_sample.shape
    past = cache_ckv.shape[2]

    w_q3 = w_uq[0].reshape(Q_RANK, N_HEADS, QK_NOPE + QK_ROPE)
    head_pad = lambda a, left: jnp.pad(
        a, ((0, 0), (0, 0), (left, HEAD_PAD - left - a.shape[2]))).reshape(Q_RANK, N_HEADS * HEAD_PAD)
    rp = w_q3[:, :, QK_NOPE:].reshape(Q_RANK, N_HEADS, 2, 2, QK_ROPE // 4)
    w_q_rot = jnp.stack([-rp[:, :, :, 1], rp[:, :, :, 0]], axis=3).reshape(Q_RANK, N_HEADS, QK_ROPE)
    w_uq_p = jnp.concatenate([head_pad(w_q3, 0), head_pad(w_q_rot, ROPE_LANE)], axis=1).astype(BF16)
    w_kv3 = w_ukv[0].reshape(KV_RANK, N_HEADS, QK_NOPE + V_DIM)
    w_knope = jnp.pad(w_kv3[:, :, :QK_NOPE], ((0, 0), (0, 0), (0, HEAD_PAD - QK_NOPE)))
    place = np.zeros((HEAD_PAD, N_HEADS, HEAD_PAD), np.float32)
    for j in range(QK_ROPE):
        place[ROPE_LANE + j, :, ROPE_LANE + j] = 1.0
    w_k = jnp.concatenate([w_knope.reshape(KV_RANK, -1), jnp.asarray(place).reshape(HEAD_PAD, -1)],
                          axis=0).astype(BF16)
    w_uvt = jnp.pad(jnp.transpose(w_kv3[:, :, QK_NOPE:], (1, 2, 0)),
                    ((0, 0), (0, V_ROWS - V_DIM), (0, 0))).reshape(VT_ROWS, KV_RANK).astype(BF16)
    ones_col = np.zeros((N_HEADS, V_ROWS, LANES), np.float32)
    ones_col[:, V_DIM, :] = 1.0
    ones_col = jnp.asarray(ones_col.reshape(VT_ROWS, LANES))
    w_f = w_f_out[0].astype(BF16)
    w_a = w_a_out[0].astype(BF16)
    w_o = w_out[0].astype(BF16)
    qg = q_norm_g[0].reshape(1, Q_RANK)
    kvg = kv_norm_g[0].reshape(1, KV_RANK)
    fg = final_norm_g.reshape(1, D_MODEL)
    wts = (qg, kvg, w_uq_p, w_k, w_uvt, ones_col, _channel_dft_table())
    w_in_p = _pack_w_in(jnp.swapaxes(w_in[0], 0, 1))

    cond8 = jnp.concatenate([c_ctx[None, :], c, jnp.zeros((8 - 1 - dec_b, D_MODEL), F32)], axis=0)
    mod3 = _adaln(cond8, w_ada[0], b_ada[0].reshape(1, -1))

    xp = x_prompt.reshape(b_ctx * n_ctx, D_MODEL)
    ucs, q, k, vt, state_ckv, state_krope = _layer_in(xp, mod3, w_in_p, wts, None, n=n_ctx, state=True)
    f = _pos_dft(ucs, n=n_ctx)
    attn = _attention(q, [k], [vt], [n_ctx], n=n_ctx, after=f)
    y_prompt = _merge(xp, mod3, f, attn, w_in_p, w_f, w_a, w_o, fg, n=n_ctx, latent=False)

    xs = x_sample.reshape(dec_b * n_lat, D_MODEL)
    ucs, q, k, vt = _layer_in(xs, mod3, w_in_p, wts, _rope_tables(n_lat), n=n_lat, state=False)
    f = _pos_dft(ucs, n=n_lat)
    k_c, vt_c = _cache_kv(cache_ckv[:, 0].reshape(dec_b * past, KV_RANK),
                          jnp.swapaxes(cache_krope[:, 0], 1, 2), w_k, w_uvt, ones_col)
    attn = _attention(q, [k, k_c], [vt, vt_c], [n_lat, past], n=n_lat, after=f)
    y_sample = _merge(xs, mod3, f, attn, w_in_p, w_f, w_a, w_o, fg, n=n_lat, latent=True)

    return (y_prompt.reshape(b_ctx, n_ctx, D_MODEL), y_sample.reshape(dec_b, n_lat, D_MODEL),
            state_ckv.reshape(b_ctx, 1, n_ctx, KV_RANK), jnp.swapaxes(state_krope, 1, 2)[:, None])
```

```python
import functools
import math

import jax
import jax.numpy as jnp
import numpy as np
from jax import lax
from jax.experimental import pallas as pl
from jax.experimental.pallas import tpu as pltpu

F32 = jnp.float32
BF16 = jnp.bfloat16

D_MODEL = 1024
GRID_W = 64
N_GROUPS = 4
GROUP = 128
D_FOURIER = N_GROUPS * GROUP
N_HEADS = 8
QK_NOPE = 64
QK_ROPE = 32
V_DIM = 64
Q_RANK = 256
KV_RANK = 128
D_ATTN = N_HEADS * V_DIM
D_IN = 2 * D_FOURIER + Q_RANK + KV_RANK + QK_ROPE + D_ATTN + 2 * D_MODEL
ROPE_THETA = 10000.0
EPS = 1e-6
LANES = 128
HEAD_PAD = LANES
ROPE_LANE = QK_NOPE
V_ROWS = V_DIM + 16
VT_ROWS = N_HEADS * V_ROWS
Q_SCALE = (QK_NOPE + QK_ROPE) ** -0.5 * math.log2(math.e)
KEY_CHUNK = 256
Q_TILE = 256
REV_BLOCK = 256
DFT_SPLIT = 64
MIN_DENOMINATOR = 2.0 ** -60
MAX_DENOMINATOR = 2.0 ** 60
W_BLK = 512
B_UF, B_ZF, B_MID, B_ZA = 0, 1, 2, 3
D_IN_PAD = 8 * W_BLK
TOKEN_TILE = 1024
LAYER_IN_SUB_TILE = TOKEN_TILE
MERGE_SUB_TILE = TOKEN_TILE // 2
VMEM_LIMIT_BYTES = 56 * 1024 * 1024


def _const_spec(shape):
    nd = len(shape)
    return pl.BlockSpec(shape, lambda *_: (0,) * nd, pipeline_mode=pl.Buffered(1))


def _row_spec(rows, blk):
    return pl.BlockSpec((rows, D_MODEL), lambda *_: (blk, 0), pipeline_mode=pl.Buffered(1))


def _params(n_axes):
    return pltpu.CompilerParams(dimension_semantics=("arbitrary",) * n_axes,
                                vmem_limit_bytes=VMEM_LIMIT_BYTES)


def _rms(x):
    return x * lax.rsqrt(jnp.mean(x * x, axis=-1, keepdims=True) + EPS)


def _sigmoid(x):
    return 0.5 * jnp.tanh(0.5 * x) + 0.5


def _dot(a, b):
    return jnp.dot(a, b, preferred_element_type=F32)


def _dot_nt(a, b):
    return lax.dot_general(a, b, (((1,), (1,)), ((), ())), preferred_element_type=F32)


def _token_tiling(t, n, *, per_sequence):
    tm = TOKEN_TILE
    assert t % tm == 0 and (not per_sequence or n % tm == 0)
    per_seq = n // tm if per_sequence else None
    mod_row = (lambda i: 1 + i // per_seq) if per_sequence else (lambda i: 0)
    return tm, per_seq, mod_row


def _sub_tiles(rows, step):
    return [slice(r, r + step) for r in range(0, rows, step)]


def _modulated(x, mod_ref):
    mod = mod_ref[0]
    shift, scale = mod[:, 0:D_MODEL], mod[:, D_MODEL:2 * D_MODEL]
    return (_rms(x) * (1.0 + scale) + shift).astype(BF16)


def _adaln_kernel(c_ref, w_ref, b_ref, o_ref):
    c = c_ref[...]
    s = (c * _sigmoid(c)).astype(BF16)
    mod = _dot(s, w_ref[...].astype(BF16)) + b_ref[...]
    for r in range(mod.shape[0]):
        o_ref[r] = mod[r:r + 1]


def _adaln(cond8, w_ada, b_ada):
    n_blk = 4
    bw = 3 * D_MODEL // n_blk
    return pl.pallas_call(
        _adaln_kernel,
        out_shape=jax.ShapeDtypeStruct((8, 1, 3 * D_MODEL), F32),
        grid=(n_blk,),
        in_specs=[pl.BlockSpec((8, D_MODEL), lambda j: (0, 0)),
                  pl.BlockSpec((D_MODEL, bw), lambda j: (0, j)),
                  pl.BlockSpec((1, bw), lambda j: (0, j))],
        out_specs=pl.BlockSpec((8, 1, bw), lambda j: (0, 0, j)),
        compiler_params=_params(1),
        name="adaln",
    )(cond8, w_ada, b_ada)


def _rot_rows(w):
    q = QK_ROPE // 4
    return jnp.concatenate([-w[q:2 * q], w[0:q], -w[3 * q:4 * q], w[2 * q:3 * q]], axis=0)


def _pack_w_in_kernel(wt_ref, o_ref):
    lo = 2 * D_FOURIER + Q_RANK + KV_RANK
    o_ref[:lo] = wt_ref[:lo].astype(BF16)
    kr = wt_ref[lo:lo + QK_ROPE]
    zeros = jnp.zeros((QK_ROPE, D_MODEL), F32)
    o_ref[lo:lo + LANES] = jnp.concatenate([_rot_rows(kr), zeros, kr, zeros], axis=0).astype(BF16)
    o_ref[lo + LANES:] = wt_ref[lo + QK_ROPE:].astype(BF16)


def _pack_w_in(w_in_t):
    return pl.pallas_call(
        _pack_w_in_kernel,
        out_shape=jax.ShapeDtypeStruct((D_IN_PAD, D_MODEL), BF16),
        grid=(1,),
        in_specs=[_const_spec((D_IN, D_MODEL))],
        out_specs=_const_spec((D_IN_PAD, D_MODEL)),
        compiler_params=_params(1),
        name="pack_w_in",
    )(w_in_t)


def _kv_outputs(ckvn_b, kr_b, w_k_ref, w_uvt_ref, ones_ref, k_ref, vt_ref, rows):
    tm = ckvn_b.shape[0]
    kp = _dot(jnp.concatenate([ckvn_b, kr_b], axis=1), w_k_ref[...])
    for hh in range(N_HEADS):
        k_ref[hh, rows] = kp[:, hh * HEAD_PAD:(hh + 1) * HEAD_PAD].astype(BF16)
    ones = jnp.concatenate([ones_ref[...]] * (tm // LANES), axis=1)
    vt_ref[:, rows] = (_dot_nt(w_uvt_ref[...], ckvn_b) + ones).astype(BF16)


def _layer_in_kernel(*refs, rope, state):
    (x_ref, mod_ref, w_uf_ref, w_mid_ref, qg_ref, kvg_ref, w_uq_ref, w_k_ref, w_uvt_ref, ones_ref,
     cs_ref), refs = refs[:11], refs[11:]
    if rope:
        (cos_ref, sin_ref), refs = refs[:2], refs[2:]
    (ucs_ref, q_ref, k_ref, vt_ref), refs = refs[:4], refs[4:]
    if state:
        ckvn_ref, krope_ref = refs

    for rows in _sub_tiles(x_ref.shape[0], LAYER_IN_SUB_TILE):
        h = _modulated(x_ref[rows], mod_ref)

        mid = _dot_nt(h, w_mid_ref[...])
        cq, ckv, kr = mid[:, :Q_RANK], mid[:, Q_RANK:Q_RANK + KV_RANK], mid[:, Q_RANK + KV_RANK:]
        cqn = (_rms(cq) * (qg_ref[...] * Q_SCALE)).astype(BF16)
        ckvn = _rms(ckv) * kvg_ref[...]
        if state:
            ckvn_ref[rows] = ckvn
            n = krope_ref.shape[2]
            kr_t = kr.T[ROPE_LANE:ROPE_LANE + QK_ROPE]
            for s in range(kr.shape[0] // n):
                krope_ref[rows.start // n + s] = kr_t[:, s * n:(s + 1) * n]
        n_q = N_HEADS * HEAD_PAD
        if rope:
            cos, sin = cos_ref[rows], sin_ref[rows]
            kr = kr * cos + pltpu.roll(kr, ROPE_LANE, 1) * sin
            qp = _dot(cqn, w_uq_ref[...])
        else:
            qp = _dot(cqn, w_uq_ref[:, :n_q])
        for hh in range(N_HEADS):
            qh = qp[:, hh * HEAD_PAD:(hh + 1) * HEAD_PAD]
            if rope:
                qh = qh * cos + qp[:, n_q + hh * HEAD_PAD:n_q + (hh + 1) * HEAD_PAD] * sin
            q_ref[hh, rows] = qh.astype(BF16)
        _kv_outputs(ckvn.astype(BF16), kr.astype(BF16), w_k_ref, w_uvt_ref, ones_ref, k_ref, vt_ref, rows)

        u = _dot_nt(h, w_uf_ref[...]).astype(BF16)
        for g in range(N_GROUPS):
            r = _dot(u[:, g * GROUP:(g + 1) * GROUP], cs_ref[...])
            ucs_ref[rows, g * GROUP:(g + 1) * GROUP] = r[:, :GROUP].astype(BF16)
            ucs_ref[rows, D_FOURIER + g * GROUP:D_FOURIER + (g + 1) * GROUP] = r[:, GROUP:].astype(BF16)


def _layer_in(x, mod3, w_in_p, wts, rope_tabs, *, n, state):
    t = x.shape[0]
    rope = rope_tabs is not None
    tm, per_seq, mod_row = _token_tiling(t, n, per_sequence=rope)
    tok = lambda w: pl.BlockSpec((tm, w), lambda i: (i, 0))
    in_specs = [tok(D_MODEL),
                pl.BlockSpec((1, 1, 3 * D_MODEL), lambda i: (mod_row(i), 0, 0)),
                _row_spec(W_BLK, B_UF), _row_spec(W_BLK, B_MID)]
    in_specs += [_const_spec(w.shape) for w in wts]
    args = [x, mod3, w_in_p, w_in_p, *wts]
    if rope:
        in_specs += [pl.BlockSpec((tm, HEAD_PAD), lambda i: (i % per_seq, 0))] * 2
        args += list(rope_tabs)
    head = pl.BlockSpec((N_HEADS, tm, HEAD_PAD), lambda i: (0, i, 0))
    out_specs = [tok(2 * D_FOURIER), head, head, pl.BlockSpec((VT_ROWS, tm), lambda i: (0, i))]
    out_shape = [jax.ShapeDtypeStruct((t, 2 * D_FOURIER), BF16),
                 jax.ShapeDtypeStruct((N_HEADS, t, HEAD_PAD), BF16),
                 jax.ShapeDtypeStruct((N_HEADS, t, HEAD_PAD), BF16),
                 jax.ShapeDtypeStruct((VT_ROWS, t), BF16)]
    if state:
        out_specs += [tok(KV_RANK), pl.BlockSpec((tm // n, QK_ROPE, n), lambda i: (i, 0, 0))]
        out_shape += [jax.ShapeDtypeStruct((t, KV_RANK), F32), jax.ShapeDtypeStruct((t // n, QK_ROPE, n), F32)]
    return pl.pallas_call(
        functools.partial(_layer_in_kernel, rope=rope, state=state),
        out_shape=out_shape,
        grid=(t // tm,),
        in_specs=in_specs,
        out_specs=out_specs,
        compiler_params=_params(1),
        name="layer_in_latent" if rope else "layer_in_context",
    )(*args)


def _cache_kv_kernel(ckv_ref, kr_ref, w_k_ref, w_uvt_ref, ones_ref, k_ref, vt_ref):
    _kv_outputs(ckv_ref[...].astype(BF16), kr_ref[...].astype(BF16), w_k_ref, w_uvt_ref, ones_ref,
                k_ref, vt_ref, slice(None))


def _cache_kv(ckv, krope_pad, w_k, w_uvt, ones_col):
    t = ckv.shape[0]
    out_shape = [jax.ShapeDtypeStruct((N_HEADS, t, HEAD_PAD), BF16), jax.ShapeDtypeStruct((VT_ROWS, t), BF16)]
    return pl.pallas_call(
        _cache_kv_kernel,
        out_shape=out_shape,
        grid=(1,),
        in_specs=[_const_spec(a.shape) for a in (ckv, krope_pad, w_k, w_uvt, ones_col)],
        out_specs=[_const_spec(s.shape) for s in out_shape],
        compiler_params=_params(1),
        name="cache_kv",
    )(ckv, krope_pad, w_k, w_uvt, ones_col)


def _pos_dft_direct_kernel(cn_ref, sn_ref, ucs_ref, f_ref):
    n = cn_ref.shape[0]
    for s in range(f_ref.shape[0] // n):
        rows = slice(s * n, (s + 1) * n)
        f = _dot(cn_ref[...], ucs_ref[rows, :D_FOURIER]) + _dot(sn_ref[...], ucs_ref[rows, D_FOURIER:])
        f_ref[rows] = f.astype(BF16)


def _pos_dft_half_kernel(c1_ref, s1_ref, c0_ref, s0_ref, rev_ref, rev1_ref, ucs_ref, f_ref, cn_ref, sn_ref):
    n = f_ref.shape[0]
    half = n // 2
    blocks = half // REV_BLOCK

    @pl.when(pl.program_id(0) == 0)
    def _():
        c0, s0 = c0_ref[...], s0_ref[...]
        for k1 in range(c1_ref.shape[0]):
            rows = slice(k1 * DFT_SPLIT, min((k1 + 1) * DFT_SPLIT, cn_ref.shape[0]))
            m = rows.stop - rows.start
            c1, s1 = c1_ref[k1:k1 + 1, :], s1_ref[k1:k1 + 1, :]
            cn_ref[rows] = (c1 * c0[:m] - s1 * s0[:m]).astype(BF16)
            sn_ref[rows] = (-(s1 * c0[:m] + c1 * s0[:m])).astype(BF16)

    def folded(cols, sign):
        first_row = lax.broadcasted_iota(jnp.int32, (REV_BLOCK, D_FOURIER), 0) == 0
        out = []
        for b in range(blocks):
            m = _dot(rev1_ref[...], ucs_ref[n - (b + 1) * REV_BLOCK:n - b * REV_BLOCK, cols])
            if b > 0:
                m = jnp.where(first_row, ucs_ref[n - b * REV_BLOCK:n - b * REV_BLOCK + 1, cols].astype(F32), m)
            out.append((ucs_ref[b * REV_BLOCK:(b + 1) * REV_BLOCK, cols] + sign * m).astype(BF16))
        return jnp.concatenate(out, axis=0)

    e = folded(slice(0, D_FOURIER), 1.0)
    o = folded(slice(D_FOURIER, 2 * D_FOURIER), -1.0)
    odd_k = (lax.broadcasted_iota(jnp.int32, (cn_ref.shape[0], D_FOURIER), 0) & 1) == 1
    mid = ucs_ref[half:half + 1, :D_FOURIER].astype(F32) * (1.0 / math.sqrt(n))
    g = _dot(cn_ref[...], e) + jnp.where(odd_k, -mid, mid)
    hn = _dot(sn_ref[...], o)
    f_ref[:half] = (g[:half] + hn[:half]).astype(BF16)
    mirrored = (g[1:half + 1] - hn[1:half + 1]).astype(BF16)
    for a in range(blocks):
        blk = mirrored[(blocks - 1 - a) * REV_BLOCK:(blocks - a) * REV_BLOCK]
        f_ref[half + a * REV_BLOCK:half + (a + 1) * REV_BLOCK] = _dot(rev_ref[...], blk).astype(BF16)


def _pos_dft(ucs, *, n):
    t = ucs.shape[0]
    out_shape = jax.ShapeDtypeStruct((t, D_FOURIER), BF16)
    if n <= REV_BLOCK:
        cn, snn = _dft_tables(n)
        seqs = min(8, t // n)
        return pl.pallas_call(
            _pos_dft_direct_kernel,
            out_shape=out_shape,
            grid=(t // (n * seqs),),
            in_specs=[_const_spec(cn.shape), _const_spec(snn.shape),
                      pl.BlockSpec((n * seqs, 2 * D_FOURIER), lambda i: (i, 0))],
            out_specs=pl.BlockSpec((n * seqs, D_FOURIER), lambda i: (i, 0)),
            compiler_params=_params(1),
            name=f"pos_dft_{n}",
        )(cn, snn, ucs)
    rows = n // 2 + 8
    consts = _dft_factors(n, rows, n // 2)
    anti = np.eye(REV_BLOCK, dtype=np.float32)[::-1]
    shifted = np.roll(anti, 1, axis=1)
    shifted[0] = 0.0
    consts += (jnp.asarray(anti, dtype=BF16), jnp.asarray(shifted, dtype=BF16))
    return pl.pallas_call(
        _pos_dft_half_kernel,
        out_shape=out_shape,
        grid=(t // n,),
        in_specs=[_const_spec(a.shape) for a in consts]
        + [pl.BlockSpec((n, 2 * D_FOURIER), lambda i: (i, 0))],
        out_specs=pl.BlockSpec((n, D_FOURIER), lambda i: (i, 0)),
        scratch_shapes=[pltpu.VMEM((rows, n // 2), BF16), pltpu.VMEM((rows, n // 2), BF16)],
        compiler_params=_params(1),
        name=f"pos_dft_{n}",
    )(*consts, ucs)


def _attention_kernel(*refs, n_kv, single_pass, units):
    q_ref, refs = refs[0], refs[1:]
    k_refs, vt_refs, refs = refs[:n_kv], refs[n_kv:2 * n_kv], refs[2 * n_kv:]
    o_ref, s_ref, pp_ref, ot_ref = refs[-4:]
    tq = q_ref.shape[1] // units
    key_lens = [k_ref.shape[1] // (1 if single_pass else units) for k_ref in k_refs]
    chunks, base = [], 0
    for j, m in enumerate(key_lens):
        chunks += [(j, c, base + c) for c in range(0, m, KEY_CHUNK)]
        base += m

    def q_block(hh, u):
        return q_ref[hh, u * tq:(u + 1) * tq, :]

    def finish(hh, ot, u):
        l = ot[V_DIM:V_DIM + 1]
        ot_ref[u, pl.ds(pl.multiple_of(hh * V_DIM, V_DIM), V_DIM), :] = ot[:V_DIM] / l
        return l

    def first_chunk_scores(hh, u):
        j, c, _ = chunks[0]
        s = _dot_nt(k_refs[j][hh, c:c + KEY_CHUNK, :], q_block(hh, u))
        s_ref[u, hh * KEY_CHUNK:(hh + 1) * KEY_CHUNK, :] = s
        return jnp.max(s, axis=0, keepdims=True)

    def shifted_softmax_pv(score_head, pv_head, shifts, u):
        if score_head is not None:
            qh = q_block(score_head, u)
            shift = shifts[score_head]
        ot = None
        for idx, (j, c, r) in enumerate(chunks):
            if score_head is not None:
                s = (s_ref[u, score_head * KEY_CHUNK:(score_head + 1) * KEY_CHUNK, :] if idx == 0 else
                     _dot_nt(k_refs[j][score_head, c:c + KEY_CHUNK, :], qh))
                pp_ref[2 * u + score_head % 2, r:r + KEY_CHUNK, :] = jnp.exp2(s - shift).astype(BF16)
            if pv_head is not None:
                part = _dot(vt_refs[j][pv_head * V_ROWS:(pv_head + 1) * V_ROWS, c:c + KEY_CHUNK],
                            pp_ref[2 * u + pv_head % 2, r:r + KEY_CHUNK, :])
                ot = part if ot is None else ot + part
        return finish(pv_head, ot, u) if pv_head is not None else None

    def exact_scores(hh, slot, u, kseq):
        qh = q_block(hh, u)
        m8 = None
        for j, c, r in chunks:
            k0 = kseq * key_lens[j] + c
            s = _dot_nt(k_refs[j][hh, k0:k0 + KEY_CHUNK, :], qh)
            s_ref[slot, r:r + KEY_CHUNK, :] = s
            mc = jnp.max(s.reshape(KEY_CHUNK // 8, 8, tq), axis=0)
            m8 = mc if m8 is None else jnp.maximum(m8, mc)
        return jnp.max(m8, axis=0, keepdims=True)

    def exact_softmax_pv(hh, s_slot, p_slot, m, u, kseq):
        for _, _, r in chunks:
            pp_ref[p_slot, r:r + KEY_CHUNK, :] = jnp.exp2(s_ref[s_slot, r:r + KEY_CHUNK, :] - m).astype(BF16)
        rows = pl.ds(pl.multiple_of(hh * V_ROWS, V_ROWS), V_ROWS)
        ot, base = None, 0
        for j, vt_ref in enumerate(vt_refs):
            mk = key_lens[j]
            part = _dot(vt_ref[rows, kseq * mk:(kseq + 1) * mk], pp_ref[p_slot, base:base + mk, :])
            ot = part if ot is None else ot + part
            base += mk
        finish(hh, ot, u)

    if not single_pass:
        work = [(u, hh) for u in range(units) for hh in range(N_HEADS)]
        maxima = [exact_scores(hh, slot, u, u) for slot, (u, hh) in enumerate(work)]
        for slot, (u, hh) in enumerate(work):
            exact_softmax_pv(hh, slot, slot, maxima[slot], u, u)
    else:
        trusted = []
        for u in range(units):
            l_min = l_max = None
            shifts = [first_chunk_scores(hh, u) for hh in range(N_HEADS)]
            for hh in range(N_HEADS + 1):
                l = shifted_softmax_pv(hh if hh < N_HEADS else None, hh - 1 if hh > 0 else None, shifts, u)
                if l is not None:
                    l_min = l if l_min is None else jnp.minimum(l_min, l)
                    l_max = l if l_max is None else jnp.maximum(l_max, l)
            trusted.append(jnp.logical_and(jnp.min(l_min) >= MIN_DENOMINATOR,
                                           jnp.max(l_max) <= MAX_DENOMINATOR))

        for u in range(units):
            @pl.when(jnp.logical_not(trusted[u]))
            def _(u=u):
                def body(hh, carry):
                    exact_softmax_pv(hh, u, 2 * u, exact_scores(hh, u, u, 0), u, 0)
                    return carry

                lax.fori_loop(0, N_HEADS, body, 0)

    for u in range(units):
        o_ref[u * tq:(u + 1) * tq, :] = ot_ref[u].T.astype(BF16)


def _attention(q, ks, vts, key_lens, *, n, after=None):
    t = q.shape[1]
    tq = Q_TILE
    m_tot = sum(key_lens)
    single_pass = m_tot > 2 * KEY_CHUNK
    if single_pass:
        units = 4
        steps = n // (units * tq)
        kv_rows = 1
        s_slots, p_slots = units, 2 * units
    else:
        assert n == tq
        units = min(4, t // n)
        steps = 1
        kv_rows = units
        s_slots = p_slots = units * N_HEADS
    in_specs = [pl.BlockSpec((N_HEADS, units * tq, HEAD_PAD), lambda bi, i: (0, bi * steps + i, 0))]
    in_specs += [pl.BlockSpec((N_HEADS, kv_rows * m, HEAD_PAD), lambda bi, i: (0, bi, 0)) for m in key_lens]
    in_specs += [pl.BlockSpec((VT_ROWS, kv_rows * m), lambda bi, i: (0, bi)) for m in key_lens]
    args = [q, *ks, *vts]
    if after is not None:
        in_specs.append(pl.BlockSpec((8, after.shape[1]), lambda bi, i: (0, 0)))
        args.append(after)
    return pl.pallas_call(
        functools.partial(_attention_kernel, n_kv=len(ks), single_pass=single_pass, units=units),
        out_shape=jax.ShapeDtypeStruct((t, D_ATTN), BF16),
        grid=(t // (n * kv_rows), steps),
        in_specs=in_specs,
        out_specs=pl.BlockSpec((units * tq, D_ATTN), lambda bi, i: (bi * steps + i, 0)),
        scratch_shapes=[pltpu.VMEM((s_slots, m_tot, tq), F32), pltpu.VMEM((p_slots, m_tot, tq), BF16),
                        pltpu.VMEM((units, D_ATTN, tq), F32)],
        compiler_params=_params(2),
        name=f"attention_{n}",
    )(*args)


def _merge_kernel(x_ref, mod_ref, f_ref, a_ref, w_zf_ref, w_za_ref, w_g_ref, w_f_ref, w_a_ref, w_o_ref,
                  g_ref, y_ref):
    gate = mod_ref[0][:, 2 * D_MODEL:]
    for rows in _sub_tiles(x_ref.shape[0], MERGE_SUB_TILE):
        x = x_ref[rows]
        h = _modulated(x, mod_ref)
        zf = _dot_nt(h, w_zf_ref[...])
        y_f = _dot((f_ref[rows] * (zf * _sigmoid(zf))).astype(BF16), w_f_ref[...])
        za = _dot_nt(h, w_za_ref[...])
        y_a = _dot((a_ref[rows] * (za * _sigmoid(za))).astype(BF16), w_a_ref[...])
        merged = (_sigmoid(_dot_nt(h, w_g_ref[:D_MODEL])) * y_f
                  + _sigmoid(_dot_nt(h, w_g_ref[D_MODEL:])) * y_a)
        out = x + gate * _dot(merged.astype(BF16), w_o_ref[...])
        y_ref[rows] = _rms(out) * g_ref[...]


def _merge(x, mod3, f, attn, w_in_p, w_f, w_a, w_o, g, *, n, latent):
    t = x.shape[0]
    tm, _, mod_row = _token_tiling(t, n, per_sequence=latent)
    tok = lambda w: pl.BlockSpec((tm, w), lambda i: (i, 0))
    return pl.pallas_call(
        _merge_kernel,
        out_shape=jax.ShapeDtypeStruct((t, D_MODEL), F32),
        grid=(t // tm,),
        in_specs=[tok(D_MODEL),
                  pl.BlockSpec((1, 1, 3 * D_MODEL), lambda i: (mod_row(i), 0, 0)),
                  tok(D_FOURIER), tok(D_ATTN),
                  _row_spec(W_BLK, B_ZF), _row_spec(W_BLK, B_ZA), _row_spec(2 * D_MODEL, 1),
                  _const_spec(w_f.shape), _const_spec(w_a.shape), _const_spec(w_o.shape),
                  _const_spec(g.shape)],
        out_specs=tok(D_MODEL),
        compiler_params=_params(1),
        name=f"merge_{n}",
    )(x, mod3, f, attn, w_in_p, w_in_p, w_in_p, w_f, w_a, w_o, g)


def _dft_tables(n):
    norm = 1.0 / math.sqrt(n)
    k = jnp.arange(n, dtype=jnp.int32)
    ang = ((k[:, None] * k[None, :]) % n).astype(F32) * (2.0 * math.pi / n)
    return (jnp.cos(ang) * norm).astype(BF16), (jnp.sin(ang) * -norm).astype(BF16)


def _dft_factors(n, rows, n_pos):
    norm = 1.0 / math.sqrt(n)
    period = n // DFT_SPLIT
    pos = jnp.arange(n_pos, dtype=jnp.int32)
    k1 = jnp.arange(-(-rows // DFT_SPLIT), dtype=jnp.int32)
    k0 = jnp.arange(DFT_SPLIT, dtype=jnp.int32)
    a1 = ((k1[:, None] * pos[None, :]) % period).astype(F32) * (2.0 * math.pi / period)
    a0 = ((k0[:, None] * pos[None, :]) % n).astype(F32) * (2.0 * math.pi / n)
    return jnp.cos(a1), jnp.sin(a1), jnp.cos(a0) * norm, jnp.sin(a0) * norm


def _channel_dft_table():
    c = jnp.arange(GROUP, dtype=jnp.int32)
    ang = ((c[:, None] * c[None, :]) % GROUP).astype(F32) * (2.0 * math.pi / GROUP)
    norm = 1.0 / math.sqrt(GROUP)
    return (jnp.concatenate([jnp.cos(ang), jnp.sin(ang)], axis=1) * norm).astype(BF16)


def _rope_tables(n):
    t = jnp.arange(n)
    row = (t // GRID_W).astype(F32)
    col = (t % GRID_W).astype(F32)
    half = QK_ROPE // 2
    inv = ROPE_THETA ** (-jnp.arange(0, half, 2, dtype=F32) / half)
    ar, ac = row[:, None] * inv, col[:, None] * inv
    ang = jnp.concatenate([ar, ar, ac, ac], axis=-1)
    pad = lambda a, fill: jnp.concatenate(
        [jnp.full((n, ROPE_LANE), fill, F32), a, jnp.full((n, HEAD_PAD - ROPE_LANE - QK_ROPE), fill, F32)], axis=1)
    return pad(jnp.cos(ang), 1.0), pad(jnp.sin(ang), 0.0)


def kernel(x_prompt, x_sample, cache_ckv, cache_krope, c, c_ctx, w_ada, b_ada, w_in, q_norm_g, w_uq,
           kv_norm_g, w_ukv, w_f_out, w_a_out, w_out, final_norm_g):
    assert w_in.shape[0] == 1
    b_ctx, n_ctx, _ = x_prompt.shape
    dec_b, n_lat, _ = x_sample.shape
    past = cache_ckv.shape[2]

    w_q3 = w_uq[0].reshape(Q_RANK, N_HEADS, QK_NOPE + QK_ROPE)
    head_pad = lambda a, left: jnp.pad(
        a, ((0, 0), (0, 0), (left, HEAD_PAD - left - a.shape[2]))).reshape(Q_RANK, N_HEADS * HEAD_PAD)
    rp = w_q3[:, :, QK_NOPE:].reshape(Q_RANK, N_HEADS, 2, 2, QK_ROPE // 4)
    w_q_rot = jnp.stack([-rp[:, :, :, 1], rp[:, :, :, 0]], axis=3).reshape(Q_RANK, N_HEADS, QK_ROPE)
    w_uq_p = jnp.concatenate([head_pad(w_q3, 0), head_pad(w_q_rot, ROPE_LANE)], axis=1).astype(BF16)
    w_kv3 = w_ukv[0].reshape(KV_RANK, N_HEADS, QK_NOPE + V_DIM)
    w_knope = jnp.pad(w_kv3[:, :, :QK_NOPE], ((0, 0), (0, 0), (0, HEAD_PAD - QK_NOPE)))
    place = np.zeros((HEAD_PAD, N_HEADS, HEAD_PAD), np.float32)
    for j in range(QK_ROPE):
        place[ROPE_LANE + j, :, ROPE_LANE + j] = 1.0
    w_k = jnp.concatenate([w_knope.reshape(KV_RANK, -1), jnp.asarray(place).reshape(HEAD_PAD, -1)],
                          axis=0).astype(BF16)
    w_uvt = jnp.pad(jnp.transpose(w_kv3[:, :, QK_NOPE:], (1, 2, 0)),
                    ((0, 0), (0, V_ROWS - V_DIM), (0, 0))).reshape(VT_ROWS, KV_RANK).astype(BF16)
    ones_col = np.zeros((N_HEADS, V_ROWS, LANES), np.float32)
    ones_col[:, V_DIM, :] = 1.0
    ones_col = jnp.asarray(ones_col.reshape(VT_ROWS, LANES))
    w_f = w_f_out[0].astype(BF16)
    w_a = w_a_out[0].astype(BF16)
    w_o = w_out[0].astype(BF16)
    qg = q_norm_g[0].reshape(1, Q_RANK)
    kvg = kv_norm_g[0].reshape(1, KV_RANK)
    fg = final_norm_g.reshape(1, D_MODEL)
    wts = (qg, kvg, w_uq_p, w_k, w_uvt, ones_col, _channel_dft_table())
    w_in_p = _pack_w_in(jnp.swapaxes(w_in[0], 0, 1))

    cond8 = jnp.concatenate([c_ctx[None, :], c, jnp.zeros((8 - 1 - dec_b, D_MODEL), F32)], axis=0)
    mod3 = _adaln(cond8, w_ada[0], b_ada[0].reshape(1, -1))

    xp = x_prompt.reshape(b_ctx * n_ctx, D_MODEL)
    ucs, q, k, vt, state_ckv, state_krope = _layer_in(xp, mod3, w_in_p, wts, None, n=n_ctx, state=True)
    f = _pos_dft(ucs, n=n_ctx)
    attn = _attention(q, [k], [vt], [n_ctx], n=n_ctx, after=f)
    y_prompt = _merge(xp, mod3, f, attn, w_in_p, w_f, w_a, w_o, fg, n=n_ctx, latent=False)

    xs = x_sample.reshape(dec_b * n_lat, D_MODEL)
    ucs, q, k, vt = _layer_in(xs, mod3, w_in_p, wts, _rope_tables(n_lat), n=n_lat, state=False)
    f = _pos_dft(ucs, n=n_lat)
    krope_pad = jnp.pad(cache_krope[:, 0], ((0, 0), (0, 0), (ROPE_LANE, HEAD_PAD - ROPE_LANE - QK_ROPE)))
    k_c, vt_c = _cache_kv(cache_ckv[:, 0].reshape(dec_b * past, KV_RANK),
                          krope_pad.reshape(dec_b * past, HEAD_PAD), w_k, w_uvt, ones_col)
    attn = _attention(q, [k, k_c], [vt, vt_c], [n_lat, past], n=n_lat, after=f)
    y_sample = _merge(xs, mod3, f, attn, w_in_p, w_f, w_a, w_o, fg, n=n_lat, latent=True)

    return (y_prompt.reshape(b_ctx, n_ctx, D_MODEL), y_sample.reshape(dec_b, n_lat, D_MODEL),
            state_ckv.reshape(b_ctx, 1, n_ctx, KV_RANK), jnp.swapaxes(state_krope, 1, 2)[:, None])
```

```python
import functools
import math

import jax
import jax.numpy as jnp
import numpy as np
from jax import lax
from jax.experimental import pallas as pl
from jax.experimental.pallas import tpu as pltpu

F32 = jnp.float32
BF16 = jnp.bfloat16

D_MODEL = 1024
GRID_W = 64
N_GROUPS = 4
GROUP = 128
D_FOURIER = N_GROUPS * GROUP
N_HEADS = 8
QK_NOPE = 64
QK_ROPE = 32
V_DIM = 64
Q_RANK = 256
KV_RANK = 128
D_ATTN = N_HEADS * V_DIM
D_IN = 2 * D_FOURIER + Q_RANK + KV_RANK + QK_ROPE + D_ATTN + 2 * D_MODEL
ROPE_THETA = 10000.0
EPS = 1e-6
LANES = 128
HEAD_PAD = LANES
ROPE_LANE = QK_NOPE
V_ROWS = V_DIM + 16
VT_ROWS = N_HEADS * V_ROWS
Q_SCALE = (QK_NOPE + QK_ROPE) ** -0.5 * math.log2(math.e)
KEY_CHUNK = 256
Q_TILE = 256
ATTENTION_UNITS = 4
SHORT_DFT_SEQUENCES = 8
REV_BLOCK = 256
DFT_SPLIT = 64
MIN_DENOMINATOR = 2.0 ** -60
MAX_DENOMINATOR = 2.0 ** 60
W_BLK = 512
B_UF, B_ZF, B_MID, B_ZA = 0, 1, 2, 3
D_IN_PAD = 8 * W_BLK
TOKEN_TILE = 1024
LAYER_IN_SUB_TILE = TOKEN_TILE
MERGE_SUB_TILE = TOKEN_TILE // 2
VMEM_LIMIT_BYTES = 56 * 1024 * 1024


def _const_spec(shape):
    nd = len(shape)
    return pl.BlockSpec(shape, lambda *_: (0,) * nd, pipeline_mode=pl.Buffered(1))


def _row_spec(rows, blk):
    return pl.BlockSpec((rows, D_MODEL), lambda *_: (blk, 0), pipeline_mode=pl.Buffered(1))


def _params(n_axes):
    return pltpu.CompilerParams(dimension_semantics=("arbitrary",) * n_axes,
                                vmem_limit_bytes=VMEM_LIMIT_BYTES)


def _rms(x):
    return x * lax.rsqrt(jnp.mean(x * x, axis=-1, keepdims=True) + EPS)


def _sigmoid(x):
    return 0.5 * jnp.tanh(0.5 * x) + 0.5


def _dot(a, b):
    return jnp.dot(a, b, preferred_element_type=F32)


def _dot_nt(a, b):
    return lax.dot_general(a, b, (((1,), (1,)), ((), ())), preferred_element_type=F32)


def _token_tiling(t, n, *, per_sequence):
    tm = TOKEN_TILE
    assert t % tm == 0 and (not per_sequence or n % tm == 0)
    per_seq = n // tm if per_sequence else None
    mod_row = (lambda i: 1 + i // per_seq) if per_sequence else (lambda i: 0)
    return tm, per_seq, mod_row


def _sub_tiles(rows, step):
    return [slice(r, r + step) for r in range(0, rows, step)]


def _modulated(x, mod_ref):
    mod = mod_ref[0]
    shift, scale = mod[:, 0:D_MODEL], mod[:, D_MODEL:2 * D_MODEL]
    return (_rms(x) * (1.0 + scale) + shift).astype(BF16)


def _adaln_kernel(c_ref, w_ref, b_ref, o_ref):
    c = c_ref[...]
    s = (c * _sigmoid(c)).astype(BF16)
    mod = _dot(s, w_ref[...].astype(BF16)) + b_ref[...]
    for r in range(mod.shape[0]):
        o_ref[r] = mod[r:r + 1]


def _adaln(cond8, w_ada, b_ada):
    n_blk = 4
    bw = 3 * D_MODEL // n_blk
    return pl.pallas_call(
        _adaln_kernel,
        out_shape=jax.ShapeDtypeStruct((8, 1, 3 * D_MODEL), F32),
        grid=(n_blk,),
        in_specs=[pl.BlockSpec((8, D_MODEL), lambda j: (0, 0)),
                  pl.BlockSpec((D_MODEL, bw), lambda j: (0, j)),
                  pl.BlockSpec((1, bw), lambda j: (0, j))],
        out_specs=pl.BlockSpec((8, 1, bw), lambda j: (0, 0, j)),
        compiler_params=_params(1),
        name="adaln",
    )(cond8, w_ada, b_ada)


def _rot_rows(w):
    q = QK_ROPE // 4
    return jnp.concatenate([-w[q:2 * q], w[0:q], -w[3 * q:4 * q], w[2 * q:3 * q]], axis=0)


def _pack_w_in_kernel(wt_ref, o_ref):
    lo = 2 * D_FOURIER + Q_RANK + KV_RANK
    zf = slice(B_ZF * W_BLK, (B_ZF + 1) * W_BLK)
    o_ref[:zf.start] = wt_ref[:zf.start].astype(BF16)
    o_ref[zf] = (0.5 * wt_ref[zf]).astype(BF16)
    o_ref[zf.stop:lo] = wt_ref[zf.stop:lo].astype(BF16)
    kr = wt_ref[lo:lo + QK_ROPE]
    zeros = jnp.zeros((QK_ROPE, D_MODEL), F32)
    o_ref[lo:lo + LANES] = jnp.concatenate([_rot_rows(kr), zeros, kr, zeros], axis=0).astype(BF16)
    o_ref[lo + LANES:] = (0.5 * wt_ref[lo + QK_ROPE:]).astype(BF16)


def _pack_w_in(w_in_t):
    return pl.pallas_call(
        _pack_w_in_kernel,
        out_shape=jax.ShapeDtypeStruct((D_IN_PAD, D_MODEL), BF16),
        grid=(1,),
        in_specs=[_const_spec((D_IN, D_MODEL))],
        out_specs=_const_spec((D_IN_PAD, D_MODEL)),
        compiler_params=_params(1),
        name="pack_w_in",
    )(w_in_t)


def _kv_outputs(ckvn_b, kr_b, w_k_ref, w_uvt_ref, ones_ref, k_ref, vt_ref, rows):
    tm = ckvn_b.shape[0]
    kp = _dot(jnp.concatenate([ckvn_b, kr_b], axis=1), w_k_ref[...])
    for hh in range(N_HEADS):
        k_ref[hh, rows] = kp[:, hh * HEAD_PAD:(hh + 1) * HEAD_PAD].astype(BF16)
    ones = jnp.concatenate([ones_ref[...]] * (tm // LANES), axis=1)
    vt_ref[:, rows] = (_dot_nt(w_uvt_ref[...], ckvn_b) + ones).astype(BF16)


def _layer_in_kernel(*refs, rope, state):
    (x_ref, mod_ref, w_uf_ref, w_mid_ref, qg_ref, kvg_ref, w_uq_ref, w_k_ref, w_uvt_ref, ones_ref,
     cs_ref), refs = refs[:11], refs[11:]
    if rope:
        (cos_ref, sin_ref), refs = refs[:2], refs[2:]
    (ucs_ref, q_ref, k_ref, vt_ref), refs = refs[:4], refs[4:]
    if state:
        ckvn_ref, krope_ref = refs

    for rows in _sub_tiles(x_ref.shape[0], LAYER_IN_SUB_TILE):
        h = _modulated(x_ref[rows], mod_ref)

        mid = _dot_nt(h, w_mid_ref[...])
        cq, ckv, kr = mid[:, :Q_RANK], mid[:, Q_RANK:Q_RANK + KV_RANK], mid[:, Q_RANK + KV_RANK:]
        cqn = (_rms(cq) * (qg_ref[...] * Q_SCALE)).astype(BF16)
        ckvn = _rms(ckv) * kvg_ref[...]
        if state:
            ckvn_ref[rows] = ckvn
            n = krope_ref.shape[2]
            kr_t = kr.T[ROPE_LANE:ROPE_LANE + QK_ROPE]
            for s in range(kr.shape[0] // n):
                krope_ref[rows.start // n + s] = kr_t[:, s * n:(s + 1) * n]
        n_q = N_HEADS * HEAD_PAD
        if rope:
            cos, sin = cos_ref[rows], sin_ref[rows]
            kr = kr * cos + pltpu.roll(kr, ROPE_LANE, 1) * sin
            qp = _dot(cqn, w_uq_ref[...])
        else:
            qp = _dot(cqn, w_uq_ref[:, :n_q])
        for hh in range(N_HEADS):
            qh = qp[:, hh * HEAD_PAD:(hh + 1) * HEAD_PAD]
            if rope:
                qh = qh * cos + qp[:, n_q + hh * HEAD_PAD:n_q + (hh + 1) * HEAD_PAD] * sin
            q_ref[hh, rows] = qh.astype(BF16)
        _kv_outputs(ckvn.astype(BF16), kr.astype(BF16), w_k_ref, w_uvt_ref, ones_ref, k_ref, vt_ref, rows)

        u = _dot_nt(h, w_uf_ref[...]).astype(BF16)
        for g in range(N_GROUPS):
            r = _dot(u[:, g * GROUP:(g + 1) * GROUP], cs_ref[...])
            ucs_ref[rows, g * GROUP:(g + 1) * GROUP] = r[:, :GROUP].astype(BF16)
            ucs_ref[rows, D_FOURIER + g * GROUP:D_FOURIER + (g + 1) * GROUP] = r[:, GROUP:].astype(BF16)


def _layer_in(x, mod3, w_in_p, wts, rope_tabs, *, n, state):
    t = x.shape[0]
    rope = rope_tabs is not None
    tm, per_seq, mod_row = _token_tiling(t, n, per_sequence=rope)
    tok = lambda w: pl.BlockSpec((tm, w), lambda i: (i, 0))
    in_specs = [tok(D_MODEL),
                pl.BlockSpec((1, 1, 3 * D_MODEL), lambda i: (mod_row(i), 0, 0)),
                _row_spec(W_BLK, B_UF), _row_spec(W_BLK, B_MID)]
    in_specs += [_const_spec(w.shape) for w in wts]
    args = [x, mod3, w_in_p, w_in_p, *wts]
    if rope:
        in_specs += [pl.BlockSpec((tm, HEAD_PAD), lambda i: (i % per_seq, 0))] * 2
        args += list(rope_tabs)
    head = pl.BlockSpec((N_HEADS, tm, HEAD_PAD), lambda i: (0, i, 0))
    out_specs = [tok(2 * D_FOURIER), head, head, pl.BlockSpec((VT_ROWS, tm), lambda i: (0, i))]
    out_shape = [jax.ShapeDtypeStruct((t, 2 * D_FOURIER), BF16),
                 jax.ShapeDtypeStruct((N_HEADS, t, HEAD_PAD), BF16),
                 jax.ShapeDtypeStruct((N_HEADS, t, HEAD_PAD), BF16),
                 jax.ShapeDtypeStruct((VT_ROWS, t), BF16)]
    if state:
        out_specs += [tok(KV_RANK), pl.BlockSpec((tm // n, QK_ROPE, n), lambda i: (i, 0, 0))]
        out_shape += [jax.ShapeDtypeStruct((t, KV_RANK), F32), jax.ShapeDtypeStruct((t // n, QK_ROPE, n), F32)]
    return pl.pallas_call(
        functools.partial(_layer_in_kernel, rope=rope, state=state),
        out_shape=out_shape,
        grid=(t // tm,),
        in_specs=in_specs,
        out_specs=out_specs,
        compiler_params=_params(1),
        name="layer_in_latent" if rope else "layer_in_context",
    )(*args)


def _cache_kv_kernel(ckv_ref, kr_ref, w_k_ref, w_uvt_ref, ones_ref, k_ref, vt_ref):
    _kv_outputs(ckv_ref[...].astype(BF16), kr_ref[...].astype(BF16), w_k_ref, w_uvt_ref, ones_ref,
                k_ref, vt_ref, slice(None))


def _cache_kv(ckv, krope_pad, w_k, w_uvt, ones_col):
    t = ckv.shape[0]
    out_shape = [jax.ShapeDtypeStruct((N_HEADS, t, HEAD_PAD), BF16), jax.ShapeDtypeStruct((VT_ROWS, t), BF16)]
    return pl.pallas_call(
        _cache_kv_kernel,
        out_shape=out_shape,
        grid=(1,),
        in_specs=[_const_spec(a.shape) for a in (ckv, krope_pad, w_k, w_uvt, ones_col)],
        out_specs=[_const_spec(s.shape) for s in out_shape],
        compiler_params=_params(1),
        name="cache_kv",
    )(ckv, krope_pad, w_k, w_uvt, ones_col)


def _pos_dft_direct_kernel(cn_ref, sn_ref, ucs_ref, f_ref):
    n = cn_ref.shape[0]
    for s in range(f_ref.shape[0] // n):
        rows = slice(s * n, (s + 1) * n)
        f = _dot(cn_ref[...], ucs_ref[rows, :D_FOURIER]) + _dot(sn_ref[...], ucs_ref[rows, D_FOURIER:])
        f_ref[rows] = f.astype(BF16)


def _pos_dft_half_kernel(c1_ref, s1_ref, c0_ref, s0_ref, rev_ref, rev1_ref, ucs_ref, f_ref, cn_ref, sn_ref):
    n = f_ref.shape[0]
    half = n // 2
    blocks = half // REV_BLOCK

    @pl.when(pl.program_id(0) == 0)
    def _():
        c0, s0 = c0_ref[...], s0_ref[...]
        for k1 in range(c1_ref.shape[0]):
            rows = slice(k1 * DFT_SPLIT, min((k1 + 1) * DFT_SPLIT, cn_ref.shape[0]))
            m = rows.stop - rows.start
            c1, s1 = c1_ref[k1:k1 + 1, :], s1_ref[k1:k1 + 1, :]
            cn_ref[rows] = (c1 * c0[:m] - s1 * s0[:m]).astype(BF16)
            sn_ref[rows] = (-(s1 * c0[:m] + c1 * s0[:m])).astype(BF16)

    def folded(cols, sign):
        first_row = lax.broadcasted_iota(jnp.int32, (REV_BLOCK, D_FOURIER), 0) == 0
        out = []
        for b in range(blocks):
            m = _dot(rev1_ref[...], ucs_ref[n - (b + 1) * REV_BLOCK:n - b * REV_BLOCK, cols])
            if b > 0:
                m = jnp.where(first_row, ucs_ref[n - b * REV_BLOCK:n - b * REV_BLOCK + 1, cols].astype(F32), m)
            out.append((ucs_ref[b * REV_BLOCK:(b + 1) * REV_BLOCK, cols] + sign * m).astype(BF16))
        return jnp.concatenate(out, axis=0)

    e = folded(slice(0, D_FOURIER), 1.0)
    o = folded(slice(D_FOURIER, 2 * D_FOURIER), -1.0)
    odd_k = (lax.broadcasted_iota(jnp.int32, (cn_ref.shape[0], D_FOURIER), 0) & 1) == 1
    mid = ucs_ref[half:half + 1, :D_FOURIER].astype(F32) * (1.0 / math.sqrt(n))
    g = _dot(cn_ref[...], e) + jnp.where(odd_k, -mid, mid)
    hn = _dot(sn_ref[...], o)
    f_ref[:half] = (g[:half] + hn[:half]).astype(BF16)
    mirrored = (g[1:half + 1] - hn[1:half + 1]).astype(BF16)
    for a in range(blocks):
        blk = mirrored[(blocks - 1 - a) * REV_BLOCK:(blocks - a) * REV_BLOCK]
        f_ref[half + a * REV_BLOCK:half + (a + 1) * REV_BLOCK] = _dot(rev_ref[...], blk).astype(BF16)


def _pos_dft(ucs, *, n):
    t = ucs.shape[0]
    out_shape = jax.ShapeDtypeStruct((t, D_FOURIER), BF16)
    if n <= REV_BLOCK:
        cn, snn = _dft_tables(n)
        seqs = min(SHORT_DFT_SEQUENCES, t // n)
        return pl.pallas_call(
            _pos_dft_direct_kernel,
            out_shape=out_shape,
            grid=(t // (n * seqs),),
            in_specs=[_const_spec(cn.shape), _const_spec(snn.shape),
                      pl.BlockSpec((n * seqs, 2 * D_FOURIER), lambda i: (i, 0))],
            out_specs=pl.BlockSpec((n * seqs, D_FOURIER), lambda i: (i, 0)),
            compiler_params=_params(1),
            name=f"pos_dft_{n}",
        )(cn, snn, ucs)
    rows = n // 2 + 8
    consts = _dft_factors(n, rows, n // 2)
    anti = np.eye(REV_BLOCK, dtype=np.float32)[::-1]
    shifted = np.roll(anti, 1, axis=1)
    shifted[0] = 0.0
    consts += (jnp.asarray(anti, dtype=BF16), jnp.asarray(shifted, dtype=BF16))
    return pl.pallas_call(
        _pos_dft_half_kernel,
        out_shape=out_shape,
        grid=(t // n,),
        in_specs=[_const_spec(a.shape) for a in consts]
        + [pl.BlockSpec((n, 2 * D_FOURIER), lambda i: (i, 0))],
        out_specs=pl.BlockSpec((n, D_FOURIER), lambda i: (i, 0)),
        scratch_shapes=[pltpu.VMEM((rows, n // 2), BF16), pltpu.VMEM((rows, n // 2), BF16)],
        compiler_params=_params(1),
        name=f"pos_dft_{n}",
    )(*consts, ucs)


def _attention_kernel(*refs, n_kv, single_pass, units):
    q_ref, refs = refs[0], refs[1:]
    k_refs, vt_refs, refs = refs[:n_kv], refs[n_kv:2 * n_kv], refs[2 * n_kv:]
    o_ref, s_ref, pp_ref, ot_ref = refs[-4:]
    tq = q_ref.shape[1] // units
    key_lens = [k_ref.shape[1] // (1 if single_pass else units) for k_ref in k_refs]
    chunks, base = [], 0
    for j, m in enumerate(key_lens):
        chunks += [(j, c, base + c) for c in range(0, m, KEY_CHUNK)]
        base += m

    def q_block(hh, u):
        return q_ref[hh, u * tq:(u + 1) * tq, :]

    def finish(hh, ot, u):
        l = ot[V_DIM:V_DIM + 1]
        ot_ref[u, pl.ds(pl.multiple_of(hh * V_DIM, V_DIM), V_DIM), :] = ot[:V_DIM] / l
        return l

    def first_chunk_scores(hh, u):
        j, c, _ = chunks[0]
        s = _dot_nt(k_refs[j][hh, c:c + KEY_CHUNK, :], q_block(hh, u))
        s_ref[u, hh * KEY_CHUNK:(hh + 1) * KEY_CHUNK, :] = s
        return jnp.max(s, axis=0, keepdims=True)

    def shifted_softmax_pv(score_head, pv_head, shifts, u):
        if score_head is not None:
            qh = q_block(score_head, u)
            shift = shifts[score_head]
        ot = None
        for idx, (j, c, r) in enumerate(chunks):
            if score_head is not None:
                s = (s_ref[u, score_head * KEY_CHUNK:(score_head + 1) * KEY_CHUNK, :] if idx == 0 else
                     _dot_nt(k_refs[j][score_head, c:c + KEY_CHUNK, :], qh))
                pp_ref[2 * u + score_head % 2, r:r + KEY_CHUNK, :] = jnp.exp2(s - shift).astype(BF16)
            if pv_head is not None:
                part = _dot(vt_refs[j][pv_head * V_ROWS:(pv_head + 1) * V_ROWS, c:c + KEY_CHUNK],
                            pp_ref[2 * u + pv_head % 2, r:r + KEY_CHUNK, :])
                ot = part if ot is None else ot + part
        return finish(pv_head, ot, u) if pv_head is not None else None

    def exact_scores(hh, slot, u, kseq):
        qh = q_block(hh, u)
        m8 = None
        for j, c, r in chunks:
            k0 = kseq * key_lens[j] + c
            s = _dot_nt(k_refs[j][hh, k0:k0 + KEY_CHUNK, :], qh)
            s_ref[slot, r:r + KEY_CHUNK, :] = s
            mc = jnp.max(s.reshape(KEY_CHUNK // 8, 8, tq), axis=0)
            m8 = mc if m8 is None else jnp.maximum(m8, mc)
        return jnp.max(m8, axis=0, keepdims=True)

    def exact_softmax_pv(hh, s_slot, p_slot, m, u, kseq):
        for _, _, r in chunks:
            pp_ref[p_slot, r:r + KEY_CHUNK, :] = jnp.exp2(s_ref[s_slot, r:r + KEY_CHUNK, :] - m).astype(BF16)
        rows = pl.ds(pl.multiple_of(hh * V_ROWS, V_ROWS), V_ROWS)
        ot, base = None, 0
        for j, vt_ref in enumerate(vt_refs):
            mk = key_lens[j]
            part = _dot(vt_ref[rows, kseq * mk:(kseq + 1) * mk], pp_ref[p_slot, base:base + mk, :])
            ot = part if ot is None else ot + part
            base += mk
        finish(hh, ot, u)

    if not single_pass:
        work = [(u, hh) for u in range(units) for hh in range(N_HEADS)]
        maxima = [exact_scores(hh, slot, u, u) for slot, (u, hh) in enumerate(work)]
        for slot, (u, hh) in enumerate(work):
            exact_softmax_pv(hh, slot, slot, maxima[slot], u, u)
    else:
        trusted = []
        for u in range(units):
            l_min = l_max = None
            shifts = [first_chunk_scores(hh, u) for hh in range(N_HEADS)]
            for hh in range(N_HEADS + 1):
                l = shifted_softmax_pv(hh if hh < N_HEADS else None, hh - 1 if hh > 0 else None, shifts, u)
                if l is not None:
                    l_min = l if l_min is None else jnp.minimum(l_min, l)
                    l_max = l if l_max is None else jnp.maximum(l_max, l)
            trusted.append(jnp.logical_and(jnp.min(l_min) >= MIN_DENOMINATOR,
                                           jnp.max(l_max) <= MAX_DENOMINATOR))

        for u in range(units):
            @pl.when(jnp.logical_not(trusted[u]))
            def _(u=u):
                def body(hh, carry):
                    exact_softmax_pv(hh, u, 2 * u, exact_scores(hh, u, u, 0), u, 0)
                    return carry

                lax.fori_loop(0, N_HEADS, body, 0)

    for u in range(units):
        o_ref[u * tq:(u + 1) * tq, :] = ot_ref[u].T.astype(BF16)


def _attention(q, ks, vts, key_lens, *, n, after=None):
    t = q.shape[1]
    tq = Q_TILE
    m_tot = sum(key_lens)
    single_pass = m_tot > 2 * KEY_CHUNK
    if single_pass:
        units = ATTENTION_UNITS
        steps = n // (units * tq)
        kv_rows = 1
        s_slots, p_slots = units, 2 * units
    else:
        assert n == tq
        units = min(ATTENTION_UNITS, t // n)
        steps = 1
        kv_rows = units
        s_slots = p_slots = units * N_HEADS
    in_specs = [pl.BlockSpec((N_HEADS, units * tq, HEAD_PAD), lambda bi, i: (0, bi * steps + i, 0))]
    in_specs += [pl.BlockSpec((N_HEADS, kv_rows * m, HEAD_PAD), lambda bi, i: (0, bi, 0)) for m in key_lens]
    in_specs += [pl.BlockSpec((VT_ROWS, kv_rows * m), lambda bi, i: (0, bi)) for m in key_lens]
    args = [q, *ks, *vts]
    if after is not None:
        in_specs.append(pl.BlockSpec((8, after.shape[1]), lambda bi, i: (0, 0)))
        args.append(after)
    return pl.pallas_call(
        functools.partial(_attention_kernel, n_kv=len(ks), single_pass=single_pass, units=units),
        out_shape=jax.ShapeDtypeStruct((t, D_ATTN), BF16),
        grid=(t // (n * kv_rows), steps),
        in_specs=in_specs,
        out_specs=pl.BlockSpec((units * tq, D_ATTN), lambda bi, i: (bi * steps + i, 0)),
        scratch_shapes=[pltpu.VMEM((s_slots, m_tot, tq), F32), pltpu.VMEM((p_slots, m_tot, tq), BF16),
                        pltpu.VMEM((units, D_ATTN, tq), F32)],
        compiler_params=_params(2),
        name=f"attention_{n}",
    )(*args)


def _merge_kernel(x_ref, mod_ref, f_ref, a_ref, w_zf_ref, w_za_ref, w_g_ref, w_f_ref, w_a_ref, w_o_ref,
                  g_ref, y_ref):
    gate = mod_ref[0][:, 2 * D_MODEL:]
    for rows in _sub_tiles(x_ref.shape[0], MERGE_SUB_TILE):
        x = x_ref[rows]
        h = _modulated(x, mod_ref)
        zf = _dot_nt(h, w_zf_ref[...])
        y_f = _dot((f_ref[rows] * (zf * (jnp.tanh(zf) + 1.0))).astype(BF16), w_f_ref[...])
        za = _dot_nt(h, w_za_ref[...])
        y_a = _dot((a_ref[rows] * (za * (jnp.tanh(za) + 1.0))).astype(BF16), w_a_ref[...])
        merged_x2 = ((jnp.tanh(_dot_nt(h, w_g_ref[:D_MODEL])) + 1.0) * y_f
                     + (jnp.tanh(_dot_nt(h, w_g_ref[D_MODEL:])) + 1.0) * y_a)
        out = x + gate * _dot(merged_x2.astype(BF16), w_o_ref[...])
        y_ref[rows] = _rms(out) * g_ref[...]


def _merge(x, mod3, f, attn, w_in_p, w_f, w_a, w_o, g, *, n, latent):
    t = x.shape[0]
    tm, _, mod_row = _token_tiling(t, n, per_sequence=latent)
    tok = lambda w: pl.BlockSpec((tm, w), lambda i: (i, 0))
    return pl.pallas_call(
        _merge_kernel,
        out_shape=jax.ShapeDtypeStruct((t, D_MODEL), F32),
        grid=(t // tm,),
        in_specs=[tok(D_MODEL),
                  pl.BlockSpec((1, 1, 3 * D_MODEL), lambda i: (mod_row(i), 0, 0)),
                  tok(D_FOURIER), tok(D_ATTN),
                  _row_spec(W_BLK, B_ZF), _row_spec(W_BLK, B_ZA), _row_spec(2 * D_MODEL, 1),
                  _const_spec(w_f.shape), _const_spec(w_a.shape), _const_spec(w_o.shape),
                  _const_spec(g.shape)],
        out_specs=tok(D_MODEL),
        compiler_params=_params(1),
        name=f"merge_{n}",
    )(x, mod3, f, attn, w_in_p, w_in_p, w_in_p, w_f, w_a, w_o, g)


def _dft_tables(n):
    norm = 1.0 / math.sqrt(n)
    k = np.arange(n)
    ang = ((k[:, None] * k[None, :]) % n) * (2.0 * math.pi / n)
    return (jnp.asarray(np.cos(ang) * norm, dtype=F32).astype(BF16),
            jnp.asarray(np.sin(ang) * -norm, dtype=F32).astype(BF16))


def _dft_factors(n, rows, n_pos):
    norm = 1.0 / math.sqrt(n)
    period = n // DFT_SPLIT
    pos = np.arange(n_pos)
    k1 = np.arange(-(-rows // DFT_SPLIT))
    k0 = np.arange(DFT_SPLIT)
    a1 = ((k1[:, None] * pos[None, :]) % period) * (2.0 * math.pi / period)
    a0 = ((k0[:, None] * pos[None, :]) % n) * (2.0 * math.pi / n)
    return tuple(a.astype(np.float32) for a in (np.cos(a1), np.sin(a1), np.cos(a0) * norm, np.sin(a0) * norm))


def _channel_dft_table():
    c = np.arange(GROUP)
    ang = ((c[:, None] * c[None, :]) % GROUP) * (2.0 * math.pi / GROUP)
    norm = 1.0 / math.sqrt(GROUP)
    return jnp.asarray(np.concatenate([np.cos(ang), np.sin(ang)], axis=1) * norm, dtype=F32).astype(BF16)


def _rope_tables(n):
    t = np.arange(n)
    row = (t // GRID_W).astype(np.float64)
    col = (t % GRID_W).astype(np.float64)
    half = QK_ROPE // 2
    inv = ROPE_THETA ** (-np.arange(0, half, 2, dtype=np.float64) / half)
    ar, ac = row[:, None] * inv, col[:, None] * inv
    ang = np.concatenate([ar, ar, ac, ac], axis=-1)
    pad = lambda a, fill: np.concatenate(
        [np.full((n, ROPE_LANE), fill), a, np.full((n, HEAD_PAD - ROPE_LANE - QK_ROPE), fill)],
        axis=1).astype(np.float32)
    return pad(np.cos(ang), 1.0), pad(np.sin(ang), 0.0)


def kernel(x_prompt, x_sample, cache_ckv, cache_krope, c, c_ctx, w_ada, b_ada, w_in, q_norm_g, w_uq,
           kv_norm_g, w_ukv, w_f_out, w_a_out, w_out, final_norm_g):
    assert w_in.shape[0] == 1
    b_ctx, n_ctx, _ = x_prompt.shape
    dec_b, n_lat, _ = x_sample.shape
    past = cache_ckv.shape[2]

    w_q3 = w_uq[0].reshape(Q_RANK, N_HEADS, QK_NOPE + QK_ROPE)
    head_pad = lambda a, left: jnp.pad(
        a, ((0, 0), (0, 0), (left, HEAD_PAD - left - a.shape[2]))).reshape(Q_RANK, N_HEADS * HEAD_PAD)
    rp = w_q3[:, :, QK_NOPE:].reshape(Q_RANK, N_HEADS, 2, 2, QK_ROPE // 4)
    w_q_rot = jnp.stack([-rp[:, :, :, 1], rp[:, :, :, 0]], axis=3).reshape(Q_RANK, N_HEADS, QK_ROPE)
    w_uq_p = jnp.concatenate([head_pad(w_q3, 0), head_pad(w_q_rot, ROPE_LANE)], axis=1).astype(BF16)
    w_kv3 = w_ukv[0].reshape(KV_RANK, N_HEADS, QK_NOPE + V_DIM)
    w_knope = jnp.pad(w_kv3[:, :, :QK_NOPE], ((0, 0), (0, 0), (0, HEAD_PAD - QK_NOPE)))
    place = np.zeros((HEAD_PAD, N_HEADS, HEAD_PAD), np.float32)
    for j in range(QK_ROPE):
        place[ROPE_LANE + j, :, ROPE_LANE + j] = 1.0
    w_k = jnp.concatenate([w_knope.reshape(KV_RANK, -1), jnp.asarray(place).reshape(HEAD_PAD, -1)],
                          axis=0).astype(BF16)
    w_uvt = jnp.pad(jnp.transpose(w_kv3[:, :, QK_NOPE:], (1, 2, 0)),
                    ((0, 0), (0, V_ROWS - V_DIM), (0, 0))).reshape(VT_ROWS, KV_RANK).astype(BF16)
    ones_col = np.zeros((N_HEADS, V_ROWS, LANES), np.float32)
    ones_col[:, V_DIM, :] = 1.0
    ones_col = jnp.asarray(ones_col.reshape(VT_ROWS, LANES))
    w_f = w_f_out[0].astype(BF16)
    w_a = w_a_out[0].astype(BF16)
    w_o = (0.5 * w_out[0]).astype(BF16)
    qg = q_norm_g[0].reshape(1, Q_RANK)
    kvg = kv_norm_g[0].reshape(1, KV_RANK)
    fg = final_norm_g.reshape(1, D_MODEL)
    wts = (qg, kvg, w_uq_p, w_k, w_uvt, ones_col, _channel_dft_table())
    w_in_p = _pack_w_in(jnp.swapaxes(w_in[0], 0, 1))

    cond8 = jnp.concatenate([c_ctx[None, :], c, jnp.zeros((8 - 1 - dec_b, D_MODEL), F32)], axis=0)
    mod3 = _adaln(cond8, w_ada[0], b_ada[0].reshape(1, -1))

    xp = x_prompt.reshape(b_ctx * n_ctx, D_MODEL)
    ucs, q, k, vt, state_ckv, state_krope = _layer_in(xp, mod3, w_in_p, wts, None, n=n_ctx, state=True)
    f = _pos_dft(ucs, n=n_ctx)
    attn = _attention(q, [k], [vt], [n_ctx], n=n_ctx, after=f)
    y_prompt = _merge(xp, mod3, f, attn, w_in_p, w_f, w_a, w_o, fg, n=n_ctx, latent=False)

    xs = x_sample.reshape(dec_b * n_lat, D_MODEL)
    ucs, q, k, vt = _layer_in(xs, mod3, w_in_p, wts, _rope_tables(n_lat), n=n_lat, state=False)
    f = _pos_dft(ucs, n=n_lat)
    krope_pad = jnp.pad(cache_krope[:, 0], ((0, 0), (0, 0), (ROPE_LANE, HEAD_PAD - ROPE_LANE - QK_ROPE)))
    k_c, vt_c = _cache_kv(cache_ckv[:, 0].reshape(dec_b * past, KV_RANK),
                          krope_pad.reshape(dec_b * past, HEAD_PAD), w_k, w_uvt, ones_col)
    attn = _attention(q, [k, k_c], [vt, vt_c], [n_lat, past], n=n_lat, after=f)
    y_sample = _merge(xs, mod3, f, attn, w_in_p, w_f, w_a, w_o, fg, n=n_lat, latent=True)

    return (y_prompt.reshape(b_ctx, n_ctx, D_MODEL), y_sample.reshape(dec_b, n_lat, D_MODEL),
            state_ckv.reshape(b_ctx, 1, n_ctx, KV_RANK), jnp.swapaxes(state_krope, 1, 2)[:, None])
```

```python
import functools
import math

import jax
import jax.numpy as jnp
import numpy as np
from jax import lax
from jax.experimental import pallas as pl
from jax.experimental.pallas import tpu as pltpu

F32 = jnp.float32
BF16 = jnp.bfloat16

D_MODEL = 1024
GRID_W = 64
N_GROUPS = 4
GROUP = 128
D_FOURIER = N_GROUPS * GROUP
N_HEADS = 8
QK_NOPE = 64
QK_ROPE = 32
V_DIM = 64
Q_RANK = 256
KV_RANK = 128
D_ATTN = N_HEADS * V_DIM
D_IN = 2 * D_FOURIER + Q_RANK + KV_RANK + QK_ROPE + D_ATTN + 2 * D_MODEL
ROPE_THETA = 10000.0
EPS = 1e-6
LANES = 128
HEAD_PAD = LANES
ROPE_LANE = QK_NOPE
VT_ROWS = N_HEADS * V_DIM
Q_SCALE = (QK_NOPE + QK_ROPE) ** -0.5 * math.log2(math.e)
KEY_CHUNK = 256
Q_TILE = 256
ATTENTION_UNITS = 4
SHORT_DFT_SEQUENCES = 8
REV_BLOCK = 256
DFT_SPLIT = 64
MIN_DENOMINATOR = 2.0 ** -60
MAX_DENOMINATOR = 2.0 ** 60
W_BLK = 512
B_UF, B_ZF, B_MID, B_ZA = 0, 1, 2, 3
D_IN_PAD = 8 * W_BLK
TOKEN_TILE = 1024
LAYER_IN_SUB_TILE = TOKEN_TILE
MERGE_SUB_TILE = TOKEN_TILE // 2
VMEM_LIMIT_BYTES = 56 * 1024 * 1024


def _const_spec(shape):
    nd = len(shape)
    return pl.BlockSpec(shape, lambda *_: (0,) * nd, pipeline_mode=pl.Buffered(1))


def _row_spec(rows, blk):
    return pl.BlockSpec((rows, D_MODEL), lambda *_: (blk, 0), pipeline_mode=pl.Buffered(1))


def _params(n_axes):
    return pltpu.CompilerParams(dimension_semantics=("arbitrary",) * n_axes,
                                vmem_limit_bytes=VMEM_LIMIT_BYTES)


def _rms(x):
    return x * lax.rsqrt(jnp.mean(x * x, axis=-1, keepdims=True) + EPS)


def _sigmoid(x):
    return 0.5 * jnp.tanh(0.5 * x) + 0.5


def _dot(a, b):
    return jnp.dot(a, b, preferred_element_type=F32)


def _dot_nt(a, b):
    return lax.dot_general(a, b, (((1,), (1,)), ((), ())), preferred_element_type=F32)


def _token_tiling(t, n, *, per_sequence):
    tm = TOKEN_TILE
    assert t % tm == 0 and (not per_sequence or n % tm == 0)
    per_seq = n // tm if per_sequence else None
    mod_row = (lambda i: 1 + i // per_seq) if per_sequence else (lambda i: 0)
    return tm, per_seq, mod_row


def _sub_tiles(rows, step):
    return [slice(r, r + step) for r in range(0, rows, step)]


def _modulated(x, mod_ref):
    mod = mod_ref[0]
    shift, scale = mod[:, 0:D_MODEL], mod[:, D_MODEL:2 * D_MODEL]
    return (_rms(x) * (1.0 + scale) + shift).astype(BF16)


def _adaln_kernel(c_ref, w_ref, b_ref, o_ref):
    c = c_ref[...]
    s = (c * _sigmoid(c)).astype(BF16)
    mod = _dot(s, w_ref[...].astype(BF16)) + b_ref[...]
    for r in range(mod.shape[0]):
        o_ref[r] = mod[r:r + 1]


def _adaln(cond8, w_ada, b_ada):
    n_blk = 4
    bw = 3 * D_MODEL // n_blk
    return pl.pallas_call(
        _adaln_kernel,
        out_shape=jax.ShapeDtypeStruct((8, 1, 3 * D_MODEL), F32),
        grid=(n_blk,),
        in_specs=[pl.BlockSpec((8, D_MODEL), lambda j: (0, 0)),
                  pl.BlockSpec((D_MODEL, bw), lambda j: (0, j)),
                  pl.BlockSpec((1, bw), lambda j: (0, j))],
        out_specs=pl.BlockSpec((8, 1, bw), lambda j: (0, 0, j)),
        compiler_params=_params(1),
        name="adaln",
    )(cond8, w_ada, b_ada)


def _rot_rows(w):
    q = QK_ROPE // 4
    return jnp.concatenate([-w[q:2 * q], w[0:q], -w[3 * q:4 * q], w[2 * q:3 * q]], axis=0)


def _pack_w_in_kernel(wt_ref, o_ref):
    lo = 2 * D_FOURIER + Q_RANK + KV_RANK
    zf = slice(B_ZF * W_BLK, (B_ZF + 1) * W_BLK)
    o_ref[:zf.start] = wt_ref[:zf.start].astype(BF16)
    o_ref[zf] = (0.5 * wt_ref[zf]).astype(BF16)
    o_ref[zf.stop:lo] = wt_ref[zf.stop:lo].astype(BF16)
    kr = wt_ref[lo:lo + QK_ROPE]
    zeros = jnp.zeros((QK_ROPE, D_MODEL), F32)
    o_ref[lo:lo + LANES] = jnp.concatenate([_rot_rows(kr), zeros, kr, zeros], axis=0).astype(BF16)
    o_ref[lo + LANES:] = (0.5 * wt_ref[lo + QK_ROPE:]).astype(BF16)


def _pack_w_in(w_in_t):
    return pl.pallas_call(
        _pack_w_in_kernel,
        out_shape=jax.ShapeDtypeStruct((D_IN_PAD, D_MODEL), BF16),
        grid=(1,),
        in_specs=[_const_spec((D_IN, D_MODEL))],
        out_specs=_const_spec((D_IN_PAD, D_MODEL)),
        compiler_params=_params(1),
        name="pack_w_in",
    )(w_in_t)


def _kv_outputs(ckvn_b, kr_b, w_k_ref, w_uvt_ref, k_ref, vt_ref, rows):
    kp = _dot(jnp.concatenate([ckvn_b, kr_b], axis=1), w_k_ref[...])
    for hh in range(N_HEADS):
        k_ref[hh, rows] = kp[:, hh * HEAD_PAD:(hh + 1) * HEAD_PAD].astype(BF16)
    vt_ref[:, rows] = _dot_nt(w_uvt_ref[...], ckvn_b).astype(BF16)


def _layer_in_kernel(*refs, rope, state):
    (x_ref, mod_ref, w_uf_ref, w_mid_ref, qg_ref, kvg_ref, w_uq_ref, w_k_ref, w_uvt_ref,
     cs_ref), refs = refs[:10], refs[10:]
    if rope:
        (cos_ref, sin_ref), refs = refs[:2], refs[2:]
    (ucs_ref, q_ref, k_ref, vt_ref), refs = refs[:4], refs[4:]
    if state:
        ckvn_ref, krope_ref = refs

    for rows in _sub_tiles(x_ref.shape[0], LAYER_IN_SUB_TILE):
        h = _modulated(x_ref[rows], mod_ref)

        mid = _dot_nt(h, w_mid_ref[...])
        cq, ckv, kr = mid[:, :Q_RANK], mid[:, Q_RANK:Q_RANK + KV_RANK], mid[:, Q_RANK + KV_RANK:]
        cqn = (_rms(cq) * (qg_ref[...] * Q_SCALE)).astype(BF16)
        ckvn = _rms(ckv) * kvg_ref[...]
        if state:
            ckvn_ref[rows] = ckvn
            n = krope_ref.shape[2]
            kr_t = kr.T[ROPE_LANE:ROPE_LANE + QK_ROPE]
            for s in range(kr.shape[0] // n):
                krope_ref[rows.start // n + s] = kr_t[:, s * n:(s + 1) * n]
        n_q = N_HEADS * HEAD_PAD
        if rope:
            cos, sin = cos_ref[rows], sin_ref[rows]
            kr = kr * cos + pltpu.roll(kr, ROPE_LANE, 1) * sin
            qp = _dot(cqn, w_uq_ref[...])
        else:
            qp = _dot(cqn, w_uq_ref[:, :n_q])
        for hh in range(N_HEADS):
            qh = qp[:, hh * HEAD_PAD:(hh + 1) * HEAD_PAD]
            if rope:
                qh = qh * cos + qp[:, n_q + hh * HEAD_PAD:n_q + (hh + 1) * HEAD_PAD] * sin
            q_ref[hh, rows] = qh.astype(BF16)
        _kv_outputs(ckvn.astype(BF16), kr.astype(BF16), w_k_ref, w_uvt_ref, k_ref, vt_ref, rows)

        u = _dot_nt(h, w_uf_ref[...]).astype(BF16)
        for g in range(N_GROUPS):
            r = _dot(u[:, g * GROUP:(g + 1) * GROUP], cs_ref[...])
            ucs_ref[rows, g * GROUP:(g + 1) * GROUP] = r[:, :GROUP].astype(BF16)
            ucs_ref[rows, D_FOURIER + g * GROUP:D_FOURIER + (g + 1) * GROUP] = r[:, GROUP:].astype(BF16)


def _layer_in(x, mod3, w_in_p, wts, rope_tabs, *, n, state):
    t = x.shape[0]
    rope = rope_tabs is not None
    tm, per_seq, mod_row = _token_tiling(t, n, per_sequence=rope)
    tok = lambda w: pl.BlockSpec((tm, w), lambda i: (i, 0))
    in_specs = [tok(D_MODEL),
                pl.BlockSpec((1, 1, 3 * D_MODEL), lambda i: (mod_row(i), 0, 0)),
                _row_spec(W_BLK, B_UF), _row_spec(W_BLK, B_MID)]
    in_specs += [_const_spec(w.shape) for w in wts]
    args = [x, mod3, w_in_p, w_in_p, *wts]
    if rope:
        in_specs += [pl.BlockSpec((tm, HEAD_PAD), lambda i: (i % per_seq, 0))] * 2
        args += list(rope_tabs)
    head = pl.BlockSpec((N_HEADS, tm, HEAD_PAD), lambda i: (0, i, 0))
    out_specs = [tok(2 * D_FOURIER), head, head, pl.BlockSpec((VT_ROWS, tm), lambda i: (0, i))]
    out_shape = [jax.ShapeDtypeStruct((t, 2 * D_FOURIER), BF16),
                 jax.ShapeDtypeStruct((N_HEADS, t, HEAD_PAD), BF16),
                 jax.ShapeDtypeStruct((N_HEADS, t, HEAD_PAD), BF16),
                 jax.ShapeDtypeStruct((VT_ROWS, t), BF16)]
    if state:
        out_specs += [tok(KV_RANK), pl.BlockSpec((tm // n, QK_ROPE, n), lambda i: (i, 0, 0))]
        out_shape += [jax.ShapeDtypeStruct((t, KV_RANK), F32), jax.ShapeDtypeStruct((t // n, QK_ROPE, n), F32)]
    return pl.pallas_call(
        functools.partial(_layer_in_kernel, rope=rope, state=state),
        out_shape=out_shape,
        grid=(t // tm,),
        in_specs=in_specs,
        out_specs=out_specs,
        compiler_params=_params(1),
        name="layer_in_latent" if rope else "layer_in_context",
    )(*args)


def _cache_kv_kernel(ckv_ref, kr_ref, w_k_ref, w_uvt_ref, k_ref, vt_ref):
    _kv_outputs(ckv_ref[...].astype(BF16), kr_ref[...].astype(BF16), w_k_ref, w_uvt_ref, k_ref, vt_ref,
                slice(None))


def _cache_kv(ckv, krope_pad, w_k, w_uvt):
    t = ckv.shape[0]
    out_shape = [jax.ShapeDtypeStruct((N_HEADS, t, HEAD_PAD), BF16), jax.ShapeDtypeStruct((VT_ROWS, t), BF16)]
    return pl.pallas_call(
        _cache_kv_kernel,
        out_shape=out_shape,
        grid=(1,),
        in_specs=[_const_spec(a.shape) for a in (ckv, krope_pad, w_k, w_uvt)],
        out_specs=[_const_spec(s.shape) for s in out_shape],
        compiler_params=_params(1),
        name="cache_kv",
    )(ckv, krope_pad, w_k, w_uvt)


def _pos_dft_direct_kernel(cn_ref, sn_ref, ucs_ref, f_ref):
    n = cn_ref.shape[0]
    for s in range(f_ref.shape[0] // n):
        rows = slice(s * n, (s + 1) * n)
        f = _dot(cn_ref[...], ucs_ref[rows, :D_FOURIER]) + _dot(sn_ref[...], ucs_ref[rows, D_FOURIER:])
        f_ref[rows] = f.astype(BF16)


def _pos_dft_half_kernel(c1_ref, s1_ref, c0_ref, s0_ref, rev_ref, rev1_ref, ucs_ref, f_ref, cn_ref, sn_ref):
    n = f_ref.shape[0]
    half = n // 2
    blocks = half // REV_BLOCK

    @pl.when(pl.program_id(0) == 0)
    def _():
        c0, s0 = c0_ref[...], s0_ref[...]
        for k1 in range(c1_ref.shape[0]):
            rows = slice(k1 * DFT_SPLIT, min((k1 + 1) * DFT_SPLIT, cn_ref.shape[0]))
            m = rows.stop - rows.start
            c1, s1 = c1_ref[k1:k1 + 1, :], s1_ref[k1:k1 + 1, :]
            cn_ref[rows] = (c1 * c0[:m] - s1 * s0[:m]).astype(BF16)
            sn_ref[rows] = (-(s1 * c0[:m] + c1 * s0[:m])).astype(BF16)

    def folded(cols, sign):
        first_row = lax.broadcasted_iota(jnp.int32, (REV_BLOCK, D_FOURIER), 0) == 0
        out = []
        for b in range(blocks):
            m = _dot(rev1_ref[...], ucs_ref[n - (b + 1) * REV_BLOCK:n - b * REV_BLOCK, cols])
            if b > 0:
                m = jnp.where(first_row, ucs_ref[n - b * REV_BLOCK:n - b * REV_BLOCK + 1, cols].astype(F32), m)
            out.append((ucs_ref[b * REV_BLOCK:(b + 1) * REV_BLOCK, cols] + sign * m).astype(BF16))
        return jnp.concatenate(out, axis=0)

    e = folded(slice(0, D_FOURIER), 1.0)
    o = folded(slice(D_FOURIER, 2 * D_FOURIER), -1.0)
    odd_k = (lax.broadcasted_iota(jnp.int32, (cn_ref.shape[0], D_FOURIER), 0) & 1) == 1
    mid = ucs_ref[half:half + 1, :D_FOURIER].astype(F32) * (1.0 / math.sqrt(n))
    g = _dot(cn_ref[...], e) + jnp.where(odd_k, -mid, mid)
    hn = _dot(sn_ref[...], o)
    f_ref[:half] = (g[:half] + hn[:half]).astype(BF16)
    mirrored = (g[1:half + 1] - hn[1:half + 1]).astype(BF16)
    for a in range(blocks):
        blk = mirrored[(blocks - 1 - a) * REV_BLOCK:(blocks - a) * REV_BLOCK]
        f_ref[half + a * REV_BLOCK:half + (a + 1) * REV_BLOCK] = _dot(rev_ref[...], blk).astype(BF16)


def _pos_dft(ucs, *, n):
    t = ucs.shape[0]
    out_shape = jax.ShapeDtypeStruct((t, D_FOURIER), BF16)
    if n <= REV_BLOCK:
        cn, snn = _dft_tables(n)
        seqs = min(SHORT_DFT_SEQUENCES, t // n)
        return pl.pallas_call(
            _pos_dft_direct_kernel,
            out_shape=out_shape,
            grid=(t // (n * seqs),),
            in_specs=[_const_spec(cn.shape), _const_spec(snn.shape),
                      pl.BlockSpec((n * seqs, 2 * D_FOURIER), lambda i: (i, 0))],
            out_specs=pl.BlockSpec((n * seqs, D_FOURIER), lambda i: (i, 0)),
            compiler_params=_params(1),
            name=f"pos_dft_{n}",
        )(cn, snn, ucs)
    rows = n // 2 + 8
    consts = _dft_factors(n, rows, n // 2)
    anti = np.eye(REV_BLOCK, dtype=np.float32)[::-1]
    shifted = np.roll(anti, 1, axis=1)
    shifted[0] = 0.0
    consts += (jnp.asarray(anti, dtype=BF16), jnp.asarray(shifted, dtype=BF16))
    return pl.pallas_call(
        _pos_dft_half_kernel,
        out_shape=out_shape,
        grid=(t // n,),
        in_specs=[_const_spec(a.shape) for a in consts]
        + [pl.BlockSpec((n, 2 * D_FOURIER), lambda i: (i, 0))],
        out_specs=pl.BlockSpec((n, D_FOURIER), lambda i: (i, 0)),
        scratch_shapes=[pltpu.VMEM((rows, n // 2), BF16), pltpu.VMEM((rows, n // 2), BF16)],
        compiler_params=_params(1),
        name=f"pos_dft_{n}",
    )(*consts, ucs)


def _attention_kernel(*refs, n_kv, single_pass, units):
    q_ref, refs = refs[0], refs[1:]
    k_refs, vt_refs, refs = refs[:n_kv], refs[n_kv:2 * n_kv], refs[2 * n_kv:]
    o_ref, s_ref, pp_ref, ot_ref = refs[-4:]
    tq = q_ref.shape[1] // units
    key_lens = [k_ref.shape[1] // (1 if single_pass else units) for k_ref in k_refs]
    chunks, base = [], 0
    for j, m in enumerate(key_lens):
        chunks += [(j, c, base + c) for c in range(0, m, KEY_CHUNK)]
        base += m

    def q_block(hh, u):
        return q_ref[hh, u * tq:(u + 1) * tq, :]

    def key_sum(p):
        return jnp.sum(p.reshape(p.shape[0] // 8, 8, tq), axis=0)

    def finish(hh, ot, l8, u):
        l = jnp.sum(l8, axis=0, keepdims=True)
        ot_ref[u, pl.ds(pl.multiple_of(hh * V_DIM, V_DIM), V_DIM), :] = ot / l
        return l

    def first_chunk_scores(hh, u):
        j, c, _ = chunks[0]
        s = _dot_nt(k_refs[j][hh, c:c + KEY_CHUNK, :], q_block(hh, u))
        s_ref[u, hh * KEY_CHUNK:(hh + 1) * KEY_CHUNK, :] = s
        return jnp.max(s, axis=0, keepdims=True)

    def shifted_softmax_pv(score_head, pv_head, shifts, u, l8_pv):
        if score_head is not None:
            qh = q_block(score_head, u)
            shift = shifts[score_head]
        ot = l8 = None
        for idx, (j, c, r) in enumerate(chunks):
            if score_head is not None:
                s = (s_ref[u, score_head * KEY_CHUNK:(score_head + 1) * KEY_CHUNK, :] if idx == 0 else
                     _dot_nt(k_refs[j][score_head, c:c + KEY_CHUNK, :], qh))
                p = jnp.exp2(s - shift)
                pp_ref[2 * u + score_head % 2, r:r + KEY_CHUNK, :] = p.astype(BF16)
                l8 = key_sum(p) if l8 is None else l8 + key_sum(p)
            if pv_head is not None:
                part = _dot(vt_refs[j][pv_head * V_DIM:(pv_head + 1) * V_DIM, c:c + KEY_CHUNK],
                            pp_ref[2 * u + pv_head % 2, r:r + KEY_CHUNK, :])
                ot = part if ot is None else ot + part
        return l8, (finish(pv_head, ot, l8_pv, u) if pv_head is not None else None)

    def exact_scores(hh, slot, u, kseq):
        qh = q_block(hh, u)
        m8 = None
        for j, c, r in chunks:
            k0 = kseq * key_lens[j] + c
            s = _dot_nt(k_refs[j][hh, k0:k0 + KEY_CHUNK, :], qh)
            s_ref[slot, r:r + KEY_CHUNK, :] = s
            mc = jnp.max(s.reshape(KEY_CHUNK // 8, 8, tq), axis=0)
            m8 = mc if m8 is None else jnp.maximum(m8, mc)
        return jnp.max(m8, axis=0, keepdims=True)

    def exact_softmax_pv(hh, s_slot, p_slot, m, u, kseq):
        l8 = None
        for _, _, r in chunks:
            p = jnp.exp2(s_ref[s_slot, r:r + KEY_CHUNK, :] - m)
            pp_ref[p_slot, r:r + KEY_CHUNK, :] = p.astype(BF16)
            l8 = key_sum(p) if l8 is None else l8 + key_sum(p)
        rows = pl.ds(pl.multiple_of(hh * V_DIM, V_DIM), V_DIM)
        ot, base = None, 0
        for j, vt_ref in enumerate(vt_refs):
            mk = key_lens[j]
            part = _dot(vt_ref[rows, kseq * mk:(kseq + 1) * mk], pp_ref[p_slot, base:base + mk, :])
            ot = part if ot is None else ot + part
            base += mk
        finish(hh, ot, l8, u)

    if not single_pass:
        work = [(u, hh) for u in range(units) for hh in range(N_HEADS)]
        maxima = [exact_scores(hh, slot, u, u) for slot, (u, hh) in enumerate(work)]
        for slot, (u, hh) in enumerate(work):
            exact_softmax_pv(hh, slot, slot, maxima[slot], u, u)
    else:
        trusted = []
        for u in range(units):
            l_min = l_max = l8 = None
            shifts = [first_chunk_scores(hh, u) for hh in range(N_HEADS)]
            for hh in range(N_HEADS + 1):
                l8, l = shifted_softmax_pv(hh if hh < N_HEADS else None, hh - 1 if hh > 0 else None, shifts, u, l8)
                if l is not None:
                    l_min = l if l_min is None else jnp.minimum(l_min, l)
                    l_max = l if l_max is None else jnp.maximum(l_max, l)
            trusted.append(jnp.logical_and(jnp.min(l_min) >= MIN_DENOMINATOR,
                                           jnp.max(l_max) <= MAX_DENOMINATOR))

        for u in range(units):
            @pl.when(jnp.logical_not(trusted[u]))
            def _(u=u):
                def body(hh, carry):
                    exact_softmax_pv(hh, u, 2 * u, exact_scores(hh, u, u, 0), u, 0)
                    return carry

                lax.fori_loop(0, N_HEADS, body, 0)

    for u in range(units):
        o_ref[u * tq:(u + 1) * tq, :] = ot_ref[u].T.astype(BF16)


def _attention(q, ks, vts, key_lens, *, n, after=None):
    t = q.shape[1]
    tq = Q_TILE
    m_tot = sum(key_lens)
    single_pass = m_tot > 2 * KEY_CHUNK
    if single_pass:
        units = ATTENTION_UNITS
        steps = n // (units * tq)
        kv_rows = 1
        s_slots, p_slots = units, 2 * units
    else:
        assert n == tq
        units = min(ATTENTION_UNITS, t // n)
        steps = 1
        kv_rows = units
        s_slots = p_slots = units * N_HEADS
    in_specs = [pl.BlockSpec((N_HEADS, units * tq, HEAD_PAD), lambda bi, i: (0, bi * steps + i, 0))]
    in_specs += [pl.BlockSpec((N_HEADS, kv_rows * m, HEAD_PAD), lambda bi, i: (0, bi, 0)) for m in key_lens]
    in_specs += [pl.BlockSpec((VT_ROWS, kv_rows * m), lambda bi, i: (0, bi)) for m in key_lens]
    args = [q, *ks, *vts]
    if after is not None:
        in_specs.append(pl.BlockSpec((8, after.shape[1]), lambda bi, i: (0, 0)))
        args.append(after)
    return pl.pallas_call(
        functools.partial(_attention_kernel, n_kv=len(ks), single_pass=single_pass, units=units),
        out_shape=jax.ShapeDtypeStruct((t, D_ATTN), BF16),
        grid=(t // (n * kv_rows), steps),
        in_specs=in_specs,
        out_specs=pl.BlockSpec((units * tq, D_ATTN), lambda bi, i: (bi * steps + i, 0)),
        scratch_shapes=[pltpu.VMEM((s_slots, m_tot, tq), F32), pltpu.VMEM((p_slots, m_tot, tq), BF16),
                        pltpu.VMEM((units, D_ATTN, tq), F32)],
        compiler_params=_params(2),
        name=f"attention_{n}",
    )(*args)


def _merge_kernel(x_ref, mod_ref, f_ref, a_ref, w_zf_ref, w_za_ref, w_g_ref, w_f_ref, w_a_ref, w_o_ref,
                  g_ref, y_ref):
    gate = mod_ref[0][:, 2 * D_MODEL:]
    for rows in _sub_tiles(x_ref.shape[0], MERGE_SUB_TILE):
        x = x_ref[rows]
        h = _modulated(x, mod_ref)
        zf = _dot_nt(h, w_zf_ref[...])
        y_f = _dot((f_ref[rows] * (zf * (jnp.tanh(zf) + 1.0))).astype(BF16), w_f_ref[...])
        za = _dot_nt(h, w_za_ref[...])
        y_a = _dot((a_ref[rows] * (za * (jnp.tanh(za) + 1.0))).astype(BF16), w_a_ref[...])
        merged_x2 = ((jnp.tanh(_dot_nt(h, w_g_ref[:D_MODEL])) + 1.0) * y_f
                     + (jnp.tanh(_dot_nt(h, w_g_ref[D_MODEL:])) + 1.0) * y_a)
        out = x + gate * _dot(merged_x2.astype(BF16), w_o_ref[...])
        y_ref[rows] = _rms(out) * g_ref[...]


def _merge(x, mod3, f, attn, w_in_p, w_f, w_a, w_o, g, *, n, latent):
    t = x.shape[0]
    tm, _, mod_row = _token_tiling(t, n, per_sequence=latent)
    tok = lambda w: pl.BlockSpec((tm, w), lambda i: (i, 0))
    return pl.pallas_call(
        _merge_kernel,
        out_shape=jax.ShapeDtypeStruct((t, D_MODEL), F32),
        grid=(t // tm,),
        in_specs=[tok(D_MODEL),
                  pl.BlockSpec((1, 1, 3 * D_MODEL), lambda i: (mod_row(i), 0, 0)),
                  tok(D_FOURIER), tok(D_ATTN),
                  _row_spec(W_BLK, B_ZF), _row_spec(W_BLK, B_ZA), _row_spec(2 * D_MODEL, 1),
                  _const_spec(w_f.shape), _const_spec(w_a.shape), _const_spec(w_o.shape),
                  _const_spec(g.shape)],
        out_specs=tok(D_MODEL),
        compiler_params=_params(1),
        name=f"merge_{n}",
    )(x, mod3, f, attn, w_in_p, w_in_p, w_in_p, w_f, w_a, w_o, g)


def _dft_tables(n):
    norm = 1.0 / math.sqrt(n)
    k = np.arange(n)
    ang = ((k[:, None] * k[None, :]) % n) * (2.0 * math.pi / n)
    return (jnp.asarray(np.cos(ang) * norm, dtype=F32).astype(BF16),
            jnp.asarray(np.sin(ang) * -norm, dtype=F32).astype(BF16))


def _dft_factors(n, rows, n_pos):
    norm = 1.0 / math.sqrt(n)
    period = n // DFT_SPLIT
    pos = np.arange(n_pos)
    k1 = np.arange(-(-rows // DFT_SPLIT))
    k0 = np.arange(DFT_SPLIT)
    a1 = ((k1[:, None] * pos[None, :]) % period) * (2.0 * math.pi / period)
    a0 = ((k0[:, None] * pos[None, :]) % n) * (2.0 * math.pi / n)
    return tuple(a.astype(np.float32) for a in (np.cos(a1), np.sin(a1), np.cos(a0) * norm, np.sin(a0) * norm))


def _channel_dft_table():
    c = np.arange(GROUP)
    ang = ((c[:, None] * c[None, :]) % GROUP) * (2.0 * math.pi / GROUP)
    norm = 1.0 / math.sqrt(GROUP)
    return jnp.asarray(np.concatenate([np.cos(ang), np.sin(ang)], axis=1) * norm, dtype=F32).astype(BF16)


def _rope_tables(n):
    t = np.arange(n)
    row = (t // GRID_W).astype(np.float64)
    col = (t % GRID_W).astype(np.float64)
    half = QK_ROPE // 2
    inv = ROPE_THETA ** (-np.arange(0, half, 2, dtype=np.float64) / half)
    ar, ac = row[:, None] * inv, col[:, None] * inv
    ang = np.concatenate([ar, ar, ac, ac], axis=-1)
    pad = lambda a, fill: np.concatenate(
        [np.full((n, ROPE_LANE), fill), a, np.full((n, HEAD_PAD - ROPE_LANE - QK_ROPE), fill)],
        axis=1).astype(np.float32)
    return pad(np.cos(ang), 1.0), pad(np.sin(ang), 0.0)


def kernel(x_prompt, x_sample, cache_ckv, cache_krope, c, c_ctx, w_ada, b_ada, w_in, q_norm_g, w_uq,
           kv_norm_g, w_ukv, w_f_out, w_a_out, w_out, final_norm_g):
    assert w_in.shape[0] == 1
    b_ctx, n_ctx, _ = x_prompt.shape
    dec_b, n_lat, _ = x_sample.shape
    past = cache_ckv.shape[2]

    w_q3 = w_uq[0].reshape(Q_RANK, N_HEADS, QK_NOPE + QK_ROPE)
    head_pad = lambda a, left: jnp.pad(
        a, ((0, 0), (0, 0), (left, HEAD_PAD - left - a.shape[2]))).reshape(Q_RANK, N_HEADS * HEAD_PAD)
    rp = w_q3[:, :, QK_NOPE:].reshape(Q_RANK, N_HEADS, 2, 2, QK_ROPE // 4)
    w_q_rot = jnp.stack([-rp[:, :, :, 1], rp[:, :, :, 0]], axis=3).reshape(Q_RANK, N_HEADS, QK_ROPE)
    w_uq_p = jnp.concatenate([head_pad(w_q3, 0), head_pad(w_q_rot, ROPE_LANE)], axis=1).astype(BF16)
    w_kv3 = w_ukv[0].reshape(KV_RANK, N_HEADS, QK_NOPE + V_DIM)
    w_knope = jnp.pad(w_kv3[:, :, :QK_NOPE], ((0, 0), (0, 0), (0, HEAD_PAD - QK_NOPE)))
    place = np.zeros((HEAD_PAD, N_HEADS, HEAD_PAD), np.float32)
    for j in range(QK_ROPE):
        place[ROPE_LANE + j, :, ROPE_LANE + j] = 1.0
    w_k = jnp.concatenate([w_knope.reshape(KV_RANK, -1), jnp.asarray(place).reshape(HEAD_PAD, -1)],
                          axis=0).astype(BF16)
    w_uvt = jnp.transpose(w_kv3[:, :, QK_NOPE:], (1, 2, 0)).reshape(VT_ROWS, KV_RANK).astype(BF16)
    w_f = w_f_out[0].astype(BF16)
    w_a = w_a_out[0].astype(BF16)
    w_o = (0.5 * w_out[0]).astype(BF16)
    qg = q_norm_g[0].reshape(1, Q_RANK)
    kvg = kv_norm_g[0].reshape(1, KV_RANK)
    fg = final_norm_g.reshape(1, D_MODEL)
    wts = (qg, kvg, w_uq_p, w_k, w_uvt, _channel_dft_table())
    w_in_p = _pack_w_in(jnp.swapaxes(w_in[0], 0, 1))

    cond8 = jnp.concatenate([c_ctx[None, :], c, jnp.zeros((8 - 1 - dec_b, D_MODEL), F32)], axis=0)
    mod3 = _adaln(cond8, w_ada[0], b_ada[0].reshape(1, -1))

    xp = x_prompt.reshape(b_ctx * n_ctx, D_MODEL)
    ucs, q, k, vt, state_ckv, state_krope = _layer_in(xp, mod3, w_in_p, wts, None, n=n_ctx, state=True)
    f = _pos_dft(ucs, n=n_ctx)
    attn = _attention(q, [k], [vt], [n_ctx], n=n_ctx, after=f)
    y_prompt = _merge(xp, mod3, f, attn, w_in_p, w_f, w_a, w_o, fg, n=n_ctx, latent=False)

    xs = x_sample.reshape(dec_b * n_lat, D_MODEL)
    ucs, q, k, vt = _layer_in(xs, mod3, w_in_p, wts, _rope_tables(n_lat), n=n_lat, state=False)
    f = _pos_dft(ucs, n=n_lat)
    krope_pad = jnp.pad(cache_krope[:, 0], ((0, 0), (0, 0), (ROPE_LANE, HEAD_PAD - ROPE_LANE - QK_ROPE)))
    k_c, vt_c = _cache_kv(cache_ckv[:, 0].reshape(dec_b * past, KV_RANK),
                          krope_pad.reshape(dec_b * past, HEAD_PAD), w_k, w_uvt)
    attn = _attention(q, [k, k_c], [vt, vt_c], [n_lat, past], n=n_lat, after=f)
    y_sample = _merge(xs, mod3, f, attn, w_in_p, w_f, w_a, w_o, fg, n=n_lat, latent=True)

    return (y_prompt.reshape(b_ctx, n_ctx, D_MODEL), y_sample.reshape(dec_b, n_lat, D_MODEL),
            state_ckv.reshape(b_ctx, 1, n_ctx, KV_RANK), jnp.swapaxes(state_krope, 1, 2)[:, None])
```

```python
import functools
import math

import jax
import jax.numpy as jnp
import numpy as np
from jax import lax
from jax.experimental import pallas as pl
from jax.experimental.pallas import tpu as pltpu

F32 = jnp.float32
BF16 = jnp.bfloat16

D_MODEL = 1024
GRID_W = 64
N_GROUPS = 4
GROUP = 128
D_FOURIER = N_GROUPS * GROUP
N_HEADS = 8
QK_NOPE = 64
QK_ROPE = 32
V_DIM = 64
Q_RANK = 256
KV_RANK = 128
D_ATTN = N_HEADS * V_DIM
D_IN = 2 * D_FOURIER + Q_RANK + KV_RANK + QK_ROPE + D_ATTN + 2 * D_MODEL
ROPE_THETA = 10000.0
EPS = 1e-6
LANES = 128
HEAD_PAD = LANES
ROPE_LANE = QK_NOPE
V_ROWS = V_DIM + 16
VT_ROWS = N_HEADS * V_ROWS
Q_SCALE = (QK_NOPE + QK_ROPE) ** -0.5 * math.log2(math.e)
KEY_CHUNK = 256
Q_TILE = 256
ATTENTION_UNITS = 4
SHORT_DFT_SEQUENCES = 8
REV_BLOCK = 256
DFT_SPLIT = 64
MIN_DENOMINATOR = 2.0 ** -60
MAX_DENOMINATOR = 2.0 ** 60
W_BLK = 512
B_UF, B_ZF, B_MID, B_ZA = 0, 1, 2, 3
D_IN_PAD = 8 * W_BLK
TOKEN_TILE = 1024
LAYER_IN_SUB_TILE = TOKEN_TILE
MERGE_SUB_TILE = TOKEN_TILE // 2
VMEM_LIMIT_BYTES = 56 * 1024 * 1024


def _const_spec(shape):
    nd = len(shape)
    return pl.BlockSpec(shape, lambda *_: (0,) * nd, pipeline_mode=pl.Buffered(1))


def _row_spec(rows, blk):
    return pl.BlockSpec((rows, D_MODEL), lambda *_: (blk, 0), pipeline_mode=pl.Buffered(1))


def _params(n_axes):
    return pltpu.CompilerParams(dimension_semantics=("arbitrary",) * n_axes,
                                vmem_limit_bytes=VMEM_LIMIT_BYTES)


def _rms(x):
    return x * lax.rsqrt(jnp.mean(x * x, axis=-1, keepdims=True) + EPS)


def _sigmoid(x):
    return 0.5 * jnp.tanh(0.5 * x) + 0.5


def _dot(a, b):
    return jnp.dot(a, b, preferred_element_type=F32)


def _dot_nt(a, b):
    return lax.dot_general(a, b, (((1,), (1,)), ((), ())), preferred_element_type=F32)


def _token_tiling(t, n, *, per_sequence):
    tm = TOKEN_TILE
    assert t % tm == 0 and (not per_sequence or n % tm == 0)
    per_seq = n // tm if per_sequence else None
    mod_row = (lambda i: 1 + i // per_seq) if per_sequence else (lambda i: 0)
    return tm, per_seq, mod_row


def _sub_tiles(rows, step):
    return [slice(r, r + step) for r in range(0, rows, step)]


def _modulated(x, mod_ref):
    mod = mod_ref[0]
    shift, scale = mod[:, 0:D_MODEL], mod[:, D_MODEL:2 * D_MODEL]
    return (_rms(x) * (1.0 + scale) + shift).astype(BF16)


def _adaln_kernel(c_ref, w_ref, b_ref, o_ref):
    c = c_ref[...]
    s = (c * _sigmoid(c)).astype(BF16)
    mod = _dot(s, w_ref[...].astype(BF16)) + b_ref[...]
    for r in range(mod.shape[0]):
        o_ref[r] = mod[r:r + 1]


def _adaln(cond8, w_ada, b_ada):
    n_blk = 4
    bw = 3 * D_MODEL // n_blk
    return pl.pallas_call(
        _adaln_kernel,
        out_shape=jax.ShapeDtypeStruct((8, 1, 3 * D_MODEL), F32),
        grid=(n_blk,),
        in_specs=[pl.BlockSpec((8, D_MODEL), lambda j: (0, 0)),
                  pl.BlockSpec((D_MODEL, bw), lambda j: (0, j)),
                  pl.BlockSpec((1, bw), lambda j: (0, j))],
        out_specs=pl.BlockSpec((8, 1, bw), lambda j: (0, 0, j)),
        compiler_params=_params(1),
        name="adaln",
    )(cond8, w_ada, b_ada)


def _rot_rows(w):
    q = QK_ROPE // 4
    return jnp.concatenate([-w[q:2 * q], w[0:q], -w[3 * q:4 * q], w[2 * q:3 * q]], axis=0)


def _pack_w_in_kernel(wt_ref, o_ref):
    lo = 2 * D_FOURIER + Q_RANK + KV_RANK
    zf = slice(B_ZF * W_BLK, (B_ZF + 1) * W_BLK)
    o_ref[:zf.start] = wt_ref[:zf.start].astype(BF16)
    o_ref[zf] = (0.5 * wt_ref[zf]).astype(BF16)
    o_ref[zf.stop:lo] = wt_ref[zf.stop:lo].astype(BF16)
    kr = wt_ref[lo:lo + QK_ROPE]
    zeros = jnp.zeros((QK_ROPE, D_MODEL), F32)
    o_ref[lo:lo + LANES] = jnp.concatenate([_rot_rows(kr), zeros, kr, zeros], axis=0).astype(BF16)
    o_ref[lo + LANES:] = (0.5 * wt_ref[lo + QK_ROPE:]).astype(BF16)


def _pack_w_in(w_in_t):
    return pl.pallas_call(
        _pack_w_in_kernel,
        out_shape=jax.ShapeDtypeStruct((D_IN_PAD, D_MODEL), BF16),
        grid=(1,),
        in_specs=[_const_spec((D_IN, D_MODEL))],
        out_specs=_const_spec((D_IN_PAD, D_MODEL)),
        compiler_params=_params(1),
        name="pack_w_in",
    )(w_in_t)


def _kv_outputs(ckvn_b, kr_b, w_k_ref, w_uvt_ref, ones_ref, k_ref, vt_ref, rows):
    tm = ckvn_b.shape[0]
    kp = _dot(jnp.concatenate([ckvn_b, kr_b], axis=1), w_k_ref[...])
    for hh in range(N_HEADS):
        k_ref[hh, rows] = kp[:, hh * HEAD_PAD:(hh + 1) * HEAD_PAD].astype(BF16)
    ones = jnp.concatenate([ones_ref[...]] * (tm // LANES), axis=1)
    vt_ref[:, rows] = (_dot_nt(w_uvt_ref[...], ckvn_b) + ones).astype(BF16)


def _layer_in_kernel(*refs, rope, state):
    (x_ref, mod_ref, w_uf_ref, w_mid_ref, qg_ref, kvg_ref, w_uq_ref, w_k_ref, w_uvt_ref, ones_ref,
     cs_ref), refs = refs[:11], refs[11:]
    if rope:
        (cos_ref, sin_ref), refs = refs[:2], refs[2:]
    (ucs_ref, q_ref, k_ref, vt_ref, h_ref), refs = refs[:5], refs[5:]
    if state:
        ckvn_ref, krope_ref = refs

    for rows in _sub_tiles(x_ref.shape[0], LAYER_IN_SUB_TILE):
        h = _modulated(x_ref[rows], mod_ref)
        h_ref[rows] = h

        mid = _dot_nt(h, w_mid_ref[...])
        cq, ckv, kr = mid[:, :Q_RANK], mid[:, Q_RANK:Q_RANK + KV_RANK], mid[:, Q_RANK + KV_RANK:]
        cqn = (_rms(cq) * (qg_ref[...] * Q_SCALE)).astype(BF16)
        ckvn = _rms(ckv) * kvg_ref[...]
        if state:
            ckvn_ref[rows] = ckvn
            n = krope_ref.shape[2]
            kr_t = kr.T[ROPE_LANE:ROPE_LANE + QK_ROPE]
            for s in range(kr.shape[0] // n):
                krope_ref[rows.start // n + s] = kr_t[:, s * n:(s + 1) * n]
        n_q = N_HEADS * HEAD_PAD
        if rope:
            cos, sin = cos_ref[rows], sin_ref[rows]
            kr = kr * cos + pltpu.roll(kr, ROPE_LANE, 1) * sin
            qp = _dot(cqn, w_uq_ref[...])
        else:
            qp = _dot(cqn, w_uq_ref[:, :n_q])
        for hh in range(N_HEADS):
            qh = qp[:, hh * HEAD_PAD:(hh + 1) * HEAD_PAD]
            if rope:
                qh = qh * cos + qp[:, n_q + hh * HEAD_PAD:n_q + (hh + 1) * HEAD_PAD] * sin
            q_ref[hh, rows] = qh.astype(BF16)
        _kv_outputs(ckvn.astype(BF16), kr.astype(BF16), w_k_ref, w_uvt_ref, ones_ref, k_ref, vt_ref, rows)

        u = _dot_nt(h, w_uf_ref[...]).astype(BF16)
        for g in range(N_GROUPS):
            r = _dot(u[:, g * GROUP:(g + 1) * GROUP], cs_ref[...])
            ucs_ref[rows, g * GROUP:(g + 1) * GROUP] = r[:, :GROUP].astype(BF16)
            ucs_ref[rows, D_FOURIER + g * GROUP:D_FOURIER + (g + 1) * GROUP] = r[:, GROUP:].astype(BF16)


def _layer_in(x, mod3, w_in_p, wts, rope_tabs, *, n, state):
    t = x.shape[0]
    rope = rope_tabs is not None
    tm, per_seq, mod_row = _token_tiling(t, n, per_sequence=rope)
    tok = lambda w: pl.BlockSpec((tm, w), lambda i: (i, 0))
    in_specs = [tok(D_MODEL),
                pl.BlockSpec((1, 1, 3 * D_MODEL), lambda i: (mod_row(i), 0, 0)),
                _row_spec(W_BLK, B_UF), _row_spec(W_BLK, B_MID)]
    in_specs += [_const_spec(w.shape) for w in wts]
    args = [x, mod3, w_in_p, w_in_p, *wts]
    if rope:
        in_specs += [pl.BlockSpec((tm, HEAD_PAD), lambda i: (i % per_seq, 0))] * 2
        args += list(rope_tabs)
    head = pl.BlockSpec((N_HEADS, tm, HEAD_PAD), lambda i: (0, i, 0))
    out_specs = [tok(2 * D_FOURIER), head, head, pl.BlockSpec((VT_ROWS, tm), lambda i: (0, i)), tok(D_MODEL)]
    out_shape = [jax.ShapeDtypeStruct((t, 2 * D_FOURIER), BF16),
                 jax.ShapeDtypeStruct((N_HEADS, t, HEAD_PAD), BF16),
                 jax.ShapeDtypeStruct((N_HEADS, t, HEAD_PAD), BF16),
                 jax.ShapeDtypeStruct((VT_ROWS, t), BF16),
                 jax.ShapeDtypeStruct((t, D_MODEL), BF16)]
    if state:
        out_specs += [tok(KV_RANK), pl.BlockSpec((tm // n, QK_ROPE, n), lambda i: (i, 0, 0))]
        out_shape += [jax.ShapeDtypeStruct((t, KV_RANK), F32), jax.ShapeDtypeStruct((t // n, QK_ROPE, n), F32)]
    return pl.pallas_call(
        functools.partial(_layer_in_kernel, rope=rope, state=state),
        out_shape=out_shape,
        grid=(t // tm,),
        in_specs=in_specs,
        out_specs=out_specs,
        compiler_params=_params(1),
        name="layer_in_latent" if rope else "layer_in_context",
    )(*args)


def _cache_kv_kernel(ckv_ref, kr_ref, w_k_ref, w_uvt_ref, ones_ref, k_ref, vt_ref):
    _kv_outputs(ckv_ref[...].astype(BF16), kr_ref[...].astype(BF16), w_k_ref, w_uvt_ref, ones_ref,
                k_ref, vt_ref, slice(None))


def _cache_kv(ckv, krope_pad, w_k, w_uvt, ones_col):
    t = ckv.shape[0]
    out_shape = [jax.ShapeDtypeStruct((N_HEADS, t, HEAD_PAD), BF16), jax.ShapeDtypeStruct((VT_ROWS, t), BF16)]
    return pl.pallas_call(
        _cache_kv_kernel,
        out_shape=out_shape,
        grid=(1,),
        in_specs=[_const_spec(a.shape) for a in (ckv, krope_pad, w_k, w_uvt, ones_col)],
        out_specs=[_const_spec(s.shape) for s in out_shape],
        compiler_params=_params(1),
        name="cache_kv",
    )(ckv, krope_pad, w_k, w_uvt, ones_col)


def _pos_dft_direct_kernel(cn_ref, sn_ref, ucs_ref, f_ref):
    n = cn_ref.shape[0]
    for s in range(f_ref.shape[0] // n):
        rows = slice(s * n, (s + 1) * n)
        f = _dot(cn_ref[...], ucs_ref[rows, :D_FOURIER]) + _dot(sn_ref[...], ucs_ref[rows, D_FOURIER:])
        f_ref[rows] = f.astype(BF16)


def _pos_dft_half_kernel(c1_ref, s1_ref, c0_ref, s0_ref, rev_ref, rev1_ref, ucs_ref, f_ref, cn_ref, sn_ref):
    n = f_ref.shape[0]
    half = n // 2
    blocks = half // REV_BLOCK

    @pl.when(pl.program_id(0) == 0)
    def _():
        c0, s0 = c0_ref[...], s0_ref[...]
        for k1 in range(c1_ref.shape[0]):
            rows = slice(k1 * DFT_SPLIT, min((k1 + 1) * DFT_SPLIT, cn_ref.shape[0]))
            m = rows.stop - rows.start
            c1, s1 = c1_ref[k1:k1 + 1, :], s1_ref[k1:k1 + 1, :]
            cn_ref[rows] = (c1 * c0[:m] - s1 * s0[:m]).astype(BF16)
            sn_ref[rows] = (-(s1 * c0[:m] + c1 * s0[:m])).astype(BF16)

    def folded(cols, sign):
        first_row = lax.broadcasted_iota(jnp.int32, (REV_BLOCK, D_FOURIER), 0) == 0
        out = []
        for b in range(blocks):
            m = _dot(rev1_ref[...], ucs_ref[n - (b + 1) * REV_BLOCK:n - b * REV_BLOCK, cols])
            if b > 0:
                m = jnp.where(first_row, ucs_ref[n - b * REV_BLOCK:n - b * REV_BLOCK + 1, cols].astype(F32), m)
            out.append((ucs_ref[b * REV_BLOCK:(b + 1) * REV_BLOCK, cols] + sign * m).astype(BF16))
        return jnp.concatenate(out, axis=0)

    e = folded(slice(0, D_FOURIER), 1.0)
    o = folded(slice(D_FOURIER, 2 * D_FOURIER), -1.0)
    odd_k = (lax.broadcasted_iota(jnp.int32, (cn_ref.shape[0], D_FOURIER), 0) & 1) == 1
    mid = ucs_ref[half:half + 1, :D_FOURIER].astype(F32) * (1.0 / math.sqrt(n))
    g = _dot(cn_ref[...], e) + jnp.where(odd_k, -mid, mid)
    hn = _dot(sn_ref[...], o)
    f_ref[:half] = (g[:half] + hn[:half]).astype(BF16)
    mirrored = (g[1:half + 1] - hn[1:half + 1]).astype(BF16)
    for a in range(blocks):
        blk = mirrored[(blocks - 1 - a) * REV_BLOCK:(blocks - a) * REV_BLOCK]
        f_ref[half + a * REV_BLOCK:half + (a + 1) * REV_BLOCK] = _dot(rev_ref[...], blk).astype(BF16)


def _pos_dft(ucs, *, n):
    t = ucs.shape[0]
    out_shape = jax.ShapeDtypeStruct((t, D_FOURIER), BF16)
    if n <= REV_BLOCK:
        cn, snn = _dft_tables(n)
        seqs = min(SHORT_DFT_SEQUENCES, t // n)
        return pl.pallas_call(
            _pos_dft_direct_kernel,
            out_shape=out_shape,
            grid=(t // (n * seqs),),
            in_specs=[_const_spec(cn.shape), _const_spec(snn.shape),
                      pl.BlockSpec((n * seqs, 2 * D_FOURIER), lambda i: (i, 0))],
            out_specs=pl.BlockSpec((n * seqs, D_FOURIER), lambda i: (i, 0)),
            compiler_params=_params(1),
            name=f"pos_dft_{n}",
        )(cn, snn, ucs)
    rows = n // 2 + 8
    consts = _dft_factors(n, rows, n // 2)
    anti = np.eye(REV_BLOCK, dtype=np.float32)[::-1]
    shifted = np.roll(anti, 1, axis=1)
    shifted[0] = 0.0
    consts += (jnp.asarray(anti, dtype=BF16), jnp.asarray(shifted, dtype=BF16))
    return pl.pallas_call(
        _pos_dft_half_kernel,
        out_shape=out_shape,
        grid=(t // n,),
        in_specs=[_const_spec(a.shape) for a in consts]
        + [pl.BlockSpec((n, 2 * D_FOURIER), lambda i: (i, 0))],
        out_specs=pl.BlockSpec((n, D_FOURIER), lambda i: (i, 0)),
        scratch_shapes=[pltpu.VMEM((rows, n // 2), BF16), pltpu.VMEM((rows, n // 2), BF16)],
        compiler_params=_params(1),
        name=f"pos_dft_{n}",
    )(*consts, ucs)


def _attention_kernel(*refs, n_kv, single_pass, units):
    q_ref, refs = refs[0], refs[1:]
    k_refs, vt_refs, refs = refs[:n_kv], refs[n_kv:2 * n_kv], refs[2 * n_kv:]
    o_ref, s_ref, pp_ref, ot_ref = refs[-4:]
    tq = q_ref.shape[1] // units
    key_lens = [k_ref.shape[1] // (1 if single_pass else units) for k_ref in k_refs]
    chunks, base = [], 0
    for j, m in enumerate(key_lens):
        chunks += [(j, c, base + c) for c in range(0, m, KEY_CHUNK)]
        base += m

    def q_block(hh, u):
        return q_ref[hh, u * tq:(u + 1) * tq, :]

    def finish(hh, ot, u):
        l = ot[V_DIM:V_DIM + 1]
        ot_ref[u, pl.ds(pl.multiple_of(hh * V_DIM, V_DIM), V_DIM), :] = ot[:V_DIM] / l
        return l

    def first_chunk_scores(hh, u):
        j, c, _ = chunks[0]
        s = _dot_nt(k_refs[j][hh, c:c + KEY_CHUNK, :], q_block(hh, u))
        s_ref[u, hh * KEY_CHUNK:(hh + 1) * KEY_CHUNK, :] = s
        return jnp.max(s, axis=0, keepdims=True)

    def shifted_softmax_pv(score_head, pv_head, shifts, u):
        if score_head is not None:
            qh = q_block(score_head, u)
            shift = shifts[score_head]
        ot = None
        for idx, (j, c, r) in enumerate(chunks):
            if score_head is not None:
                s = (s_ref[u, score_head * KEY_CHUNK:(score_head + 1) * KEY_CHUNK, :] if idx == 0 else
                     _dot_nt(k_refs[j][score_head, c:c + KEY_CHUNK, :], qh))
                pp_ref[2 * u + score_head % 2, r:r + KEY_CHUNK, :] = jnp.exp2(s - shift).astype(BF16)
            if pv_head is not None:
                part = _dot(vt_refs[j][pv_head * V_ROWS:(pv_head + 1) * V_ROWS, c:c + KEY_CHUNK],
                            pp_ref[2 * u + pv_head % 2, r:r + KEY_CHUNK, :])
                ot = part if ot is None else ot + part
        return finish(pv_head, ot, u) if pv_head is not None else None

    def exact_scores(hh, slot, u, kseq):
        qh = q_block(hh, u)
        m8 = None
        for j, c, r in chunks:
            k0 = kseq * key_lens[j] + c
            s = _dot_nt(k_refs[j][hh, k0:k0 + KEY_CHUNK, :], qh)
            s_ref[slot, r:r + KEY_CHUNK, :] = s
            mc = jnp.max(s.reshape(KEY_CHUNK // 8, 8, tq), axis=0)
            m8 = mc if m8 is None else jnp.maximum(m8, mc)
        return jnp.max(m8, axis=0, keepdims=True)

    def exact_softmax_pv(hh, s_slot, p_slot, m, u, kseq):
        for _, _, r in chunks:
            pp_ref[p_slot, r:r + KEY_CHUNK, :] = jnp.exp2(s_ref[s_slot, r:r + KEY_CHUNK, :] - m).astype(BF16)
        rows = pl.ds(pl.multiple_of(hh * V_ROWS, V_ROWS), V_ROWS)
        ot, base = None, 0
        for j, vt_ref in enumerate(vt_refs):
            mk = key_lens[j]
            part = _dot(vt_ref[rows, kseq * mk:(kseq + 1) * mk], pp_ref[p_slot, base:base + mk, :])
            ot = part if ot is None else ot + part
            base += mk
        finish(hh, ot, u)

    if not single_pass:
        work = [(u, hh) for u in range(units) for hh in range(N_HEADS)]
        maxima = [exact_scores(hh, slot, u, u) for slot, (u, hh) in enumerate(work)]
        for slot, (u, hh) in enumerate(work):
            exact_softmax_pv(hh, slot, slot, maxima[slot], u, u)
    else:
        trusted = []
        for u in range(units):
            l_min = l_max = None
            shifts = [first_chunk_scores(hh, u) for hh in range(N_HEADS)]
            for hh in range(N_HEADS + 1):
                l = shifted_softmax_pv(hh if hh < N_HEADS else None, hh - 1 if hh > 0 else None, shifts, u)
                if l is not None:
                    l_min = l if l_min is None else jnp.minimum(l_min, l)
                    l_max = l if l_max is None else jnp.maximum(l_max, l)
            trusted.append(jnp.logical_and(jnp.min(l_min) >= MIN_DENOMINATOR,
                                           jnp.max(l_max) <= MAX_DENOMINATOR))

        for u in range(units):
            @pl.when(jnp.logical_not(trusted[u]))
            def _(u=u):
                def body(hh, carry):
                    exact_softmax_pv(hh, u, 2 * u, exact_scores(hh, u, u, 0), u, 0)
                    return carry

                lax.fori_loop(0, N_HEADS, body, 0)

    for u in range(units):
        o_ref[u * tq:(u + 1) * tq, :] = ot_ref[u].T.astype(BF16)


def _attention(q, ks, vts, key_lens, *, n, after=None):
    t = q.shape[1]
    tq = Q_TILE
    m_tot = sum(key_lens)
    single_pass = m_tot > 2 * KEY_CHUNK
    if single_pass:
        units = ATTENTION_UNITS
        steps = n // (units * tq)
        kv_rows = 1
        s_slots, p_slots = units, 2 * units
    else:
        assert n == tq
        units = min(ATTENTION_UNITS, t // n)
        steps = 1
        kv_rows = units
        s_slots = p_slots = units * N_HEADS
    in_specs = [pl.BlockSpec((N_HEADS, units * tq, HEAD_PAD), lambda bi, i: (0, bi * steps + i, 0))]
    in_specs += [pl.BlockSpec((N_HEADS, kv_rows * m, HEAD_PAD), lambda bi, i: (0, bi, 0)) for m in key_lens]
    in_specs += [pl.BlockSpec((VT_ROWS, kv_rows * m), lambda bi, i: (0, bi)) for m in key_lens]
    args = [q, *ks, *vts]
    if after is not None:
        in_specs.append(pl.BlockSpec((8, after.shape[1]), lambda bi, i: (0, 0)))
        args.append(after)
    return pl.pallas_call(
        functools.partial(_attention_kernel, n_kv=len(ks), single_pass=single_pass, units=units),
        out_shape=jax.ShapeDtypeStruct((t, D_ATTN), BF16),
        grid=(t // (n * kv_rows), steps),
        in_specs=in_specs,
        out_specs=pl.BlockSpec((units * tq, D_ATTN), lambda bi, i: (bi * steps + i, 0)),
        scratch_shapes=[pltpu.VMEM((s_slots, m_tot, tq), F32), pltpu.VMEM((p_slots, m_tot, tq), BF16),
                        pltpu.VMEM((units, D_ATTN, tq), F32)],
        compiler_params=_params(2),
        name=f"attention_{n}",
    )(*args)


def _merge_kernel(x_ref, h_ref, mod_ref, f_ref, a_ref, w_zf_ref, w_za_ref, w_g_ref, w_f_ref, w_a_ref, w_o_ref,
                  g_ref, y_ref):
    gate = mod_ref[0][:, 2 * D_MODEL:]
    for rows in _sub_tiles(x_ref.shape[0], MERGE_SUB_TILE):
        x = x_ref[rows]
        h = h_ref[rows]
        zf = _dot_nt(h, w_zf_ref[...])
        y_f = _dot((f_ref[rows] * (zf * (jnp.tanh(zf) + 1.0))).astype(BF16), w_f_ref[...])
        za = _dot_nt(h, w_za_ref[...])
        y_a = _dot((a_ref[rows] * (za * (jnp.tanh(za) + 1.0))).astype(BF16), w_a_ref[...])
        merged_x2 = ((jnp.tanh(_dot_nt(h, w_g_ref[:D_MODEL])) + 1.0) * y_f
                     + (jnp.tanh(_dot_nt(h, w_g_ref[D_MODEL:])) + 1.0) * y_a)
        out = x + gate * _dot(merged_x2.astype(BF16), w_o_ref[...])
        y_ref[rows] = _rms(out) * g_ref[...]


def _merge(x, h, mod3, f, attn, w_in_p, w_f, w_a, w_o, g, *, n, latent):
    t = x.shape[0]
    tm, _, mod_row = _token_tiling(t, n, per_sequence=latent)
    tok = lambda w: pl.BlockSpec((tm, w), lambda i: (i, 0))
    return pl.pallas_call(
        _merge_kernel,
        out_shape=jax.ShapeDtypeStruct((t, D_MODEL), F32),
        grid=(t // tm,),
        in_specs=[tok(D_MODEL), tok(D_MODEL),
                  pl.BlockSpec((1, 1, 3 * D_MODEL), lambda i: (mod_row(i), 0, 0)),
                  tok(D_FOURIER), tok(D_ATTN),
                  _row_spec(W_BLK, B_ZF), _row_spec(W_BLK, B_ZA), _row_spec(2 * D_MODEL, 1),
                  _const_spec(w_f.shape), _const_spec(w_a.shape), _const_spec(w_o.shape),
                  _const_spec(g.shape)],
        out_specs=tok(D_MODEL),
        compiler_params=_params(1),
        name=f"merge_{n}",
    )(x, h, mod3, f, attn, w_in_p, w_in_p, w_in_p, w_f, w_a, w_o, g)


def _dft_tables(n):
    norm = 1.0 / math.sqrt(n)
    k = np.arange(n)
    ang = ((k[:, None] * k[None, :]) % n) * (2.0 * math.pi / n)
    return (jnp.asarray(np.cos(ang) * norm, dtype=F32).astype(BF16),
            jnp.asarray(np.sin(ang) * -norm, dtype=F32).astype(BF16))


def _dft_factors(n, rows, n_pos):
    norm = 1.0 / math.sqrt(n)
    period = n // DFT_SPLIT
    pos = np.arange(n_pos)
    k1 = np.arange(-(-rows // DFT_SPLIT))
    k0 = np.arange(DFT_SPLIT)
    a1 = ((k1[:, None] * pos[None, :]) % period) * (2.0 * math.pi / period)
    a0 = ((k0[:, None] * pos[None, :]) % n) * (2.0 * math.pi / n)
    return tuple(a.astype(np.float32) for a in (np.cos(a1), np.sin(a1), np.cos(a0) * norm, np.sin(a0) * norm))


def _channel_dft_table():
    c = np.arange(GROUP)
    ang = ((c[:, None] * c[None, :]) % GROUP) * (2.0 * math.pi / GROUP)
    norm = 1.0 / math.sqrt(GROUP)
    return jnp.asarray(np.concatenate([np.cos(ang), np.sin(ang)], axis=1) * norm, dtype=F32).astype(BF16)


def _rope_tables(n):
    t = np.arange(n)
    row = (t // GRID_W).astype(np.float64)
    col = (t % GRID_W).astype(np.float64)
    half = QK_ROPE // 2
    inv = ROPE_THETA ** (-np.arange(0, half, 2, dtype=np.float64) / half)
    ar, ac = row[:, None] * inv, col[:, None] * inv
    ang = np.concatenate([ar, ar, ac, ac], axis=-1)
    pad = lambda a, fill: np.concatenate(
        [np.full((n, ROPE_LANE), fill), a, np.full((n, HEAD_PAD - ROPE_LANE - QK_ROPE), fill)],
        axis=1).astype(np.float32)
    return pad(np.cos(ang), 1.0), pad(np.sin(ang), 0.0)


def kernel(x_prompt, x_sample, cache_ckv, cache_krope, c, c_ctx, w_ada, b_ada, w_in, q_norm_g, w_uq,
           kv_norm_g, w_ukv, w_f_out, w_a_out, w_out, final_norm_g):
    assert w_in.shape[0] == 1
    b_ctx, n_ctx, _ = x_prompt.shape
    dec_b, n_lat, _ = x_sample.shape
    past = cache_ckv.shape[2]

    w_q3 = w_uq[0].reshape(Q_RANK, N_HEADS, QK_NOPE + QK_ROPE)
    head_pad = lambda a, left: jnp.pad(
        a, ((0, 0), (0, 0), (left, HEAD_PAD - left - a.shape[2]))).reshape(Q_RANK, N_HEADS * HEAD_PAD)
    rp = w_q3[:, :, QK_NOPE:].reshape(Q_RANK, N_HEADS, 2, 2, QK_ROPE // 4)
    w_q_rot = jnp.stack([-rp[:, :, :, 1], rp[:, :, :, 0]], axis=3).reshape(Q_RANK, N_HEADS, QK_ROPE)
    w_uq_p = jnp.concatenate([head_pad(w_q3, 0), head_pad(w_q_rot, ROPE_LANE)], axis=1).astype(BF16)
    w_kv3 = w_ukv[0].reshape(KV_RANK, N_HEADS, QK_NOPE + V_DIM)
    w_knope = jnp.pad(w_kv3[:, :, :QK_NOPE], ((0, 0), (0, 0), (0, HEAD_PAD - QK_NOPE)))
    place = np.zeros((HEAD_PAD, N_HEADS, HEAD_PAD), np.float32)
    for j in range(QK_ROPE):
        place[ROPE_LANE + j, :, ROPE_LANE + j] = 1.0
    w_k = jnp.concatenate([w_knope.reshape(KV_RANK, -1), jnp.asarray(place).reshape(HEAD_PAD, -1)],
                          axis=0).astype(BF16)
    w_uvt = jnp.pad(jnp.transpose(w_kv3[:, :, QK_NOPE:], (1, 2, 0)),
                    ((0, 0), (0, V_ROWS - V_DIM), (0, 0))).reshape(VT_ROWS, KV_RANK).astype(BF16)
    ones_col = np.zeros((N_HEADS, V_ROWS, LANES), np.float32)
    ones_col[:, V_DIM, :] = 1.0
    ones_col = jnp.asarray(ones_col.reshape(VT_ROWS, LANES))
    w_f = w_f_out[0].astype(BF16)
    w_a = w_a_out[0].astype(BF16)
    w_o = (0.5 * w_out[0]).astype(BF16)
    qg = q_norm_g[0].reshape(1, Q_RANK)
    kvg = kv_norm_g[0].reshape(1, KV_RANK)
    fg = final_norm_g.reshape(1, D_MODEL)
    wts = (qg, kvg, w_uq_p, w_k, w_uvt, ones_col, _channel_dft_table())
    w_in_p = _pack_w_in(jnp.swapaxes(w_in[0], 0, 1))

    cond8 = jnp.concatenate([c_ctx[None, :], c, jnp.zeros((8 - 1 - dec_b, D_MODEL), F32)], axis=0)
    mod3 = _adaln(cond8, w_ada[0], b_ada[0].reshape(1, -1))

    xp = x_prompt.reshape(b_ctx * n_ctx, D_MODEL)
    ucs, q, k, vt, h, state_ckv, state_krope = _layer_in(xp, mod3, w_in_p, wts, None, n=n_ctx, state=True)
    f = _pos_dft(ucs, n=n_ctx)
    attn = _attention(q, [k], [vt], [n_ctx], n=n_ctx, after=f)
    y_prompt = _merge(xp, h, mod3, f, attn, w_in_p, w_f, w_a, w_o, fg, n=n_ctx, latent=False)

    xs = x_sample.reshape(dec_b * n_lat, D_MODEL)
    ucs, q, k, vt, h = _layer_in(xs, mod3, w_in_p, wts, _rope_tables(n_lat), n=n_lat, state=False)
    f = _pos_dft(ucs, n=n_lat)
    krope_pad = jnp.pad(cache_krope[:, 0], ((0, 0), (0, 0), (ROPE_LANE, HEAD_PAD - ROPE_LANE - QK_ROPE)))
    k_c, vt_c = _cache_kv(cache_ckv[:, 0].reshape(dec_b * past, KV_RANK),
                          krope_pad.reshape(dec_b * past, HEAD_PAD), w_k, w_uvt, ones_col)
    attn = _attention(q, [k, k_c], [vt, vt_c], [n_lat, past], n=n_lat, after=f)
    y_sample = _merge(xs, h, mod3, f, attn, w_in_p, w_f, w_a, w_o, fg, n=n_lat, latent=True)

    return (y_prompt.reshape(b_ctx, n_ctx, D_MODEL), y_sample.reshape(dec_b, n_lat, D_MODEL),
            state_ckv.reshape(b_ctx, 1, n_ctx, KV_RANK), jnp.swapaxes(state_krope, 1, 2)[:, None])
```

```python
import functools
import math

import jax
import jax.numpy as jnp
import numpy as np
from jax import lax
from jax.experimental import pallas as pl
from jax.experimental.pallas import tpu as pltpu

F32 = jnp.float32
BF16 = jnp.bfloat16

D_MODEL = 1024
GRID_W = 64
N_GROUPS = 4
GROUP = 128
D_FOURIER = N_GROUPS * GROUP
N_HEADS = 8
QK_NOPE = 64
QK_ROPE = 32
V_DIM = 64
Q_RANK = 256
KV_RANK = 128
D_ATTN = N_HEADS * V_DIM
D_IN = 2 * D_FOURIER + Q_RANK + KV_RANK + QK_ROPE + D_ATTN + 2 * D_MODEL
ROPE_THETA = 10000.0
EPS = 1e-6
LANES = 128
HEAD_PAD = LANES
ROPE_LANE = QK_NOPE
V_ROWS = V_DIM + 16
VT_ROWS = N_HEADS * V_ROWS
Q_SCALE = (QK_NOPE + QK_ROPE) ** -0.5 * math.log2(math.e)
KEY_CHUNK = 256
Q_TILE = 256
ATTENTION_UNITS = 4
SHORT_DFT_SEQUENCES = 8
REV_BLOCK = 256
DFT_SPLIT = 64
MIN_DENOMINATOR = 2.0 ** -60
MAX_DENOMINATOR = 2.0 ** 60
W_BLK = 512
B_UF, B_ZF, B_MID, B_ZA = 0, 1, 2, 3
D_IN_PAD = 8 * W_BLK
TOKEN_TILE = 1024
LAYER_IN_SUB_TILE = TOKEN_TILE
MERGE_SUB_TILE = TOKEN_TILE // 2
VMEM_LIMIT_BYTES = 56 * 1024 * 1024


def _const_spec(shape):
    nd = len(shape)
    return pl.BlockSpec(shape, lambda *_: (0,) * nd, pipeline_mode=pl.Buffered(1))


def _row_spec(rows, blk):
    return pl.BlockSpec((rows, D_MODEL), lambda *_: (blk, 0), pipeline_mode=pl.Buffered(1))


def _params(n_axes):
    return pltpu.CompilerParams(dimension_semantics=("arbitrary",) * n_axes,
                                vmem_limit_bytes=VMEM_LIMIT_BYTES)


def _rms(x):
    return x * lax.rsqrt(jnp.mean(x * x, axis=-1, keepdims=True) + EPS)


def _sigmoid(x):
    return 0.5 * jnp.tanh(0.5 * x) + 0.5


def _dot(a, b):
    return jnp.dot(a, b, preferred_element_type=F32)


def _dot_nt(a, b):
    return lax.dot_general(a, b, (((1,), (1,)), ((), ())), preferred_element_type=F32)


def _token_tiling(t, n, *, per_sequence):
    tm = TOKEN_TILE
    assert t % tm == 0 and (not per_sequence or n % tm == 0)
    per_seq = n // tm if per_sequence else None
    mod_row = (lambda i: 1 + i // per_seq) if per_sequence else (lambda i: 0)
    return tm, per_seq, mod_row


def _sub_tiles(rows, step):
    return [slice(r, r + step) for r in range(0, rows, step)]


def _modulated(x, mod_ref):
    mod = mod_ref[0]
    shift, scale = mod[:, 0:D_MODEL], mod[:, D_MODEL:2 * D_MODEL]
    return (_rms(x) * (1.0 + scale) + shift).astype(BF16)


def _adaln_kernel(c_ref, w_ref, b_ref, o_ref):
    c = c_ref[...]
    s = (c * _sigmoid(c)).astype(BF16)
    mod = _dot(s, w_ref[...].astype(BF16)) + b_ref[...]
    for r in range(mod.shape[0]):
        o_ref[r] = mod[r:r + 1]


def _adaln(cond8, w_ada, b_ada):
    n_blk = 8
    bw = 3 * D_MODEL // n_blk
    return pl.pallas_call(
        _adaln_kernel,
        out_shape=jax.ShapeDtypeStruct((8, 1, 3 * D_MODEL), F32),
        grid=(n_blk,),
        in_specs=[pl.BlockSpec((8, D_MODEL), lambda j: (0, 0)),
                  pl.BlockSpec((D_MODEL, bw), lambda j: (0, j)),
                  pl.BlockSpec((1, bw), lambda j: (0, j))],
        out_specs=pl.BlockSpec((8, 1, bw), lambda j: (0, 0, j)),
        compiler_params=_params(1),
        name="adaln",
    )(cond8, w_ada, b_ada)


def _rot_rows(w):
    q = QK_ROPE // 4
    return jnp.concatenate([-w[q:2 * q], w[0:q], -w[3 * q:4 * q], w[2 * q:3 * q]], axis=0)


def _pack_w_in_kernel(wt_ref, o_ref):
    lo = 2 * D_FOURIER + Q_RANK + KV_RANK
    zf = slice(B_ZF * W_BLK, (B_ZF + 1) * W_BLK)
    o_ref[:zf.start] = wt_ref[:zf.start].astype(BF16)
    o_ref[zf] = (0.5 * wt_ref[zf]).astype(BF16)
    o_ref[zf.stop:lo] = wt_ref[zf.stop:lo].astype(BF16)
    kr = wt_ref[lo:lo + QK_ROPE]
    zeros = jnp.zeros((QK_ROPE, D_MODEL), F32)
    o_ref[lo:lo + LANES] = jnp.concatenate([_rot_rows(kr), zeros, kr, zeros], axis=0).astype(BF16)
    o_ref[lo + LANES:] = (0.5 * wt_ref[lo + QK_ROPE:]).astype(BF16)


def _pack_w_in(w_in_t):
    return pl.pallas_call(
        _pack_w_in_kernel,
        out_shape=jax.ShapeDtypeStruct((D_IN_PAD, D_MODEL), BF16),
        grid=(1,),
        in_specs=[_const_spec((D_IN, D_MODEL))],
        out_specs=_const_spec((D_IN_PAD, D_MODEL)),
        compiler_params=_params(1),
        name="pack_w_in",
    )(w_in_t)


def _kv_outputs(ckvn_b, kr_b, w_k_ref, w_uvt_ref, ones_ref, k_ref, vt_ref, rows):
    tm = ckvn_b.shape[0]
    kp = _dot(jnp.concatenate([ckvn_b, kr_b], axis=1), w_k_ref[...])
    for hh in range(N_HEADS):
        k_ref[hh, rows] = kp[:, hh * HEAD_PAD:(hh + 1) * HEAD_PAD].astype(BF16)
    ones = jnp.concatenate([ones_ref[...]] * (tm // LANES), axis=1)
    vt_ref[:, rows] = (_dot_nt(w_uvt_ref[...], ckvn_b) + ones).astype(BF16)


def _layer_in_kernel(*refs, rope, state):
    (x_ref, mod_ref, w_uf_ref, w_mid_ref, qg_ref, kvg_ref, w_uq_ref, w_k_ref, w_uvt_ref, ones_ref,
     cs_ref), refs = refs[:11], refs[11:]
    if rope:
        (cos_ref, sin_ref), refs = refs[:2], refs[2:]
    (ucs_ref, q_ref, k_ref, vt_ref), refs = refs[:4], refs[4:]
    if state:
        ckvn_ref, krope_ref = refs

    for rows in _sub_tiles(x_ref.shape[0], LAYER_IN_SUB_TILE):
        h = _modulated(x_ref[rows], mod_ref)

        mid = _dot_nt(h, w_mid_ref[...])
        cq, ckv, kr = mid[:, :Q_RANK], mid[:, Q_RANK:Q_RANK + KV_RANK], mid[:, Q_RANK + KV_RANK:]
        cqn = (_rms(cq) * (qg_ref[...] * Q_SCALE)).astype(BF16)
        ckvn = _rms(ckv) * kvg_ref[...]
        if state:
            ckvn_ref[rows] = ckvn
            n = krope_ref.shape[2]
            kr_t = kr.T[ROPE_LANE:ROPE_LANE + QK_ROPE]
            for s in range(kr.shape[0] // n):
                krope_ref[rows.start // n + s] = kr_t[:, s * n:(s + 1) * n]
        n_q = N_HEADS * HEAD_PAD
        if rope:
            cos, sin = cos_ref[rows], sin_ref[rows]
            kr = kr * cos + pltpu.roll(kr, ROPE_LANE, 1) * sin
            qp = _dot(cqn, w_uq_ref[...])
        else:
            qp = _dot(cqn, w_uq_ref[:, :n_q])
        for hh in range(N_HEADS):
            qh = qp[:, hh * HEAD_PAD:(hh + 1) * HEAD_PAD]
            if rope:
                qh = qh * cos + qp[:, n_q + hh * HEAD_PAD:n_q + (hh + 1) * HEAD_PAD] * sin
            q_ref[hh, rows] = qh.astype(BF16)
        _kv_outputs(ckvn.astype(BF16), kr.astype(BF16), w_k_ref, w_uvt_ref, ones_ref, k_ref, vt_ref, rows)

        u = _dot_nt(h, w_uf_ref[...]).astype(BF16)
        for g in range(N_GROUPS):
            r = _dot(u[:, g * GROUP:(g + 1) * GROUP], cs_ref[...])
            ucs_ref[rows, g * GROUP:(g + 1) * GROUP] = r[:, :GROUP].astype(BF16)
            ucs_ref[rows, D_FOURIER + g * GROUP:D_FOURIER + (g + 1) * GROUP] = r[:, GROUP:].astype(BF16)


def _layer_in(x, mod3, w_in_p, wts, rope_tabs, *, n, state):
    t = x.shape[0]
    rope = rope_tabs is not None
    tm, per_seq, mod_row = _token_tiling(t, n, per_sequence=rope)
    tok = lambda w: pl.BlockSpec((tm, w), lambda i: (i, 0))
    in_specs = [tok(D_MODEL),
                pl.BlockSpec((1, 1, 3 * D_MODEL), lambda i: (mod_row(i), 0, 0)),
                _row_spec(W_BLK, B_UF), _row_spec(W_BLK, B_MID)]
    in_specs += [_const_spec(w.shape) for w in wts]
    args = [x, mod3, w_in_p, w_in_p, *wts]
    if rope:
        in_specs += [pl.BlockSpec((tm, HEAD_PAD), lambda i: (i % per_seq, 0))] * 2
        args += list(rope_tabs)
    head = pl.BlockSpec((N_HEADS, tm, HEAD_PAD), lambda i: (0, i, 0))
    out_specs = [tok(2 * D_FOURIER), head, head, pl.BlockSpec((VT_ROWS, tm), lambda i: (0, i))]
    out_shape = [jax.ShapeDtypeStruct((t, 2 * D_FOURIER), BF16),
                 jax.ShapeDtypeStruct((N_HEADS, t, HEAD_PAD), BF16),
                 jax.ShapeDtypeStruct((N_HEADS, t, HEAD_PAD), BF16),
                 jax.ShapeDtypeStruct((VT_ROWS, t), BF16)]
    if state:
        out_specs += [tok(KV_RANK), pl.BlockSpec((tm // n, QK_ROPE, n), lambda i: (i, 0, 0))]
        out_shape += [jax.ShapeDtypeStruct((t, KV_RANK), F32), jax.ShapeDtypeStruct((t // n, QK_ROPE, n), F32)]
    return pl.pallas_call(
        functools.partial(_layer_in_kernel, rope=rope, state=state),
        out_shape=out_shape,
        grid=(t // tm,),
        in_specs=in_specs,
        out_specs=out_specs,
        compiler_params=_params(1),
        name="layer_in_latent" if rope else "layer_in_context",
    )(*args)


def _cache_kv_kernel(ckv_ref, kr_ref, w_k_ref, w_uvt_ref, ones_ref, k_ref, vt_ref):
    _kv_outputs(ckv_ref[...].astype(BF16), kr_ref[...].astype(BF16), w_k_ref, w_uvt_ref, ones_ref,
                k_ref, vt_ref, slice(None))


def _cache_kv(ckv, krope_pad, w_k, w_uvt, ones_col):
    t = ckv.shape[0]
    out_shape = [jax.ShapeDtypeStruct((N_HEADS, t, HEAD_PAD), BF16), jax.ShapeDtypeStruct((VT_ROWS, t), BF16)]
    return pl.pallas_call(
        _cache_kv_kernel,
        out_shape=out_shape,
        grid=(1,),
        in_specs=[_const_spec(a.shape) for a in (ckv, krope_pad, w_k, w_uvt, ones_col)],
        out_specs=[_const_spec(s.shape) for s in out_shape],
        compiler_params=_params(1),
        name="cache_kv",
    )(ckv, krope_pad, w_k, w_uvt, ones_col)


def _pos_dft_direct_kernel(cn_ref, sn_ref, ucs_ref, f_ref):
    n = cn_ref.shape[0]
    for s in range(f_ref.shape[0] // n):
        rows = slice(s * n, (s + 1) * n)
        f = _dot(cn_ref[...], ucs_ref[rows, :D_FOURIER]) + _dot(sn_ref[...], ucs_ref[rows, D_FOURIER:])
        f_ref[rows] = f.astype(BF16)


def _pos_dft_half_kernel(c1_ref, s1_ref, c0_ref, s0_ref, rev_ref, rev1_ref, ucs_ref, f_ref, cn_ref, sn_ref):
    n = f_ref.shape[0]
    half = n // 2
    blocks = half // REV_BLOCK

    @pl.when(pl.program_id(0) == 0)
    def _():
        c0, s0 = c0_ref[...], s0_ref[...]
        for k1 in range(c1_ref.shape[0]):
            rows = slice(k1 * DFT_SPLIT, min((k1 + 1) * DFT_SPLIT, cn_ref.shape[0]))
            m = rows.stop - rows.start
            c1, s1 = c1_ref[k1:k1 + 1, :], s1_ref[k1:k1 + 1, :]
            cn_ref[rows] = (c1 * c0[:m] - s1 * s0[:m]).astype(BF16)
            sn_ref[rows] = (-(s1 * c0[:m] + c1 * s0[:m])).astype(BF16)

    def folded(cols, sign):
        first_row = lax.broadcasted_iota(jnp.int32, (REV_BLOCK, D_FOURIER), 0) == 0
        out = []
        for b in range(blocks):
            m = _dot(rev1_ref[...], ucs_ref[n - (b + 1) * REV_BLOCK:n - b * REV_BLOCK, cols])
            if b > 0:
                m = jnp.where(first_row, ucs_ref[n - b * REV_BLOCK:n - b * REV_BLOCK + 1, cols].astype(F32), m)
            out.append((ucs_ref[b * REV_BLOCK:(b + 1) * REV_BLOCK, cols] + sign * m).astype(BF16))
        return jnp.concatenate(out, axis=0)

    e = folded(slice(0, D_FOURIER), 1.0)
    o = folded(slice(D_FOURIER, 2 * D_FOURIER), -1.0)
    odd_k = (lax.broadcasted_iota(jnp.int32, (cn_ref.shape[0], D_FOURIER), 0) & 1) == 1
    mid = ucs_ref[half:half + 1, :D_FOURIER].astype(F32) * (1.0 / math.sqrt(n))
    g = _dot(cn_ref[...], e) + jnp.where(odd_k, -mid, mid)
    hn = _dot(sn_ref[...], o)
    f_ref[:half] = (g[:half] + hn[:half]).astype(BF16)
    mirrored = (g[1:half + 1] - hn[1:half + 1]).astype(BF16)
    for a in range(blocks):
        blk = mirrored[(blocks - 1 - a) * REV_BLOCK:(blocks - a) * REV_BLOCK]
        f_ref[half + a * REV_BLOCK:half + (a + 1) * REV_BLOCK] = _dot(rev_ref[...], blk).astype(BF16)


def _pos_dft(ucs, *, n):
    t = ucs.shape[0]
    out_shape = jax.ShapeDtypeStruct((t, D_FOURIER), BF16)
    if n <= REV_BLOCK:
        cn, snn = _dft_tables(n)
        seqs = min(SHORT_DFT_SEQUENCES, t // n)
        return pl.pallas_call(
            _pos_dft_direct_kernel,
            out_shape=out_shape,
            grid=(t // (n * seqs),),
            in_specs=[_const_spec(cn.shape), _const_spec(snn.shape),
                      pl.BlockSpec((n * seqs, 2 * D_FOURIER), lambda i: (i, 0))],
            out_specs=pl.BlockSpec((n * seqs, D_FOURIER), lambda i: (i, 0)),
            compiler_params=_params(1),
            name=f"pos_dft_{n}",
        )(cn, snn, ucs)
    rows = n // 2 + 8
    consts = _dft_factors(n, rows, n // 2)
    anti = np.eye(REV_BLOCK, dtype=np.float32)[::-1]
    shifted = np.roll(anti, 1, axis=1)
    shifted[0] = 0.0
    consts += (jnp.asarray(anti, dtype=BF16), jnp.asarray(shifted, dtype=BF16))
    return pl.pallas_call(
        _pos_dft_half_kernel,
        out_shape=out_shape,
        grid=(t // n,),
        in_specs=[_const_spec(a.shape) for a in consts]
        + [pl.BlockSpec((n, 2 * D_FOURIER), lambda i: (i, 0))],
        out_specs=pl.BlockSpec((n, D_FOURIER), lambda i: (i, 0)),
        scratch_shapes=[pltpu.VMEM((rows, n // 2), BF16), pltpu.VMEM((rows, n // 2), BF16)],
        compiler_params=_params(1),
        name=f"pos_dft_{n}",
    )(*consts, ucs)


def _attention_kernel(*refs, n_kv, single_pass, units):
    q_ref, refs = refs[0], refs[1:]
    k_refs, vt_refs, refs = refs[:n_kv], refs[n_kv:2 * n_kv], refs[2 * n_kv:]
    o_ref, s_ref, pp_ref, ot_ref = refs[-4:]
    tq = q_ref.shape[1] // units
    key_lens = [k_ref.shape[1] // (1 if single_pass else units) for k_ref in k_refs]
    chunks, base = [], 0
    for j, m in enumerate(key_lens):
        chunks += [(j, c, base + c) for c in range(0, m, KEY_CHUNK)]
        base += m

    def q_block(hh, u):
        return q_ref[hh, u * tq:(u + 1) * tq, :]

    def finish(hh, ot, u):
        l = ot[V_DIM:V_DIM + 1]
        ot_ref[u, pl.ds(pl.multiple_of(hh * V_DIM, V_DIM), V_DIM), :] = ot[:V_DIM] / l
        return l

    def first_chunk_scores(hh, u):
        j, c, _ = chunks[0]
        s = _dot_nt(k_refs[j][hh, c:c + KEY_CHUNK, :], q_block(hh, u))
        s_ref[u, hh * KEY_CHUNK:(hh + 1) * KEY_CHUNK, :] = s
        return jnp.max(s, axis=0, keepdims=True)

    def shifted_softmax_pv(score_head, pv_head, shifts, u):
        if score_head is not None:
            qh = q_block(score_head, u)
            shift = shifts[score_head]
        ot = None
        for idx, (j, c, r) in enumerate(chunks):
            if score_head is not None:
                s = (s_ref[u, score_head * KEY_CHUNK:(score_head + 1) * KEY_CHUNK, :] if idx == 0 else
                     _dot_nt(k_refs[j][score_head, c:c + KEY_CHUNK, :], qh))
                pp_ref[2 * u + score_head % 2, r:r + KEY_CHUNK, :] = jnp.exp2(s - shift).astype(BF16)
            if pv_head is not None:
                part = _dot(vt_refs[j][pv_head * V_ROWS:(pv_head + 1) * V_ROWS, c:c + KEY_CHUNK],
                            pp_ref[2 * u + pv_head % 2, r:r + KEY_CHUNK, :])
                ot = part if ot is None else ot + part
        return finish(pv_head, ot, u) if pv_head is not None else None

    def exact_scores(hh, slot, u, kseq):
        qh = q_block(hh, u)
        m8 = None
        for j, c, r in chunks:
            k0 = kseq * key_lens[j] + c
            s = _dot_nt(k_refs[j][hh, k0:k0 + KEY_CHUNK, :], qh)
            s_ref[slot, r:r + KEY_CHUNK, :] = s
            mc = jnp.max(s.reshape(KEY_CHUNK // 8, 8, tq), axis=0)
            m8 = mc if m8 is None else jnp.maximum(m8, mc)
        return jnp.max(m8, axis=0, keepdims=True)

    def exact_softmax_pv(hh, s_slot, p_slot, m, u, kseq):
        for _, _, r in chunks:
            pp_ref[p_slot, r:r + KEY_CHUNK, :] = jnp.exp2(s_ref[s_slot, r:r + KEY_CHUNK, :] - m).astype(BF16)
        rows = pl.ds(pl.multiple_of(hh * V_ROWS, V_ROWS), V_ROWS)
        ot, base = None, 0
        for j, vt_ref in enumerate(vt_refs):
            mk = key_lens[j]
            part = _dot(vt_ref[rows, kseq * mk:(kseq + 1) * mk], pp_ref[p_slot, base:base + mk, :])
            ot = part if ot is None else ot + part
            base += mk
        finish(hh, ot, u)

    if not single_pass:
        work = [(u, hh) for u in range(units) for hh in range(N_HEADS)]
        maxima = [exact_scores(hh, slot, u, u) for slot, (u, hh) in enumerate(work)]
        for slot, (u, hh) in enumerate(work):
            exact_softmax_pv(hh, slot, slot, maxima[slot], u, u)
    else:
        trusted = []
        for u in range(units):
            l_min = l_max = None
            shifts = [first_chunk_scores(hh, u) for hh in range(N_HEADS)]
            for hh in range(N_HEADS + 1):
                l = shifted_softmax_pv(hh if hh < N_HEADS else None, hh - 1 if hh > 0 else None, shifts, u)
                if l is not None:
                    l_min = l if l_min is None else jnp.minimum(l_min, l)
                    l_max = l if l_max is None else jnp.maximum(l_max, l)
            trusted.append(jnp.logical_and(jnp.min(l_min) >= MIN_DENOMINATOR,
                                           jnp.max(l_max) <= MAX_DENOMINATOR))

        for u in range(units):
            @pl.when(jnp.logical_not(trusted[u]))
            def _(u=u):
                def body(hh, carry):
                    exact_softmax_pv(hh, u, 2 * u, exact_scores(hh, u, u, 0), u, 0)
                    return carry

                lax.fori_loop(0, N_HEADS, body, 0)

    for u in range(units):
        o_ref[u * tq:(u + 1) * tq, :] = ot_ref[u].T.astype(BF16)


def _attention(q, ks, vts, key_lens, *, n, after=None):
    t = q.shape[1]
    tq = Q_TILE
    m_tot = sum(key_lens)
    single_pass = m_tot > 2 * KEY_CHUNK
    if single_pass:
        units = ATTENTION_UNITS
        steps = n // (units * tq)
        kv_rows = 1
        s_slots, p_slots = units, 2 * units
    else:
        assert n == tq
        units = min(ATTENTION_UNITS, t // n)
        steps = 1
        kv_rows = units
        s_slots = p_slots = units * N_HEADS
    in_specs = [pl.BlockSpec((N_HEADS, units * tq, HEAD_PAD), lambda bi, i: (0, bi * steps + i, 0))]
    in_specs += [pl.BlockSpec((N_HEADS, kv_rows * m, HEAD_PAD), lambda bi, i: (0, bi, 0)) for m in key_lens]
    in_specs += [pl.BlockSpec((VT_ROWS, kv_rows * m), lambda bi, i: (0, bi)) for m in key_lens]
    args = [q, *ks, *vts]
    if after is not None:
        in_specs.append(pl.BlockSpec((8, after.shape[1]), lambda bi, i: (0, 0)))
        args.append(after)
    return pl.pallas_call(
        functools.partial(_attention_kernel, n_kv=len(ks), single_pass=single_pass, units=units),
        out_shape=jax.ShapeDtypeStruct((t, D_ATTN), BF16),
        grid=(t // (n * kv_rows), steps),
        in_specs=in_specs,
        out_specs=pl.BlockSpec((units * tq, D_ATTN), lambda bi, i: (bi * steps + i, 0)),
        scratch_shapes=[pltpu.VMEM((s_slots, m_tot, tq), F32), pltpu.VMEM((p_slots, m_tot, tq), BF16),
                        pltpu.VMEM((units, D_ATTN, tq), F32)],
        compiler_params=_params(2),
        name=f"attention_{n}",
    )(*args)


def _merge_kernel(x_ref, mod_ref, f_ref, a_ref, w_zf_ref, w_za_ref, w_g_ref, w_f_ref, w_a_ref, w_o_ref,
                  g_ref, y_ref):
    gate = mod_ref[0][:, 2 * D_MODEL:]
    for rows in _sub_tiles(x_ref.shape[0], MERGE_SUB_TILE):
        x = x_ref[rows]
        h = _modulated(x, mod_ref)
        zf = _dot_nt(h, w_zf_ref[...])
        y_f = _dot(f_ref[rows] * (zf * (jnp.tanh(zf) + 1.0)).astype(BF16), w_f_ref[...])
        za = _dot_nt(h, w_za_ref[...])
        y_a = _dot(a_ref[rows] * (za * (jnp.tanh(za) + 1.0)).astype(BF16), w_a_ref[...])
        merged_x2 = ((jnp.tanh(_dot_nt(h, w_g_ref[:D_MODEL])) + 1.0) * y_f
                     + (jnp.tanh(_dot_nt(h, w_g_ref[D_MODEL:])) + 1.0) * y_a)
        out = x + gate * _dot(merged_x2.astype(BF16), w_o_ref[...])
        y_ref[rows] = _rms(out) * g_ref[...]


def _merge(x, mod3, f, attn, w_in_p, w_f, w_a, w_o, g, *, n, latent):
    t = x.shape[0]
    tm, _, mod_row = _token_tiling(t, n, per_sequence=latent)
    tok = lambda w: pl.BlockSpec((tm, w), lambda i: (i, 0))
    return pl.pallas_call(
        _merge_kernel,
        out_shape=jax.ShapeDtypeStruct((t, D_MODEL), F32),
        grid=(t // tm,),
        in_specs=[tok(D_MODEL),
                  pl.BlockSpec((1, 1, 3 * D_MODEL), lambda i: (mod_row(i), 0, 0)),
                  tok(D_FOURIER), tok(D_ATTN),
                  _row_spec(W_BLK, B_ZF), _row_spec(W_BLK, B_ZA), _row_spec(2 * D_MODEL, 1),
                  _const_spec(w_f.shape), _const_spec(w_a.shape), _const_spec(w_o.shape),
                  _const_spec(g.shape)],
        out_specs=tok(D_MODEL),
        compiler_params=_params(1),
        name=f"merge_{n}",
    )(x, mod3, f, attn, w_in_p, w_in_p, w_in_p, w_f, w_a, w_o, g)


def _dft_tables(n):
    norm = 1.0 / math.sqrt(n)
    k = np.arange(n)
    ang = ((k[:, None] * k[None, :]) % n) * (2.0 * math.pi / n)
    return (jnp.asarray(np.cos(ang) * norm, dtype=F32).astype(BF16),
            jnp.asarray(np.sin(ang) * -norm, dtype=F32).astype(BF16))


def _dft_factors(n, rows, n_pos):
    norm = 1.0 / math.sqrt(n)
    period = n // DFT_SPLIT
    pos = np.arange(n_pos)
    k1 = np.arange(-(-rows // DFT_SPLIT))
    k0 = np.arange(DFT_SPLIT)
    a1 = ((k1[:, None] * pos[None, :]) % period) * (2.0 * math.pi / period)
    a0 = ((k0[:, None] * pos[None, :]) % n) * (2.0 * math.pi / n)
    return tuple(a.astype(np.float32) for a in (np.cos(a1), np.sin(a1), np.cos(a0) * norm, np.sin(a0) * norm))


def _channel_dft_table():
    c = np.arange(GROUP)
    ang = ((c[:, None] * c[None, :]) % GROUP) * (2.0 * math.pi / GROUP)
    norm = 1.0 / math.sqrt(GROUP)
    return jnp.asarray(np.concatenate([np.cos(ang), np.sin(ang)], axis=1) * norm, dtype=F32).astype(BF16)


def _rope_tables(n):
    t = np.arange(n)
    row = (t // GRID_W).astype(np.float64)
    col = (t % GRID_W).astype(np.float64)
    half = QK_ROPE // 2
    inv = ROPE_THETA ** (-np.arange(0, half, 2, dtype=np.float64) / half)
    ar, ac = row[:, None] * inv, col[:, None] * inv
    ang = np.concatenate([ar, ar, ac, ac], axis=-1)
    pad = lambda a, fill: np.concatenate(
        [np.full((n, ROPE_LANE), fill), a, np.full((n, HEAD_PAD - ROPE_LANE - QK_ROPE), fill)],
        axis=1).astype(np.float32)
    return pad(np.cos(ang), 1.0), pad(np.sin(ang), 0.0)


def kernel(x_prompt, x_sample, cache_ckv, cache_krope, c, c_ctx, w_ada, b_ada, w_in, q_norm_g, w_uq,
           kv_norm_g, w_ukv, w_f_out, w_a_out, w_out, final_norm_g):
    assert w_in.shape[0] == 1
    b_ctx, n_ctx, _ = x_prompt.shape
    dec_b, n_lat, _ = x_sample.shape
    past = cache_ckv.shape[2]

    w_q3 = w_uq[0].reshape(Q_RANK, N_HEADS, QK_NOPE + QK_ROPE)
    head_pad = lambda a, left: jnp.pad(
        a, ((0, 0), (0, 0), (left, HEAD_PAD - left - a.shape[2]))).reshape(Q_RANK, N_HEADS * HEAD_PAD)
    rp = w_q3[:, :, QK_NOPE:].reshape(Q_RANK, N_HEADS, 2, 2, QK_ROPE // 4)
    w_q_rot = jnp.stack([-rp[:, :, :, 1], rp[:, :, :, 0]], axis=3).reshape(Q_RANK, N_HEADS, QK_ROPE)
    w_uq_p = jnp.concatenate([head_pad(w_q3, 0), head_pad(w_q_rot, ROPE_LANE)], axis=1).astype(BF16)
    w_kv3 = w_ukv[0].reshape(KV_RANK, N_HEADS, QK_NOPE + V_DIM)
    w_knope = jnp.pad(w_kv3[:, :, :QK_NOPE], ((0, 0), (0, 0), (0, HEAD_PAD - QK_NOPE)))
    place = np.zeros((HEAD_PAD, N_HEADS, HEAD_PAD), np.float32)
    for j in range(QK_ROPE):
        place[ROPE_LANE + j, :, ROPE_LANE + j] = 1.0
    w_k = jnp.concatenate([w_knope.reshape(KV_RANK, -1), jnp.asarray(place).reshape(HEAD_PAD, -1)],
                          axis=0).astype(BF16)
    w_uvt = jnp.pad(jnp.transpose(w_kv3[:, :, QK_NOPE:], (1, 2, 0)),
                    ((0, 0), (0, V_ROWS - V_DIM), (0, 0))).reshape(VT_ROWS, KV_RANK).astype(BF16)
    ones_col = np.zeros((N_HEADS, V_ROWS, LANES), np.float32)
    ones_col[:, V_DIM, :] = 1.0
    ones_col = jnp.asarray(ones_col.reshape(VT_ROWS, LANES))
    w_f = w_f_out[0].astype(BF16)
    w_a = w_a_out[0].astype(BF16)
    w_o = (0.5 * w_out[0]).astype(BF16)
    qg = q_norm_g[0].reshape(1, Q_RANK)
    kvg = kv_norm_g[0].reshape(1, KV_RANK)
    fg = final_norm_g.reshape(1, D_MODEL)
    wts = (qg, kvg, w_uq_p, w_k, w_uvt, ones_col, _channel_dft_table())
    w_in_p = _pack_w_in(jnp.swapaxes(w_in[0], 0, 1))

    cond8 = jnp.concatenate([c_ctx[None, :], c, jnp.zeros((8 - 1 - dec_b, D_MODEL), F32)], axis=0)
    mod3 = _adaln(cond8, w_ada[0], b_ada[0].reshape(1, -1))

    xp = x_prompt.reshape(b_ctx * n_ctx, D_MODEL)
    ucs, q, k, vt, state_ckv, state_krope = _layer_in(xp, mod3, w_in_p, wts, None, n=n_ctx, state=True)
    f = _pos_dft(ucs, n=n_ctx)
    attn = _attention(q, [k], [vt], [n_ctx], n=n_ctx, after=f)
    y_prompt = _merge(xp, mod3, f, attn, w_in_p, w_f, w_a, w_o, fg, n=n_ctx, latent=False)

    xs = x_sample.reshape(dec_b * n_lat, D_MODEL)
    ucs, q, k, vt = _layer_in(xs, mod3, w_in_p, wts, _rope_tables(n_lat), n=n_lat, state=False)
    f = _pos_dft(ucs, n=n_lat)
    krope_pad = jnp.pad(cache_krope[:, 0], ((0, 0), (0, 0), (ROPE_LANE, HEAD_PAD - ROPE_LANE - QK_ROPE)))
    k_c, vt_c = _cache_kv(cache_ckv[:, 0].reshape(dec_b * past, KV_RANK),
                          krope_pad.reshape(dec_b * past, HEAD_PAD), w_k, w_uvt, ones_col)
    attn = _attention(q, [k, k_c], [vt, vt_c], [n_lat, past], n=n_lat, after=f)
    y_sample = _merge(xs, mod3, f, attn, w_in_p, w_f, w_a, w_o, fg, n=n_lat, latent=True)

    return (y_prompt.reshape(b_ctx, n_ctx, D_MODEL), y_sample.reshape(dec_b, n_lat, D_MODEL),
            state_ckv.reshape(b_ctx, 1, n_ctx, KV_RANK), jnp.swapaxes(state_krope, 1, 2)[:, None])
```

```python
import functools
import math

import jax
import jax.numpy as jnp
import numpy as np
from jax import lax
from jax.experimental import pallas as pl
from jax.experimental.pallas import tpu as pltpu

F32 = jnp.float32
BF16 = jnp.bfloat16

D_MODEL = 1024
GRID_W = 64
N_GROUPS = 4
GROUP = 128
D_FOURIER = N_GROUPS * GROUP
N_HEADS = 8
QK_NOPE = 64
QK_ROPE = 32
V_DIM = 64
Q_RANK = 256
KV_RANK = 128
D_ATTN = N_HEADS * V_DIM
D_IN = 2 * D_FOURIER + Q_RANK + KV_RANK + QK_ROPE + D_ATTN + 2 * D_MODEL
ROPE_THETA = 10000.0
EPS = 1e-6
LANES = 128
HEAD_PAD = LANES
ROPE_LANE = QK_NOPE
V_ROWS = V_DIM + 16
VT_ROWS = N_HEADS * V_ROWS
Q_SCALE = (QK_NOPE + QK_ROPE) ** -0.5 * math.log2(math.e)
KEY_CHUNK = 256
Q_TILE = 256
ATTENTION_UNITS = 4
SHORT_DFT_SEQUENCES = 8
REV_BLOCK = 256
DFT_SPLIT = 64
MIN_DENOMINATOR = 2.0 ** -60
MAX_DENOMINATOR = 2.0 ** 60
W_BLK = 512
B_UF, B_ZF, B_MID, B_ZA = 0, 1, 2, 3
D_IN_PAD = 8 * W_BLK
TOKEN_TILE = 1024
LAYER_IN_SUB_TILE = TOKEN_TILE
MERGE_SUB_TILE = TOKEN_TILE // 2
VMEM_LIMIT_BYTES = 56 * 1024 * 1024


def _const_spec(shape):
    nd = len(shape)
    return pl.BlockSpec(shape, lambda *_: (0,) * nd, pipeline_mode=pl.Buffered(1))


def _row_spec(rows, blk):
    return pl.BlockSpec((rows, D_MODEL), lambda *_: (blk, 0), pipeline_mode=pl.Buffered(1))


def _params(n_axes):
    return pltpu.CompilerParams(dimension_semantics=("arbitrary",) * n_axes,
                                vmem_limit_bytes=VMEM_LIMIT_BYTES)


def _rms(x):
    return x * lax.rsqrt(jnp.mean(x * x, axis=-1, keepdims=True) + EPS)


def _sigmoid(x):
    return 0.5 * jnp.tanh(0.5 * x) + 0.5


def _dot(a, b):
    return jnp.dot(a, b, preferred_element_type=F32)


def _dot_nt(a, b):
    return lax.dot_general(a, b, (((1,), (1,)), ((), ())), preferred_element_type=F32)


def _token_tiling(t, n, *, per_sequence):
    tm = TOKEN_TILE
    assert t % tm == 0 and (not per_sequence or n % tm == 0)
    per_seq = n // tm if per_sequence else None
    mod_row = (lambda i: 1 + i // per_seq) if per_sequence else (lambda i: 0)
    return tm, per_seq, mod_row


def _sub_tiles(rows, step):
    return [slice(r, r + step) for r in range(0, rows, step)]


def _modulated(x, mod_ref):
    mod = mod_ref[0]
    shift, scale = mod[:, 0:D_MODEL], mod[:, D_MODEL:2 * D_MODEL]
    return (_rms(x) * (1.0 + scale) + shift).astype(BF16)


def _adaln_kernel(c_ref, w_ref, b_ref, o_ref):
    c = c_ref[...]
    s = (c * _sigmoid(c)).astype(BF16)
    mod = _dot(s, w_ref[...].astype(BF16)) + b_ref[...]
    for r in range(mod.shape[0]):
        o_ref[r] = mod[r:r + 1]


def _adaln(cond8, w_ada, b_ada):
    n_blk = 4
    bw = 3 * D_MODEL // n_blk
    return pl.pallas_call(
        _adaln_kernel,
        out_shape=jax.ShapeDtypeStruct((8, 1, 3 * D_MODEL), F32),
        grid=(n_blk,),
        in_specs=[pl.BlockSpec((8, D_MODEL), lambda j: (0, 0)),
                  pl.BlockSpec((D_MODEL, bw), lambda j: (0, j)),
                  pl.BlockSpec((1, bw), lambda j: (0, j))],
        out_specs=pl.BlockSpec((8, 1, bw), lambda j: (0, 0, j)),
        compiler_params=_params(1),
        name="adaln",
    )(cond8, w_ada, b_ada)


def _rot_rows(w):
    q = QK_ROPE // 4
    return jnp.concatenate([-w[q:2 * q], w[0:q], -w[3 * q:4 * q], w[2 * q:3 * q]], axis=0)


def _pack_w_in_kernel(wt_ref, o_ref):
    lo = 2 * D_FOURIER + Q_RANK + KV_RANK
    zf = slice(B_ZF * W_BLK, (B_ZF + 1) * W_BLK)
    o_ref[:zf.start] = wt_ref[:zf.start].astype(BF16)
    o_ref[zf] = (0.5 * wt_ref[zf]).astype(BF16)
    o_ref[zf.stop:lo] = wt_ref[zf.stop:lo].astype(BF16)
    kr = wt_ref[lo:lo + QK_ROPE]
    zeros = jnp.zeros((QK_ROPE, D_MODEL), F32)
    o_ref[lo:lo + LANES] = jnp.concatenate([_rot_rows(kr), zeros, kr, zeros], axis=0).astype(BF16)
    o_ref[lo + LANES:] = (0.5 * wt_ref[lo + QK_ROPE:]).astype(BF16)


def _pack_w_in(w_in_t):
    return pl.pallas_call(
        _pack_w_in_kernel,
        out_shape=jax.ShapeDtypeStruct((D_IN_PAD, D_MODEL), BF16),
        grid=(1,),
        in_specs=[_const_spec((D_IN, D_MODEL))],
        out_specs=_const_spec((D_IN_PAD, D_MODEL)),
        compiler_params=_params(1),
        name="pack_w_in",
    )(w_in_t)


def _kv_outputs(ckvn_b, kr_b, w_k_ref, w_uvt_ref, ones_ref, k_ref, vt_ref, rows):
    tm = ckvn_b.shape[0]
    kp = _dot(jnp.concatenate([ckvn_b, kr_b], axis=1), w_k_ref[...])
    for hh in range(N_HEADS):
        k_ref[hh, rows] = kp[:, hh * HEAD_PAD:(hh + 1) * HEAD_PAD].astype(BF16)
    ones = jnp.concatenate([ones_ref[...]] * (tm // LANES), axis=1)
    vt_ref[:, rows] = (_dot_nt(w_uvt_ref[...], ckvn_b) + ones).astype(BF16)


def _layer_in_kernel(*refs, rope, state):
    (x_ref, mod_ref, w_uf_ref, w_mid_ref, qg_ref, kvg_ref, w_uq_ref, w_k_ref, w_uvt_ref, ones_ref,
     cs_ref), refs = refs[:11], refs[11:]
    if rope:
        (cos_ref, sin_ref), refs = refs[:2], refs[2:]
    (ucs_ref, q_ref, k_ref, vt_ref), refs = refs[:4], refs[4:]
    if state:
        ckvn_ref, krope_ref = refs

    for rows in _sub_tiles(x_ref.shape[0], LAYER_IN_SUB_TILE):
        h = _modulated(x_ref[rows], mod_ref)

        mid = _dot_nt(h, w_mid_ref[...])
        cq, ckv, kr = mid[:, :Q_RANK], mid[:, Q_RANK:Q_RANK + KV_RANK], mid[:, Q_RANK + KV_RANK:]
        cqn = (_rms(cq) * (qg_ref[...] * Q_SCALE)).astype(BF16)
        ckvn = _rms(ckv) * kvg_ref[...]
        if state:
            ckvn_ref[rows] = ckvn
            n = krope_ref.shape[2]
            kr_t = kr.T[ROPE_LANE:ROPE_LANE + QK_ROPE]
            for s in range(kr.shape[0] // n):
                krope_ref[rows.start // n + s] = kr_t[:, s * n:(s + 1) * n]
        n_q = N_HEADS * HEAD_PAD
        if rope:
            cos, sin = cos_ref[rows], sin_ref[rows]
            kr = kr * cos + pltpu.roll(kr, ROPE_LANE, 1) * sin
            qp = _dot(cqn, w_uq_ref[...])
        else:
            qp = _dot(cqn, w_uq_ref[:, :n_q])
        for hh in range(N_HEADS):
            qh = qp[:, hh * HEAD_PAD:(hh + 1) * HEAD_PAD]
            if rope:
                qh = qh * cos + qp[:, n_q + hh * HEAD_PAD:n_q + (hh + 1) * HEAD_PAD] * sin
            q_ref[hh, rows] = qh.astype(BF16)
        _kv_outputs(ckvn.astype(BF16), kr.astype(BF16), w_k_ref, w_uvt_ref, ones_ref, k_ref, vt_ref, rows)

        u = _dot_nt(h, w_uf_ref[...]).astype(BF16)
        for g in range(N_GROUPS):
            r = _dot(u[:, g * GROUP:(g + 1) * GROUP], cs_ref[...])
            ucs_ref[rows, g * GROUP:(g + 1) * GROUP] = r[:, :GROUP].astype(BF16)
            ucs_ref[rows, D_FOURIER + g * GROUP:D_FOURIER + (g + 1) * GROUP] = r[:, GROUP:].astype(BF16)


def _layer_in(x, mod3, w_in_p, wts, rope_tabs, *, n, state):
    t = x.shape[0]
    rope = rope_tabs is not None
    tm, per_seq, mod_row = _token_tiling(t, n, per_sequence=rope)
    tok = lambda w: pl.BlockSpec((tm, w), lambda i: (i, 0))
    in_specs = [tok(D_MODEL),
                pl.BlockSpec((1, 1, 3 * D_MODEL), lambda i: (mod_row(i), 0, 0)),
                _row_spec(W_BLK, B_UF), _row_spec(W_BLK, B_MID)]
    in_specs += [_const_spec(w.shape) for w in wts]
    args = [x, mod3, w_in_p, w_in_p, *wts]
    if rope:
        in_specs += [pl.BlockSpec((tm, HEAD_PAD), lambda i: (i % per_seq, 0))] * 2
        args += list(rope_tabs)
    head = pl.BlockSpec((N_HEADS, tm, HEAD_PAD), lambda i: (0, i, 0))
    out_specs = [tok(2 * D_FOURIER), head, head, pl.BlockSpec((VT_ROWS, tm), lambda i: (0, i))]
    out_shape = [jax.ShapeDtypeStruct((t, 2 * D_FOURIER), BF16),
                 jax.ShapeDtypeStruct((N_HEADS, t, HEAD_PAD), BF16),
                 jax.ShapeDtypeStruct((N_HEADS, t, HEAD_PAD), BF16),
                 jax.ShapeDtypeStruct((VT_ROWS, t), BF16)]
    if state:
        out_specs += [tok(KV_RANK), pl.BlockSpec((tm // n, QK_ROPE, n), lambda i: (i, 0, 0))]
        out_shape += [jax.ShapeDtypeStruct((t, KV_RANK), F32), jax.ShapeDtypeStruct((t // n, QK_ROPE, n), F32)]
    return pl.pallas_call(
        functools.partial(_layer_in_kernel, rope=rope, state=state),
        out_shape=out_shape,
        grid=(t // tm,),
        in_specs=in_specs,
        out_specs=out_specs,
        compiler_params=_params(1),
        name="layer_in_latent" if rope else "layer_in_context",
    )(*args)


def _cache_kv_kernel(ckv_ref, kr_ref, w_k_ref, w_uvt_ref, ones_ref, k_ref, vt_ref):
    _kv_outputs(ckv_ref[...].astype(BF16), kr_ref[...].astype(BF16), w_k_ref, w_uvt_ref, ones_ref,
                k_ref, vt_ref, slice(None))


def _cache_kv(ckv, krope_pad, w_k, w_uvt, ones_col):
    t = ckv.shape[0]
    out_shape = [jax.ShapeDtypeStruct((N_HEADS, t, HEAD_PAD), BF16), jax.ShapeDtypeStruct((VT_ROWS, t), BF16)]
    return pl.pallas_call(
        _cache_kv_kernel,
        out_shape=out_shape,
        grid=(1,),
        in_specs=[_const_spec(a.shape) for a in (ckv, krope_pad, w_k, w_uvt, ones_col)],
        out_specs=[_const_spec(s.shape) for s in out_shape],
        compiler_params=_params(1),
        name="cache_kv",
    )(ckv, krope_pad, w_k, w_uvt, ones_col)


def _pos_dft_direct_kernel(cn_ref, sn_ref, ucs_ref, f_ref):
    n = cn_ref.shape[0]
    for s in range(f_ref.shape[0] // n):
        rows = slice(s * n, (s + 1) * n)
        f = _dot(cn_ref[...], ucs_ref[rows, :D_FOURIER]) + _dot(sn_ref[...], ucs_ref[rows, D_FOURIER:])
        f_ref[rows] = f.astype(BF16)


def _pos_dft_half_kernel(c1_ref, s1_ref, c0_ref, s0_ref, rev_ref, rev1_ref, ucs_ref, f_ref, cn_ref, sn_ref):
    n = f_ref.shape[0]
    half = n // 2
    blocks = half // REV_BLOCK

    @pl.when(pl.program_id(0) == 0)
    def _():
        c0, s0 = c0_ref[...], s0_ref[...]
        for k1 in range(c1_ref.shape[0]):
            rows = slice(k1 * DFT_SPLIT, min((k1 + 1) * DFT_SPLIT, cn_ref.shape[0]))
            m = rows.stop - rows.start
            c1, s1 = c1_ref[k1:k1 + 1, :], s1_ref[k1:k1 + 1, :]
            cn_ref[rows] = (c1 * c0[:m] - s1 * s0[:m]).astype(BF16)
            sn_ref[rows] = (-(s1 * c0[:m] + c1 * s0[:m])).astype(BF16)

    def folded(cols, sign):
        first_row = lax.broadcasted_iota(jnp.int32, (REV_BLOCK, D_FOURIER), 0) == 0
        out = []
        for b in range(blocks):
            m = _dot(rev1_ref[...], ucs_ref[n - (b + 1) * REV_BLOCK:n - b * REV_BLOCK, cols])
            if b > 0:
                m = jnp.where(first_row, ucs_ref[n - b * REV_BLOCK:n - b * REV_BLOCK + 1, cols].astype(F32), m)
            out.append((ucs_ref[b * REV_BLOCK:(b + 1) * REV_BLOCK, cols] + sign * m).astype(BF16))
        return jnp.concatenate(out, axis=0)

    e = folded(slice(0, D_FOURIER), 1.0)
    o = folded(slice(D_FOURIER, 2 * D_FOURIER), -1.0)
    odd_k = (lax.broadcasted_iota(jnp.int32, (cn_ref.shape[0], D_FOURIER), 0) & 1) == 1
    mid = ucs_ref[half:half + 1, :D_FOURIER].astype(F32) * (1.0 / math.sqrt(n))
    g = _dot(cn_ref[...], e) + jnp.where(odd_k, -mid, mid)
    hn = _dot(sn_ref[...], o)
    f_ref[:half] = (g[:half] + hn[:half]).astype(BF16)
    mirrored = (g[1:half + 1] - hn[1:half + 1]).astype(BF16)
    for a in range(blocks):
        blk = mirrored[(blocks - 1 - a) * REV_BLOCK:(blocks - a) * REV_BLOCK]
        f_ref[half + a * REV_BLOCK:half + (a + 1) * REV_BLOCK] = _dot(rev_ref[...], blk).astype(BF16)


def _pos_dft(ucs, *, n):
    t = ucs.shape[0]
    out_shape = jax.ShapeDtypeStruct((t, D_FOURIER), BF16)
    if n <= REV_BLOCK:
        cn, snn = _dft_tables(n)
        seqs = min(SHORT_DFT_SEQUENCES, t // n)
        return pl.pallas_call(
            _pos_dft_direct_kernel,
            out_shape=out_shape,
            grid=(t // (n * seqs),),
            in_specs=[_const_spec(cn.shape), _const_spec(snn.shape),
                      pl.BlockSpec((n * seqs, 2 * D_FOURIER), lambda i: (i, 0))],
            out_specs=pl.BlockSpec((n * seqs, D_FOURIER), lambda i: (i, 0)),
            compiler_params=_params(1),
            name=f"pos_dft_{n}",
        )(cn, snn, ucs)
    rows = n // 2 + 8
    consts = _dft_factors(n, rows, n // 2)
    anti = np.eye(REV_BLOCK, dtype=np.float32)[::-1]
    shifted = np.roll(anti, 1, axis=1)
    shifted[0] = 0.0
    consts += (jnp.asarray(anti, dtype=BF16), jnp.asarray(shifted, dtype=BF16))
    return pl.pallas_call(
        _pos_dft_half_kernel,
        out_shape=out_shape,
        grid=(t // n,),
        in_specs=[_const_spec(a.shape) for a in consts]
        + [pl.BlockSpec((n, 2 * D_FOURIER), lambda i: (i, 0))],
        out_specs=pl.BlockSpec((n, D_FOURIER), lambda i: (i, 0)),
        scratch_shapes=[pltpu.VMEM((rows, n // 2), BF16), pltpu.VMEM((rows, n // 2), BF16)],
        compiler_params=_params(1),
        name=f"pos_dft_{n}",
    )(*consts, ucs)


def _attention_kernel(*refs, n_kv, single_pass, units):
    q_ref, refs = refs[0], refs[1:]
    k_refs, vt_refs, refs = refs[:n_kv], refs[n_kv:2 * n_kv], refs[2 * n_kv:]
    o_ref, s_ref, pp_ref, ot_ref = refs[-4:]
    tq = q_ref.shape[1] // units
    key_lens = [k_ref.shape[1] // (1 if single_pass else units) for k_ref in k_refs]
    chunks, base = [], 0
    for j, m in enumerate(key_lens):
        chunks += [(j, c, base + c) for c in range(0, m, KEY_CHUNK)]
        base += m

    def q_block(hh, u):
        return q_ref[hh, u * tq:(u + 1) * tq, :]

    def finish(hh, ot, u):
        l = ot[V_DIM:V_DIM + 1]
        ot_ref[u, pl.ds(pl.multiple_of(hh * V_DIM, V_DIM), V_DIM), :] = ot[:V_DIM] / l
        return l

    def first_chunk_scores(hh, u):
        j, c, _ = chunks[0]
        s = _dot_nt(k_refs[j][hh, c:c + KEY_CHUNK, :], q_block(hh, u))
        s_ref[u, hh * KEY_CHUNK:(hh + 1) * KEY_CHUNK, :] = s
        return jnp.max(s, axis=0, keepdims=True)

    def shifted_softmax_pv(score_head, pv_head, shifts, u):
        if score_head is not None:
            qh = q_block(score_head, u)
            shift = shifts[score_head]
        ot = None
        for idx, (j, c, r) in enumerate(chunks):
            if score_head is not None:
                s = (s_ref[u, score_head * KEY_CHUNK:(score_head + 1) * KEY_CHUNK, :] if idx == 0 else
                     _dot_nt(k_refs[j][score_head, c:c + KEY_CHUNK, :], qh))
                pp_ref[2 * u + score_head % 2, r:r + KEY_CHUNK, :] = jnp.exp2(s - shift).astype(BF16)
            if pv_head is not None:
                part = _dot(vt_refs[j][pv_head * V_ROWS:(pv_head + 1) * V_ROWS, c:c + KEY_CHUNK],
                            pp_ref[2 * u + pv_head % 2, r:r + KEY_CHUNK, :])
                ot = part if ot is None else ot + part
        return finish(pv_head, ot, u) if pv_head is not None else None

    def exact_scores(hh, slot, u, kseq):
        qh = q_block(hh, u)
        m8 = None
        for j, c, r in chunks:
            k0 = kseq * key_lens[j] + c
            s = _dot_nt(k_refs[j][hh, k0:k0 + KEY_CHUNK, :], qh)
            s_ref[slot, r:r + KEY_CHUNK, :] = s
            mc = jnp.max(s.reshape(KEY_CHUNK // 8, 8, tq), axis=0)
            m8 = mc if m8 is None else jnp.maximum(m8, mc)
        return jnp.max(m8, axis=0, keepdims=True)

    def exact_softmax_pv(hh, s_slot, p_slot, m, u, kseq):
        for _, _, r in chunks:
            pp_ref[p_slot, r:r + KEY_CHUNK, :] = jnp.exp2(s_ref[s_slot, r:r + KEY_CHUNK, :] - m).astype(BF16)
        rows = pl.ds(pl.multiple_of(hh * V_ROWS, V_ROWS), V_ROWS)
        ot, base = None, 0
        for j, vt_ref in enumerate(vt_refs):
            mk = key_lens[j]
            part = _dot(vt_ref[rows, kseq * mk:(kseq + 1) * mk], pp_ref[p_slot, base:base + mk, :])
            ot = part if ot is None else ot + part
            base += mk
        finish(hh, ot, u)

    if not single_pass:
        work = [(u, hh) for u in range(units) for hh in range(N_HEADS)]
        maxima = [exact_scores(hh, slot, u, u) for slot, (u, hh) in enumerate(work)]
        for slot, (u, hh) in enumerate(work):
            exact_softmax_pv(hh, slot, slot, maxima[slot], u, u)
    else:
        trusted = []
        for u in range(units):
            l_min = l_max = None
            shifts = [first_chunk_scores(hh, u) for hh in range(N_HEADS)]
            for hh in range(N_HEADS + 1):
                l = shifted_softmax_pv(hh if hh < N_HEADS else None, hh - 1 if hh > 0 else None, shifts, u)
                if l is not None:
                    l_min = l if l_min is None else jnp.minimum(l_min, l)
                    l_max = l if l_max is None else jnp.maximum(l_max, l)
            trusted.append(jnp.logical_and(jnp.min(l_min) >= MIN_DENOMINATOR,
                                           jnp.max(l_max) <= MAX_DENOMINATOR))

        for u in range(units):
            @pl.when(jnp.logical_not(trusted[u]))
            def _(u=u):
                def body(hh, carry):
                    exact_softmax_pv(hh, u, 2 * u, exact_scores(hh, u, u, 0), u, 0)
                    return carry

                lax.fori_loop(0, N_HEADS, body, 0)

    for u in range(units):
        o_ref[u * tq:(u + 1) * tq, :] = ot_ref[u].T.astype(BF16)


def _attention(q, ks, vts, key_lens, *, n, after=None):
    t = q.shape[1]
    tq = Q_TILE
    m_tot = sum(key_lens)
    single_pass = m_tot > 2 * KEY_CHUNK
    if single_pass:
        units = ATTENTION_UNITS
        steps = n // (units * tq)
        kv_rows = 1
        s_slots, p_slots = units, 2 * units
    else:
        assert n == tq
        units = min(ATTENTION_UNITS, t // n)
        steps = 1
        kv_rows = units
        s_slots = p_slots = units * N_HEADS
    in_specs = [pl.BlockSpec((N_HEADS, units * tq, HEAD_PAD), lambda bi, i: (0, bi * steps + i, 0))]
    in_specs += [pl.BlockSpec((N_HEADS, kv_rows * m, HEAD_PAD), lambda bi, i: (0, bi, 0)) for m in key_lens]
    in_specs += [pl.BlockSpec((VT_ROWS, kv_rows * m), lambda bi, i: (0, bi)) for m in key_lens]
    args = [q, *ks, *vts]
    if after is not None:
        in_specs.append(pl.BlockSpec((8, after.shape[1]), lambda bi, i: (0, 0)))
        args.append(after)
    return pl.pallas_call(
        functools.partial(_attention_kernel, n_kv=len(ks), single_pass=single_pass, units=units),
        out_shape=jax.ShapeDtypeStruct((t, D_ATTN), BF16),
        grid=(t // (n * kv_rows), steps),
        in_specs=in_specs,
        out_specs=pl.BlockSpec((units * tq, D_ATTN), lambda bi, i: (bi * steps + i, 0)),
        scratch_shapes=[pltpu.VMEM((s_slots, m_tot, tq), F32), pltpu.VMEM((p_slots, m_tot, tq), BF16),
                        pltpu.VMEM((units, D_ATTN, tq), F32)],
        compiler_params=_params(2),
        name=f"attention_{n}",
    )(*args)


def _merge_kernel(x_ref, mod_ref, f_ref, a_ref, w_zf_ref, w_za_ref, w_g_ref, w_f_ref, w_a_ref, w_o_ref,
                  g_ref, y_ref):
    gate = mod_ref[0][:, 2 * D_MODEL:]
    for rows in _sub_tiles(x_ref.shape[0], MERGE_SUB_TILE):
        x = x_ref[rows]
        h = _modulated(x, mod_ref)
        zf = _dot_nt(h, w_zf_ref[...])
        y_f = _dot(f_ref[rows] * (zf * (jnp.tanh(zf) + 1.0)).astype(BF16), w_f_ref[...])
        za = _dot_nt(h, w_za_ref[...])
        y_a = _dot(a_ref[rows] * (za * (jnp.tanh(za) + 1.0)).astype(BF16), w_a_ref[...])
        merged_x2 = ((jnp.tanh(_dot_nt(h, w_g_ref[:D_MODEL])) + 1.0) * y_f
                     + (jnp.tanh(_dot_nt(h, w_g_ref[D_MODEL:])) + 1.0) * y_a)
        out = x + gate * _dot(merged_x2.astype(BF16), w_o_ref[...])
        y_ref[rows] = _rms(out) * g_ref[...]


def _merge(x, mod3, f, attn, w_in_p, w_f, w_a, w_o, g, *, n, latent):
    t = x.shape[0]
    tm, _, mod_row = _token_tiling(t, n, per_sequence=latent)
    tok = lambda w: pl.BlockSpec((tm, w), lambda i: (i, 0))
    return pl.pallas_call(
        _merge_kernel,
        out_shape=jax.ShapeDtypeStruct((t, D_MODEL), F32),
        grid=(t // tm,),
        in_specs=[tok(D_MODEL),
                  pl.BlockSpec((1, 1, 3 * D_MODEL), lambda i: (mod_row(i), 0, 0)),
                  tok(D_FOURIER), tok(D_ATTN),
                  _row_spec(W_BLK, B_ZF), _row_spec(W_BLK, B_ZA), _row_spec(2 * D_MODEL, 1),
                  _const_spec(w_f.shape), _const_spec(w_a.shape), _const_spec(w_o.shape),
                  _const_spec(g.shape)],
        out_specs=tok(D_MODEL),
        compiler_params=_params(1),
        name=f"merge_{n}",
    )(x, mod3, f, attn, w_in_p, w_in_p, w_in_p, w_f, w_a, w_o, g)


def _dft_tables(n):
    norm = 1.0 / math.sqrt(n)
    k = np.arange(n)
    ang = ((k[:, None] * k[None, :]) % n) * (2.0 * math.pi / n)
    return (jnp.asarray(np.cos(ang) * norm, dtype=F32).astype(BF16),
            jnp.asarray(np.sin(ang) * -norm, dtype=F32).astype(BF16))


def _dft_factors(n, rows, n_pos):
    norm = 1.0 / math.sqrt(n)
    period = n // DFT_SPLIT
    pos = np.arange(n_pos)
    k1 = np.arange(-(-rows // DFT_SPLIT))
    k0 = np.arange(DFT_SPLIT)
    a1 = ((k1[:, None] * pos[None, :]) % period) * (2.0 * math.pi / period)
    a0 = ((k0[:, None] * pos[None, :]) % n) * (2.0 * math.pi / n)
    return tuple(a.astype(np.float32) for a in (np.cos(a1), np.sin(a1), np.cos(a0) * norm, np.sin(a0) * norm))


def _channel_dft_table():
    c = np.arange(GROUP)
    ang = ((c[:, None] * c[None, :]) % GROUP) * (2.0 * math.pi / GROUP)
    norm = 1.0 / math.sqrt(GROUP)
    return jnp.asarray(np.concatenate([np.cos(ang), np.sin(ang)], axis=1) * norm, dtype=F32).astype(BF16)


def _rope_tables(n):
    t = np.arange(n)
    row = (t // GRID_W).astype(np.float64)
    col = (t % GRID_W).astype(np.float64)
    half = QK_ROPE // 2
    inv = ROPE_THETA ** (-np.arange(0, half, 2, dtype=np.float64) / half)
    ar, ac = row[:, None] * inv, col[:, None] * inv
    ang = np.concatenate([ar, ar, ac, ac], axis=-1)
    pad = lambda a, fill: np.concatenate(
        [np.full((n, ROPE_LANE), fill), a, np.full((n, HEAD_PAD - ROPE_LANE - QK_ROPE), fill)],
        axis=1).astype(np.float32)
    return pad(np.cos(ang), 1.0), pad(np.sin(ang), 0.0)


def kernel(x_prompt, x_sample, cache_ckv, cache_krope, c, c_ctx, w_ada, b_ada, w_in, q_norm_g, w_uq,
           kv_norm_g, w_ukv, w_f_out, w_a_out, w_out, final_norm_g):
    assert w_in.shape[0] == 1
    b_ctx, n_ctx, _ = x_prompt.shape
    dec_b, n_lat, _ = x_sample.shape
    past = cache_ckv.shape[2]

    w_q3 = w_uq[0].reshape(Q_RANK, N_HEADS, QK_NOPE + QK_ROPE)
    head_pad = lambda a, left: jnp.pad(
        a, ((0, 0), (0, 0), (left, HEAD_PAD - left - a.shape[2]))).reshape(Q_RANK, N_HEADS * HEAD_PAD)
    rp = w_q3[:, :, QK_NOPE:].reshape(Q_RANK, N_HEADS, 2, 2, QK_ROPE // 4)
    w_q_rot = jnp.stack([-rp[:, :, :, 1], rp[:, :, :, 0]], axis=3).reshape(Q_RANK, N_HEADS, QK_ROPE)
    w_uq_p = jnp.concatenate([head_pad(w_q3, 0), head_pad(w_q_rot, ROPE_LANE)], axis=1).astype(BF16)
    w_kv3 = w_ukv[0].reshape(KV_RANK, N_HEADS, QK_NOPE + V_DIM)
    w_knope = jnp.pad(w_kv3[:, :, :QK_NOPE], ((0, 0), (0, 0), (0, HEAD_PAD - QK_NOPE)))
    place = np.zeros((HEAD_PAD, N_HEADS, HEAD_PAD), np.float32)
    for j in range(QK_ROPE):
        place[ROPE_LANE + j, :, ROPE_LANE + j] = 1.0
    w_k = jnp.concatenate([w_knope.reshape(KV_RANK, -1), jnp.asarray(place).reshape(HEAD_PAD, -1)],
                          axis=0).astype(BF16)
    w_uvt = jnp.pad(jnp.transpose(w_kv3[:, :, QK_NOPE:], (1, 2, 0)),
                    ((0, 0), (0, V_ROWS - V_DIM), (0, 0))).reshape(VT_ROWS, KV_RANK).astype(BF16)
    ones_col = np.zeros((N_HEADS, V_ROWS, LANES), np.float32)
    ones_col[:, V_DIM, :] = 1.0
    ones_col = jnp.asarray(ones_col.reshape(VT_ROWS, LANES))
    w_f = w_f_out[0].astype(BF16)
    w_a = w_a_out[0].astype(BF16)
    w_o = (0.5 * w_out[0]).astype(BF16)
    qg = q_norm_g[0].reshape(1, Q_RANK)
    kvg = kv_norm_g[0].reshape(1, KV_RANK)
    fg = final_norm_g.reshape(1, D_MODEL)
    wts = (qg, kvg, w_uq_p, w_k, w_uvt, ones_col, _channel_dft_table())
    w_in_p = _pack_w_in(jnp.swapaxes(w_in[0], 0, 1))

    cond8 = jnp.concatenate([c_ctx[None, :], c, jnp.zeros((8 - 1 - dec_b, D_MODEL), F32)], axis=0)
    mod3 = _adaln(cond8, w_ada[0], b_ada[0].reshape(1, -1))

    xp = x_prompt.reshape(b_ctx * n_ctx, D_MODEL)
    ucs, q, k, vt, state_ckv, state_krope = _layer_in(xp, mod3, w_in_p, wts, None, n=n_ctx, state=True)
    f = _pos_dft(ucs, n=n_ctx)
    attn = _attention(q, [k], [vt], [n_ctx], n=n_ctx, after=f)
    y_prompt = _merge(xp, mod3, f, attn, w_in_p, w_f, w_a, w_o, fg, n=n_ctx, latent=False)

    xs = x_sample.reshape(dec_b * n_lat, D_MODEL)
    ucs, q, k, vt = _layer_in(xs, mod3, w_in_p, wts, _rope_tables(n_lat), n=n_lat, state=False)
    f = _pos_dft(ucs, n=n_lat)
    krope_pad = jnp.pad(cache_krope[:, 0], ((0, 0), (0, 0), (ROPE_LANE, HEAD_PAD - ROPE_LANE - QK_ROPE)))
    k_c, vt_c = _cache_kv(cache_ckv[:, 0].reshape(dec_b * past, KV_RANK),
                          krope_pad.reshape(dec_b * past, HEAD_PAD), w_k, w_uvt, ones_col)
    attn = _attention(q, [k, k_c], [vt, vt_c], [n_lat, past], n=n_lat, after=f)
    y_sample = _merge(xs, mod3, f, attn, w_in_p, w_f, w_a, w_o, fg, n=n_lat, latent=True)

    return (y_prompt.reshape(b_ctx, n_ctx, D_MODEL), y_sample.reshape(dec_b, n_lat, D_MODEL),
            state_ckv.reshape(b_ctx, 1, n_ctx, KV_RANK), jnp.swapaxes(state_krope, 1, 2)[:, None])
```

```python
import functools
import math

import jax
import jax.numpy as jnp
import numpy as np
from jax import lax
from jax.experimental import pallas as pl
from jax.experimental.pallas import tpu as pltpu

F32 = jnp.float32
BF16 = jnp.bfloat16

D_MODEL = 1024
GRID_W = 64
N_GROUPS = 4
GROUP = 128
D_FOURIER = N_GROUPS * GROUP
N_HEADS = 8
QK_NOPE = 64
QK_ROPE = 32
V_DIM = 64
Q_RANK = 256
KV_RANK = 128
D_ATTN = N_HEADS * V_DIM
D_IN = 2 * D_FOURIER + Q_RANK + KV_RANK + QK_ROPE + D_ATTN + 2 * D_MODEL
ROPE_THETA = 10000.0
EPS = 1e-6
LANES = 128
HEAD_PAD = LANES
ROPE_LANE = QK_NOPE
V_ROWS = V_DIM + 16
VT_ROWS = N_HEADS * V_ROWS
Q_SCALE = (QK_NOPE + QK_ROPE) ** -0.5 * math.log2(math.e)
KEY_CHUNK = 256
Q_TILE = 256
ATTENTION_UNITS = 4
SHORT_DFT_SEQUENCES = 8
REV_BLOCK = 256
DFT_SPLIT = 64
MIN_DENOMINATOR = 2.0 ** -60
MAX_DENOMINATOR = 2.0 ** 60
W_BLK = 512
B_UF, B_ZF, B_MID, B_ZA = 0, 1, 2, 3
D_IN_PAD = 8 * W_BLK
TOKEN_TILE = 1024
LAYER_IN_SUB_TILE = TOKEN_TILE
MERGE_SUB_TILE = TOKEN_TILE // 2
VMEM_LIMIT_BYTES = 56 * 1024 * 1024


def _const_spec(shape):
    nd = len(shape)
    return pl.BlockSpec(shape, lambda *_: (0,) * nd, pipeline_mode=pl.Buffered(1))


def _row_spec(rows, blk):
    return pl.BlockSpec((rows, D_MODEL), lambda *_: (blk, 0), pipeline_mode=pl.Buffered(1))


def _params(n_axes):
    return pltpu.CompilerParams(dimension_semantics=("arbitrary",) * n_axes,
                                vmem_limit_bytes=VMEM_LIMIT_BYTES)


def _rms(x):
    return x * lax.rsqrt(jnp.mean(x * x, axis=-1, keepdims=True) + EPS)


def _sigmoid(x):
    return 0.5 * jnp.tanh(0.5 * x) + 0.5


def _dot(a, b):
    return jnp.dot(a, b, preferred_element_type=F32)


def _dot_nt(a, b):
    return lax.dot_general(a, b, (((1,), (1,)), ((), ())), preferred_element_type=F32)


def _token_tiling(t, n, *, per_sequence):
    tm = TOKEN_TILE
    assert t % tm == 0 and (not per_sequence or n % tm == 0)
    per_seq = n // tm if per_sequence else None
    mod_row = (lambda i: 1 + i // per_seq) if per_sequence else (lambda i: 0)
    return tm, per_seq, mod_row


def _sub_tiles(rows, step):
    return [slice(r, r + step) for r in range(0, rows, step)]


def _modulated(x, mod_ref):
    mod = mod_ref[0]
    shift, scale = mod[:, 0:D_MODEL], mod[:, D_MODEL:2 * D_MODEL]
    return (_rms(x) * (1.0 + scale) + shift).astype(BF16)


def _adaln_kernel(c_ref, w_ref, b_ref, o_ref):
    c = c_ref[...]
    s = (c * _sigmoid(c)).astype(BF16)
    mod = _dot(s, w_ref[...].astype(BF16)) + b_ref[...]
    for r in range(mod.shape[0]):
        o_ref[r] = mod[r:r + 1]


def _adaln(cond8, w_ada, b_ada):
    n_blk = 4
    bw = 3 * D_MODEL // n_blk
    return pl.pallas_call(
        _adaln_kernel,
        out_shape=jax.ShapeDtypeStruct((8, 1, 3 * D_MODEL), F32),
        grid=(n_blk,),
        in_specs=[pl.BlockSpec((8, D_MODEL), lambda j: (0, 0)),
                  pl.BlockSpec((D_MODEL, bw), lambda j: (0, j)),
                  pl.BlockSpec((1, bw), lambda j: (0, j))],
        out_specs=pl.BlockSpec((8, 1, bw), lambda j: (0, 0, j)),
        compiler_params=_params(1),
        name="adaln",
    )(cond8, w_ada, b_ada)


def _rot_rows(w):
    q = QK_ROPE // 4
    return jnp.concatenate([-w[q:2 * q], w[0:q], -w[3 * q:4 * q], w[2 * q:3 * q]], axis=0)


def _pack_w_in_kernel(wt_ref, o_ref):
    lo = 2 * D_FOURIER + Q_RANK + KV_RANK
    zf = slice(B_ZF * W_BLK, (B_ZF + 1) * W_BLK)
    o_ref[:zf.start] = wt_ref[:zf.start].astype(BF16)
    o_ref[zf] = (0.5 * wt_ref[zf]).astype(BF16)
    o_ref[zf.stop:lo] = wt_ref[zf.stop:lo].astype(BF16)
    kr = wt_ref[lo:lo + QK_ROPE]
    zeros = jnp.zeros((QK_ROPE, D_MODEL), F32)
    o_ref[lo:lo + LANES] = jnp.concatenate([_rot_rows(kr), zeros, kr, zeros], axis=0).astype(BF16)
    o_ref[lo + LANES:] = (0.5 * wt_ref[lo + QK_ROPE:]).astype(BF16)


def _pack_w_in(w_in_t):
    return pl.pallas_call(
        _pack_w_in_kernel,
        out_shape=jax.ShapeDtypeStruct((D_IN_PAD, D_MODEL), BF16),
        grid=(1,),
        in_specs=[_const_spec((D_IN, D_MODEL))],
        out_specs=_const_spec((D_IN_PAD, D_MODEL)),
        compiler_params=_params(1),
        name="pack_w_in",
    )(w_in_t)


def _kv_outputs(ckvn_b, kr_b, w_k_ref, w_uvt_ref, ones_ref, k_ref, vt_ref, rows):
    tm = ckvn_b.shape[0]
    kp = _dot(jnp.concatenate([ckvn_b, kr_b], axis=1), w_k_ref[...])
    for hh in range(N_HEADS):
        k_ref[hh, rows] = kp[:, hh * HEAD_PAD:(hh + 1) * HEAD_PAD].astype(BF16)
    ones = jnp.concatenate([ones_ref[...]] * (tm // LANES), axis=1)
    vt_ref[:, rows] = (_dot_nt(w_uvt_ref[...], ckvn_b) + ones).astype(BF16)


def _layer_in_kernel(*refs, rope, state, cache):
    (x_ref, mod_ref, w_uf_ref, w_mid_ref, qg_ref, kvg_ref, w_uq_ref, w_k_ref, w_uvt_ref, ones_ref,
     cs_ref), refs = refs[:11], refs[11:]
    if rope:
        (cos_ref, sin_ref), refs = refs[:2], refs[2:]
    if cache:
        (cache_ckv_ref, cache_kr_ref), refs = refs[:2], refs[2:]
    (ucs_ref, q_ref, k_ref, vt_ref), refs = refs[:4], refs[4:]
    if cache:
        (cache_k_ref, cache_vt_ref), refs = refs[:2], refs[2:]
    if state:
        ckvn_ref, krope_ref = refs

    for rows in _sub_tiles(x_ref.shape[0], LAYER_IN_SUB_TILE):
        h = _modulated(x_ref[rows], mod_ref)

        mid = _dot_nt(h, w_mid_ref[...])
        cq, ckv, kr = mid[:, :Q_RANK], mid[:, Q_RANK:Q_RANK + KV_RANK], mid[:, Q_RANK + KV_RANK:]
        cqn = (_rms(cq) * (qg_ref[...] * Q_SCALE)).astype(BF16)
        ckvn = _rms(ckv) * kvg_ref[...]
        if state:
            ckvn_ref[rows] = ckvn
            n = krope_ref.shape[2]
            kr_t = kr.T[ROPE_LANE:ROPE_LANE + QK_ROPE]
            for s in range(kr.shape[0] // n):
                krope_ref[rows.start // n + s] = kr_t[:, s * n:(s + 1) * n]
        n_q = N_HEADS * HEAD_PAD
        if rope:
            cos, sin = cos_ref[rows], sin_ref[rows]
            kr = kr * cos + pltpu.roll(kr, ROPE_LANE, 1) * sin
            qp = _dot(cqn, w_uq_ref[...])
        else:
            qp = _dot(cqn, w_uq_ref[:, :n_q])
        for hh in range(N_HEADS):
            qh = qp[:, hh * HEAD_PAD:(hh + 1) * HEAD_PAD]
            if rope:
                qh = qh * cos + qp[:, n_q + hh * HEAD_PAD:n_q + (hh + 1) * HEAD_PAD] * sin
            q_ref[hh, rows] = qh.astype(BF16)
        _kv_outputs(ckvn.astype(BF16), kr.astype(BF16), w_k_ref, w_uvt_ref, ones_ref, k_ref, vt_ref, rows)

        u = _dot_nt(h, w_uf_ref[...]).astype(BF16)
        for g in range(N_GROUPS):
            r = _dot(u[:, g * GROUP:(g + 1) * GROUP], cs_ref[...])
            ucs_ref[rows, g * GROUP:(g + 1) * GROUP] = r[:, :GROUP].astype(BF16)
            ucs_ref[rows, D_FOURIER + g * GROUP:D_FOURIER + (g + 1) * GROUP] = r[:, GROUP:].astype(BF16)

    if cache:
        @pl.when(pl.program_id(0) == 0)
        def _():
            _kv_outputs(cache_ckv_ref[...].astype(BF16), cache_kr_ref[...].astype(BF16), w_k_ref, w_uvt_ref,
                        ones_ref, cache_k_ref, cache_vt_ref, slice(None))


def _layer_in(x, mod3, w_in_p, wts, rope_tabs, *, n, state, cache=None):
    t = x.shape[0]
    rope = rope_tabs is not None
    tm, per_seq, mod_row = _token_tiling(t, n, per_sequence=rope)
    tok = lambda w: pl.BlockSpec((tm, w), lambda i: (i, 0))
    in_specs = [tok(D_MODEL),
                pl.BlockSpec((1, 1, 3 * D_MODEL), lambda i: (mod_row(i), 0, 0)),
                _row_spec(W_BLK, B_UF), _row_spec(W_BLK, B_MID)]
    in_specs += [_const_spec(w.shape) for w in wts]
    args = [x, mod3, w_in_p, w_in_p, *wts]
    if rope:
        in_specs += [pl.BlockSpec((tm, HEAD_PAD), lambda i: (i % per_seq, 0))] * 2
        args += list(rope_tabs)
    head = pl.BlockSpec((N_HEADS, tm, HEAD_PAD), lambda i: (0, i, 0))
    out_specs = [tok(2 * D_FOURIER), head, head, pl.BlockSpec((VT_ROWS, tm), lambda i: (0, i))]
    out_shape = [jax.ShapeDtypeStruct((t, 2 * D_FOURIER), BF16),
                 jax.ShapeDtypeStruct((N_HEADS, t, HEAD_PAD), BF16),
                 jax.ShapeDtypeStruct((N_HEADS, t, HEAD_PAD), BF16),
                 jax.ShapeDtypeStruct((VT_ROWS, t), BF16)]
    if cache is not None:
        t_c = cache[0].shape[0]
        in_specs += [_const_spec(a.shape) for a in cache]
        args += list(cache)
        cache_out = [jax.ShapeDtypeStruct((N_HEADS, t_c, HEAD_PAD), BF16), jax.ShapeDtypeStruct((VT_ROWS, t_c), BF16)]
        out_specs += [_const_spec(s.shape) for s in cache_out]
        out_shape += cache_out
    if state:
        out_specs += [tok(KV_RANK), pl.BlockSpec((tm // n, QK_ROPE, n), lambda i: (i, 0, 0))]
        out_shape += [jax.ShapeDtypeStruct((t, KV_RANK), F32), jax.ShapeDtypeStruct((t // n, QK_ROPE, n), F32)]
    return pl.pallas_call(
        functools.partial(_layer_in_kernel, rope=rope, state=state, cache=cache is not None),
        out_shape=out_shape,
        grid=(t // tm,),
        in_specs=in_specs,
        out_specs=out_specs,
        compiler_params=_params(1),
        name="layer_in_latent" if rope else "layer_in_context",
    )(*args)


def _pos_dft_direct_kernel(cn_ref, sn_ref, ucs_ref, f_ref):
    n = cn_ref.shape[0]
    for s in range(f_ref.shape[0] // n):
        rows = slice(s * n, (s + 1) * n)
        f = _dot(cn_ref[...], ucs_ref[rows, :D_FOURIER]) + _dot(sn_ref[...], ucs_ref[rows, D_FOURIER:])
        f_ref[rows] = f.astype(BF16)


def _pos_dft_half_kernel(c1_ref, s1_ref, c0_ref, s0_ref, rev_ref, rev1_ref, ucs_ref, f_ref, cn_ref, sn_ref):
    n = f_ref.shape[0]
    half = n // 2
    blocks = half // REV_BLOCK

    @pl.when(pl.program_id(0) == 0)
    def _():
        c0, s0 = c0_ref[...], s0_ref[...]
        for k1 in range(c1_ref.shape[0]):
            rows = slice(k1 * DFT_SPLIT, min((k1 + 1) * DFT_SPLIT, cn_ref.shape[0]))
            m = rows.stop - rows.start
            c1, s1 = c1_ref[k1:k1 + 1, :], s1_ref[k1:k1 + 1, :]
            cn_ref[rows] = (c1 * c0[:m] - s1 * s0[:m]).astype(BF16)
            sn_ref[rows] = (-(s1 * c0[:m] + c1 * s0[:m])).astype(BF16)

    def folded(cols, sign):
        first_row = lax.broadcasted_iota(jnp.int32, (REV_BLOCK, D_FOURIER), 0) == 0
        out = []
        for b in range(blocks):
            m = _dot(rev1_ref[...], ucs_ref[n - (b + 1) * REV_BLOCK:n - b * REV_BLOCK, cols])
            if b > 0:
                m = jnp.where(first_row, ucs_ref[n - b * REV_BLOCK:n - b * REV_BLOCK + 1, cols].astype(F32), m)
            out.append((ucs_ref[b * REV_BLOCK:(b + 1) * REV_BLOCK, cols] + sign * m).astype(BF16))
        return jnp.concatenate(out, axis=0)

    e = folded(slice(0, D_FOURIER), 1.0)
    o = folded(slice(D_FOURIER, 2 * D_FOURIER), -1.0)
    odd_k = (lax.broadcasted_iota(jnp.int32, (cn_ref.shape[0], D_FOURIER), 0) & 1) == 1
    mid = ucs_ref[half:half + 1, :D_FOURIER].astype(F32) * (1.0 / math.sqrt(n))
    g = _dot(cn_ref[...], e) + jnp.where(odd_k, -mid, mid)
    hn = _dot(sn_ref[...], o)
    f_ref[:half] = (g[:half] + hn[:half]).astype(BF16)
    mirrored = (g[1:half + 1] - hn[1:half + 1]).astype(BF16)
    for a in range(blocks):
        blk = mirrored[(blocks - 1 - a) * REV_BLOCK:(blocks - a) * REV_BLOCK]
        f_ref[half + a * REV_BLOCK:half + (a + 1) * REV_BLOCK] = _dot(rev_ref[...], blk).astype(BF16)


def _pos_dft(ucs, *, n):
    t = ucs.shape[0]
    out_shape = jax.ShapeDtypeStruct((t, D_FOURIER), BF16)
    if n <= REV_BLOCK:
        cn, snn = _dft_tables(n)
        seqs = min(SHORT_DFT_SEQUENCES, t // n)
        return pl.pallas_call(
            _pos_dft_direct_kernel,
            out_shape=out_shape,
            grid=(t // (n * seqs),),
            in_specs=[_const_spec(cn.shape), _const_spec(snn.shape),
                      pl.BlockSpec((n * seqs, 2 * D_FOURIER), lambda i: (i, 0))],
            out_specs=pl.BlockSpec((n * seqs, D_FOURIER), lambda i: (i, 0)),
            compiler_params=_params(1),
            name=f"pos_dft_{n}",
        )(cn, snn, ucs)
    rows = n // 2 + 8
    consts = _dft_factors(n, rows, n // 2)
    anti = np.eye(REV_BLOCK, dtype=np.float32)[::-1]
    shifted = np.roll(anti, 1, axis=1)
    shifted[0] = 0.0
    consts += (jnp.asarray(anti, dtype=BF16), jnp.asarray(shifted, dtype=BF16))
    return pl.pallas_call(
        _pos_dft_half_kernel,
        out_shape=out_shape,
        grid=(t // n,),
        in_specs=[_const_spec(a.shape) for a in consts]
        + [pl.BlockSpec((n, 2 * D_FOURIER), lambda i: (i, 0))],
        out_specs=pl.BlockSpec((n, D_FOURIER), lambda i: (i, 0)),
        scratch_shapes=[pltpu.VMEM((rows, n // 2), BF16), pltpu.VMEM((rows, n // 2), BF16)],
        compiler_params=_params(1),
        name=f"pos_dft_{n}",
    )(*consts, ucs)


def _attention_kernel(*refs, n_kv, single_pass, units):
    q_ref, refs = refs[0], refs[1:]
    k_refs, vt_refs, refs = refs[:n_kv], refs[n_kv:2 * n_kv], refs[2 * n_kv:]
    o_ref, s_ref, pp_ref, ot_ref = refs[-4:]
    tq = q_ref.shape[1] // units
    key_lens = [k_ref.shape[1] // (1 if single_pass else units) for k_ref in k_refs]
    chunks, base = [], 0
    for j, m in enumerate(key_lens):
        chunks += [(j, c, base + c) for c in range(0, m, KEY_CHUNK)]
        base += m

    def q_block(hh, u):
        return q_ref[hh, u * tq:(u + 1) * tq, :]

    def finish(hh, ot, u):
        l = ot[V_DIM:V_DIM + 1]
        ot_ref[u, pl.ds(pl.multiple_of(hh * V_DIM, V_DIM), V_DIM), :] = ot[:V_DIM] / l
        return l

    def first_chunk_scores(hh, u):
        j, c, _ = chunks[0]
        s = _dot_nt(k_refs[j][hh, c:c + KEY_CHUNK, :], q_block(hh, u))
        s_ref[u, hh * KEY_CHUNK:(hh + 1) * KEY_CHUNK, :] = s
        return jnp.max(s, axis=0, keepdims=True)

    def shifted_softmax_pv(score_head, pv_head, shifts, u):
        if score_head is not None:
            qh = q_block(score_head, u)
            shift = shifts[score_head]
        ot = None
        for idx, (j, c, r) in enumerate(chunks):
            if score_head is not None:
                s = (s_ref[u, score_head * KEY_CHUNK:(score_head + 1) * KEY_CHUNK, :] if idx == 0 else
                     _dot_nt(k_refs[j][score_head, c:c + KEY_CHUNK, :], qh))
                pp_ref[2 * u + score_head % 2, r:r + KEY_CHUNK, :] = jnp.exp2(s - shift).astype(BF16)
            if pv_head is not None:
                part = _dot(vt_refs[j][pv_head * V_ROWS:(pv_head + 1) * V_ROWS, c:c + KEY_CHUNK],
                            pp_ref[2 * u + pv_head % 2, r:r + KEY_CHUNK, :])
                ot = part if ot is None else ot + part
        return finish(pv_head, ot, u) if pv_head is not None else None

    def exact_scores(hh, slot, u, kseq):
        qh = q_block(hh, u)
        m8 = None
        for j, c, r in chunks:
            k0 = kseq * key_lens[j] + c
            s = _dot_nt(k_refs[j][hh, k0:k0 + KEY_CHUNK, :], qh)
            s_ref[slot, r:r + KEY_CHUNK, :] = s
            mc = jnp.max(s.reshape(KEY_CHUNK // 8, 8, tq), axis=0)
            m8 = mc if m8 is None else jnp.maximum(m8, mc)
        return jnp.max(m8, axis=0, keepdims=True)

    def exact_softmax_pv(hh, s_slot, p_slot, m, u, kseq):
        for _, _, r in chunks:
            pp_ref[p_slot, r:r + KEY_CHUNK, :] = jnp.exp2(s_ref[s_slot, r:r + KEY_CHUNK, :] - m).astype(BF16)
        rows = pl.ds(pl.multiple_of(hh * V_ROWS, V_ROWS), V_ROWS)
        ot, base = None, 0
        for j, vt_ref in enumerate(vt_refs):
            mk = key_lens[j]
            part = _dot(vt_ref[rows, kseq * mk:(kseq + 1) * mk], pp_ref[p_slot, base:base + mk, :])
            ot = part if ot is None else ot + part
            base += mk
        finish(hh, ot, u)

    if not single_pass:
        work = [(u, hh) for u in range(units) for hh in range(N_HEADS)]
        maxima = [exact_scores(hh, slot, u, u) for slot, (u, hh) in enumerate(work)]
        for slot, (u, hh) in enumerate(work):
            exact_softmax_pv(hh, slot, slot, maxima[slot], u, u)
    else:
        trusted = []
        for u in range(units):
            l_min = l_max = None
            shifts = [first_chunk_scores(hh, u) for hh in range(N_HEADS)]
            for hh in range(N_HEADS + 1):
                l = shifted_softmax_pv(hh if hh < N_HEADS else None, hh - 1 if hh > 0 else None, shifts, u)
                if l is not None:
                    l_min = l if l_min is None else jnp.minimum(l_min, l)
                    l_max = l if l_max is None else jnp.maximum(l_max, l)
            trusted.append(jnp.logical_and(jnp.min(l_min) >= MIN_DENOMINATOR,
                                           jnp.max(l_max) <= MAX_DENOMINATOR))

        for u in range(units):
            @pl.when(jnp.logical_not(trusted[u]))
            def _(u=u):
                def body(hh, carry):
                    exact_softmax_pv(hh, u, 2 * u, exact_scores(hh, u, u, 0), u, 0)
                    return carry

                lax.fori_loop(0, N_HEADS, body, 0)

    for u in range(units):
        o_ref[u * tq:(u + 1) * tq, :] = ot_ref[u].T.astype(BF16)


def _attention(q, ks, vts, key_lens, *, n, after=None):
    t = q.shape[1]
    tq = Q_TILE
    m_tot = sum(key_lens)
    single_pass = m_tot > 2 * KEY_CHUNK
    if single_pass:
        units = ATTENTION_UNITS
        steps = n // (units * tq)
        kv_rows = 1
        s_slots, p_slots = units, 2 * units
    else:
        assert n == tq
        units = min(ATTENTION_UNITS, t // n)
        steps = 1
        kv_rows = units
        s_slots = p_slots = units * N_HEADS
    in_specs = [pl.BlockSpec((N_HEADS, units * tq, HEAD_PAD), lambda bi, i: (0, bi * steps + i, 0))]
    in_specs += [pl.BlockSpec((N_HEADS, kv_rows * m, HEAD_PAD), lambda bi, i: (0, bi, 0)) for m in key_lens]
    in_specs += [pl.BlockSpec((VT_ROWS, kv_rows * m), lambda bi, i: (0, bi)) for m in key_lens]
    args = [q, *ks, *vts]
    if after is not None:
        in_specs.append(pl.BlockSpec((8, after.shape[1]), lambda bi, i: (0, 0)))
        args.append(after)
    return pl.pallas_call(
        functools.partial(_attention_kernel, n_kv=len(ks), single_pass=single_pass, units=units),
        out_shape=jax.ShapeDtypeStruct((t, D_ATTN), BF16),
        grid=(t // (n * kv_rows), steps),
        in_specs=in_specs,
        out_specs=pl.BlockSpec((units * tq, D_ATTN), lambda bi, i: (bi * steps + i, 0)),
        scratch_shapes=[pltpu.VMEM((s_slots, m_tot, tq), F32), pltpu.VMEM((p_slots, m_tot, tq), BF16),
                        pltpu.VMEM((units, D_ATTN, tq), F32)],
        compiler_params=_params(2),
        name=f"attention_{n}",
    )(*args)


def _merge_kernel(x_ref, mod_ref, f_ref, a_ref, w_zf_ref, w_za_ref, w_g_ref, w_f_ref, w_a_ref, w_o_ref,
                  g_ref, y_ref):
    gate = mod_ref[0][:, 2 * D_MODEL:]
    for rows in _sub_tiles(x_ref.shape[0], MERGE_SUB_TILE):
        x = x_ref[rows]
        h = _modulated(x, mod_ref)
        zf = _dot_nt(h, w_zf_ref[...])
        y_f = _dot(f_ref[rows] * (zf * (jnp.tanh(zf) + 1.0)).astype(BF16), w_f_ref[...])
        za = _dot_nt(h, w_za_ref[...])
        y_a = _dot(a_ref[rows] * (za * (jnp.tanh(za) + 1.0)).astype(BF16), w_a_ref[...])
        merged_x2 = ((jnp.tanh(_dot_nt(h, w_g_ref[:D_MODEL])) + 1.0) * y_f
                     + (jnp.tanh(_dot_nt(h, w_g_ref[D_MODEL:])) + 1.0) * y_a)
        out = x + gate * _dot(merged_x2.astype(BF16), w_o_ref[...])
        y_ref[rows] = _rms(out) * g_ref[...]


def _merge(x, mod3, f, attn, w_in_p, w_f, w_a, w_o, g, *, n, latent):
    t = x.shape[0]
    tm, _, mod_row = _token_tiling(t, n, per_sequence=latent)
    tok = lambda w: pl.BlockSpec((tm, w), lambda i: (i, 0))
    return pl.pallas_call(
        _merge_kernel,
        out_shape=jax.ShapeDtypeStruct((t, D_MODEL), F32),
        grid=(t // tm,),
        in_specs=[tok(D_MODEL),
                  pl.BlockSpec((1, 1, 3 * D_MODEL), lambda i: (mod_row(i), 0, 0)),
                  tok(D_FOURIER), tok(D_ATTN),
                  _row_spec(W_BLK, B_ZF), _row_spec(W_BLK, B_ZA), _row_spec(2 * D_MODEL, 1),
                  _const_spec(w_f.shape), _const_spec(w_a.shape), _const_spec(w_o.shape),
                  _const_spec(g.shape)],
        out_specs=tok(D_MODEL),
        compiler_params=_params(1),
        name=f"merge_{n}",
    )(x, mod3, f, attn, w_in_p, w_in_p, w_in_p, w_f, w_a, w_o, g)


def _dft_tables(n):
    norm = 1.0 / math.sqrt(n)
    k = np.arange(n)
    ang = ((k[:, None] * k[None, :]) % n) * (2.0 * math.pi / n)
    return (jnp.asarray(np.cos(ang) * norm, dtype=F32).astype(BF16),
            jnp.asarray(np.sin(ang) * -norm, dtype=F32).astype(BF16))


def _dft_factors(n, rows, n_pos):
    norm = 1.0 / math.sqrt(n)
    period = n // DFT_SPLIT
    pos = np.arange(n_pos)
    k1 = np.arange(-(-rows // DFT_SPLIT))
    k0 = np.arange(DFT_SPLIT)
    a1 = ((k1[:, None] * pos[None, :]) % period) * (2.0 * math.pi / period)
    a0 = ((k0[:, None] * pos[None, :]) % n) * (2.0 * math.pi / n)
    return tuple(a.astype(np.float32) for a in (np.cos(a1), np.sin(a1), np.cos(a0) * norm, np.sin(a0) * norm))


def _channel_dft_table():
    c = np.arange(GROUP)
    ang = ((c[:, None] * c[None, :]) % GROUP) * (2.0 * math.pi / GROUP)
    norm = 1.0 / math.sqrt(GROUP)
    return jnp.asarray(np.concatenate([np.cos(ang), np.sin(ang)], axis=1) * norm, dtype=F32).astype(BF16)


def _rope_tables(n):
    t = np.arange(n)
    row = (t // GRID_W).astype(np.float64)
    col = (t % GRID_W).astype(np.float64)
    half = QK_ROPE // 2
    inv = ROPE_THETA ** (-np.arange(0, half, 2, dtype=np.float64) / half)
    ar, ac = row[:, None] * inv, col[:, None] * inv
    ang = np.concatenate([ar, ar, ac, ac], axis=-1)
    pad = lambda a, fill: np.concatenate(
        [np.full((n, ROPE_LANE), fill), a, np.full((n, HEAD_PAD - ROPE_LANE - QK_ROPE), fill)],
        axis=1).astype(np.float32)
    return pad(np.cos(ang), 1.0), pad(np.sin(ang), 0.0)


def kernel(x_prompt, x_sample, cache_ckv, cache_krope, c, c_ctx, w_ada, b_ada, w_in, q_norm_g, w_uq,
           kv_norm_g, w_ukv, w_f_out, w_a_out, w_out, final_norm_g):
    assert w_in.shape[0] == 1
    b_ctx, n_ctx, _ = x_prompt.shape
    dec_b, n_lat, _ = x_sample.shape
    past = cache_ckv.shape[2]

    w_q3 = w_uq[0].reshape(Q_RANK, N_HEADS, QK_NOPE + QK_ROPE)
    head_pad = lambda a, left: jnp.pad(
        a, ((0, 0), (0, 0), (left, HEAD_PAD - left - a.shape[2]))).reshape(Q_RANK, N_HEADS * HEAD_PAD)
    rp = w_q3[:, :, QK_NOPE:].reshape(Q_RANK, N_HEADS, 2, 2, QK_ROPE // 4)
    w_q_rot = jnp.stack([-rp[:, :, :, 1], rp[:, :, :, 0]], axis=3).reshape(Q_RANK, N_HEADS, QK_ROPE)
    w_uq_p = jnp.concatenate([head_pad(w_q3, 0), head_pad(w_q_rot, ROPE_LANE)], axis=1).astype(BF16)
    w_kv3 = w_ukv[0].reshape(KV_RANK, N_HEADS, QK_NOPE + V_DIM)
    w_knope = jnp.pad(w_kv3[:, :, :QK_NOPE], ((0, 0), (0, 0), (0, HEAD_PAD - QK_NOPE)))
    place = np.zeros((HEAD_PAD, N_HEADS, HEAD_PAD), np.float32)
    for j in range(QK_ROPE):
        place[ROPE_LANE + j, :, ROPE_LANE + j] = 1.0
    w_k = jnp.concatenate([w_knope.reshape(KV_RANK, -1), jnp.asarray(place).reshape(HEAD_PAD, -1)],
                          axis=0).astype(BF16)
    w_uvt = jnp.pad(jnp.transpose(w_kv3[:, :, QK_NOPE:], (1, 2, 0)),
                    ((0, 0), (0, V_ROWS - V_DIM), (0, 0))).reshape(VT_ROWS, KV_RANK).astype(BF16)
    ones_col = np.zeros((N_HEADS, V_ROWS, LANES), np.float32)
    ones_col[:, V_DIM, :] = 1.0
    ones_col = jnp.asarray(ones_col.reshape(VT_ROWS, LANES))
    w_f = w_f_out[0].astype(BF16)
    w_a = w_a_out[0].astype(BF16)
    w_o = (0.5 * w_out[0]).astype(BF16)
    qg = q_norm_g[0].reshape(1, Q_RANK)
    kvg = kv_norm_g[0].reshape(1, KV_RANK)
    fg = final_norm_g.reshape(1, D_MODEL)
    wts = (qg, kvg, w_uq_p, w_k, w_uvt, ones_col, _channel_dft_table())
    w_in_p = _pack_w_in(jnp.swapaxes(w_in[0], 0, 1))

    cond8 = jnp.concatenate([c_ctx[None, :], c, jnp.zeros((8 - 1 - dec_b, D_MODEL), F32)], axis=0)
    mod3 = _adaln(cond8, w_ada[0], b_ada[0].reshape(1, -1))

    xp = x_prompt.reshape(b_ctx * n_ctx, D_MODEL)
    ucs, q, k, vt, state_ckv, state_krope = _layer_in(xp, mod3, w_in_p, wts, None, n=n_ctx, state=True)
    f = _pos_dft(ucs, n=n_ctx)
    attn = _attention(q, [k], [vt], [n_ctx], n=n_ctx, after=f)
    y_prompt = _merge(xp, mod3, f, attn, w_in_p, w_f, w_a, w_o, fg, n=n_ctx, latent=False)

    xs = x_sample.reshape(dec_b * n_lat, D_MODEL)
    krope_pad = jnp.pad(cache_krope[:, 0], ((0, 0), (0, 0), (ROPE_LANE, HEAD_PAD - ROPE_LANE - QK_ROPE)))
    cache = (cache_ckv[:, 0].reshape(dec_b * past, KV_RANK), krope_pad.reshape(dec_b * past, HEAD_PAD))
    ucs, q, k, vt, k_c, vt_c = _layer_in(xs, mod3, w_in_p, wts, _rope_tables(n_lat), n=n_lat, state=False,
                                         cache=cache)
    f = _pos_dft(ucs, n=n_lat)
    attn = _attention(q, [k, k_c], [vt, vt_c], [n_lat, past], n=n_lat, after=f)
    y_sample = _merge(xs, mod3, f, attn, w_in_p, w_f, w_a, w_o, fg, n=n_lat, latent=True)

    return (y_prompt.reshape(b_ctx, n_ctx, D_MODEL), y_sample.reshape(dec_b, n_lat, D_MODEL),
            state_ckv.reshape(b_ctx, 1, n_ctx, KV_RANK), jnp.swapaxes(state_krope, 1, 2)[:, None])
```

```python
import functools
import math

import jax
import jax.numpy as jnp
import numpy as np
from jax import lax
from jax.experimental import pallas as pl
from jax.experimental.pallas import tpu as pltpu

F32 = jnp.float32
BF16 = jnp.bfloat16

D_MODEL = 1024
GRID_W = 64
N_GROUPS = 4
GROUP = 128
D_FOURIER = N_GROUPS * GROUP
N_HEADS = 8
QK_NOPE = 64
QK_ROPE = 32
V_DIM = 64
Q_RANK = 256
KV_RANK = 128
D_ATTN = N_HEADS * V_DIM
D_IN = 2 * D_FOURIER + Q_RANK + KV_RANK + QK_ROPE + D_ATTN + 2 * D_MODEL
ROPE_THETA = 10000.0
EPS = 1e-6
LANES = 128
HEAD_PAD = LANES
ROPE_LANE = QK_NOPE
V_ROWS = V_DIM + 16
VT_ROWS = N_HEADS * V_ROWS
Q_SCALE = (QK_NOPE + QK_ROPE) ** -0.5 * math.log2(math.e)
KEY_CHUNK = 256
Q_TILE = 256
ATTENTION_UNITS = 4
SHORT_DFT_SEQUENCES = 8
REV_BLOCK = 256
DFT_SPLIT = 64
MIN_DENOMINATOR = 2.0 ** -60
MAX_DENOMINATOR = 2.0 ** 60
W_BLK = 512
B_UF, B_ZF, B_MID, B_ZA = 0, 1, 2, 3
D_IN_PAD = 8 * W_BLK
TOKEN_TILE = 1024
LAYER_IN_SUB_TILE = TOKEN_TILE
MERGE_SUB_TILE = TOKEN_TILE // 2
VMEM_LIMIT_BYTES = 56 * 1024 * 1024


def _const_spec(shape):
    nd = len(shape)
    return pl.BlockSpec(shape, lambda *_: (0,) * nd, pipeline_mode=pl.Buffered(1))


def _row_spec(rows, blk):
    return pl.BlockSpec((rows, D_MODEL), lambda *_: (blk, 0), pipeline_mode=pl.Buffered(1))


def _params(n_axes):
    return pltpu.CompilerParams(dimension_semantics=("arbitrary",) * n_axes,
                                vmem_limit_bytes=VMEM_LIMIT_BYTES)


def _rms(x):
    return x * lax.rsqrt(jnp.mean(x * x, axis=-1, keepdims=True) + EPS)


def _sigmoid(x):
    return 0.5 * jnp.tanh(0.5 * x) + 0.5


def _dot(a, b):
    return jnp.dot(a, b, preferred_element_type=F32)


def _dot_nt(a, b):
    return lax.dot_general(a, b, (((1,), (1,)), ((), ())), preferred_element_type=F32)


def _token_tiling(t, n, *, per_sequence):
    tm = TOKEN_TILE
    assert t % tm == 0 and (not per_sequence or n % tm == 0)
    per_seq = n // tm if per_sequence else None
    mod_row = (lambda i: 1 + i // per_seq) if per_sequence else (lambda i: 0)
    return tm, per_seq, mod_row


def _sub_tiles(rows, step):
    return [slice(r, r + step) for r in range(0, rows, step)]


def _modulated(x, mod_ref):
    mod = mod_ref[0]
    shift, scale = mod[:, 0:D_MODEL], mod[:, D_MODEL:2 * D_MODEL]
    return (_rms(x) * (1.0 + scale) + shift).astype(BF16)


def _adaln_kernel(c_ref, w_ref, b_ref, o_ref):
    c = c_ref[...]
    s = (c * _sigmoid(c)).astype(BF16)
    mod = _dot(s, w_ref[...].astype(BF16)) + b_ref[...]
    for r in range(mod.shape[0]):
        o_ref[r] = mod[r:r + 1]


def _adaln(cond8, w_ada, b_ada):
    n_blk = 4
    bw = 3 * D_MODEL // n_blk
    return pl.pallas_call(
        _adaln_kernel,
        out_shape=jax.ShapeDtypeStruct((8, 1, 3 * D_MODEL), F32),
        grid=(n_blk,),
        in_specs=[pl.BlockSpec((8, D_MODEL), lambda j: (0, 0)),
                  pl.BlockSpec((D_MODEL, bw), lambda j: (0, j)),
                  pl.BlockSpec((1, bw), lambda j: (0, j))],
        out_specs=pl.BlockSpec((8, 1, bw), lambda j: (0, 0, j)),
        compiler_params=_params(1),
        name="adaln",
    )(cond8, w_ada, b_ada)


def _rot_rows(w):
    q = QK_ROPE // 4
    return jnp.concatenate([-w[q:2 * q], w[0:q], -w[3 * q:4 * q], w[2 * q:3 * q]], axis=0)


def _pack_w_in_kernel(wt_ref, o_ref):
    lo = 2 * D_FOURIER + Q_RANK + KV_RANK
    zf = slice(B_ZF * W_BLK, (B_ZF + 1) * W_BLK)
    o_ref[:zf.start] = wt_ref[:zf.start].astype(BF16)
    o_ref[zf] = (0.5 * wt_ref[zf]).astype(BF16)
    o_ref[zf.stop:lo] = wt_ref[zf.stop:lo].astype(BF16)
    kr = wt_ref[lo:lo + QK_ROPE]
    zeros = jnp.zeros((QK_ROPE, D_MODEL), F32)
    o_ref[lo:lo + LANES] = jnp.concatenate([_rot_rows(kr), zeros, kr, zeros], axis=0).astype(BF16)
    o_ref[lo + LANES:] = (0.5 * wt_ref[lo + QK_ROPE:]).astype(BF16)


def _pack_w_in(w_in_t):
    return pl.pallas_call(
        _pack_w_in_kernel,
        out_shape=jax.ShapeDtypeStruct((D_IN_PAD, D_MODEL), BF16),
        grid=(1,),
        in_specs=[_const_spec((D_IN, D_MODEL))],
        out_specs=_const_spec((D_IN_PAD, D_MODEL)),
        compiler_params=_params(1),
        name="pack_w_in",
    )(w_in_t)


def _kv_outputs(ckvn_b, kr_b, w_k_ref, w_uvt_ref, ones_ref, k_ref, vt_ref, rows):
    tm = ckvn_b.shape[0]
    kp = _dot(jnp.concatenate([ckvn_b, kr_b], axis=1), w_k_ref[...])
    for hh in range(N_HEADS):
        k_ref[hh, rows] = kp[:, hh * HEAD_PAD:(hh + 1) * HEAD_PAD].astype(BF16)
    ones = jnp.concatenate([ones_ref[...]] * (tm // LANES), axis=1)
    vt_ref[:, rows] = (_dot_nt(w_uvt_ref[...], ckvn_b) + ones).astype(BF16)


def _layer_in_kernel(*refs, rope, state, cache, pos_dft):
    (x_ref, mod_ref, w_uf_ref, w_mid_ref, qg_ref, kvg_ref, w_uq_ref, w_k_ref, w_uvt_ref, ones_ref,
     cs_ref), refs = refs[:11], refs[11:]
    if rope:
        (cos_ref, sin_ref), refs = refs[:2], refs[2:]
    if cache:
        (cache_ckv_ref, cache_kr_ref), refs = refs[:2], refs[2:]
    if pos_dft:
        (cn_ref, sn_ref), refs = refs[:2], refs[2:]
    (ucs_ref, q_ref, k_ref, vt_ref), refs = refs[:4], refs[4:]
    if cache:
        (cache_k_ref, cache_vt_ref), refs = refs[:2], refs[2:]
    if state:
        ckvn_ref, krope_ref = refs

    for rows in _sub_tiles(x_ref.shape[0], LAYER_IN_SUB_TILE):
        h = _modulated(x_ref[rows], mod_ref)

        mid = _dot_nt(h, w_mid_ref[...])
        cq, ckv, kr = mid[:, :Q_RANK], mid[:, Q_RANK:Q_RANK + KV_RANK], mid[:, Q_RANK + KV_RANK:]
        cqn = (_rms(cq) * (qg_ref[...] * Q_SCALE)).astype(BF16)
        ckvn = _rms(ckv) * kvg_ref[...]
        if state:
            ckvn_ref[rows] = ckvn
            n = krope_ref.shape[2]
            kr_t = kr.T[ROPE_LANE:ROPE_LANE + QK_ROPE]
            for s in range(kr.shape[0] // n):
                krope_ref[rows.start // n + s] = kr_t[:, s * n:(s + 1) * n]
        n_q = N_HEADS * HEAD_PAD
        if rope:
            cos, sin = cos_ref[rows], sin_ref[rows]
            kr = kr * cos + pltpu.roll(kr, ROPE_LANE, 1) * sin
            qp = _dot(cqn, w_uq_ref[...])
        else:
            qp = _dot(cqn, w_uq_ref[:, :n_q])
        for hh in range(N_HEADS):
            qh = qp[:, hh * HEAD_PAD:(hh + 1) * HEAD_PAD]
            if rope:
                qh = qh * cos + qp[:, n_q + hh * HEAD_PAD:n_q + (hh + 1) * HEAD_PAD] * sin
            q_ref[hh, rows] = qh.astype(BF16)
        _kv_outputs(ckvn.astype(BF16), kr.astype(BF16), w_k_ref, w_uvt_ref, ones_ref, k_ref, vt_ref, rows)

        u = _dot_nt(h, w_uf_ref[...]).astype(BF16)
        uc, us = [], []
        for g in range(N_GROUPS):
            r = _dot(u[:, g * GROUP:(g + 1) * GROUP], cs_ref[...])
            if pos_dft:
                uc.append(r[:, :GROUP].astype(BF16))
                us.append(r[:, GROUP:].astype(BF16))
            else:
                ucs_ref[rows, g * GROUP:(g + 1) * GROUP] = r[:, :GROUP].astype(BF16)
                ucs_ref[rows, D_FOURIER + g * GROUP:D_FOURIER + (g + 1) * GROUP] = r[:, GROUP:].astype(BF16)
        if pos_dft:
            uc, us = jnp.concatenate(uc, axis=1), jnp.concatenate(us, axis=1)
            n = cn_ref.shape[0]
            for s in range(uc.shape[0] // n):
                seq = slice(s * n, (s + 1) * n)
                f = _dot(cn_ref[...], uc[seq]) + _dot(sn_ref[...], us[seq])
                ucs_ref[rows.start + s * n:rows.start + (s + 1) * n] = f.astype(BF16)

    if cache:
        @pl.when(pl.program_id(0) == 0)
        def _():
            _kv_outputs(cache_ckv_ref[...].astype(BF16), cache_kr_ref[...].astype(BF16), w_k_ref, w_uvt_ref,
                        ones_ref, cache_k_ref, cache_vt_ref, slice(None))


def _layer_in(x, mod3, w_in_p, wts, rope_tabs, *, n, state, cache=None, dft_tabs=None):
    t = x.shape[0]
    rope = rope_tabs is not None
    tm, per_seq, mod_row = _token_tiling(t, n, per_sequence=rope)
    tok = lambda w: pl.BlockSpec((tm, w), lambda i: (i, 0))
    in_specs = [tok(D_MODEL),
                pl.BlockSpec((1, 1, 3 * D_MODEL), lambda i: (mod_row(i), 0, 0)),
                _row_spec(W_BLK, B_UF), _row_spec(W_BLK, B_MID)]
    in_specs += [_const_spec(w.shape) for w in wts]
    args = [x, mod3, w_in_p, w_in_p, *wts]
    if rope:
        in_specs += [pl.BlockSpec((tm, HEAD_PAD), lambda i: (i % per_seq, 0))] * 2
        args += list(rope_tabs)
    head = pl.BlockSpec((N_HEADS, tm, HEAD_PAD), lambda i: (0, i, 0))
    out_specs = [tok(2 * D_FOURIER), head, head, pl.BlockSpec((VT_ROWS, tm), lambda i: (0, i))]
    out_shape = [jax.ShapeDtypeStruct((t, 2 * D_FOURIER), BF16),
                 jax.ShapeDtypeStruct((N_HEADS, t, HEAD_PAD), BF16),
                 jax.ShapeDtypeStruct((N_HEADS, t, HEAD_PAD), BF16),
                 jax.ShapeDtypeStruct((VT_ROWS, t), BF16)]
    if cache is not None:
        t_c = cache[0].shape[0]
        in_specs += [_const_spec(a.shape) for a in cache]
        args += list(cache)
        cache_out = [jax.ShapeDtypeStruct((N_HEADS, t_c, HEAD_PAD), BF16), jax.ShapeDtypeStruct((VT_ROWS, t_c), BF16)]
        out_specs += [_const_spec(s.shape) for s in cache_out]
        out_shape += cache_out
    if dft_tabs is not None:
        assert tm % n == 0
        in_specs += [_const_spec(a.shape) for a in dft_tabs]
        args += list(dft_tabs)
        out_specs[0], out_shape[0] = tok(D_FOURIER), jax.ShapeDtypeStruct((t, D_FOURIER), BF16)
    if state:
        out_specs += [tok(KV_RANK), pl.BlockSpec((tm // n, QK_ROPE, n), lambda i: (i, 0, 0))]
        out_shape += [jax.ShapeDtypeStruct((t, KV_RANK), F32), jax.ShapeDtypeStruct((t // n, QK_ROPE, n), F32)]
    return pl.pallas_call(
        functools.partial(_layer_in_kernel, rope=rope, state=state, cache=cache is not None,
                          pos_dft=dft_tabs is not None),
        out_shape=out_shape,
        grid=(t // tm,),
        in_specs=in_specs,
        out_specs=out_specs,
        compiler_params=_params(1),
        name="layer_in_latent" if rope else "layer_in_context",
    )(*args)


def _pos_dft_direct_kernel(cn_ref, sn_ref, ucs_ref, f_ref):
    n = cn_ref.shape[0]
    for s in range(f_ref.shape[0] // n):
        rows = slice(s * n, (s + 1) * n)
        f = _dot(cn_ref[...], ucs_ref[rows, :D_FOURIER]) + _dot(sn_ref[...], ucs_ref[rows, D_FOURIER:])
        f_ref[rows] = f.astype(BF16)


def _pos_dft_half_kernel(c1_ref, s1_ref, c0_ref, s0_ref, rev_ref, rev1_ref, ucs_ref, f_ref, cn_ref, sn_ref):
    n = f_ref.shape[0]
    half = n // 2
    blocks = half // REV_BLOCK

    @pl.when(pl.program_id(0) == 0)
    def _():
        c0, s0 = c0_ref[...], s0_ref[...]
        for k1 in range(c1_ref.shape[0]):
            rows = slice(k1 * DFT_SPLIT, min((k1 + 1) * DFT_SPLIT, cn_ref.shape[0]))
            m = rows.stop - rows.start
            c1, s1 = c1_ref[k1:k1 + 1, :], s1_ref[k1:k1 + 1, :]
            cn_ref[rows] = (c1 * c0[:m] - s1 * s0[:m]).astype(BF16)
            sn_ref[rows] = (-(s1 * c0[:m] + c1 * s0[:m])).astype(BF16)

    def folded(cols, sign):
        first_row = lax.broadcasted_iota(jnp.int32, (REV_BLOCK, D_FOURIER), 0) == 0
        out = []
        for b in range(blocks):
            m = _dot(rev1_ref[...], ucs_ref[n - (b + 1) * REV_BLOCK:n - b * REV_BLOCK, cols])
            if b > 0:
                m = jnp.where(first_row, ucs_ref[n - b * REV_BLOCK:n - b * REV_BLOCK + 1, cols].astype(F32), m)
            out.append((ucs_ref[b * REV_BLOCK:(b + 1) * REV_BLOCK, cols] + sign * m).astype(BF16))
        return jnp.concatenate(out, axis=0)

    e = folded(slice(0, D_FOURIER), 1.0)
    o = folded(slice(D_FOURIER, 2 * D_FOURIER), -1.0)
    odd_k = (lax.broadcasted_iota(jnp.int32, (cn_ref.shape[0], D_FOURIER), 0) & 1) == 1
    mid = ucs_ref[half:half + 1, :D_FOURIER].astype(F32) * (1.0 / math.sqrt(n))
    g = _dot(cn_ref[...], e) + jnp.where(odd_k, -mid, mid)
    hn = _dot(sn_ref[...], o)
    f_ref[:half] = (g[:half] + hn[:half]).astype(BF16)
    mirrored = (g[1:half + 1] - hn[1:half + 1]).astype(BF16)
    for a in range(blocks):
        blk = mirrored[(blocks - 1 - a) * REV_BLOCK:(blocks - a) * REV_BLOCK]
        f_ref[half + a * REV_BLOCK:half + (a + 1) * REV_BLOCK] = _dot(rev_ref[...], blk).astype(BF16)


def _pos_dft(ucs, *, n):
    t = ucs.shape[0]
    out_shape = jax.ShapeDtypeStruct((t, D_FOURIER), BF16)
    if n <= REV_BLOCK:
        cn, snn = _dft_tables(n)
        seqs = min(SHORT_DFT_SEQUENCES, t // n)
        return pl.pallas_call(
            _pos_dft_direct_kernel,
            out_shape=out_shape,
            grid=(t // (n * seqs),),
            in_specs=[_const_spec(cn.shape), _const_spec(snn.shape),
                      pl.BlockSpec((n * seqs, 2 * D_FOURIER), lambda i: (i, 0))],
            out_specs=pl.BlockSpec((n * seqs, D_FOURIER), lambda i: (i, 0)),
            compiler_params=_params(1),
            name=f"pos_dft_{n}",
        )(cn, snn, ucs)
    rows = n // 2 + 8
    consts = _dft_factors(n, rows, n // 2)
    anti = np.eye(REV_BLOCK, dtype=np.float32)[::-1]
    shifted = np.roll(anti, 1, axis=1)
    shifted[0] = 0.0
    consts += (jnp.asarray(anti, dtype=BF16), jnp.asarray(shifted, dtype=BF16))
    return pl.pallas_call(
        _pos_dft_half_kernel,
        out_shape=out_shape,
        grid=(t // n,),
        in_specs=[_const_spec(a.shape) for a in consts]
        + [pl.BlockSpec((n, 2 * D_FOURIER), lambda i: (i, 0))],
        out_specs=pl.BlockSpec((n, D_FOURIER), lambda i: (i, 0)),
        scratch_shapes=[pltpu.VMEM((rows, n // 2), BF16), pltpu.VMEM((rows, n // 2), BF16)],
        compiler_params=_params(1),
        name=f"pos_dft_{n}",
    )(*consts, ucs)


def _attention_kernel(*refs, n_kv, single_pass, units):
    q_ref, refs = refs[0], refs[1:]
    k_refs, vt_refs, refs = refs[:n_kv], refs[n_kv:2 * n_kv], refs[2 * n_kv:]
    o_ref, s_ref, pp_ref, ot_ref = refs[-4:]
    tq = q_ref.shape[1] // units
    key_lens = [k_ref.shape[1] // (1 if single_pass else units) for k_ref in k_refs]
    chunks, base = [], 0
    for j, m in enumerate(key_lens):
        chunks += [(j, c, base + c) for c in range(0, m, KEY_CHUNK)]
        base += m

    def q_block(hh, u):
        return q_ref[hh, u * tq:(u + 1) * tq, :]

    def finish(hh, ot, u):
        l = ot[V_DIM:V_DIM + 1]
        ot_ref[u, pl.ds(pl.multiple_of(hh * V_DIM, V_DIM), V_DIM), :] = ot[:V_DIM] / l
        return l

    def first_chunk_scores(hh, u):
        j, c, _ = chunks[0]
        s = _dot_nt(k_refs[j][hh, c:c + KEY_CHUNK, :], q_block(hh, u))
        s_ref[u, hh * KEY_CHUNK:(hh + 1) * KEY_CHUNK, :] = s
        return jnp.max(s, axis=0, keepdims=True)

    def shifted_softmax_pv(score_head, pv_head, shifts, u):
        if score_head is not None:
            qh = q_block(score_head, u)
            shift = shifts[score_head]
        ot = None
        for idx, (j, c, r) in enumerate(chunks):
            if score_head is not None:
                s = (s_ref[u, score_head * KEY_CHUNK:(score_head + 1) * KEY_CHUNK, :] if idx == 0 else
                     _dot_nt(k_refs[j][score_head, c:c + KEY_CHUNK, :], qh))
                pp_ref[2 * u + score_head % 2, r:r + KEY_CHUNK, :] = jnp.exp2(s - shift).astype(BF16)
            if pv_head is not None:
                part = _dot(vt_refs[j][pv_head * V_ROWS:(pv_head + 1) * V_ROWS, c:c + KEY_CHUNK],
                            pp_ref[2 * u + pv_head % 2, r:r + KEY_CHUNK, :])
                ot = part if ot is None else ot + part
        return finish(pv_head, ot, u) if pv_head is not None else None

    def exact_scores(hh, slot, u, kseq):
        qh = q_block(hh, u)
        m8 = None
        for j, c, r in chunks:
            k0 = kseq * key_lens[j] + c
            s = _dot_nt(k_refs[j][hh, k0:k0 + KEY_CHUNK, :], qh)
            s_ref[slot, r:r + KEY_CHUNK, :] = s
            mc = jnp.max(s.reshape(KEY_CHUNK // 8, 8, tq), axis=0)
            m8 = mc if m8 is None else jnp.maximum(m8, mc)
        return jnp.max(m8, axis=0, keepdims=True)

    def exact_softmax_pv(hh, s_slot, p_slot, m, u, kseq):
        for _, _, r in chunks:
            pp_ref[p_slot, r:r + KEY_CHUNK, :] = jnp.exp2(s_ref[s_slot, r:r + KEY_CHUNK, :] - m).astype(BF16)
        rows = pl.ds(pl.multiple_of(hh * V_ROWS, V_ROWS), V_ROWS)
        ot, base = None, 0
        for j, vt_ref in enumerate(vt_refs):
            mk = key_lens[j]
            part = _dot(vt_ref[rows, kseq * mk:(kseq + 1) * mk], pp_ref[p_slot, base:base + mk, :])
            ot = part if ot is None else ot + part
            base += mk
        finish(hh, ot, u)

    if not single_pass:
        work = [(u, hh) for u in range(units) for hh in range(N_HEADS)]
        maxima = [exact_scores(hh, slot, u, u) for slot, (u, hh) in enumerate(work)]
        for slot, (u, hh) in enumerate(work):
            exact_softmax_pv(hh, slot, slot, maxima[slot], u, u)
    else:
        trusted = []
        for u in range(units):
            l_min = l_max = None
            shifts = [first_chunk_scores(hh, u) for hh in range(N_HEADS)]
            for hh in range(N_HEADS + 1):
                l = shifted_softmax_pv(hh if hh < N_HEADS else None, hh - 1 if hh > 0 else None, shifts, u)
                if l is not None:
                    l_min = l if l_min is None else jnp.minimum(l_min, l)
                    l_max = l if l_max is None else jnp.maximum(l_max, l)
            trusted.append(jnp.logical_and(jnp.min(l_min) >= MIN_DENOMINATOR,
                                           jnp.max(l_max) <= MAX_DENOMINATOR))

        for u in range(units):
            @pl.when(jnp.logical_not(trusted[u]))
            def _(u=u):
                def body(hh, carry):
                    exact_softmax_pv(hh, u, 2 * u, exact_scores(hh, u, u, 0), u, 0)
                    return carry

                lax.fori_loop(0, N_HEADS, body, 0)

    for u in range(units):
        o_ref[u * tq:(u + 1) * tq, :] = ot_ref[u].T.astype(BF16)


def _attention(q, ks, vts, key_lens, *, n, after=None):
    t = q.shape[1]
    tq = Q_TILE
    m_tot = sum(key_lens)
    single_pass = m_tot > 2 * KEY_CHUNK
    if single_pass:
        units = ATTENTION_UNITS
        steps = n // (units * tq)
        kv_rows = 1
        s_slots, p_slots = units, 2 * units
    else:
        assert n == tq
        units = min(ATTENTION_UNITS, t // n)
        steps = 1
        kv_rows = units
        s_slots = p_slots = units * N_HEADS
    in_specs = [pl.BlockSpec((N_HEADS, units * tq, HEAD_PAD), lambda bi, i: (0, bi * steps + i, 0))]
    in_specs += [pl.BlockSpec((N_HEADS, kv_rows * m, HEAD_PAD), lambda bi, i: (0, bi, 0)) for m in key_lens]
    in_specs += [pl.BlockSpec((VT_ROWS, kv_rows * m), lambda bi, i: (0, bi)) for m in key_lens]
    args = [q, *ks, *vts]
    if after is not None:
        in_specs.append(pl.BlockSpec((8, after.shape[1]), lambda bi, i: (0, 0)))
        args.append(after)
    return pl.pallas_call(
        functools.partial(_attention_kernel, n_kv=len(ks), single_pass=single_pass, units=units),
        out_shape=jax.ShapeDtypeStruct((t, D_ATTN), BF16),
        grid=(t // (n * kv_rows), steps),
        in_specs=in_specs,
        out_specs=pl.BlockSpec((units * tq, D_ATTN), lambda bi, i: (bi * steps + i, 0)),
        scratch_shapes=[pltpu.VMEM((s_slots, m_tot, tq), F32), pltpu.VMEM((p_slots, m_tot, tq), BF16),
                        pltpu.VMEM((units, D_ATTN, tq), F32)],
        compiler_params=_params(2),
        name=f"attention_{n}",
    )(*args)


def _merge_kernel(x_ref, mod_ref, f_ref, a_ref, w_zf_ref, w_za_ref, w_g_ref, w_f_ref, w_a_ref, w_o_ref,
                  g_ref, y_ref):
    gate = mod_ref[0][:, 2 * D_MODEL:]
    for rows in _sub_tiles(x_ref.shape[0], MERGE_SUB_TILE):
        x = x_ref[rows]
        h = _modulated(x, mod_ref)
        zf = _dot_nt(h, w_zf_ref[...])
        y_f = _dot(f_ref[rows] * (zf * (jnp.tanh(zf) + 1.0)).astype(BF16), w_f_ref[...])
        za = _dot_nt(h, w_za_ref[...])
        y_a = _dot(a_ref[rows] * (za * (jnp.tanh(za) + 1.0)).astype(BF16), w_a_ref[...])
        merged_x2 = ((jnp.tanh(_dot_nt(h, w_g_ref[:D_MODEL])) + 1.0) * y_f
                     + (jnp.tanh(_dot_nt(h, w_g_ref[D_MODEL:])) + 1.0) * y_a)
        out = x + gate * _dot(merged_x2.astype(BF16), w_o_ref[...])
        y_ref[rows] = _rms(out) * g_ref[...]


def _merge(x, mod3, f, attn, w_in_p, w_f, w_a, w_o, g, *, n, latent):
    t = x.shape[0]
    tm, _, mod_row = _token_tiling(t, n, per_sequence=latent)
    tok = lambda w: pl.BlockSpec((tm, w), lambda i: (i, 0))
    return pl.pallas_call(
        _merge_kernel,
        out_shape=jax.ShapeDtypeStruct((t, D_MODEL), F32),
        grid=(t // tm,),
        in_specs=[tok(D_MODEL),
                  pl.BlockSpec((1, 1, 3 * D_MODEL), lambda i: (mod_row(i), 0, 0)),
                  tok(D_FOURIER), tok(D_ATTN),
                  _row_spec(W_BLK, B_ZF), _row_spec(W_BLK, B_ZA), _row_spec(2 * D_MODEL, 1),
                  _const_spec(w_f.shape), _const_spec(w_a.shape), _const_spec(w_o.shape),
                  _const_spec(g.shape)],
        out_specs=tok(D_MODEL),
        compiler_params=_params(1),
        name=f"merge_{n}",
    )(x, mod3, f, attn, w_in_p, w_in_p, w_in_p, w_f, w_a, w_o, g)


def _dft_tables(n):
    norm = 1.0 / math.sqrt(n)
    k = np.arange(n)
    ang = ((k[:, None] * k[None, :]) % n) * (2.0 * math.pi / n)
    return (jnp.asarray(np.cos(ang) * norm, dtype=F32).astype(BF16),
            jnp.asarray(np.sin(ang) * -norm, dtype=F32).astype(BF16))


def _dft_factors(n, rows, n_pos):
    norm = 1.0 / math.sqrt(n)
    period = n // DFT_SPLIT
    pos = np.arange(n_pos)
    k1 = np.arange(-(-rows // DFT_SPLIT))
    k0 = np.arange(DFT_SPLIT)
    a1 = ((k1[:, None] * pos[None, :]) % period) * (2.0 * math.pi / period)
    a0 = ((k0[:, None] * pos[None, :]) % n) * (2.0 * math.pi / n)
    return tuple(a.astype(np.float32) for a in (np.cos(a1), np.sin(a1), np.cos(a0) * norm, np.sin(a0) * norm))


def _channel_dft_table():
    c = np.arange(GROUP)
    ang = ((c[:, None] * c[None, :]) % GROUP) * (2.0 * math.pi / GROUP)
    norm = 1.0 / math.sqrt(GROUP)
    return jnp.asarray(np.concatenate([np.cos(ang), np.sin(ang)], axis=1) * norm, dtype=F32).astype(BF16)


def _rope_tables(n):
    t = np.arange(n)
    row = (t // GRID_W).astype(np.float64)
    col = (t % GRID_W).astype(np.float64)
    half = QK_ROPE // 2
    inv = ROPE_THETA ** (-np.arange(0, half, 2, dtype=np.float64) / half)
    ar, ac = row[:, None] * inv, col[:, None] * inv
    ang = np.concatenate([ar, ar, ac, ac], axis=-1)
    pad = lambda a, fill: np.concatenate(
        [np.full((n, ROPE_LANE), fill), a, np.full((n, HEAD_PAD - ROPE_LANE - QK_ROPE), fill)],
        axis=1).astype(np.float32)
    return pad(np.cos(ang), 1.0), pad(np.sin(ang), 0.0)


def kernel(x_prompt, x_sample, cache_ckv, cache_krope, c, c_ctx, w_ada, b_ada, w_in, q_norm_g, w_uq,
           kv_norm_g, w_ukv, w_f_out, w_a_out, w_out, final_norm_g):
    assert w_in.shape[0] == 1
    b_ctx, n_ctx, _ = x_prompt.shape
    dec_b, n_lat, _ = x_sample.shape
    past = cache_ckv.shape[2]

    w_q3 = w_uq[0].reshape(Q_RANK, N_HEADS, QK_NOPE + QK_ROPE)
    head_pad = lambda a, left: jnp.pad(
        a, ((0, 0), (0, 0), (left, HEAD_PAD - left - a.shape[2]))).reshape(Q_RANK, N_HEADS * HEAD_PAD)
    rp = w_q3[:, :, QK_NOPE:].reshape(Q_RANK, N_HEADS, 2, 2, QK_ROPE // 4)
    w_q_rot = jnp.stack([-rp[:, :, :, 1], rp[:, :, :, 0]], axis=3).reshape(Q_RANK, N_HEADS, QK_ROPE)
    w_uq_p = jnp.concatenate([head_pad(w_q3, 0), head_pad(w_q_rot, ROPE_LANE)], axis=1).astype(BF16)
    w_kv3 = w_ukv[0].reshape(KV_RANK, N_HEADS, QK_NOPE + V_DIM)
    w_knope = jnp.pad(w_kv3[:, :, :QK_NOPE], ((0, 0), (0, 0), (0, HEAD_PAD - QK_NOPE)))
    place = np.zeros((HEAD_PAD, N_HEADS, HEAD_PAD), np.float32)
    for j in range(QK_ROPE):
        place[ROPE_LANE + j, :, ROPE_LANE + j] = 1.0
    w_k = jnp.concatenate([w_knope.reshape(KV_RANK, -1), jnp.asarray(place).reshape(HEAD_PAD, -1)],
                          axis=0).astype(BF16)
    w_uvt = jnp.pad(jnp.transpose(w_kv3[:, :, QK_NOPE:], (1, 2, 0)),
                    ((0, 0), (0, V_ROWS - V_DIM), (0, 0))).reshape(VT_ROWS, KV_RANK).astype(BF16)
    ones_col = np.zeros((N_HEADS, V_ROWS, LANES), np.float32)
    ones_col[:, V_DIM, :] = 1.0
    ones_col = jnp.asarray(ones_col.reshape(VT_ROWS, LANES))
    w_f = w_f_out[0].astype(BF16)
    w_a = w_a_out[0].astype(BF16)
    w_o = (0.5 * w_out[0]).astype(BF16)
    qg = q_norm_g[0].reshape(1, Q_RANK)
    kvg = kv_norm_g[0].reshape(1, KV_RANK)
    fg = final_norm_g.reshape(1, D_MODEL)
    wts = (qg, kvg, w_uq_p, w_k, w_uvt, ones_col, _channel_dft_table())
    w_in_p = _pack_w_in(jnp.swapaxes(w_in[0], 0, 1))

    cond8 = jnp.concatenate([c_ctx[None, :], c, jnp.zeros((8 - 1 - dec_b, D_MODEL), F32)], axis=0)
    mod3 = _adaln(cond8, w_ada[0], b_ada[0].reshape(1, -1))

    xp = x_prompt.reshape(b_ctx * n_ctx, D_MODEL)
    if n_ctx <= REV_BLOCK and TOKEN_TILE % n_ctx == 0:
        f, q, k, vt, state_ckv, state_krope = _layer_in(xp, mod3, w_in_p, wts, None, n=n_ctx, state=True,
                                                        dft_tabs=_dft_tables(n_ctx))
    else:
        ucs, q, k, vt, state_ckv, state_krope = _layer_in(xp, mod3, w_in_p, wts, None, n=n_ctx, state=True)
        f = _pos_dft(ucs, n=n_ctx)
    attn = _attention(q, [k], [vt], [n_ctx], n=n_ctx, after=f)
    y_prompt = _merge(xp, mod3, f, attn, w_in_p, w_f, w_a, w_o, fg, n=n_ctx, latent=False)

    xs = x_sample.reshape(dec_b * n_lat, D_MODEL)
    krope_pad = jnp.pad(cache_krope[:, 0], ((0, 0), (0, 0), (ROPE_LANE, HEAD_PAD - ROPE_LANE - QK_ROPE)))
    cache = (cache_ckv[:, 0].reshape(dec_b * past, KV_RANK), krope_pad.reshape(dec_b * past, HEAD_PAD))
    ucs, q, k, vt, k_c, vt_c = _layer_in(xs, mod3, w_in_p, wts, _rope_tables(n_lat), n=n_lat, state=False,
                                         cache=cache)
    f = _pos_dft(ucs, n=n_lat)
    attn = _attention(q, [k, k_c], [vt, vt_c], [n_lat, past], n=n_lat, after=f)
    y_sample = _merge(xs, mod3, f, attn, w_in_p, w_f, w_a, w_o, fg, n=n_lat, latent=True)

    return (y_prompt.reshape(b_ctx, n_ctx, D_MODEL), y_sample.reshape(dec_b, n_lat, D_MODEL),
            state_ckv.reshape(b_ctx, 1, n_ctx, KV_RANK), jnp.swapaxes(state_krope, 1, 2)[:, None])
```

```python
import functools
import math

import jax
import jax.numpy as jnp
import numpy as np
from jax import lax
from jax.experimental import pallas as pl
from jax.experimental.pallas import tpu as pltpu

F32 = jnp.float32
BF16 = jnp.bfloat16

D_MODEL = 1024
GRID_W = 64
N_GROUPS = 4
GROUP = 128
D_FOURIER = N_GROUPS * GROUP
N_HEADS = 8
QK_NOPE = 64
QK_ROPE = 32
V_DIM = 64
Q_RANK = 256
KV_RANK = 128
D_ATTN = N_HEADS * V_DIM
D_IN = 2 * D_FOURIER + Q_RANK + KV_RANK + QK_ROPE + D_ATTN + 2 * D_MODEL
ROPE_THETA = 10000.0
EPS = 1e-6
LANES = 128
HEAD_PAD = LANES
ROPE_LANE = QK_NOPE
V_ROWS = V_DIM + 16
VT_ROWS = N_HEADS * V_ROWS
Q_SCALE = (QK_NOPE + QK_ROPE) ** -0.5 * math.log2(math.e)
KEY_CHUNK = 256
Q_TILE = 256
ATTENTION_UNITS = 4
SHORT_DFT_SEQUENCES = 8
REV_BLOCK = 256
DFT_SPLIT = 64
MIN_DENOMINATOR = 2.0 ** -60
MAX_DENOMINATOR = 2.0 ** 60
W_BLK = 512
B_UF, B_ZF, B_MID, B_ZA = 0, 1, 2, 3
D_IN_PAD = 8 * W_BLK
TOKEN_TILE = 1024
LAYER_IN_SUB_TILE = TOKEN_TILE
MERGE_SUB_TILE = TOKEN_TILE // 2
VMEM_LIMIT_BYTES = 56 * 1024 * 1024


def _const_spec(shape):
    nd = len(shape)
    return pl.BlockSpec(shape, lambda *_: (0,) * nd, pipeline_mode=pl.Buffered(1))


def _row_spec(rows, blk):
    return pl.BlockSpec((rows, D_MODEL), lambda *_: (blk, 0), pipeline_mode=pl.Buffered(1))


def _params(n_axes):
    return pltpu.CompilerParams(dimension_semantics=("arbitrary",) * n_axes,
                                vmem_limit_bytes=VMEM_LIMIT_BYTES)


def _rms(x):
    return x * lax.rsqrt(jnp.mean(x * x, axis=-1, keepdims=True) + EPS)


def _sigmoid(x):
    return 0.5 * jnp.tanh(0.5 * x) + 0.5


def _dot(a, b):
    return jnp.dot(a, b, preferred_element_type=F32)


def _dot_nt(a, b):
    return lax.dot_general(a, b, (((1,), (1,)), ((), ())), preferred_element_type=F32)


def _token_tiling(t, n, *, per_sequence):
    tm = TOKEN_TILE
    assert t % tm == 0 and (not per_sequence or n % tm == 0)
    per_seq = n // tm if per_sequence else None
    mod_row = (lambda i: 1 + i // per_seq) if per_sequence else (lambda i: 0)
    return tm, per_seq, mod_row


def _sub_tiles(rows, step):
    return [slice(r, r + step) for r in range(0, rows, step)]


def _modulated(x, mod_ref):
    mod = mod_ref[0]
    shift, scale = mod[:, 0:D_MODEL], mod[:, D_MODEL:2 * D_MODEL]
    return (_rms(x) * (1.0 + scale) + shift).astype(BF16)


def _adaln_pack_kernel(c_ref, w_ref, b_ref, wt_ref, o_ref, wp_ref):
    c = c_ref[...]
    s = (c * _sigmoid(c)).astype(BF16)
    mod = _dot(s, w_ref[...].astype(BF16)) + b_ref[...]
    for r in range(mod.shape[0]):
        o_ref[r] = mod[r:r + 1]

    @pl.when(pl.program_id(0) == 0)
    def _():
        _pack_w_in_kernel(wt_ref, wp_ref)


def _adaln_pack(cond8, w_ada, b_ada, w_in_t):
    n_blk = 4
    bw = 3 * D_MODEL // n_blk
    return pl.pallas_call(
        _adaln_pack_kernel,
        out_shape=[jax.ShapeDtypeStruct((8, 1, 3 * D_MODEL), F32),
                   jax.ShapeDtypeStruct((D_IN_PAD, D_MODEL), BF16)],
        grid=(n_blk,),
        in_specs=[pl.BlockSpec((8, D_MODEL), lambda j: (0, 0)),
                  pl.BlockSpec((D_MODEL, bw), lambda j: (0, j)),
                  pl.BlockSpec((1, bw), lambda j: (0, j)),
                  _const_spec((D_IN, D_MODEL))],
        out_specs=[pl.BlockSpec((8, 1, bw), lambda j: (0, 0, j)), _const_spec((D_IN_PAD, D_MODEL))],
        compiler_params=_params(1),
        name="adaln_pack",
    )(cond8, w_ada, b_ada, w_in_t)


def _rot_rows(w):
    q = QK_ROPE // 4
    return jnp.concatenate([-w[q:2 * q], w[0:q], -w[3 * q:4 * q], w[2 * q:3 * q]], axis=0)


def _pack_w_in_kernel(wt_ref, o_ref):
    lo = 2 * D_FOURIER + Q_RANK + KV_RANK
    zf = slice(B_ZF * W_BLK, (B_ZF + 1) * W_BLK)
    o_ref[:zf.start] = wt_ref[:zf.start].astype(BF16)
    o_ref[zf] = (0.5 * wt_ref[zf]).astype(BF16)
    o_ref[zf.stop:lo] = wt_ref[zf.stop:lo].astype(BF16)
    kr = wt_ref[lo:lo + QK_ROPE]
    zeros = jnp.zeros((QK_ROPE, D_MODEL), F32)
    o_ref[lo:lo + LANES] = jnp.concatenate([_rot_rows(kr), zeros, kr, zeros], axis=0).astype(BF16)
    o_ref[lo + LANES:] = (0.5 * wt_ref[lo + QK_ROPE:]).astype(BF16)


def _kv_outputs(ckvn_b, kr_b, w_k_ref, w_uvt_ref, ones_ref, k_ref, vt_ref, rows):
    tm = ckvn_b.shape[0]
    kp = _dot(jnp.concatenate([ckvn_b, kr_b], axis=1), w_k_ref[...])
    for hh in range(N_HEADS):
        k_ref[hh, rows] = kp[:, hh * HEAD_PAD:(hh + 1) * HEAD_PAD].astype(BF16)
    ones = jnp.concatenate([ones_ref[...]] * (tm // LANES), axis=1)
    vt_ref[:, rows] = (_dot_nt(w_uvt_ref[...], ckvn_b) + ones).astype(BF16)


def _layer_in_kernel(*refs, rope, state, cache, pos_dft):
    (x_ref, mod_ref, w_uf_ref, w_mid_ref, qg_ref, kvg_ref, w_uq_ref, w_k_ref, w_uvt_ref, ones_ref,
     cs_ref), refs = refs[:11], refs[11:]
    if rope:
        (cos_ref, sin_ref), refs = refs[:2], refs[2:]
    if cache:
        (cache_ckv_ref, cache_kr_ref), refs = refs[:2], refs[2:]
    if pos_dft:
        (cn_ref, sn_ref), refs = refs[:2], refs[2:]
    (ucs_ref, q_ref, k_ref, vt_ref), refs = refs[:4], refs[4:]
    if cache:
        (cache_k_ref, cache_vt_ref), refs = refs[:2], refs[2:]
    if state:
        ckvn_ref, krope_ref = refs

    for rows in _sub_tiles(x_ref.shape[0], LAYER_IN_SUB_TILE):
        h = _modulated(x_ref[rows], mod_ref)

        mid = _dot_nt(h, w_mid_ref[...])
        cq, ckv, kr = mid[:, :Q_RANK], mid[:, Q_RANK:Q_RANK + KV_RANK], mid[:, Q_RANK + KV_RANK:]
        cqn = (_rms(cq) * (qg_ref[...] * Q_SCALE)).astype(BF16)
        ckvn = _rms(ckv) * kvg_ref[...]
        if state:
            ckvn_ref[rows] = ckvn
            n = krope_ref.shape[2]
            kr_t = kr.T[ROPE_LANE:ROPE_LANE + QK_ROPE]
            for s in range(kr.shape[0] // n):
                krope_ref[rows.start // n + s] = kr_t[:, s * n:(s + 1) * n]
        n_q = N_HEADS * HEAD_PAD
        if rope:
            cos, sin = cos_ref[rows], sin_ref[rows]
            kr = kr * cos + pltpu.roll(kr, ROPE_LANE, 1) * sin
            qp = _dot(cqn, w_uq_ref[...])
        else:
            qp = _dot(cqn, w_uq_ref[:, :n_q])
        for hh in range(N_HEADS):
            qh = qp[:, hh * HEAD_PAD:(hh + 1) * HEAD_PAD]
            if rope:
                qh = qh * cos + qp[:, n_q + hh * HEAD_PAD:n_q + (hh + 1) * HEAD_PAD] * sin
            q_ref[hh, rows] = qh.astype(BF16)
        _kv_outputs(ckvn.astype(BF16), kr.astype(BF16), w_k_ref, w_uvt_ref, ones_ref, k_ref, vt_ref, rows)

        u = _dot_nt(h, w_uf_ref[...]).astype(BF16)
        uc, us = [], []
        for g in range(N_GROUPS):
            r = _dot(u[:, g * GROUP:(g + 1) * GROUP], cs_ref[...])
            if pos_dft:
                uc.append(r[:, :GROUP].astype(BF16))
                us.append(r[:, GROUP:].astype(BF16))
            else:
                ucs_ref[rows, g * GROUP:(g + 1) * GROUP] = r[:, :GROUP].astype(BF16)
                ucs_ref[rows, D_FOURIER + g * GROUP:D_FOURIER + (g + 1) * GROUP] = r[:, GROUP:].astype(BF16)
        if pos_dft:
            uc, us = jnp.concatenate(uc, axis=1), jnp.concatenate(us, axis=1)
            n = cn_ref.shape[0]
            for s in range(uc.shape[0] // n):
                seq = slice(s * n, (s + 1) * n)
                f = _dot(cn_ref[...], uc[seq]) + _dot(sn_ref[...], us[seq])
                ucs_ref[rows.start + s * n:rows.start + (s + 1) * n] = f.astype(BF16)

    if cache:
        @pl.when(pl.program_id(0) == 0)
        def _():
            _kv_outputs(cache_ckv_ref[...].astype(BF16), cache_kr_ref[...].astype(BF16), w_k_ref, w_uvt_ref,
                        ones_ref, cache_k_ref, cache_vt_ref, slice(None))


def _layer_in(x, mod3, w_in_p, wts, rope_tabs, *, n, state, cache=None, dft_tabs=None):
    t = x.shape[0]
    rope = rope_tabs is not None
    tm, per_seq, mod_row = _token_tiling(t, n, per_sequence=rope)
    tok = lambda w: pl.BlockSpec((tm, w), lambda i: (i, 0))
    in_specs = [tok(D_MODEL),
                pl.BlockSpec((1, 1, 3 * D_MODEL), lambda i: (mod_row(i), 0, 0)),
                _row_spec(W_BLK, B_UF), _row_spec(W_BLK, B_MID)]
    in_specs += [_const_spec(w.shape) for w in wts]
    args = [x, mod3, w_in_p, w_in_p, *wts]
    if rope:
        in_specs += [pl.BlockSpec((tm, HEAD_PAD), lambda i: (i % per_seq, 0))] * 2
        args += list(rope_tabs)
    head = pl.BlockSpec((N_HEADS, tm, HEAD_PAD), lambda i: (0, i, 0))
    out_specs = [tok(2 * D_FOURIER), head, head, pl.BlockSpec((VT_ROWS, tm), lambda i: (0, i))]
    out_shape = [jax.ShapeDtypeStruct((t, 2 * D_FOURIER), BF16),
                 jax.ShapeDtypeStruct((N_HEADS, t, HEAD_PAD), BF16),
                 jax.ShapeDtypeStruct((N_HEADS, t, HEAD_PAD), BF16),
                 jax.ShapeDtypeStruct((VT_ROWS, t), BF16)]
    if cache is not None:
        t_c = cache[0].shape[0]
        in_specs += [_const_spec(a.shape) for a in cache]
        args += list(cache)
        cache_out = [jax.ShapeDtypeStruct((N_HEADS, t_c, HEAD_PAD), BF16), jax.ShapeDtypeStruct((VT_ROWS, t_c), BF16)]
        out_specs += [_const_spec(s.shape) for s in cache_out]
        out_shape += cache_out
    if dft_tabs is not None:
        assert tm % n == 0
        in_specs += [_const_spec(a.shape) for a in dft_tabs]
        args += list(dft_tabs)
        out_specs[0], out_shape[0] = tok(D_FOURIER), jax.ShapeDtypeStruct((t, D_FOURIER), BF16)
    if state:
        out_specs += [tok(KV_RANK), pl.BlockSpec((tm // n, QK_ROPE, n), lambda i: (i, 0, 0))]
        out_shape += [jax.ShapeDtypeStruct((t, KV_RANK), F32), jax.ShapeDtypeStruct((t // n, QK_ROPE, n), F32)]
    return pl.pallas_call(
        functools.partial(_layer_in_kernel, rope=rope, state=state, cache=cache is not None,
                          pos_dft=dft_tabs is not None),
        out_shape=out_shape,
        grid=(t // tm,),
        in_specs=in_specs,
        out_specs=out_specs,
        compiler_params=_params(1),
        name="layer_in_latent" if rope else "layer_in_context",
    )(*args)


def _pos_dft_direct_kernel(cn_ref, sn_ref, ucs_ref, f_ref):
    n = cn_ref.shape[0]
    for s in range(f_ref.shape[0] // n):
        rows = slice(s * n, (s + 1) * n)
        f = _dot(cn_ref[...], ucs_ref[rows, :D_FOURIER]) + _dot(sn_ref[...], ucs_ref[rows, D_FOURIER:])
        f_ref[rows] = f.astype(BF16)


def _pos_dft_half_kernel(c1_ref, s1_ref, c0_ref, s0_ref, rev_ref, rev1_ref, ucs_ref, f_ref, cn_ref, sn_ref):
    n = f_ref.shape[0]
    half = n // 2
    blocks = half // REV_BLOCK

    @pl.when(pl.program_id(0) == 0)
    def _():
        c0, s0 = c0_ref[...], s0_ref[...]
        for k1 in range(c1_ref.shape[0]):
            rows = slice(k1 * DFT_SPLIT, min((k1 + 1) * DFT_SPLIT, cn_ref.shape[0]))
            m = rows.stop - rows.start
            c1, s1 = c1_ref[k1:k1 + 1, :], s1_ref[k1:k1 + 1, :]
            cn_ref[rows] = (c1 * c0[:m] - s1 * s0[:m]).astype(BF16)
            sn_ref[rows] = (-(s1 * c0[:m] + c1 * s0[:m])).astype(BF16)

    def folded(cols, sign):
        first_row = lax.broadcasted_iota(jnp.int32, (REV_BLOCK, D_FOURIER), 0) == 0
        out = []
        for b in range(blocks):
            m = _dot(rev1_ref[...], ucs_ref[n - (b + 1) * REV_BLOCK:n - b * REV_BLOCK, cols])
            if b > 0:
                m = jnp.where(first_row, ucs_ref[n - b * REV_BLOCK:n - b * REV_BLOCK + 1, cols].astype(F32), m)
            out.append((ucs_ref[b * REV_BLOCK:(b + 1) * REV_BLOCK, cols] + sign * m).astype(BF16))
        return jnp.concatenate(out, axis=0)

    e = folded(slice(0, D_FOURIER), 1.0)
    o = folded(slice(D_FOURIER, 2 * D_FOURIER), -1.0)
    odd_k = (lax.broadcasted_iota(jnp.int32, (cn_ref.shape[0], D_FOURIER), 0) & 1) == 1
    mid = ucs_ref[half:half + 1, :D_FOURIER].astype(F32) * (1.0 / math.sqrt(n))
    g = _dot(cn_ref[...], e) + jnp.where(odd_k, -mid, mid)
    hn = _dot(sn_ref[...], o)
    f_ref[:half] = (g[:half] + hn[:half]).astype(BF16)
    mirrored = (g[1:half + 1] - hn[1:half + 1]).astype(BF16)
    for a in range(blocks):
        blk = mirrored[(blocks - 1 - a) * REV_BLOCK:(blocks - a) * REV_BLOCK]
        f_ref[half + a * REV_BLOCK:half + (a + 1) * REV_BLOCK] = _dot(rev_ref[...], blk).astype(BF16)


def _pos_dft(ucs, *, n):
    t = ucs.shape[0]
    out_shape = jax.ShapeDtypeStruct((t, D_FOURIER), BF16)
    if n <= REV_BLOCK:
        cn, snn = _dft_tables(n)
        seqs = min(SHORT_DFT_SEQUENCES, t // n)
        return pl.pallas_call(
            _pos_dft_direct_kernel,
            out_shape=out_shape,
            grid=(t // (n * seqs),),
            in_specs=[_const_spec(cn.shape), _const_spec(snn.shape),
                      pl.BlockSpec((n * seqs, 2 * D_FOURIER), lambda i: (i, 0))],
            out_specs=pl.BlockSpec((n * seqs, D_FOURIER), lambda i: (i, 0)),
            compiler_params=_params(1),
            name=f"pos_dft_{n}",
        )(cn, snn, ucs)
    rows = n // 2 + 8
    consts = _dft_factors(n, rows, n // 2)
    anti = np.eye(REV_BLOCK, dtype=np.float32)[::-1]
    shifted = np.roll(anti, 1, axis=1)
    shifted[0] = 0.0
    consts += (jnp.asarray(anti, dtype=BF16), jnp.asarray(shifted, dtype=BF16))
    return pl.pallas_call(
        _pos_dft_half_kernel,
        out_shape=out_shape,
        grid=(t // n,),
        in_specs=[_const_spec(a.shape) for a in consts]
        + [pl.BlockSpec((n, 2 * D_FOURIER), lambda i: (i, 0))],
        out_specs=pl.BlockSpec((n, D_FOURIER), lambda i: (i, 0)),
        scratch_shapes=[pltpu.VMEM((rows, n // 2), BF16), pltpu.VMEM((rows, n // 2), BF16)],
        compiler_params=_params(1),
        name=f"pos_dft_{n}",
    )(*consts, ucs)


def _attention_kernel(*refs, n_kv, single_pass, units):
    q_ref, refs = refs[0], refs[1:]
    k_refs, vt_refs, refs = refs[:n_kv], refs[n_kv:2 * n_kv], refs[2 * n_kv:]
    o_ref, s_ref, pp_ref, ot_ref = refs[-4:]
    tq = q_ref.shape[1] // units
    key_lens = [k_ref.shape[1] // (1 if single_pass else units) for k_ref in k_refs]
    chunks, base = [], 0
    for j, m in enumerate(key_lens):
        chunks += [(j, c, base + c) for c in range(0, m, KEY_CHUNK)]
        base += m

    def q_block(hh, u):
        return q_ref[hh, u * tq:(u + 1) * tq, :]

    def finish(hh, ot, u):
        l = ot[V_DIM:V_DIM + 1]
        ot_ref[u, pl.ds(pl.multiple_of(hh * V_DIM, V_DIM), V_DIM), :] = ot[:V_DIM] / l
        return l

    def first_chunk_scores(hh, u):
        j, c, _ = chunks[0]
        s = _dot_nt(k_refs[j][hh, c:c + KEY_CHUNK, :], q_block(hh, u))
        s_ref[u, hh * KEY_CHUNK:(hh + 1) * KEY_CHUNK, :] = s
        return jnp.max(s, axis=0, keepdims=True)

    def shifted_softmax_pv(score_head, pv_head, shifts, u):
        if score_head is not None:
            qh = q_block(score_head, u)
            shift = shifts[score_head]
        ot = None
        for idx, (j, c, r) in enumerate(chunks):
            if score_head is not None:
                s = (s_ref[u, score_head * KEY_CHUNK:(score_head + 1) * KEY_CHUNK, :] if idx == 0 else
                     _dot_nt(k_refs[j][score_head, c:c + KEY_CHUNK, :], qh))
                pp_ref[2 * u + score_head % 2, r:r + KEY_CHUNK, :] = jnp.exp2(s - shift).astype(BF16)
            if pv_head is not None:
                part = _dot(vt_refs[j][pv_head * V_ROWS:(pv_head + 1) * V_ROWS, c:c + KEY_CHUNK],
                            pp_ref[2 * u + pv_head % 2, r:r + KEY_CHUNK, :])
                ot = part if ot is None else ot + part
        return finish(pv_head, ot, u) if pv_head is not None else None

    def exact_scores(hh, slot, u, kseq):
        qh = q_block(hh, u)
        m8 = None
        for j, c, r in chunks:
            k0 = kseq * key_lens[j] + c
            s = _dot_nt(k_refs[j][hh, k0:k0 + KEY_CHUNK, :], qh)
            s_ref[slot, r:r + KEY_CHUNK, :] = s
            mc = jnp.max(s.reshape(KEY_CHUNK // 8, 8, tq), axis=0)
            m8 = mc if m8 is None else jnp.maximum(m8, mc)
        return jnp.max(m8, axis=0, keepdims=True)

    def exact_softmax_pv(hh, s_slot, p_slot, m, u, kseq):
        for _, _, r in chunks:
            pp_ref[p_slot, r:r + KEY_CHUNK, :] = jnp.exp2(s_ref[s_slot, r:r + KEY_CHUNK, :] - m).astype(BF16)
        rows = pl.ds(pl.multiple_of(hh * V_ROWS, V_ROWS), V_ROWS)
        ot, base = None, 0
        for j, vt_ref in enumerate(vt_refs):
            mk = key_lens[j]
            part = _dot(vt_ref[rows, kseq * mk:(kseq + 1) * mk], pp_ref[p_slot, base:base + mk, :])
            ot = part if ot is None else ot + part
            base += mk
        finish(hh, ot, u)

    if not single_pass:
        work = [(u, hh) for u in range(units) for hh in range(N_HEADS)]
        maxima = [exact_scores(hh, slot, u, u) for slot, (u, hh) in enumerate(work)]
        for slot, (u, hh) in enumerate(work):
            exact_softmax_pv(hh, slot, slot, maxima[slot], u, u)
    else:
        trusted = []
        for u in range(units):
            l_min = l_max = None
            shifts = [first_chunk_scores(hh, u) for hh in range(N_HEADS)]
            for hh in range(N_HEADS + 1):
                l = shifted_softmax_pv(hh if hh < N_HEADS else None, hh - 1 if hh > 0 else None, shifts, u)
                if l is not None:
                    l_min = l if l_min is None else jnp.minimum(l_min, l)
                    l_max = l if l_max is None else jnp.maximum(l_max, l)
            trusted.append(jnp.logical_and(jnp.min(l_min) >= MIN_DENOMINATOR,
                                           jnp.max(l_max) <= MAX_DENOMINATOR))

        for u in range(units):
            @pl.when(jnp.logical_not(trusted[u]))
            def _(u=u):
                def body(hh, carry):
                    exact_softmax_pv(hh, u, 2 * u, exact_scores(hh, u, u, 0), u, 0)
                    return carry

                lax.fori_loop(0, N_HEADS, body, 0)

    for u in range(units):
        o_ref[u * tq:(u + 1) * tq, :] = ot_ref[u].T.astype(BF16)


def _attention(q, ks, vts, key_lens, *, n, after=None):
    t = q.shape[1]
    tq = Q_TILE
    m_tot = sum(key_lens)
    single_pass = m_tot > 2 * KEY_CHUNK
    if single_pass:
        units = ATTENTION_UNITS
        steps = n // (units * tq)
        kv_rows = 1
        s_slots, p_slots = units, 2 * units
    else:
        assert n == tq
        units = min(ATTENTION_UNITS, t // n)
        steps = 1
        kv_rows = units
        s_slots = p_slots = units * N_HEADS
    in_specs = [pl.BlockSpec((N_HEADS, units * tq, HEAD_PAD), lambda bi, i: (0, bi * steps + i, 0))]
    in_specs += [pl.BlockSpec((N_HEADS, kv_rows * m, HEAD_PAD), lambda bi, i: (0, bi, 0)) for m in key_lens]
    in_specs += [pl.BlockSpec((VT_ROWS, kv_rows * m), lambda bi, i: (0, bi)) for m in key_lens]
    args = [q, *ks, *vts]
    if after is not None:
        in_specs.append(pl.BlockSpec((8, after.shape[1]), lambda bi, i: (0, 0)))
        args.append(after)
    return pl.pallas_call(
        functools.partial(_attention_kernel, n_kv=len(ks), single_pass=single_pass, units=units),
        out_shape=jax.ShapeDtypeStruct((t, D_ATTN), BF16),
        grid=(t // (n * kv_rows), steps),
        in_specs=in_specs,
        out_specs=pl.BlockSpec((units * tq, D_ATTN), lambda bi, i: (bi * steps + i, 0)),
        scratch_shapes=[pltpu.VMEM((s_slots, m_tot, tq), F32), pltpu.VMEM((p_slots, m_tot, tq), BF16),
                        pltpu.VMEM((units, D_ATTN, tq), F32)],
        compiler_params=_params(2),
        name=f"attention_{n}",
    )(*args)


def _merge_kernel(x_ref, mod_ref, f_ref, a_ref, w_zf_ref, w_za_ref, w_g_ref, w_f_ref, w_a_ref, w_o_ref,
                  g_ref, y_ref):
    gate = mod_ref[0][:, 2 * D_MODEL:]
    for rows in _sub_tiles(x_ref.shape[0], MERGE_SUB_TILE):
        x = x_ref[rows]
        h = _modulated(x, mod_ref)
        zf = _dot_nt(h, w_zf_ref[...])
        y_f = _dot(f_ref[rows] * (zf * (jnp.tanh(zf) + 1.0)).astype(BF16), w_f_ref[...])
        za = _dot_nt(h, w_za_ref[...])
        y_a = _dot(a_ref[rows] * (za * (jnp.tanh(za) + 1.0)).astype(BF16), w_a_ref[...])
        merged_x2 = ((jnp.tanh(_dot_nt(h, w_g_ref[:D_MODEL])) + 1.0) * y_f
                     + (jnp.tanh(_dot_nt(h, w_g_ref[D_MODEL:])) + 1.0) * y_a)
        out = x + gate * _dot(merged_x2.astype(BF16), w_o_ref[...])
        y_ref[rows] = _rms(out) * g_ref[...]


def _merge(x, mod3, f, attn, w_in_p, w_f, w_a, w_o, g, *, n, latent):
    t = x.shape[0]
    tm, _, mod_row = _token_tiling(t, n, per_sequence=latent)
    tok = lambda w: pl.BlockSpec((tm, w), lambda i: (i, 0))
    return pl.pallas_call(
        _merge_kernel,
        out_shape=jax.ShapeDtypeStruct((t, D_MODEL), F32),
        grid=(t // tm,),
        in_specs=[tok(D_MODEL),
                  pl.BlockSpec((1, 1, 3 * D_MODEL), lambda i: (mod_row(i), 0, 0)),
                  tok(D_FOURIER), tok(D_ATTN),
                  _row_spec(W_BLK, B_ZF), _row_spec(W_BLK, B_ZA), _row_spec(2 * D_MODEL, 1),
                  _const_spec(w_f.shape), _const_spec(w_a.shape), _const_spec(w_o.shape),
                  _const_spec(g.shape)],
        out_specs=tok(D_MODEL),
        compiler_params=_params(1),
        name=f"merge_{n}",
    )(x, mod3, f, attn, w_in_p, w_in_p, w_in_p, w_f, w_a, w_o, g)


def _dft_tables(n):
    norm = 1.0 / math.sqrt(n)
    k = np.arange(n)
    ang = ((k[:, None] * k[None, :]) % n) * (2.0 * math.pi / n)
    return (jnp.asarray(np.cos(ang) * norm, dtype=F32).astype(BF16),
            jnp.asarray(np.sin(ang) * -norm, dtype=F32).astype(BF16))


def _dft_factors(n, rows, n_pos):
    norm = 1.0 / math.sqrt(n)
    period = n // DFT_SPLIT
    pos = np.arange(n_pos)
    k1 = np.arange(-(-rows // DFT_SPLIT))
    k0 = np.arange(DFT_SPLIT)
    a1 = ((k1[:, None] * pos[None, :]) % period) * (2.0 * math.pi / period)
    a0 = ((k0[:, None] * pos[None, :]) % n) * (2.0 * math.pi / n)
    return tuple(a.astype(np.float32) for a in (np.cos(a1), np.sin(a1), np.cos(a0) * norm, np.sin(a0) * norm))


def _channel_dft_table():
    c = np.arange(GROUP)
    ang = ((c[:, None] * c[None, :]) % GROUP) * (2.0 * math.pi / GROUP)
    norm = 1.0 / math.sqrt(GROUP)
    return jnp.asarray(np.concatenate([np.cos(ang), np.sin(ang)], axis=1) * norm, dtype=F32).astype(BF16)


def _rope_tables(n):
    t = np.arange(n)
    row = (t // GRID_W).astype(np.float64)
    col = (t % GRID_W).astype(np.float64)
    half = QK_ROPE // 2
    inv = ROPE_THETA ** (-np.arange(0, half, 2, dtype=np.float64) / half)
    ar, ac = row[:, None] * inv, col[:, None] * inv
    ang = np.concatenate([ar, ar, ac, ac], axis=-1)
    pad = lambda a, fill: np.concatenate(
        [np.full((n, ROPE_LANE), fill), a, np.full((n, HEAD_PAD - ROPE_LANE - QK_ROPE), fill)],
        axis=1).astype(np.float32)
    return pad(np.cos(ang), 1.0), pad(np.sin(ang), 0.0)


def kernel(x_prompt, x_sample, cache_ckv, cache_krope, c, c_ctx, w_ada, b_ada, w_in, q_norm_g, w_uq,
           kv_norm_g, w_ukv, w_f_out, w_a_out, w_out, final_norm_g):
    assert w_in.shape[0] == 1
    b_ctx, n_ctx, _ = x_prompt.shape
    dec_b, n_lat, _ = x_sample.shape
    past = cache_ckv.shape[2]

    w_q3 = w_uq[0].reshape(Q_RANK, N_HEADS, QK_NOPE + QK_ROPE)
    head_pad = lambda a, left: jnp.pad(
        a, ((0, 0), (0, 0), (left, HEAD_PAD - left - a.shape[2]))).reshape(Q_RANK, N_HEADS * HEAD_PAD)
    rp = w_q3[:, :, QK_NOPE:].reshape(Q_RANK, N_HEADS, 2, 2, QK_ROPE // 4)
    w_q_rot = jnp.stack([-rp[:, :, :, 1], rp[:, :, :, 0]], axis=3).reshape(Q_RANK, N_HEADS, QK_ROPE)
    w_uq_p = jnp.concatenate([head_pad(w_q3, 0), head_pad(w_q_rot, ROPE_LANE)], axis=1).astype(BF16)
    w_kv3 = w_ukv[0].reshape(KV_RANK, N_HEADS, QK_NOPE + V_DIM)
    w_knope = jnp.pad(w_kv3[:, :, :QK_NOPE], ((0, 0), (0, 0), (0, HEAD_PAD - QK_NOPE)))
    place = np.zeros((HEAD_PAD, N_HEADS, HEAD_PAD), np.float32)
    for j in range(QK_ROPE):
        place[ROPE_LANE + j, :, ROPE_LANE + j] = 1.0
    w_k = jnp.concatenate([w_knope.reshape(KV_RANK, -1), jnp.asarray(place).reshape(HEAD_PAD, -1)],
                          axis=0).astype(BF16)
    w_uvt = jnp.pad(jnp.transpose(w_kv3[:, :, QK_NOPE:], (1, 2, 0)),
                    ((0, 0), (0, V_ROWS - V_DIM), (0, 0))).reshape(VT_ROWS, KV_RANK).astype(BF16)
    ones_col = np.zeros((N_HEADS, V_ROWS, LANES), np.float32)
    ones_col[:, V_DIM, :] = 1.0
    ones_col = jnp.asarray(ones_col.reshape(VT_ROWS, LANES))
    w_f = w_f_out[0].astype(BF16)
    w_a = w_a_out[0].astype(BF16)
    w_o = (0.5 * w_out[0]).astype(BF16)
    qg = q_norm_g[0].reshape(1, Q_RANK)
    kvg = kv_norm_g[0].reshape(1, KV_RANK)
    fg = final_norm_g.reshape(1, D_MODEL)
    wts = (qg, kvg, w_uq_p, w_k, w_uvt, ones_col, _channel_dft_table())

    cond8 = jnp.concatenate([c_ctx[None, :], c, jnp.zeros((8 - 1 - dec_b, D_MODEL), F32)], axis=0)
    mod3, w_in_p = _adaln_pack(cond8, w_ada[0], b_ada[0].reshape(1, -1), jnp.swapaxes(w_in[0], 0, 1))

    xp = x_prompt.reshape(b_ctx * n_ctx, D_MODEL)
    if n_ctx <= REV_BLOCK and TOKEN_TILE % n_ctx == 0:
        f, q, k, vt, state_ckv, state_krope = _layer_in(xp, mod3, w_in_p, wts, None, n=n_ctx, state=True,
                                                        dft_tabs=_dft_tables(n_ctx))
    else:
        ucs, q, k, vt, state_ckv, state_krope = _layer_in(xp, mod3, w_in_p, wts, None, n=n_ctx, state=True)
        f = _pos_dft(ucs, n=n_ctx)
    attn = _attention(q, [k], [vt], [n_ctx], n=n_ctx, after=f)
    y_prompt = _merge(xp, mod3, f, attn, w_in_p, w_f, w_a, w_o, fg, n=n_ctx, latent=False)

    xs = x_sample.reshape(dec_b * n_lat, D_MODEL)
    krope_pad = jnp.pad(cache_krope[:, 0], ((0, 0), (0, 0), (ROPE_LANE, HEAD_PAD - ROPE_LANE - QK_ROPE)))
    cache = (cache_ckv[:, 0].reshape(dec_b * past, KV_RANK), krope_pad.reshape(dec_b * past, HEAD_PAD))
    ucs, q, k, vt, k_c, vt_c = _layer_in(xs, mod3, w_in_p, wts, _rope_tables(n_lat), n=n_lat, state=False,
                                         cache=cache)
    f = _pos_dft(ucs, n=n_lat)
    attn = _attention(q, [k, k_c], [vt, vt_c], [n_lat, past], n=n_lat, after=f)
    y_sample = _merge(xs, mod3, f, attn, w_in_p, w_f, w_a, w_o, fg, n=n_lat, latent=True)

    return (y_prompt.reshape(b_ctx, n_ctx, D_MODEL), y_sample.reshape(dec_b, n_lat, D_MODEL),
            state_ckv.reshape(b_ctx, 1, n_ctx, KV_RANK), jnp.swapaxes(state_krope, 1, 2)[:, None])
```

```python
import functools
import math

import jax
import jax.numpy as jnp
import numpy as np
from jax import lax
from jax.experimental import pallas as pl
from jax.experimental.pallas import tpu as pltpu

F32 = jnp.float32
BF16 = jnp.bfloat16

D_MODEL = 1024
GRID_W = 64
N_GROUPS = 4
GROUP = 128
D_FOURIER = N_GROUPS * GROUP
N_HEADS = 8
QK_NOPE = 64
QK_ROPE = 32
V_DIM = 64
Q_RANK = 256
KV_RANK = 128
D_ATTN = N_HEADS * V_DIM
D_IN = 2 * D_FOURIER + Q_RANK + KV_RANK + QK_ROPE + D_ATTN + 2 * D_MODEL
ROPE_THETA = 10000.0
EPS = 1e-6
LANES = 128
HEAD_PAD = LANES
ROPE_LANE = QK_NOPE
V_ROWS = V_DIM + 16
VT_ROWS = N_HEADS * V_ROWS
Q_SCALE = (QK_NOPE + QK_ROPE) ** -0.5 * math.log2(math.e)
KEY_CHUNK = 256
Q_TILE = 256
ATTENTION_UNITS = 4
SHORT_DFT_SEQUENCES = 8
REV_BLOCK = 256
DFT_SPLIT = 64
MIN_DENOMINATOR = 2.0 ** -60
MAX_DENOMINATOR = 2.0 ** 60
W_BLK = 512
B_UF, B_ZF, B_MID, B_ZA = 0, 1, 2, 3
D_IN_PAD = 8 * W_BLK
TOKEN_TILE = 1024
LAYER_IN_SUB_TILE = TOKEN_TILE
MERGE_SUB_TILE = TOKEN_TILE // 2
VMEM_LIMIT_BYTES = 56 * 1024 * 1024


def _const_spec(shape):
    nd = len(shape)
    return pl.BlockSpec(shape, lambda *_: (0,) * nd, pipeline_mode=pl.Buffered(1))


def _row_spec(rows, blk):
    return pl.BlockSpec((rows, D_MODEL), lambda *_: (blk, 0), pipeline_mode=pl.Buffered(1))


def _params(n_axes):
    return pltpu.CompilerParams(dimension_semantics=("arbitrary",) * n_axes,
                                vmem_limit_bytes=VMEM_LIMIT_BYTES)


def _rms(x):
    return x * lax.rsqrt(jnp.mean(x * x, axis=-1, keepdims=True) + EPS)


def _sigmoid(x):
    return 0.5 * jnp.tanh(0.5 * x) + 0.5


def _dot(a, b):
    return jnp.dot(a, b, preferred_element_type=F32)


def _dot_nt(a, b):
    return lax.dot_general(a, b, (((1,), (1,)), ((), ())), preferred_element_type=F32)


def _token_tiling(t, n, *, per_sequence):
    tm = TOKEN_TILE
    assert t % tm == 0 and (not per_sequence or n % tm == 0)
    per_seq = n // tm if per_sequence else None
    mod_row = (lambda i: 1 + i // per_seq) if per_sequence else (lambda i: 0)
    return tm, per_seq, mod_row


def _sub_tiles(rows, step):
    return [slice(r, r + step) for r in range(0, rows, step)]


def _modulated(x, mod_ref):
    mod = mod_ref[0]
    shift, scale = mod[:, 0:D_MODEL], mod[:, D_MODEL:2 * D_MODEL]
    return (_rms(x) * (1.0 + scale) + shift).astype(BF16)


def _adaln_kernel(c_ref, w_ref, b_ref, o_ref):
    c = c_ref[...]
    s = (c * _sigmoid(c)).astype(BF16)
    mod = _dot(s, w_ref[...].astype(BF16)) + b_ref[...]
    for r in range(mod.shape[0]):
        o_ref[r] = mod[r:r + 1]


def _adaln(cond8, w_ada, b_ada):
    n_blk = 4
    bw = 3 * D_MODEL // n_blk
    return pl.pallas_call(
        _adaln_kernel,
        out_shape=jax.ShapeDtypeStruct((8, 1, 3 * D_MODEL), F32),
        grid=(n_blk,),
        in_specs=[pl.BlockSpec((8, D_MODEL), lambda j: (0, 0)),
                  pl.BlockSpec((D_MODEL, bw), lambda j: (0, j)),
                  pl.BlockSpec((1, bw), lambda j: (0, j))],
        out_specs=pl.BlockSpec((8, 1, bw), lambda j: (0, 0, j)),
        compiler_params=_params(1),
        name="adaln",
    )(cond8, w_ada, b_ada)


def _rot_rows(w):
    q = QK_ROPE // 4
    return jnp.concatenate([-w[q:2 * q], w[0:q], -w[3 * q:4 * q], w[2 * q:3 * q]], axis=0)


def _pack_w_in_kernel(wt_ref, o_ref):
    lo = 2 * D_FOURIER + Q_RANK + KV_RANK
    zf = slice(B_ZF * W_BLK, (B_ZF + 1) * W_BLK)
    o_ref[:zf.start] = wt_ref[:zf.start].astype(BF16)
    o_ref[zf] = (0.5 * wt_ref[zf]).astype(BF16)
    o_ref[zf.stop:lo] = wt_ref[zf.stop:lo].astype(BF16)
    kr = wt_ref[lo:lo + QK_ROPE]
    zeros = jnp.zeros((QK_ROPE, D_MODEL), F32)
    o_ref[lo:lo + LANES] = jnp.concatenate([_rot_rows(kr), zeros, kr, zeros], axis=0).astype(BF16)
    o_ref[lo + LANES:] = (0.5 * wt_ref[lo + QK_ROPE:]).astype(BF16)


def _pack_w_in(w_in_t):
    return pl.pallas_call(
        _pack_w_in_kernel,
        out_shape=jax.ShapeDtypeStruct((D_IN_PAD, D_MODEL), BF16),
        grid=(1,),
        in_specs=[_const_spec((D_IN, D_MODEL))],
        out_specs=_const_spec((D_IN_PAD, D_MODEL)),
        compiler_params=_params(1),
        name="pack_w_in",
    )(w_in_t)


def _kv_outputs(ckvn_b, kr_b, w_k_ref, w_uvt_ref, ones_ref, k_ref, vt_ref, rows):
    tm = ckvn_b.shape[0]
    kp = _dot(jnp.concatenate([ckvn_b, kr_b], axis=1), w_k_ref[...])
    for hh in range(N_HEADS):
        k_ref[hh, rows] = kp[:, hh * HEAD_PAD:(hh + 1) * HEAD_PAD].astype(BF16)
    ones = jnp.concatenate([ones_ref[...]] * (tm // LANES), axis=1)
    vt_ref[:, rows] = (_dot_nt(w_uvt_ref[...], ckvn_b) + ones).astype(BF16)


def _layer_in_kernel(*refs, rope, state, cache, pos_dft):
    (x_ref, mod_ref, w_uf_ref, w_mid_ref, qg_ref, kvg_ref, w_uq_ref, w_k_ref, w_uvt_ref, ones_ref,
     cs_ref), refs = refs[:11], refs[11:]
    if rope:
        (cos_ref, sin_ref), refs = refs[:2], refs[2:]
    if cache:
        (cache_ckv_ref, cache_kr_ref), refs = refs[:2], refs[2:]
    if pos_dft:
        (cn_ref, sn_ref), refs = refs[:2], refs[2:]
    (ucs_ref, q_ref, k_ref, vt_ref), refs = refs[:4], refs[4:]
    if cache:
        (cache_k_ref, cache_vt_ref), refs = refs[:2], refs[2:]
    if state:
        ckvn_ref, krope_ref = refs

    for rows in _sub_tiles(x_ref.shape[0], LAYER_IN_SUB_TILE):
        h = _modulated(x_ref[rows], mod_ref)

        mid = _dot_nt(h, w_mid_ref[...])
        cq, ckv, kr = mid[:, :Q_RANK], mid[:, Q_RANK:Q_RANK + KV_RANK], mid[:, Q_RANK + KV_RANK:]
        cqn = (_rms(cq) * (qg_ref[...] * Q_SCALE)).astype(BF16)
        ckvn = _rms(ckv) * kvg_ref[...]
        if state:
            ckvn_ref[rows] = ckvn
            n = krope_ref.shape[2]
            kr_t = kr.T[ROPE_LANE:ROPE_LANE + QK_ROPE]
            for s in range(kr.shape[0] // n):
                krope_ref[rows.start // n + s] = kr_t[:, s * n:(s + 1) * n]
        n_q = N_HEADS * HEAD_PAD
        if rope:
            cos, sin = cos_ref[rows], sin_ref[rows]
            kr = kr * cos + pltpu.roll(kr, ROPE_LANE, 1) * sin
            cos_b, sin_b = cos.astype(BF16), sin.astype(BF16)
            qp = _dot(cqn, w_uq_ref[...])
        else:
            qp = _dot(cqn, w_uq_ref[:, :n_q])
        for hh in range(N_HEADS):
            qh = qp[:, hh * HEAD_PAD:(hh + 1) * HEAD_PAD].astype(BF16)
            if rope:
                qh = qh * cos_b + qp[:, n_q + hh * HEAD_PAD:n_q + (hh + 1) * HEAD_PAD].astype(BF16) * sin_b
            q_ref[hh, rows] = qh
        _kv_outputs(ckvn.astype(BF16), kr.astype(BF16), w_k_ref, w_uvt_ref, ones_ref, k_ref, vt_ref, rows)

        u = _dot_nt(h, w_uf_ref[...]).astype(BF16)
        uc, us = [], []
        for g in range(N_GROUPS):
            r = _dot(u[:, g * GROUP:(g + 1) * GROUP], cs_ref[...])
            if pos_dft:
                uc.append(r[:, :GROUP].astype(BF16))
                us.append(r[:, GROUP:].astype(BF16))
            else:
                ucs_ref[rows, g * GROUP:(g + 1) * GROUP] = r[:, :GROUP].astype(BF16)
                ucs_ref[rows, D_FOURIER + g * GROUP:D_FOURIER + (g + 1) * GROUP] = r[:, GROUP:].astype(BF16)
        if pos_dft:
            uc, us = jnp.concatenate(uc, axis=1), jnp.concatenate(us, axis=1)
            n = cn_ref.shape[0]
            for s in range(uc.shape[0] // n):
                seq = slice(s * n, (s + 1) * n)
                f = _dot(cn_ref[...], uc[seq]) + _dot(sn_ref[...], us[seq])
                ucs_ref[rows.start + s * n:rows.start + (s + 1) * n] = f.astype(BF16)

    if cache:
        @pl.when(pl.program_id(0) == 0)
        def _():
            _kv_outputs(cache_ckv_ref[...].astype(BF16), cache_kr_ref[...].astype(BF16), w_k_ref, w_uvt_ref,
                        ones_ref, cache_k_ref, cache_vt_ref, slice(None))


def _layer_in(x, mod3, w_in_p, wts, rope_tabs, *, n, state, cache=None, dft_tabs=None):
    t = x.shape[0]
    rope = rope_tabs is not None
    tm, per_seq, mod_row = _token_tiling(t, n, per_sequence=rope)
    tok = lambda w: pl.BlockSpec((tm, w), lambda i: (i, 0))
    in_specs = [tok(D_MODEL),
                pl.BlockSpec((1, 1, 3 * D_MODEL), lambda i: (mod_row(i), 0, 0)),
                _row_spec(W_BLK, B_UF), _row_spec(W_BLK, B_MID)]
    in_specs += [_const_spec(w.shape) for w in wts]
    args = [x, mod3, w_in_p, w_in_p, *wts]
    if rope:
        in_specs += [pl.BlockSpec((tm, HEAD_PAD), lambda i: (i % per_seq, 0))] * 2
        args += list(rope_tabs)
    head = pl.BlockSpec((N_HEADS, tm, HEAD_PAD), lambda i: (0, i, 0))
    out_specs = [tok(2 * D_FOURIER), head, head, pl.BlockSpec((VT_ROWS, tm), lambda i: (0, i))]
    out_shape = [jax.ShapeDtypeStruct((t, 2 * D_FOURIER), BF16),
                 jax.ShapeDtypeStruct((N_HEADS, t, HEAD_PAD), BF16),
                 jax.ShapeDtypeStruct((N_HEADS, t, HEAD_PAD), BF16),
                 jax.ShapeDtypeStruct((VT_ROWS, t), BF16)]
    if cache is not None:
        t_c = cache[0].shape[0]
        in_specs += [_const_spec(a.shape) for a in cache]
        args += list(cache)
        cache_out = [jax.ShapeDtypeStruct((N_HEADS, t_c, HEAD_PAD), BF16), jax.ShapeDtypeStruct((VT_ROWS, t_c), BF16)]
        out_specs += [_const_spec(s.shape) for s in cache_out]
        out_shape += cache_out
    if dft_tabs is not None:
        assert tm % n == 0
        in_specs += [_const_spec(a.shape) for a in dft_tabs]
        args += list(dft_tabs)
        out_specs[0], out_shape[0] = tok(D_FOURIER), jax.ShapeDtypeStruct((t, D_FOURIER), BF16)
    if state:
        out_specs += [tok(KV_RANK), pl.BlockSpec((tm // n, QK_ROPE, n), lambda i: (i, 0, 0))]
        out_shape += [jax.ShapeDtypeStruct((t, KV_RANK), F32), jax.ShapeDtypeStruct((t // n, QK_ROPE, n), F32)]
    return pl.pallas_call(
        functools.partial(_layer_in_kernel, rope=rope, state=state, cache=cache is not None,
                          pos_dft=dft_tabs is not None),
        out_shape=out_shape,
        grid=(t // tm,),
        in_specs=in_specs,
        out_specs=out_specs,
        compiler_params=_params(1),
        name="layer_in_latent" if rope else "layer_in_context",
    )(*args)


def _pos_dft_direct_kernel(cn_ref, sn_ref, ucs_ref, f_ref):
    n = cn_ref.shape[0]
    for s in range(f_ref.shape[0] // n):
        rows = slice(s * n, (s + 1) * n)
        f = _dot(cn_ref[...], ucs_ref[rows, :D_FOURIER]) + _dot(sn_ref[...], ucs_ref[rows, D_FOURIER:])
        f_ref[rows] = f.astype(BF16)


def _pos_dft_half_kernel(c1_ref, s1_ref, c0_ref, s0_ref, rev_ref, rev1_ref, ucs_ref, f_ref, cn_ref, sn_ref):
    n = f_ref.shape[0]
    half = n // 2
    blocks = half // REV_BLOCK

    @pl.when(pl.program_id(0) == 0)
    def _():
        c0, s0 = c0_ref[...], s0_ref[...]
        for k1 in range(c1_ref.shape[0]):
            rows = slice(k1 * DFT_SPLIT, min((k1 + 1) * DFT_SPLIT, cn_ref.shape[0]))
            m = rows.stop - rows.start
            c1, s1 = c1_ref[k1:k1 + 1, :], s1_ref[k1:k1 + 1, :]
            cn_ref[rows] = (c1 * c0[:m] - s1 * s0[:m]).astype(BF16)
            sn_ref[rows] = (-(s1 * c0[:m] + c1 * s0[:m])).astype(BF16)

    def folded(cols, sign):
        first_row = lax.broadcasted_iota(jnp.int32, (REV_BLOCK, D_FOURIER), 0) == 0
        out = []
        for b in range(blocks):
            m = _dot(rev1_ref[...], ucs_ref[n - (b + 1) * REV_BLOCK:n - b * REV_BLOCK, cols])
            if b > 0:
                m = jnp.where(first_row, ucs_ref[n - b * REV_BLOCK:n - b * REV_BLOCK + 1, cols].astype(F32), m)
            out.append((ucs_ref[b * REV_BLOCK:(b + 1) * REV_BLOCK, cols] + sign * m).astype(BF16))
        return jnp.concatenate(out, axis=0)

    e = folded(slice(0, D_FOURIER), 1.0)
    o = folded(slice(D_FOURIER, 2 * D_FOURIER), -1.0)
    odd_k = (lax.broadcasted_iota(jnp.int32, (cn_ref.shape[0], D_FOURIER), 0) & 1) == 1
    mid = ucs_ref[half:half + 1, :D_FOURIER].astype(F32) * (1.0 / math.sqrt(n))
    g = _dot(cn_ref[...], e) + jnp.where(odd_k, -mid, mid)
    hn = _dot(sn_ref[...], o)
    f_ref[:half] = (g[:half] + hn[:half]).astype(BF16)
    mirrored = (g[1:half + 1] - hn[1:half + 1]).astype(BF16)
    for a in range(blocks):
        blk = mirrored[(blocks - 1 - a) * REV_BLOCK:(blocks - a) * REV_BLOCK]
        f_ref[half + a * REV_BLOCK:half + (a + 1) * REV_BLOCK] = _dot(rev_ref[...], blk).astype(BF16)


def _pos_dft(ucs, *, n):
    t = ucs.shape[0]
    out_shape = jax.ShapeDtypeStruct((t, D_FOURIER), BF16)
    if n <= REV_BLOCK:
        cn, snn = _dft_tables(n)
        seqs = min(SHORT_DFT_SEQUENCES, t // n)
        return pl.pallas_call(
            _pos_dft_direct_kernel,
            out_shape=out_shape,
            grid=(t // (n * seqs),),
            in_specs=[_const_spec(cn.shape), _const_spec(snn.shape),
                      pl.BlockSpec((n * seqs, 2 * D_FOURIER), lambda i: (i, 0))],
            out_specs=pl.BlockSpec((n * seqs, D_FOURIER), lambda i: (i, 0)),
            compiler_params=_params(1),
            name=f"pos_dft_{n}",
        )(cn, snn, ucs)
    rows = n // 2 + 8
    consts = _dft_factors(n, rows, n // 2)
    anti = np.eye(REV_BLOCK, dtype=np.float32)[::-1]
    shifted = np.roll(anti, 1, axis=1)
    shifted[0] = 0.0
    consts += (jnp.asarray(anti, dtype=BF16), jnp.asarray(shifted, dtype=BF16))
    return pl.pallas_call(
        _pos_dft_half_kernel,
        out_shape=out_shape,
        grid=(t // n,),
        in_specs=[_const_spec(a.shape) for a in consts]
        + [pl.BlockSpec((n, 2 * D_FOURIER), lambda i: (i, 0))],
        out_specs=pl.BlockSpec((n, D_FOURIER), lambda i: (i, 0)),
        scratch_shapes=[pltpu.VMEM((rows, n // 2), BF16), pltpu.VMEM((rows, n // 2), BF16)],
        compiler_params=_params(1),
        name=f"pos_dft_{n}",
    )(*consts, ucs)


def _attention_kernel(*refs, n_kv, single_pass, units):
    q_ref, refs = refs[0], refs[1:]
    k_refs, vt_refs, refs = refs[:n_kv], refs[n_kv:2 * n_kv], refs[2 * n_kv:]
    o_ref, s_ref, pp_ref, ot_ref = refs[-4:]
    tq = q_ref.shape[1] // units
    key_lens = [k_ref.shape[1] // (1 if single_pass else units) for k_ref in k_refs]
    chunks, base = [], 0
    for j, m in enumerate(key_lens):
        chunks += [(j, c, base + c) for c in range(0, m, KEY_CHUNK)]
        base += m

    def q_block(hh, u):
        return q_ref[hh, u * tq:(u + 1) * tq, :]

    def finish(hh, ot, u):
        l = ot[V_DIM:V_DIM + 1]
        ot_ref[u, pl.ds(pl.multiple_of(hh * V_DIM, V_DIM), V_DIM), :] = ot[:V_DIM] / l
        return l

    def first_chunk_scores(hh, u):
        j, c, _ = chunks[0]
        s = _dot_nt(k_refs[j][hh, c:c + KEY_CHUNK, :], q_block(hh, u))
        s_ref[u, hh * KEY_CHUNK:(hh + 1) * KEY_CHUNK, :] = s
        return jnp.max(s, axis=0, keepdims=True)

    def shifted_softmax_pv(score_head, pv_head, shifts, u):
        if score_head is not None:
            qh = q_block(score_head, u)
            shift = shifts[score_head]
        ot = None
        for idx, (j, c, r) in enumerate(chunks):
            if score_head is not None:
                s = (s_ref[u, score_head * KEY_CHUNK:(score_head + 1) * KEY_CHUNK, :] if idx == 0 else
                     _dot_nt(k_refs[j][score_head, c:c + KEY_CHUNK, :], qh))
                pp_ref[2 * u + score_head % 2, r:r + KEY_CHUNK, :] = jnp.exp2(s - shift).astype(BF16)
            if pv_head is not None:
                part = _dot(vt_refs[j][pv_head * V_ROWS:(pv_head + 1) * V_ROWS, c:c + KEY_CHUNK],
                            pp_ref[2 * u + pv_head % 2, r:r + KEY_CHUNK, :])
                ot = part if ot is None else ot + part
        return finish(pv_head, ot, u) if pv_head is not None else None

    def exact_scores(hh, slot, u, kseq):
        qh = q_block(hh, u)
        m8 = None
        for j, c, r in chunks:
            k0 = kseq * key_lens[j] + c
            s = _dot_nt(k_refs[j][hh, k0:k0 + KEY_CHUNK, :], qh)
            s_ref[slot, r:r + KEY_CHUNK, :] = s
            mc = jnp.max(s.reshape(KEY_CHUNK // 8, 8, tq), axis=0)
            m8 = mc if m8 is None else jnp.maximum(m8, mc)
        return jnp.max(m8, axis=0, keepdims=True)

    def exact_softmax_pv(hh, s_slot, p_slot, m, u, kseq):
        for _, _, r in chunks:
            pp_ref[p_slot, r:r + KEY_CHUNK, :] = jnp.exp2(s_ref[s_slot, r:r + KEY_CHUNK, :] - m).astype(BF16)
        rows = pl.ds(pl.multiple_of(hh * V_ROWS, V_ROWS), V_ROWS)
        ot, base = None, 0
        for j, vt_ref in enumerate(vt_refs):
            mk = key_lens[j]
            part = _dot(vt_ref[rows, kseq * mk:(kseq + 1) * mk], pp_ref[p_slot, base:base + mk, :])
            ot = part if ot is None else ot + part
            base += mk
        finish(hh, ot, u)

    if not single_pass:
        work = [(u, hh) for u in range(units) for hh in range(N_HEADS)]
        maxima = [exact_scores(hh, slot, u, u) for slot, (u, hh) in enumerate(work)]
        for slot, (u, hh) in enumerate(work):
            exact_softmax_pv(hh, slot, slot, maxima[slot], u, u)
    else:
        trusted = []
        for u in range(units):
            l_min = l_max = None
            shifts = [first_chunk_scores(hh, u) for hh in range(N_HEADS)]
            for hh in range(N_HEADS + 1):
                l = shifted_softmax_pv(hh if hh < N_HEADS else None, hh - 1 if hh > 0 else None, shifts, u)
                if l is not None:
                    l_min = l if l_min is None else jnp.minimum(l_min, l)
                    l_max = l if l_max is None else jnp.maximum(l_max, l)
            trusted.append(jnp.logical_and(jnp.min(l_min) >= MIN_DENOMINATOR,
                                           jnp.max(l_max) <= MAX_DENOMINATOR))

        for u in range(units):
            @pl.when(jnp.logical_not(trusted[u]))
            def _(u=u):
                def body(hh, carry):
                    exact_softmax_pv(hh, u, 2 * u, exact_scores(hh, u, u, 0), u, 0)
                    return carry

                lax.fori_loop(0, N_HEADS, body, 0)

    for u in range(units):
        o_ref[u * tq:(u + 1) * tq, :] = ot_ref[u].T.astype(BF16)


def _attention(q, ks, vts, key_lens, *, n, after=None):
    t = q.shape[1]
    tq = Q_TILE
    m_tot = sum(key_lens)
    single_pass = m_tot > 2 * KEY_CHUNK
    if single_pass:
        units = ATTENTION_UNITS
        steps = n // (units * tq)
        kv_rows = 1
        s_slots, p_slots = units, 2 * units
    else:
        assert n == tq
        units = min(ATTENTION_UNITS, t // n)
        steps = 1
        kv_rows = units
        s_slots = p_slots = units * N_HEADS
    in_specs = [pl.BlockSpec((N_HEADS, units * tq, HEAD_PAD), lambda bi, i: (0, bi * steps + i, 0))]
    in_specs += [pl.BlockSpec((N_HEADS, kv_rows * m, HEAD_PAD), lambda bi, i: (0, bi, 0)) for m in key_lens]
    in_specs += [pl.BlockSpec((VT_ROWS, kv_rows * m), lambda bi, i: (0, bi)) for m in key_lens]
    args = [q, *ks, *vts]
    if after is not None:
        in_specs.append(pl.BlockSpec((8, after.shape[1]), lambda bi, i: (0, 0)))
        args.append(after)
    return pl.pallas_call(
        functools.partial(_attention_kernel, n_kv=len(ks), single_pass=single_pass, units=units),
        out_shape=jax.ShapeDtypeStruct((t, D_ATTN), BF16),
        grid=(t // (n * kv_rows), steps),
        in_specs=in_specs,
        out_specs=pl.BlockSpec((units * tq, D_ATTN), lambda bi, i: (bi * steps + i, 0)),
        scratch_shapes=[pltpu.VMEM((s_slots, m_tot, tq), F32), pltpu.VMEM((p_slots, m_tot, tq), BF16),
                        pltpu.VMEM((units, D_ATTN, tq), F32)],
        compiler_params=_params(2),
        name=f"attention_{n}",
    )(*args)


def _merge_kernel(x_ref, mod_ref, f_ref, a_ref, w_zf_ref, w_za_ref, w_g_ref, w_f_ref, w_a_ref, w_o_ref,
                  g_ref, y_ref):
    gate = mod_ref[0][:, 2 * D_MODEL:]
    for rows in _sub_tiles(x_ref.shape[0], MERGE_SUB_TILE):
        x = x_ref[rows]
        h = _modulated(x, mod_ref)
        zf = _dot_nt(h, w_zf_ref[...])
        y_f = _dot(f_ref[rows] * (zf * (jnp.tanh(zf) + 1.0)).astype(BF16), w_f_ref[...])
        za = _dot_nt(h, w_za_ref[...])
        y_a = _dot(a_ref[rows] * (za * (jnp.tanh(za) + 1.0)).astype(BF16), w_a_ref[...])
        merged_x2 = ((jnp.tanh(_dot_nt(h, w_g_ref[:D_MODEL])) + 1.0) * y_f
                     + (jnp.tanh(_dot_nt(h, w_g_ref[D_MODEL:])) + 1.0) * y_a)
        out = x + gate * _dot(merged_x2.astype(BF16), w_o_ref[...])
        y_ref[rows] = _rms(out) * g_ref[...]


def _merge(x, mod3, f, attn, w_in_p, w_f, w_a, w_o, g, *, n, latent):
    t = x.shape[0]
    tm, _, mod_row = _token_tiling(t, n, per_sequence=latent)
    tok = lambda w: pl.BlockSpec((tm, w), lambda i: (i, 0))
    return pl.pallas_call(
        _merge_kernel,
        out_shape=jax.ShapeDtypeStruct((t, D_MODEL), F32),
        grid=(t // tm,),
        in_specs=[tok(D_MODEL),
                  pl.BlockSpec((1, 1, 3 * D_MODEL), lambda i: (mod_row(i), 0, 0)),
                  tok(D_FOURIER), tok(D_ATTN),
                  _row_spec(W_BLK, B_ZF), _row_spec(W_BLK, B_ZA), _row_spec(2 * D_MODEL, 1),
                  _const_spec(w_f.shape), _const_spec(w_a.shape), _const_spec(w_o.shape),
                  _const_spec(g.shape)],
        out_specs=tok(D_MODEL),
        compiler_params=_params(1),
        name=f"merge_{n}",
    )(x, mod3, f, attn, w_in_p, w_in_p, w_in_p, w_f, w_a, w_o, g)


def _dft_tables(n):
    norm = 1.0 / math.sqrt(n)
    k = np.arange(n)
    ang = ((k[:, None] * k[None, :]) % n) * (2.0 * math.pi / n)
    return (jnp.asarray(np.cos(ang) * norm, dtype=F32).astype(BF16),
            jnp.asarray(np.sin(ang) * -norm, dtype=F32).astype(BF16))


def _dft_factors(n, rows, n_pos):
    norm = 1.0 / math.sqrt(n)
    period = n // DFT_SPLIT
    pos = np.arange(n_pos)
    k1 = np.arange(-(-rows // DFT_SPLIT))
    k0 = np.arange(DFT_SPLIT)
    a1 = ((k1[:, None] * pos[None, :]) % period) * (2.0 * math.pi / period)
    a0 = ((k0[:, None] * pos[None, :]) % n) * (2.0 * math.pi / n)
    return tuple(a.astype(np.float32) for a in (np.cos(a1), np.sin(a1), np.cos(a0) * norm, np.sin(a0) * norm))


def _channel_dft_table():
    c = np.arange(GROUP)
    ang = ((c[:, None] * c[None, :]) % GROUP) * (2.0 * math.pi / GROUP)
    norm = 1.0 / math.sqrt(GROUP)
    return jnp.asarray(np.concatenate([np.cos(ang), np.sin(ang)], axis=1) * norm, dtype=F32).astype(BF16)


def _rope_tables(n):
    t = np.arange(n)
    row = (t // GRID_W).astype(np.float64)
    col = (t % GRID_W).astype(np.float64)
    half = QK_ROPE // 2
    inv = ROPE_THETA ** (-np.arange(0, half, 2, dtype=np.float64) / half)
    ar, ac = row[:, None] * inv, col[:, None] * inv
    ang = np.concatenate([ar, ar, ac, ac], axis=-1)
    pad = lambda a, fill: np.concatenate(
        [np.full((n, ROPE_LANE), fill), a, np.full((n, HEAD_PAD - ROPE_LANE - QK_ROPE), fill)],
        axis=1).astype(np.float32)
    return pad(np.cos(ang), 1.0), pad(np.sin(ang), 0.0)


def kernel(x_prompt, x_sample, cache_ckv, cache_krope, c, c_ctx, w_ada, b_ada, w_in, q_norm_g, w_uq,
           kv_norm_g, w_ukv, w_f_out, w_a_out, w_out, final_norm_g):
    assert w_in.shape[0] == 1
    b_ctx, n_ctx, _ = x_prompt.shape
    dec_b, n_lat, _ = x_sample.shape
    past = cache_ckv.shape[2]

    w_q3 = w_uq[0].reshape(Q_RANK, N_HEADS, QK_NOPE + QK_ROPE)
    head_pad = lambda a, left: jnp.pad(
        a, ((0, 0), (0, 0), (left, HEAD_PAD - left - a.shape[2]))).reshape(Q_RANK, N_HEADS * HEAD_PAD)
    rp = w_q3[:, :, QK_NOPE:].reshape(Q_RANK, N_HEADS, 2, 2, QK_ROPE // 4)
    w_q_rot = jnp.stack([-rp[:, :, :, 1], rp[:, :, :, 0]], axis=3).reshape(Q_RANK, N_HEADS, QK_ROPE)
    w_uq_p = jnp.concatenate([head_pad(w_q3, 0), head_pad(w_q_rot, ROPE_LANE)], axis=1).astype(BF16)
    w_kv3 = w_ukv[0].reshape(KV_RANK, N_HEADS, QK_NOPE + V_DIM)
    w_knope = jnp.pad(w_kv3[:, :, :QK_NOPE], ((0, 0), (0, 0), (0, HEAD_PAD - QK_NOPE)))
    place = np.zeros((HEAD_PAD, N_HEADS, HEAD_PAD), np.float32)
    for j in range(QK_ROPE):
        place[ROPE_LANE + j, :, ROPE_LANE + j] = 1.0
    w_k = jnp.concatenate([w_knope.reshape(KV_RANK, -1), jnp.asarray(place).reshape(HEAD_PAD, -1)],
                          axis=0).astype(BF16)
    w_uvt = jnp.pad(jnp.transpose(w_kv3[:, :, QK_NOPE:], (1, 2, 0)),
                    ((0, 0), (0, V_ROWS - V_DIM), (0, 0))).reshape(VT_ROWS, KV_RANK).astype(BF16)
    ones_col = np.zeros((N_HEADS, V_ROWS, LANES), np.float32)
    ones_col[:, V_DIM, :] = 1.0
    ones_col = jnp.asarray(ones_col.reshape(VT_ROWS, LANES))
    w_f = w_f_out[0].astype(BF16)
    w_a = w_a_out[0].astype(BF16)
    w_o = (0.5 * w_out[0]).astype(BF16)
    qg = q_norm_g[0].reshape(1, Q_RANK)
    kvg = kv_norm_g[0].reshape(1, KV_RANK)
    fg = final_norm_g.reshape(1, D_MODEL)
    wts = (qg, kvg, w_uq_p, w_k, w_uvt, ones_col, _channel_dft_table())
    w_in_p = _pack_w_in(jnp.swapaxes(w_in[0], 0, 1))

    cond8 = jnp.concatenate([c_ctx[None, :], c, jnp.zeros((8 - 1 - dec_b, D_MODEL), F32)], axis=0)
    mod3 = _adaln(cond8, w_ada[0], b_ada[0].reshape(1, -1))

    xp = x_prompt.reshape(b_ctx * n_ctx, D_MODEL)
    if n_ctx <= REV_BLOCK and TOKEN_TILE % n_ctx == 0:
        f, q, k, vt, state_ckv, state_krope = _layer_in(xp, mod3, w_in_p, wts, None, n=n_ctx, state=True,
                                                        dft_tabs=_dft_tables(n_ctx))
    else:
        ucs, q, k, vt, state_ckv, state_krope = _layer_in(xp, mod3, w_in_p, wts, None, n=n_ctx, state=True)
        f = _pos_dft(ucs, n=n_ctx)
    attn = _attention(q, [k], [vt], [n_ctx], n=n_ctx, after=f)
    y_prompt = _merge(xp, mod3, f, attn, w_in_p, w_f, w_a, w_o, fg, n=n_ctx, latent=False)

    xs = x_sample.reshape(dec_b * n_lat, D_MODEL)
    krope_pad = jnp.pad(cache_krope[:, 0], ((0, 0), (0, 0), (ROPE_LANE, HEAD_PAD - ROPE_LANE - QK_ROPE)))
    cache = (cache_ckv[:, 0].reshape(dec_b * past, KV_RANK), krope_pad.reshape(dec_b * past, HEAD_PAD))
    ucs, q, k, vt, k_c, vt_c = _layer_in(xs, mod3, w_in_p, wts, _rope_tables(n_lat), n=n_lat, state=False,
                                         cache=cache)
    f = _pos_dft(ucs, n=n_lat)
    attn = _attention(q, [k, k_c], [vt, vt_c], [n_lat, past], n=n_lat, after=f)
    y_sample = _merge(xs, mod3, f, attn, w_in_p, w_f, w_a, w_o, fg, n=n_lat, latent=True)

    return (y_prompt.reshape(b_ctx, n_ctx, D_MODEL), y_sample.reshape(dec_b, n_lat, D_MODEL),
            state_ckv.reshape(b_ctx, 1, n_ctx, KV_RANK), jnp.swapaxes(state_krope, 1, 2)[:, None])
```

```python
import functools
import math

import jax
import jax.numpy as jnp
import numpy as np
from jax import lax
from jax.experimental import pallas as pl
from jax.experimental.pallas import tpu as pltpu

F32 = jnp.float32
BF16 = jnp.bfloat16

D_MODEL = 1024
GRID_W = 64
N_GROUPS = 4
GROUP = 128
D_FOURIER = N_GROUPS * GROUP
N_HEADS = 8
QK_NOPE = 64
QK_ROPE = 32
V_DIM = 64
Q_RANK = 256
KV_RANK = 128
D_ATTN = N_HEADS * V_DIM
D_IN = 2 * D_FOURIER + Q_RANK + KV_RANK + QK_ROPE + D_ATTN + 2 * D_MODEL
ROPE_THETA = 10000.0
EPS = 1e-6
LANES = 128
HEAD_PAD = LANES
ROPE_LANE = QK_NOPE
V_ROWS = V_DIM + 16
VT_ROWS = N_HEADS * V_ROWS
Q_SCALE = (QK_NOPE + QK_ROPE) ** -0.5 * math.log2(math.e)
KEY_CHUNK = 256
Q_TILE = 256
ATTENTION_UNITS = 4
SHORT_DFT_SEQUENCES = 8
REV_BLOCK = 256
DFT_SPLIT = 64
MIN_DENOMINATOR = 2.0 ** -60
MAX_DENOMINATOR = 2.0 ** 60
W_BLK = 512
B_UF, B_ZF, B_MID, B_ZA = 0, 1, 2, 3
D_IN_PAD = 8 * W_BLK
TOKEN_TILE = 1024
LAYER_IN_SUB_TILE = TOKEN_TILE
MERGE_SUB_TILE = TOKEN_TILE // 2
VMEM_LIMIT_BYTES = 56 * 1024 * 1024


def _const_spec(shape):
    nd = len(shape)
    return pl.BlockSpec(shape, lambda *_: (0,) * nd, pipeline_mode=pl.Buffered(1))


def _row_spec(rows, blk):
    return pl.BlockSpec((rows, D_MODEL), lambda *_: (blk, 0), pipeline_mode=pl.Buffered(1))


def _params(n_axes):
    return pltpu.CompilerParams(dimension_semantics=("arbitrary",) * n_axes,
                                vmem_limit_bytes=VMEM_LIMIT_BYTES)


def _rms(x):
    return x * lax.rsqrt(jnp.mean(x * x, axis=-1, keepdims=True) + EPS)


def _sigmoid(x):
    return 0.5 * jnp.tanh(0.5 * x) + 0.5


def _dot(a, b):
    return jnp.dot(a, b, preferred_element_type=F32)


def _dot_nt(a, b):
    return lax.dot_general(a, b, (((1,), (1,)), ((), ())), preferred_element_type=F32)


def _token_tiling(t, n, *, per_sequence):
    tm = TOKEN_TILE
    assert t % tm == 0 and (not per_sequence or n % tm == 0)
    per_seq = n // tm if per_sequence else None
    mod_row = (lambda i: 1 + i // per_seq) if per_sequence else (lambda i: 0)
    return tm, per_seq, mod_row


def _sub_tiles(rows, step):
    return [slice(r, r + step) for r in range(0, rows, step)]


def _modulated(x, mod_ref):
    mod = mod_ref[0]
    shift, scale = mod[:, 0:D_MODEL], mod[:, D_MODEL:2 * D_MODEL]
    return (_rms(x) * (1.0 + scale) + shift).astype(BF16)


def _adaln_kernel(c_ref, w_ref, b_ref, o_ref):
    c = c_ref[...]
    s = (c * _sigmoid(c)).astype(BF16)
    mod = _dot(s, w_ref[...].astype(BF16)) + b_ref[...]
    for r in range(mod.shape[0]):
        o_ref[r] = mod[r:r + 1]


def _adaln(cond8, w_ada, b_ada):
    n_blk = 4
    bw = 3 * D_MODEL // n_blk
    return pl.pallas_call(
        _adaln_kernel,
        out_shape=jax.ShapeDtypeStruct((8, 1, 3 * D_MODEL), F32),
        grid=(n_blk,),
        in_specs=[pl.BlockSpec((8, D_MODEL), lambda j: (0, 0)),
                  pl.BlockSpec((D_MODEL, bw), lambda j: (0, j)),
                  pl.BlockSpec((1, bw), lambda j: (0, j))],
        out_specs=pl.BlockSpec((8, 1, bw), lambda j: (0, 0, j)),
        compiler_params=_params(1),
        name="adaln",
    )(cond8, w_ada, b_ada)


def _rot_rows(w):
    q = QK_ROPE // 4
    return jnp.concatenate([-w[q:2 * q], w[0:q], -w[3 * q:4 * q], w[2 * q:3 * q]], axis=0)


def _pack_w_in_kernel(wt_ref, o_ref):
    lo = 2 * D_FOURIER + Q_RANK + KV_RANK
    zf = slice(B_ZF * W_BLK, (B_ZF + 1) * W_BLK)
    o_ref[:zf.start] = wt_ref[:zf.start].astype(BF16)
    o_ref[zf] = (0.5 * wt_ref[zf]).astype(BF16)
    o_ref[zf.stop:lo] = wt_ref[zf.stop:lo].astype(BF16)
    kr = wt_ref[lo:lo + QK_ROPE]
    zeros = jnp.zeros((QK_ROPE, D_MODEL), F32)
    o_ref[lo:lo + LANES] = jnp.concatenate([_rot_rows(kr), zeros, kr, zeros], axis=0).astype(BF16)
    o_ref[lo + LANES:] = (0.5 * wt_ref[lo + QK_ROPE:]).astype(BF16)


def _pack_w_in(w_in_t):
    return pl.pallas_call(
        _pack_w_in_kernel,
        out_shape=jax.ShapeDtypeStruct((D_IN_PAD, D_MODEL), BF16),
        grid=(1,),
        in_specs=[_const_spec((D_IN, D_MODEL))],
        out_specs=_const_spec((D_IN_PAD, D_MODEL)),
        compiler_params=_params(1),
        name="pack_w_in",
    )(w_in_t)


def _kv_outputs(ckvn_b, kr_b, w_k_ref, w_uvt_ref, ones_ref, k_ref, vt_ref, rows):
    tm = ckvn_b.shape[0]
    kp = _dot(jnp.concatenate([ckvn_b, kr_b], axis=1), w_k_ref[...])
    for hh in range(N_HEADS):
        k_ref[hh, rows] = kp[:, hh * HEAD_PAD:(hh + 1) * HEAD_PAD].astype(BF16)
    ones = jnp.concatenate([ones_ref[...]] * (tm // LANES), axis=1)
    vt_ref[:, rows] = (_dot_nt(w_uvt_ref[...], ckvn_b) + ones).astype(BF16)


def _layer_in_kernel(*refs, rope, state, cache, pos_dft):
    (x_ref, mod_ref, w_uf_ref, w_mid_ref, qg_ref, kvg_ref, w_uq_ref, w_k_ref, w_uvt_ref, ones_ref,
     cs_ref, place_ref), refs = refs[:12], refs[12:]
    if rope:
        (cos_ref, sin_ref, sinc_ref), refs = refs[:3], refs[3:]
    if cache:
        (cache_ckv_ref, cache_kr_ref), refs = refs[:2], refs[2:]
    if pos_dft:
        (cn_ref, sn_ref), refs = refs[:2], refs[2:]
    (ucs_ref, q_ref, k_ref, vt_ref), refs = refs[:4], refs[4:]
    if cache:
        (cache_k_ref, cache_vt_ref), refs = refs[:2], refs[2:]
    if state:
        ckvn_ref, krope_ref = refs

    for rows in _sub_tiles(x_ref.shape[0], LAYER_IN_SUB_TILE):
        h = _modulated(x_ref[rows], mod_ref)

        mid = _dot_nt(h, w_mid_ref[...])
        cq, ckv, kr = mid[:, :Q_RANK], mid[:, Q_RANK:Q_RANK + KV_RANK], mid[:, Q_RANK + KV_RANK:]
        cqn = (_rms(cq) * (qg_ref[...] * Q_SCALE)).astype(BF16)
        ckvn = _rms(ckv) * kvg_ref[...]
        if state:
            ckvn_ref[rows] = ckvn
            n = krope_ref.shape[2]
            kr_t = kr.T[ROPE_LANE:ROPE_LANE + QK_ROPE]
            for s in range(kr.shape[0] // n):
                krope_ref[rows.start // n + s] = kr_t[:, s * n:(s + 1) * n]
        n_q = N_HEADS * HEAD_PAD
        if rope:
            cos, sin = cos_ref[rows], sin_ref[rows]
            kr = kr * cos + pltpu.roll(kr, ROPE_LANE, 1) * sin
            cos_b = cos.astype(BF16)
            qp = _dot(cqn, w_uq_ref[...])
            placed = _dot((qp[:, n_q:] * sinc_ref[rows]).astype(BF16), place_ref[...])
        else:
            qp = _dot(cqn, w_uq_ref[:, :n_q])
        for hh in range(N_HEADS):
            qh = qp[:, hh * HEAD_PAD:(hh + 1) * HEAD_PAD].astype(BF16)
            if rope:
                qh = qh * cos_b + placed[:, hh * HEAD_PAD:(hh + 1) * HEAD_PAD].astype(BF16)
            q_ref[hh, rows] = qh
        _kv_outputs(ckvn.astype(BF16), kr.astype(BF16), w_k_ref, w_uvt_ref, ones_ref, k_ref, vt_ref, rows)

        u = _dot_nt(h, w_uf_ref[...]).astype(BF16)
        uc, us = [], []
        for g in range(N_GROUPS):
            r = _dot(u[:, g * GROUP:(g + 1) * GROUP], cs_ref[...])
            if pos_dft:
                uc.append(r[:, :GROUP].astype(BF16))
                us.append(r[:, GROUP:].astype(BF16))
            else:
                ucs_ref[rows, g * GROUP:(g + 1) * GROUP] = r[:, :GROUP].astype(BF16)
                ucs_ref[rows, D_FOURIER + g * GROUP:D_FOURIER + (g + 1) * GROUP] = r[:, GROUP:].astype(BF16)
        if pos_dft:
            uc, us = jnp.concatenate(uc, axis=1), jnp.concatenate(us, axis=1)
            n = cn_ref.shape[0]
            for s in range(uc.shape[0] // n):
                seq = slice(s * n, (s + 1) * n)
                f = _dot(cn_ref[...], uc[seq]) + _dot(sn_ref[...], us[seq])
                ucs_ref[rows.start + s * n:rows.start + (s + 1) * n] = f.astype(BF16)

    if cache:
        @pl.when(pl.program_id(0) == 0)
        def _():
            _kv_outputs(cache_ckv_ref[...].astype(BF16), cache_kr_ref[...].astype(BF16), w_k_ref, w_uvt_ref,
                        ones_ref, cache_k_ref, cache_vt_ref, slice(None))


def _layer_in(x, mod3, w_in_p, wts, rope_tabs, *, n, state, cache=None, dft_tabs=None):
    t = x.shape[0]
    rope = rope_tabs is not None
    tm, per_seq, mod_row = _token_tiling(t, n, per_sequence=rope)
    tok = lambda w: pl.BlockSpec((tm, w), lambda i: (i, 0))
    in_specs = [tok(D_MODEL),
                pl.BlockSpec((1, 1, 3 * D_MODEL), lambda i: (mod_row(i), 0, 0)),
                _row_spec(W_BLK, B_UF), _row_spec(W_BLK, B_MID)]
    in_specs += [_const_spec(w.shape) for w in wts]
    args = [x, mod3, w_in_p, w_in_p, *wts]
    if rope:
        in_specs += [pl.BlockSpec((tm, a.shape[1]), lambda i: (i % per_seq, 0)) for a in rope_tabs]
        args += list(rope_tabs)
    head = pl.BlockSpec((N_HEADS, tm, HEAD_PAD), lambda i: (0, i, 0))
    out_specs = [tok(2 * D_FOURIER), head, head, pl.BlockSpec((VT_ROWS, tm), lambda i: (0, i))]
    out_shape = [jax.ShapeDtypeStruct((t, 2 * D_FOURIER), BF16),
                 jax.ShapeDtypeStruct((N_HEADS, t, HEAD_PAD), BF16),
                 jax.ShapeDtypeStruct((N_HEADS, t, HEAD_PAD), BF16),
                 jax.ShapeDtypeStruct((VT_ROWS, t), BF16)]
    if cache is not None:
        t_c = cache[0].shape[0]
        in_specs += [_const_spec(a.shape) for a in cache]
        args += list(cache)
        cache_out = [jax.ShapeDtypeStruct((N_HEADS, t_c, HEAD_PAD), BF16), jax.ShapeDtypeStruct((VT_ROWS, t_c), BF16)]
        out_specs += [_const_spec(s.shape) for s in cache_out]
        out_shape += cache_out
    if dft_tabs is not None:
        assert tm % n == 0
        in_specs += [_const_spec(a.shape) for a in dft_tabs]
        args += list(dft_tabs)
        out_specs[0], out_shape[0] = tok(D_FOURIER), jax.ShapeDtypeStruct((t, D_FOURIER), BF16)
    if state:
        out_specs += [tok(KV_RANK), pl.BlockSpec((tm // n, QK_ROPE, n), lambda i: (i, 0, 0))]
        out_shape += [jax.ShapeDtypeStruct((t, KV_RANK), F32), jax.ShapeDtypeStruct((t // n, QK_ROPE, n), F32)]
    return pl.pallas_call(
        functools.partial(_layer_in_kernel, rope=rope, state=state, cache=cache is not None,
                          pos_dft=dft_tabs is not None),
        out_shape=out_shape,
        grid=(t // tm,),
        in_specs=in_specs,
        out_specs=out_specs,
        compiler_params=_params(1),
        name="layer_in_latent" if rope else "layer_in_context",
    )(*args)


def _pos_dft_direct_kernel(cn_ref, sn_ref, ucs_ref, f_ref):
    n = cn_ref.shape[0]
    for s in range(f_ref.shape[0] // n):
        rows = slice(s * n, (s + 1) * n)
        f = _dot(cn_ref[...], ucs_ref[rows, :D_FOURIER]) + _dot(sn_ref[...], ucs_ref[rows, D_FOURIER:])
        f_ref[rows] = f.astype(BF16)


def _pos_dft_half_kernel(c1_ref, s1_ref, c0_ref, s0_ref, rev_ref, rev1_ref, ucs_ref, f_ref, cn_ref, sn_ref):
    n = f_ref.shape[0]
    half = n // 2
    blocks = half // REV_BLOCK

    @pl.when(pl.program_id(0) == 0)
    def _():
        c0, s0 = c0_ref[...], s0_ref[...]
        for k1 in range(c1_ref.shape[0]):
            rows = slice(k1 * DFT_SPLIT, min((k1 + 1) * DFT_SPLIT, cn_ref.shape[0]))
            m = rows.stop - rows.start
            c1, s1 = c1_ref[k1:k1 + 1, :], s1_ref[k1:k1 + 1, :]
            cn_ref[rows] = (c1 * c0[:m] - s1 * s0[:m]).astype(BF16)
            sn_ref[rows] = (-(s1 * c0[:m] + c1 * s0[:m])).astype(BF16)

    def folded(cols, sign):
        first_row = lax.broadcasted_iota(jnp.int32, (REV_BLOCK, D_FOURIER), 0) == 0
        out = []
        for b in range(blocks):
            m = _dot(rev1_ref[...], ucs_ref[n - (b + 1) * REV_BLOCK:n - b * REV_BLOCK, cols])
            if b > 0:
                m = jnp.where(first_row, ucs_ref[n - b * REV_BLOCK:n - b * REV_BLOCK + 1, cols].astype(F32), m)
            out.append((ucs_ref[b * REV_BLOCK:(b + 1) * REV_BLOCK, cols] + sign * m).astype(BF16))
        return jnp.concatenate(out, axis=0)

    e = folded(slice(0, D_FOURIER), 1.0)
    o = folded(slice(D_FOURIER, 2 * D_FOURIER), -1.0)
    odd_k = (lax.broadcasted_iota(jnp.int32, (cn_ref.shape[0], D_FOURIER), 0) & 1) == 1
    mid = ucs_ref[half:half + 1, :D_FOURIER].astype(F32) * (1.0 / math.sqrt(n))
    g = _dot(cn_ref[...], e) + jnp.where(odd_k, -mid, mid)
    hn = _dot(sn_ref[...], o)
    f_ref[:half] = (g[:half] + hn[:half]).astype(BF16)
    mirrored = (g[1:half + 1] - hn[1:half + 1]).astype(BF16)
    for a in range(blocks):
        blk = mirrored[(blocks - 1 - a) * REV_BLOCK:(blocks - a) * REV_BLOCK]
        f_ref[half + a * REV_BLOCK:half + (a + 1) * REV_BLOCK] = _dot(rev_ref[...], blk).astype(BF16)


def _pos_dft(ucs, *, n):
    t = ucs.shape[0]
    out_shape = jax.ShapeDtypeStruct((t, D_FOURIER), BF16)
    if n <= REV_BLOCK:
        cn, snn = _dft_tables(n)
        seqs = min(SHORT_DFT_SEQUENCES, t // n)
        return pl.pallas_call(
            _pos_dft_direct_kernel,
            out_shape=out_shape,
            grid=(t // (n * seqs),),
            in_specs=[_const_spec(cn.shape), _const_spec(snn.shape),
                      pl.BlockSpec((n * seqs, 2 * D_FOURIER), lambda i: (i, 0))],
            out_specs=pl.BlockSpec((n * seqs, D_FOURIER), lambda i: (i, 0)),
            compiler_params=_params(1),
            name=f"pos_dft_{n}",
        )(cn, snn, ucs)
    rows = n // 2 + 8
    consts = _dft_factors(n, rows, n // 2)
    anti = np.eye(REV_BLOCK, dtype=np.float32)[::-1]
    shifted = np.roll(anti, 1, axis=1)
    shifted[0] = 0.0
    consts += (jnp.asarray(anti, dtype=BF16), jnp.asarray(shifted, dtype=BF16))
    return pl.pallas_call(
        _pos_dft_half_kernel,
        out_shape=out_shape,
        grid=(t // n,),
        in_specs=[_const_spec(a.shape) for a in consts]
        + [pl.BlockSpec((n, 2 * D_FOURIER), lambda i: (i, 0))],
        out_specs=pl.BlockSpec((n, D_FOURIER), lambda i: (i, 0)),
        scratch_shapes=[pltpu.VMEM((rows, n // 2), BF16), pltpu.VMEM((rows, n // 2), BF16)],
        compiler_params=_params(1),
        name=f"pos_dft_{n}",
    )(*consts, ucs)


def _attention_kernel(*refs, n_kv, single_pass, units):
    q_ref, refs = refs[0], refs[1:]
    k_refs, vt_refs, refs = refs[:n_kv], refs[n_kv:2 * n_kv], refs[2 * n_kv:]
    o_ref, s_ref, pp_ref, ot_ref = refs[-4:]
    tq = q_ref.shape[1] // units
    key_lens = [k_ref.shape[1] // (1 if single_pass else units) for k_ref in k_refs]
    chunks, base = [], 0
    for j, m in enumerate(key_lens):
        chunks += [(j, c, base + c) for c in range(0, m, KEY_CHUNK)]
        base += m

    def q_block(hh, u):
        return q_ref[hh, u * tq:(u + 1) * tq, :]

    def finish(hh, ot, u):
        l = ot[V_DIM:V_DIM + 1]
        ot_ref[u, pl.ds(pl.multiple_of(hh * V_DIM, V_DIM), V_DIM), :] = ot[:V_DIM] / l
        return l

    def first_chunk_scores(hh, u):
        j, c, _ = chunks[0]
        s = _dot_nt(k_refs[j][hh, c:c + KEY_CHUNK, :], q_block(hh, u))
        s_ref[u, hh * KEY_CHUNK:(hh + 1) * KEY_CHUNK, :] = s
        return jnp.max(s, axis=0, keepdims=True)

    def shifted_softmax_pv(score_head, pv_head, shifts, u):
        if score_head is not None:
            qh = q_block(score_head, u)
            shift = shifts[score_head]
        ot = None
        for idx, (j, c, r) in enumerate(chunks):
            if score_head is not None:
                s = (s_ref[u, score_head * KEY_CHUNK:(score_head + 1) * KEY_CHUNK, :] if idx == 0 else
                     _dot_nt(k_refs[j][score_head, c:c + KEY_CHUNK, :], qh))
                pp_ref[2 * u + score_head % 2, r:r + KEY_CHUNK, :] = jnp.exp2(s - shift).astype(BF16)
            if pv_head is not None:
                part = _dot(vt_refs[j][pv_head * V_ROWS:(pv_head + 1) * V_ROWS, c:c + KEY_CHUNK],
                            pp_ref[2 * u + pv_head % 2, r:r + KEY_CHUNK, :])
                ot = part if ot is None else ot + part
        return finish(pv_head, ot, u) if pv_head is not None else None

    def exact_scores(hh, slot, u, kseq):
        qh = q_block(hh, u)
        m8 = None
        for j, c, r in chunks:
            k0 = kseq * key_lens[j] + c
            s = _dot_nt(k_refs[j][hh, k0:k0 + KEY_CHUNK, :], qh)
            s_ref[slot, r:r + KEY_CHUNK, :] = s
            mc = jnp.max(s.reshape(KEY_CHUNK // 8, 8, tq), axis=0)
            m8 = mc if m8 is None else jnp.maximum(m8, mc)
        return jnp.max(m8, axis=0, keepdims=True)

    def exact_softmax_pv(hh, s_slot, p_slot, m, u, kseq):
        for _, _, r in chunks:
            pp_ref[p_slot, r:r + KEY_CHUNK, :] = jnp.exp2(s_ref[s_slot, r:r + KEY_CHUNK, :] - m).astype(BF16)
        rows = pl.ds(pl.multiple_of(hh * V_ROWS, V_ROWS), V_ROWS)
        ot, base = None, 0
        for j, vt_ref in enumerate(vt_refs):
            mk = key_lens[j]
            part = _dot(vt_ref[rows, kseq * mk:(kseq + 1) * mk], pp_ref[p_slot, base:base + mk, :])
            ot = part if ot is None else ot + part
            base += mk
        finish(hh, ot, u)

    if not single_pass:
        work = [(u, hh) for u in range(units) for hh in range(N_HEADS)]
        maxima = [exact_scores(hh, slot, u, u) for slot, (u, hh) in enumerate(work)]
        for slot, (u, hh) in enumerate(work):
            exact_softmax_pv(hh, slot, slot, maxima[slot], u, u)
    else:
        trusted = []
        for u in range(units):
            l_min = l_max = None
            shifts = [first_chunk_scores(hh, u) for hh in range(N_HEADS)]
            for hh in range(N_HEADS + 1):
                l = shifted_softmax_pv(hh if hh < N_HEADS else None, hh - 1 if hh > 0 else None, shifts, u)
                if l is not None:
                    l_min = l if l_min is None else jnp.minimum(l_min, l)
                    l_max = l if l_max is None else jnp.maximum(l_max, l)
            trusted.append(jnp.logical_and(jnp.min(l_min) >= MIN_DENOMINATOR,
                                           jnp.max(l_max) <= MAX_DENOMINATOR))

        for u in range(units):
            @pl.when(jnp.logical_not(trusted[u]))
            def _(u=u):
                def body(hh, carry):
                    exact_softmax_pv(hh, u, 2 * u, exact_scores(hh, u, u, 0), u, 0)
                    return carry

                lax.fori_loop(0, N_HEADS, body, 0)

    for u in range(units):
        o_ref[u * tq:(u + 1) * tq, :] = ot_ref[u].T.astype(BF16)


def _attention(q, ks, vts, key_lens, *, n, after=None):
    t = q.shape[1]
    tq = Q_TILE
    m_tot = sum(key_lens)
    single_pass = m_tot > 2 * KEY_CHUNK
    if single_pass:
        units = ATTENTION_UNITS
        steps = n // (units * tq)
        kv_rows = 1
        s_slots, p_slots = units, 2 * units
    else:
        assert n == tq
        units = min(ATTENTION_UNITS, t // n)
        steps = 1
        kv_rows = units
        s_slots = p_slots = units * N_HEADS
    in_specs = [pl.BlockSpec((N_HEADS, units * tq, HEAD_PAD), lambda bi, i: (0, bi * steps + i, 0))]
    in_specs += [pl.BlockSpec((N_HEADS, kv_rows * m, HEAD_PAD), lambda bi, i: (0, bi, 0)) for m in key_lens]
    in_specs += [pl.BlockSpec((VT_ROWS, kv_rows * m), lambda bi, i: (0, bi)) for m in key_lens]
    args = [q, *ks, *vts]
    if after is not None:
        in_specs.append(pl.BlockSpec((8, after.shape[1]), lambda bi, i: (0, 0)))
        args.append(after)
    return pl.pallas_call(
        functools.partial(_attention_kernel, n_kv=len(ks), single_pass=single_pass, units=units),
        out_shape=jax.ShapeDtypeStruct((t, D_ATTN), BF16),
        grid=(t // (n * kv_rows), steps),
        in_specs=in_specs,
        out_specs=pl.BlockSpec((units * tq, D_ATTN), lambda bi, i: (bi * steps + i, 0)),
        scratch_shapes=[pltpu.VMEM((s_slots, m_tot, tq), F32), pltpu.VMEM((p_slots, m_tot, tq), BF16),
                        pltpu.VMEM((units, D_ATTN, tq), F32)],
        compiler_params=_params(2),
        name=f"attention_{n}",
    )(*args)


def _merge_kernel(x_ref, mod_ref, f_ref, a_ref, w_zf_ref, w_za_ref, w_g_ref, w_f_ref, w_a_ref, w_o_ref,
                  g_ref, y_ref):
    gate = mod_ref[0][:, 2 * D_MODEL:]
    for rows in _sub_tiles(x_ref.shape[0], MERGE_SUB_TILE):
        x = x_ref[rows]
        h = _modulated(x, mod_ref)
        zf = _dot_nt(h, w_zf_ref[...])
        y_f = _dot(f_ref[rows] * (zf * (jnp.tanh(zf) + 1.0)).astype(BF16), w_f_ref[...])
        za = _dot_nt(h, w_za_ref[...])
        y_a = _dot(a_ref[rows] * (za * (jnp.tanh(za) + 1.0)).astype(BF16), w_a_ref[...])
        merged_x2 = ((jnp.tanh(_dot_nt(h, w_g_ref[:D_MODEL])) + 1.0) * y_f
                     + (jnp.tanh(_dot_nt(h, w_g_ref[D_MODEL:])) + 1.0) * y_a)
        out = x + gate * _dot(merged_x2.astype(BF16), w_o_ref[...])
        y_ref[rows] = _rms(out) * g_ref[...]


def _merge(x, mod3, f, attn, w_in_p, w_f, w_a, w_o, g, *, n, latent):
    t = x.shape[0]
    tm, _, mod_row = _token_tiling(t, n, per_sequence=latent)
    tok = lambda w: pl.BlockSpec((tm, w), lambda i: (i, 0))
    return pl.pallas_call(
        _merge_kernel,
        out_shape=jax.ShapeDtypeStruct((t, D_MODEL), F32),
        grid=(t // tm,),
        in_specs=[tok(D_MODEL),
                  pl.BlockSpec((1, 1, 3 * D_MODEL), lambda i: (mod_row(i), 0, 0)),
                  tok(D_FOURIER), tok(D_ATTN),
                  _row_spec(W_BLK, B_ZF), _row_spec(W_BLK, B_ZA), _row_spec(2 * D_MODEL, 1),
                  _const_spec(w_f.shape), _const_spec(w_a.shape), _const_spec(w_o.shape),
                  _const_spec(g.shape)],
        out_specs=tok(D_MODEL),
        compiler_params=_params(1),
        name=f"merge_{n}",
    )(x, mod3, f, attn, w_in_p, w_in_p, w_in_p, w_f, w_a, w_o, g)


def _dft_tables(n):
    norm = 1.0 / math.sqrt(n)
    k = np.arange(n)
    ang = ((k[:, None] * k[None, :]) % n) * (2.0 * math.pi / n)
    return (jnp.asarray(np.cos(ang) * norm, dtype=F32).astype(BF16),
            jnp.asarray(np.sin(ang) * -norm, dtype=F32).astype(BF16))


def _dft_factors(n, rows, n_pos):
    norm = 1.0 / math.sqrt(n)
    period = n // DFT_SPLIT
    pos = np.arange(n_pos)
    k1 = np.arange(-(-rows // DFT_SPLIT))
    k0 = np.arange(DFT_SPLIT)
    a1 = ((k1[:, None] * pos[None, :]) % period) * (2.0 * math.pi / period)
    a0 = ((k0[:, None] * pos[None, :]) % n) * (2.0 * math.pi / n)
    return tuple(a.astype(np.float32) for a in (np.cos(a1), np.sin(a1), np.cos(a0) * norm, np.sin(a0) * norm))


def _channel_dft_table():
    c = np.arange(GROUP)
    ang = ((c[:, None] * c[None, :]) % GROUP) * (2.0 * math.pi / GROUP)
    norm = 1.0 / math.sqrt(GROUP)
    return jnp.asarray(np.concatenate([np.cos(ang), np.sin(ang)], axis=1) * norm, dtype=F32).astype(BF16)


def _rope_tables(n):
    t = np.arange(n)
    row = (t // GRID_W).astype(np.float64)
    col = (t % GRID_W).astype(np.float64)
    half = QK_ROPE // 2
    inv = ROPE_THETA ** (-np.arange(0, half, 2, dtype=np.float64) / half)
    ar, ac = row[:, None] * inv, col[:, None] * inv
    ang = np.concatenate([ar, ar, ac, ac], axis=-1)
    pad = lambda a, fill: np.concatenate(
        [np.full((n, ROPE_LANE), fill), a, np.full((n, HEAD_PAD - ROPE_LANE - QK_ROPE), fill)],
        axis=1).astype(np.float32)
    sin_heads = np.tile(np.sin(ang), (1, N_HEADS)).astype(np.float32)
    return pad(np.cos(ang), 1.0), pad(np.sin(ang), 0.0), sin_heads


def kernel(x_prompt, x_sample, cache_ckv, cache_krope, c, c_ctx, w_ada, b_ada, w_in, q_norm_g, w_uq,
           kv_norm_g, w_ukv, w_f_out, w_a_out, w_out, final_norm_g):
    assert w_in.shape[0] == 1
    b_ctx, n_ctx, _ = x_prompt.shape
    dec_b, n_lat, _ = x_sample.shape
    past = cache_ckv.shape[2]

    w_q3 = w_uq[0].reshape(Q_RANK, N_HEADS, QK_NOPE + QK_ROPE)
    head_pad = lambda a, left: jnp.pad(
        a, ((0, 0), (0, 0), (left, HEAD_PAD - left - a.shape[2]))).reshape(Q_RANK, N_HEADS * HEAD_PAD)
    rp = w_q3[:, :, QK_NOPE:].reshape(Q_RANK, N_HEADS, 2, 2, QK_ROPE // 4)
    w_q_rot = jnp.stack([-rp[:, :, :, 1], rp[:, :, :, 0]], axis=3).reshape(Q_RANK, N_HEADS, QK_ROPE)
    w_uq_p = jnp.concatenate([head_pad(w_q3, 0), w_q_rot.reshape(Q_RANK, N_HEADS * QK_ROPE)], axis=1).astype(BF16)
    place_q = np.zeros((N_HEADS, QK_ROPE, N_HEADS, HEAD_PAD), np.float32)
    for hh in range(N_HEADS):
        for j in range(QK_ROPE):
            place_q[hh, j, hh, ROPE_LANE + j] = 1.0
    place_q = jnp.asarray(place_q.reshape(N_HEADS * QK_ROPE, N_HEADS * HEAD_PAD), dtype=BF16)
    w_kv3 = w_ukv[0].reshape(KV_RANK, N_HEADS, QK_NOPE + V_DIM)
    w_knope = jnp.pad(w_kv3[:, :, :QK_NOPE], ((0, 0), (0, 0), (0, HEAD_PAD - QK_NOPE)))
    place = np.zeros((HEAD_PAD, N_HEADS, HEAD_PAD), np.float32)
    for j in range(QK_ROPE):
        place[ROPE_LANE + j, :, ROPE_LANE + j] = 1.0
    w_k = jnp.concatenate([w_knope.reshape(KV_RANK, -1), jnp.asarray(place).reshape(HEAD_PAD, -1)],
                          axis=0).astype(BF16)
    w_uvt = jnp.pad(jnp.transpose(w_kv3[:, :, QK_NOPE:], (1, 2, 0)),
                    ((0, 0), (0, V_ROWS - V_DIM), (0, 0))).reshape(VT_ROWS, KV_RANK).astype(BF16)
    ones_col = np.zeros((N_HEADS, V_ROWS, LANES), np.float32)
    ones_col[:, V_DIM, :] = 1.0
    ones_col = jnp.asarray(ones_col.reshape(VT_ROWS, LANES))
    w_f = w_f_out[0].astype(BF16)
    w_a = w_a_out[0].astype(BF16)
    w_o = (0.5 * w_out[0]).astype(BF16)
    qg = q_norm_g[0].reshape(1, Q_RANK)
    kvg = kv_norm_g[0].reshape(1, KV_RANK)
    fg = final_norm_g.reshape(1, D_MODEL)
    wts = (qg, kvg, w_uq_p, w_k, w_uvt, ones_col, _channel_dft_table(), place_q)
    w_in_p = _pack_w_in(jnp.swapaxes(w_in[0], 0, 1))

    cond8 = jnp.concatenate([c_ctx[None, :], c, jnp.zeros((8 - 1 - dec_b, D_MODEL), F32)], axis=0)
    mod3 = _adaln(cond8, w_ada[0], b_ada[0].reshape(1, -1))

    xp = x_prompt.reshape(b_ctx * n_ctx, D_MODEL)
    if n_ctx <= REV_BLOCK and TOKEN_TILE % n_ctx == 0:
        f, q, k, vt, state_ckv, state_krope = _layer_in(xp, mod3, w_in_p, wts, None, n=n_ctx, state=True,
                                                        dft_tabs=_dft_tables(n_ctx))
    else:
        ucs, q, k, vt, state_ckv, state_krope = _layer_in(xp, mod3, w_in_p, wts, None, n=n_ctx, state=True)
        f = _pos_dft(ucs, n=n_ctx)
    attn = _attention(q, [k], [vt], [n_ctx], n=n_ctx, after=f)
    y_prompt = _merge(xp, mod3, f, attn, w_in_p, w_f, w_a, w_o, fg, n=n_ctx, latent=False)

    xs = x_sample.reshape(dec_b * n_lat, D_MODEL)
    krope_pad = jnp.pad(cache_krope[:, 0], ((0, 0), (0, 0), (ROPE_LANE, HEAD_PAD - ROPE_LANE - QK_ROPE)))
    cache = (cache_ckv[:, 0].reshape(dec_b * past, KV_RANK), krope_pad.reshape(dec_b * past, HEAD_PAD))
    ucs, q, k, vt, k_c, vt_c = _layer_in(xs, mod3, w_in_p, wts, _rope_tables(n_lat), n=n_lat, state=False,
                                         cache=cache)
    f = _pos_dft(ucs, n=n_lat)
    attn = _attention(q, [k, k_c], [vt, vt_c], [n_lat, past], n=n_lat, after=f)
    y_sample = _merge(xs, mod3, f, attn, w_in_p, w_f, w_a, w_o, fg, n=n_lat, latent=True)

    return (y_prompt.reshape(b_ctx, n_ctx, D_MODEL), y_sample.reshape(dec_b, n_lat, D_MODEL),
            state_ckv.reshape(b_ctx, 1, n_ctx, KV_RANK), jnp.swapaxes(state_krope, 1, 2)[:, None])
```
